```python
import math
import jax, jax.numpy as jnp
from jax import lax
import numpy as np

D_MODEL = 1024
BATCH = 2
SEQ = 16384
DEPTH = 1

EPS = 1e-5
NEG_INF = -1e30

A_HEADS = 8
A_HEAD_DIM = 64
A_WIDTH = A_HEADS * A_HEAD_DIM
A_PATTERNS = ((128, 1), (512, 4), (2048, 16))
A_QBLOCK = 64

B_HEADS = 8
B_NOPE = 64
B_ROPE = 32
B_V = 64
B_WIDTH = B_HEADS * B_V
Q_LORA = 256
KV_LORA = 128
ROPE_THETA = 10000.0
B_QBLOCK = 128

OFF_QA = 0
OFF_KA = OFF_QA + A_WIDTH
OFF_VA = OFF_KA + A_WIDTH
OFF_CQ = OFF_VA + A_WIDTH
OFF_CKV = OFF_CQ + Q_LORA
OFF_KR = OFF_CKV + KV_LORA
OFF_GA = OFF_KR + B_ROPE
OFF_GB = OFF_GA + D_MODEL
IN_COLS = OFF_GB + D_MODEL

N_EXPERTS = 32
TOP_K = 4
D_EXPERT = D_MODEL
SWIGLU_LIMIT = 7.0
SWIGLU_ALPHA = 1.702
MOE_BLOCK = 256

kernel_name = "hybrid_dilated_mla_moe_encoder"


def rmsnorm(x, g):
    xf = x.astype(jnp.float32)
    y = xf * lax.rsqrt(jnp.mean(xf * xf, axis=-1, keepdims=True) + EPS)
    return (y * g.astype(jnp.float32)).astype(x.dtype)


def alibi_slopes(n_heads):
    return 2.0 ** (-8.0 * (jnp.arange(n_heads, dtype=jnp.float32) + 1.0) / n_heads)


def dilated_pattern(q, k, v, window, dilation, slopes):
    B, S, H, Dh = q.shape
    hw = window // (2 * dilation)
    L = S // dilation
    nb = -(-L // A_QBLOCK)
    Lp = nb * A_QBLOCK
    W = A_QBLOCK + 2 * hw

    def strided(t):
        return t.reshape(B, L, dilation, H, Dh).transpose(0, 2, 1, 3, 4)

    qs = jnp.pad(strided(q), ((0, 0), (0, 0), (0, Lp - L), (0, 0), (0, 0)))
    kpad = ((0, 0), (0, 0), (hw, Lp - L + hw), (0, 0), (0, 0))
    ks = jnp.pad(strided(k), kpad)
    vs = jnp.pad(strided(v), kpad)
    qb = qs.reshape(B, dilation, nb, A_QBLOCK, H, Dh)
    kidx = jnp.arange(nb)[:, None] * A_QBLOCK + jnp.arange(W)[None, :]
    kb = ks[:, :, kidx]
    vb = vs[:, :, kidx]
    qpos = jnp.arange(nb)[:, None] * A_QBLOCK + jnp.arange(A_QBLOCK)[None, :]
    kpos = kidx - hw
    rel = kpos[:, None, :] - qpos[:, :, None]
    valid = (jnp.abs(rel) <= hw) & (kpos[:, None, :] >= 0) & (kpos[:, None, :] < L)
    dist = (jnp.abs(rel) * dilation).astype(jnp.float32)
    bias = -slopes[None, :, None, None] * dist[:, None, :, :]
    s = jnp.einsum('bdnqhe,bdnkhe->bdnhqk', qb, kb).astype(jnp.float32) * (A_HEAD_DIM ** -0.5)
    s = jnp.where(valid[:, None], s + bias, NEG_INF)
    m = jnp.max(s, axis=-1, keepdims=True)
    p = jnp.exp(s - m)
    l = jnp.sum(p, axis=-1)
    o = jnp.einsum('bdnhqk,bdnkhe->bdnqhe', (p / l[..., None]).astype(v.dtype), vb).astype(jnp.float32)
    lse = m[..., 0] + jnp.log(l)
    o = o.reshape(B, dilation, Lp, H, Dh)[:, :, :L].transpose(0, 2, 1, 3, 4).reshape(B, S, H, Dh)
    lse = lse.transpose(0, 1, 2, 4, 3).reshape(B, dilation, Lp, H)[:, :, :L]
    lse = lse.transpose(0, 2, 1, 3).reshape(B, S, H)
    return o, lse


def dilated_mixture(q, k, v):
    slopes = alibi_slopes(A_HEADS)
    outs, lses = [], []
    for window, dilation in A_PATTERNS:
        o, lse = dilated_pattern(q, k, v, window, dilation, slopes)
        outs.append(o)
        lses.append(lse)
    w = jax.nn.softmax(jnp.stack(lses, axis=0), axis=0)
    o = jnp.sum(w[..., None] * jnp.stack(outs, axis=0), axis=0)
    B, S = q.shape[0], q.shape[1]
    return o.reshape(B, S, A_WIDTH).astype(q.dtype)


def rope(x, cos, sin):
    xf = x.astype(jnp.float32)
    half = xf.shape[-1] // 2
    x1, x2 = xf[..., :half], xf[..., half:]
    return jnp.concatenate([x1 * cos - x2 * sin, x2 * cos + x1 * sin], axis=-1).astype(x.dtype)


def mla(c_q_raw, c_kv_raw, k_rope_raw, g_q, g_kv, w_uq, w_uk, w_uv):
    B, S, _ = c_q_raw.shape
    c_q = rmsnorm(c_q_raw, g_q)
    q = jnp.einsum('bsr,rhe->bshe', c_q, w_uq)
    q_nope, q_rope = q[..., :B_NOPE], q[..., B_NOPE:]
    c_kv = rmsnorm(c_kv_raw, g_kv)
    k_nope = jnp.einsum('bsr,rhe->bshe', c_kv, w_uk)
    v = jnp.einsum('bsr,rhe->bshe', c_kv, w_uv)
    pos = jnp.arange(S, dtype=jnp.float32)
    inv_freq = ROPE_THETA ** (-jnp.arange(0, B_ROPE, 2, dtype=jnp.float32) / B_ROPE)
    ang = pos[:, None] * inv_freq[None, :]
    cos, sin = jnp.cos(ang), jnp.sin(ang)
    q_rope = rope(q_rope, cos[None, :, None], sin[None, :, None])
    k_rope = rope(k_rope_raw, cos[None], sin[None])
    scale = (B_NOPE + B_ROPE) ** -0.5
    nb = S // B_QBLOCK
    qn = q_nope.reshape(B, nb, B_QBLOCK, B_HEADS, B_NOPE).transpose(1, 0, 2, 3, 4)
    qr = q_rope.reshape(B, nb, B_QBLOCK, B_HEADS, B_ROPE).transpose(1, 0, 2, 3, 4)

    def attend(blk):
        qn_b, qr_b = blk
        s = (jnp.einsum('bqhe,bkhe->bhqk', qn_b, k_nope)
             + jnp.einsum('bqhe,bke->bhqk', qr_b, k_rope)).astype(jnp.float32) * scale
        p = jax.nn.softmax(s, axis=-1)
        return jnp.einsum('bhqk,bkhe->bqhe', p.astype(v.dtype), v)

    o = lax.map(attend, (qn, qr))
    return o.transpose(1, 0, 2, 3, 4).reshape(B, S, B_WIDTH)


def moe(h, w_router, b_router, w_gu, b_gu, w_down, b_down):
    B, S, D = h.shape
    T = B * S
    xt = h.reshape(T, D)
    logits = (xt @ w_router + b_router).astype(jnp.float32)
    top_v, top_i = lax.top_k(logits, TOP_K)
    gates = jax.nn.softmax(top_v, axis=-1)
    N = T * TOP_K
    e_flat = top_i.reshape(N).astype(jnp.int32)
    tok_flat = (jnp.arange(N, dtype=jnp.int32) // TOP_K).astype(jnp.int32)
    g_flat = gates.reshape(N)
    order = jnp.argsort(e_flat)
    e_sorted = e_flat[order]
    counts = jnp.bincount(e_flat, length=N_EXPERTS).astype(jnp.int32)
    padded = (counts + MOE_BLOCK - 1) // MOE_BLOCK * MOE_BLOCK
    start = jnp.cumsum(counts) - counts
    pend = jnp.cumsum(padded)
    pstart = pend - padded
    rank = jnp.arange(N, dtype=jnp.int32) - start[e_sorted]
    dest = pstart[e_sorted] + rank
    NP = N + N_EXPERTS * MOE_BLOCK
    row_tok = jnp.full((NP,), T, dtype=jnp.int32).at[dest].set(tok_flat[order])
    row_w = jnp.zeros((NP,), jnp.float32).at[dest].set(g_flat[order])
    nblk = NP // MOE_BLOCK
    blk_start = jnp.arange(nblk, dtype=jnp.int32) * MOE_BLOCK
    blk_e = jnp.minimum(jnp.sum(pend[None, :] <= blk_start[:, None], axis=1), N_EXPERTS - 1).astype(jnp.int32)
    x_pad = jnp.concatenate([xt, jnp.zeros((1, D), xt.dtype)], axis=0)
    xb = x_pad[row_tok].reshape(nblk, MOE_BLOCK, D)

    def expert_block(args):
        xblk, e = args
        gu = xblk @ w_gu[e] + b_gu[e]
        gate = jnp.minimum(gu[:, :D_EXPERT], SWIGLU_LIMIT)
        up = jnp.clip(gu[:, D_EXPERT:], -SWIGLU_LIMIT, SWIGLU_LIMIT)
        act = (up + 1.0) * (gate * jax.nn.sigmoid(SWIGLU_ALPHA * gate))
        return act @ w_down[e] + b_down[e]

    yb = lax.map(expert_block, (xb, blk_e)).reshape(NP, D)
    out = jnp.zeros((T + 1, D), jnp.float32).at[row_tok].add(row_w[:, None] * yb.astype(jnp.float32))
    return out[:T].reshape(B, S, D).astype(h.dtype)


def setup_inputs(seed: int = 0) -> dict:
    key = jax.random.key(seed)
    ks = jax.random.split(key, 20)
    n = jax.random.normal
    f = jnp.float32
    return {
        "x": n(ks[0], (BATCH, SEQ, D_MODEL), f),
        "norm1_g": 1.0 + 0.02 * n(ks[1], (DEPTH, D_MODEL), f),
        "w_in": n(ks[2], (DEPTH, D_MODEL, IN_COLS), f) * D_MODEL ** -0.5,
        "q_a_norm_g": 1.0 + 0.02 * n(ks[3], (DEPTH, Q_LORA), f),
        "kv_a_norm_g": 1.0 + 0.02 * n(ks[4], (DEPTH, KV_LORA), f),
        "w_uq": n(ks[5], (DEPTH, Q_LORA, B_HEADS, B_NOPE + B_ROPE), f) * Q_LORA ** -0.5,
        "w_uk": n(ks[6], (DEPTH, KV_LORA, B_HEADS, B_NOPE), f) * KV_LORA ** -0.5,
        "w_uv": n(ks[7], (DEPTH, KV_LORA, B_HEADS, B_V), f) * KV_LORA ** -0.5,
        "w_a_out": n(ks[8], (DEPTH, A_WIDTH, D_MODEL), f) * A_WIDTH ** -0.5,
        "w_b_out": n(ks[9], (DEPTH, B_WIDTH, D_MODEL), f) * B_WIDTH ** -0.5,
        "w_o": n(ks[10], (DEPTH, D_MODEL, D_MODEL), f) * D_MODEL ** -0.5,
        "norm2_g": 1.0 + 0.02 * n(ks[11], (DEPTH, D_MODEL), f),
        "w_router": n(ks[12], (DEPTH, D_MODEL, N_EXPERTS), f) * D_MODEL ** -0.5,
        "b_router": 0.01 * n(ks[13], (DEPTH, N_EXPERTS), f),
        "w_gate_up": n(ks[14], (DEPTH, N_EXPERTS, D_MODEL, 2 * D_EXPERT), f) * D_MODEL ** -0.5,
        "b_gate_up": 0.02 * n(ks[15], (DEPTH, N_EXPERTS, 2 * D_EXPERT), f),
        "w_down": n(ks[16], (DEPTH, N_EXPERTS, D_EXPERT, D_MODEL), f) * D_EXPERT ** -0.5,
        "b_down": 0.02 * n(ks[17], (DEPTH, N_EXPERTS, D_MODEL), f),
        "final_g": 1.0 + 0.02 * n(ks[18], (D_MODEL,), f),
    }


def reference(x, norm1_g, w_in, q_a_norm_g, kv_a_norm_g, w_uq, w_uk, w_uv, w_a_out, w_b_out, w_o,
              norm2_g, w_router, b_router, w_gate_up, b_gate_up, w_down, b_down, final_g):
    B, S, D = x.shape
    for l in range(DEPTH):
        h = rmsnorm(x, norm1_g[l])
        z = h @ w_in[l]
        qa = z[..., OFF_QA:OFF_KA].reshape(B, S, A_HEADS, A_HEAD_DIM)
        ka = z[..., OFF_KA:OFF_VA].reshape(B, S, A_HEADS, A_HEAD_DIM)
        va = z[..., OFF_VA:OFF_CQ].reshape(B, S, A_HEADS, A_HEAD_DIM)
        o_a = dilated_mixture(qa, ka, va)
        o_b = mla(z[..., OFF_CQ:OFF_CKV], z[..., OFF_CKV:OFF_KR], z[..., OFF_KR:OFF_GA],
                  q_a_norm_g[l], kv_a_norm_g[l], w_uq[l], w_uk[l], w_uv[l])
        gate_a = jax.nn.sigmoid(z[..., OFF_GA:OFF_GB])
        gate_b = jax.nn.sigmoid(z[..., OFF_GB:IN_COLS])
        merged = gate_a * (o_a @ w_a_out[l]) + gate_b * (o_b @ w_b_out[l])
        x = x + merged @ w_o[l]
        h2 = rmsnorm(x, norm2_g[l])
        x = x + moe(h2, w_router[l], b_router[l], w_gate_up[l], b_gate_up[l], w_down[l], b_down[l])
    return rmsnorm(x, final_g)
```

```python
import functools
import math

import jax
import jax.numpy as jnp
from jax import lax
from jax.experimental import pallas as pl
from jax.experimental.pallas import tpu as pltpu

F32 = jnp.float32
BF16 = jnp.bfloat16

D_MODEL = 1024
EPS = 1e-5
NEG = -1e30

A_HEADS = 8
A_HEAD_DIM = 64
A_WIDTH = 512
A_PATTERNS = ((128, 1), (512, 4), (2048, 16))
A_HALF = 64
A_QB = 64
A_WIN = A_QB + 2 * A_HALF

B_HEADS = 8
B_NOPE = 64
B_ROPE = 32
B_V = 64
B_WIDTH = 512
Q_LORA = 256
KV_LORA = 128
ROPE_THETA = 10000.0
HEAD_PAD = 128

N_EXPERTS = 32
TOP_K = 4
SWIGLU_LIMIT = 7.0
SWIGLU_ALPHA = 1.702
LOGIT_PAD = 128

LOG2E = 1.4426950408889634
LN2 = 0.6931471805599453

VMEM_LIMIT = 56 * 1024 * 1024


def _cparams(*sem):
    return pltpu.CompilerParams(dimension_semantics=sem, vmem_limit_bytes=VMEM_LIMIT)


def _inproj_kernel(x_ref, g1_ref, wp_ref, gq_ref, gkv_ref, wq1_ref, wq2_ref, wuk_ref, wuv_ref,
                   cq_ref, sq_ref, ck_ref, sk_ref,
                   qa_ref, ka_ref, va_ref, qb_ref, kb_ref, vb_ref, gate_ref):
    x = x_ref[...]
    h = x * lax.rsqrt(jnp.mean(x * x, axis=-1, keepdims=True) + EPS) * g1_ref[...]
    h = h.astype(BF16)

    def proj(c0, c1):
        return jnp.dot(h, wp_ref[:, c0:c1], preferred_element_type=F32)

    qa_ref[...] = (proj(0, 512) * (A_HEAD_DIM ** -0.5)).astype(BF16)
    ka_ref[...] = proj(512, 1024).astype(BF16)
    va_ref[...] = proj(1024, 1536).astype(BF16)

    cq = proj(1536, 1792)
    cq = cq * lax.rsqrt(jnp.mean(cq * cq, axis=-1, keepdims=True) + EPS) * gq_ref[...]
    cq = cq.astype(BF16)
    q_main = jnp.dot(cq, wq1_ref[...], preferred_element_type=F32)
    q_rot = jnp.dot(cq, wq2_ref[...], preferred_element_type=F32)
    cq_t = cq_ref[...]
    sq_t = sq_ref[...]
    for hd in range(B_HEADS):
        sl = slice(hd * HEAD_PAD, (hd + 1) * HEAD_PAD)
        qb_ref[:, sl] = (q_main[:, sl] * cq_t + q_rot[:, sl] * sq_t).astype(BF16)

    ckv = proj(1792, 1920)
    ckv = ckv * lax.rsqrt(jnp.mean(ckv * ckv, axis=-1, keepdims=True) + EPS) * gkv_ref[...]
    ckv = ckv.astype(BF16)
    k_nope = jnp.dot(ckv, wuk_ref[...], preferred_element_type=F32)
    vb_ref[...] = jnp.dot(ckv, wuv_ref[...], preferred_element_type=F32).astype(BF16)
    k_rope = proj(3968, 4096) * ck_ref[...] + proj(4096, 4224) * sk_ref[...]
    for hd in range(B_HEADS):
        sl = slice(hd * HEAD_PAD, (hd + 1) * HEAD_PAD)
        kb_ref[:, sl] = (k_nope[:, sl] + k_rope).astype(BF16)

    gate_ref[...] = jax.nn.sigmoid(proj(1920, 3968)).astype(BF16)


def _inproj(x2, g1, wp, gq, gkv, wq1, wq2, wuk, wuv, cq_t, sq_t, ck_t, sk_t, seq, tm=512):
    T = x2.shape[0]
    nseq = seq // tm
    row = lambda i: (i, 0)
    fix = lambda i: (0, 0)
    tab = lambda i: (i % nseq, 0)
    full = lambda a: pl.BlockSpec(a.shape, fix)
    out_shape = (
        jax.ShapeDtypeStruct((T, A_WIDTH), BF16),
        jax.ShapeDtypeStruct((T, A_WIDTH), BF16),
        jax.ShapeDtypeStruct((T, A_WIDTH), BF16),
        jax.ShapeDtypeStruct((T, B_HEADS * HEAD_PAD), BF16),
        jax.ShapeDtypeStruct((T, B_HEADS * HEAD_PAD), BF16),
        jax.ShapeDtypeStruct((T, B_WIDTH), BF16),
        jax.ShapeDtypeStruct((T, 2 * D_MODEL), BF16),
    )
    return pl.pallas_call(
        _inproj_kernel,
        grid=(T // tm,),
        in_specs=[pl.BlockSpec((tm, D_MODEL), row), full(g1), full(wp), full(gq), full(gkv),
                  full(wq1), full(wq2), full(wuk), full(wuv),
                  pl.BlockSpec((tm, HEAD_PAD), tab), pl.BlockSpec((tm, HEAD_PAD), tab),
                  pl.BlockSpec((tm, HEAD_PAD), tab), pl.BlockSpec((tm, HEAD_PAD), tab)],
        out_specs=[pl.BlockSpec((tm, s.shape[1]), row) for s in out_shape],
        out_shape=out_shape,
        compiler_params=_cparams("parallel"),
        name="inproj",
    )(x2, g1, wp, gq, gkv, wq1, wq2, wuk, wuv, cq_t, sq_t, ck_t, sk_t)


def _dilated_kernel(q_ref, kc_ref, kp_ref, kn_ref, vc_ref, vp_ref, vn_ref, bias_ref,
                    o_ref, lse_ref, kbuf, vbuf, *, rows):
    i = pl.program_id(2)
    last = pl.num_programs(2) - 1
    kbuf[0:A_HALF, :] = kp_ref[0]
    kbuf[A_HALF:A_HALF + rows, :] = kc_ref[0]
    kbuf[A_HALF + rows:, :] = kn_ref[0]
    vbuf[0:A_HALF, :] = vp_ref[0]
    vbuf[A_HALF:A_HALF + rows, :] = vc_ref[0]
    vbuf[A_HALF + rows:, :] = vn_ref[0]

    nj = rows // A_QB
    col = lax.broadcasted_iota(jnp.int32, (1, A_WIN), 1)
    lane = lax.broadcasted_iota(jnp.int32, (A_QB, 128), 1)
    lo_half = lane < A_HEAD_DIM

    def body(j, carry):
        r0 = pl.multiple_of(j * A_QB, A_QB)
        lo_cut = jnp.where(jnp.logical_and(i == 0, j == 0), A_HALF, 0)
        hi_cut = jnp.where(jnp.logical_and(i == last, j == nj - 1), A_HALF + A_QB, A_WIN)
        edge = jnp.where(jnp.logical_or(col < lo_cut, col >= hi_cut), NEG, 0.0)
        for pr in range(A_HEADS // 2):
            ls = slice(pr * 128, (pr + 1) * 128)
            qp = q_ref[0, pl.ds(r0, A_QB), ls]
            kw = kbuf[pl.ds(r0, A_WIN), ls]
            vw = vbuf[pl.ds(r0, A_WIN), ls]
            outs, lses = [], []
            for hh in range(2):
                qh = jnp.where(lo_half if hh == 0 else jnp.logical_not(lo_half), qp, jnp.zeros_like(qp))
                s = lax.dot_general(qh, kw, (((1,), (1,)), ((), ())), preferred_element_type=F32)
                s = s + bias_ref[2 * pr + hh] + edge
                m = jnp.max(s, axis=-1, keepdims=True)
                p = jnp.exp(s - m)
                l = jnp.sum(p, axis=-1, keepdims=True)
                o = jnp.dot(p.astype(BF16), vw, preferred_element_type=F32) / l
                outs.append(o)
                lses.append(jnp.broadcast_to(m + jnp.log(l), (A_QB, 128)))
            o_ref[0, pl.ds(r0, A_QB), ls] = jnp.where(lo_half, outs[0], outs[1])
            lse_ref[0, pl.ds(r0, A_QB), ls] = jnp.where(lo_half, lses[0], lses[1])
        return carry

    lax.fori_loop(0, nj, body, 0)


def _dilated_pattern(qa, ka, va, bias, batch, seq, dil, rows=1024):
    L = seq // dil
    rows = min(rows, L)
    nblk = L // rows
    hb = rows // A_HALF
    nh = L // A_HALF
    view = lambda a: a.reshape(batch, L, dil * A_WIDTH)
    cur = lambda b, r, i: (b, i, r)
    prev = lambda b, r, i: (b, jnp.maximum(i * hb - 1, 0), r)
    nxt = lambda b, r, i: (b, jnp.minimum((i + 1) * hb, nh - 1), r)
    cur_spec = pl.BlockSpec((1, rows, A_WIDTH), cur)
    out_sds = jax.ShapeDtypeStruct((batch, L, dil * A_WIDTH), F32)
    o, lse = pl.pallas_call(
        functools.partial(_dilated_kernel, rows=rows),
        grid=(batch, dil, nblk),
        in_specs=[cur_spec, cur_spec,
                  pl.BlockSpec((1, A_HALF, A_WIDTH), prev), pl.BlockSpec((1, A_HALF, A_WIDTH), nxt),
                  cur_spec,
                  pl.BlockSpec((1, A_HALF, A_WIDTH), prev), pl.BlockSpec((1, A_HALF, A_WIDTH), nxt),
                  pl.BlockSpec(bias.shape, lambda b, r, i: (0, 0, 0))],
        out_specs=[cur_spec, cur_spec],
        out_shape=(out_sds, out_sds),
        scratch_shapes=[pltpu.VMEM((rows + 2 * A_HALF, A_WIDTH), BF16),
                        pltpu.VMEM((rows + 2 * A_HALF, A_WIDTH), BF16)],
        compiler_params=_cparams("parallel", "parallel", "parallel"),
        name=f"dilated_d{dil}",
    )(view(qa), view(ka), view(ka), view(ka), view(va), view(va), view(va), bias)
    T = batch * seq
    return o.reshape(T, A_WIDTH), lse.reshape(T, A_WIDTH)


def _dilated_bias(dil):
    slopes = 2.0 ** (-8.0 * (jnp.arange(A_HEADS, dtype=F32) + 1.0) / A_HEADS)
    rel = (jnp.arange(A_WIN)[None, :] - A_HALF) - jnp.arange(A_QB)[:, None]
    dist = (jnp.abs(rel) * dil).astype(F32)
    bias = -slopes[:, None, None] * dist[None]
    return jnp.where((jnp.abs(rel) <= A_HALF)[None], bias, NEG)


def _mla_kernel(q_ref, k_ref, v_ref, o_ref, *, tk):
    tq = q_ref.shape[1]
    nk = k_ref.shape[1] // tk
    outs = []
    for hh in range(2):
        ls = slice(hh * HEAD_PAD, (hh + 1) * HEAD_PAD)
        q = q_ref[0, :, ls]

        def body(j, carry, ls=ls, q=q):
            m, l, acc = carry
            off = pl.multiple_of(j * tk, tk)
            k = k_ref[0, pl.ds(off, tk), ls]
            v = v_ref[0, pl.ds(off, tk), :]
            s = lax.dot_general(q, k, (((1,), (1,)), ((), ())), preferred_element_type=F32)
            m_new = jnp.maximum(m, jnp.max(s, axis=-1, keepdims=True))
            alpha = jnp.exp2(m - m_new)
            p = jnp.exp2(s - m_new)
            l = alpha * l + jnp.sum(p, axis=-1, keepdims=True)
            acc = alpha * acc + jnp.dot(p.astype(BF16), v, preferred_element_type=F32)
            return m_new, l, acc

        init = (jnp.full((tq, 1), NEG, F32), jnp.zeros((tq, 1), F32), jnp.zeros((tq, 128), F32))
        m, l, acc = lax.fori_loop(0, nk, body, init)
        outs.append(acc / l)
    lane = lax.broadcasted_iota(jnp.int32, (tq, 128), 1)
    o_ref[0] = jnp.where(lane < B_V, outs[0], outs[1]).astype(BF16)


def _mla_attention(qb, kb, vb, batch, seq, tq=256, tk=512):
    return pl.pallas_call(
        functools.partial(_mla_kernel, tk=tk),
        grid=(batch, B_HEADS // 2, seq // tq),
        in_specs=[pl.BlockSpec((1, tq, 2 * HEAD_PAD), lambda b, p, i: (b, i, p)),
                  pl.BlockSpec((1, seq, 2 * HEAD_PAD), lambda b, p, i: (b, 0, p)),
                  pl.BlockSpec((1, seq, 128), lambda b, p, i: (b, 0, p))],
        out_specs=pl.BlockSpec((1, tq, 128), lambda b, p, i: (b, i, p)),
        out_shape=jax.ShapeDtypeStruct((batch, seq, B_WIDTH), BF16),
        compiler_params=_cparams("parallel", "parallel", "parallel"),
        name="mla_attention",
    )(qb, kb, vb)


def _outproj_kernel(o1_ref, o2_ref, o3_ref, l1_ref, l2_ref, l3_ref, ob_ref, gate_ref, x_ref,
                    wa_ref, wb_ref, wo_ref, g2_ref, wrh_ref, wrl_ref, br_ref,
                    x1_ref, h2_ref, logit_ref):
    l1, l2, l3 = l1_ref[...], l2_ref[...], l3_ref[...]
    mx = jnp.maximum(jnp.maximum(l1, l2), l3)
    e1, e2, e3 = jnp.exp(l1 - mx), jnp.exp(l2 - mx), jnp.exp(l3 - mx)
    oa = (e1 * o1_ref[...] + e2 * o2_ref[...] + e3 * o3_ref[...]) / (e1 + e2 + e3)
    ya = jnp.dot(oa.astype(BF16), wa_ref[...], preferred_element_type=F32)
    yb = jnp.dot(ob_ref[...], wb_ref[...], preferred_element_type=F32)
    merged = gate_ref[:, :D_MODEL].astype(F32) * ya + gate_ref[:, D_MODEL:].astype(F32) * yb
    x1 = x_ref[...] + jnp.dot(merged.astype(BF16), wo_ref[...], preferred_element_type=F32)
    x1_ref[...] = x1
    h2 = x1 * lax.rsqrt(jnp.mean(x1 * x1, axis=-1, keepdims=True) + EPS) * g2_ref[...]
    hi = h2.astype(BF16)
    h2_ref[...] = hi
    lo = (h2 - hi.astype(F32)).astype(BF16)
    logits = jnp.dot(hi, wrh_ref[...], preferred_element_type=F32)
    logits = logits + jnp.dot(lo, wrh_ref[...], preferred_element_type=F32)
    logits = logits + jnp.dot(hi, wrl_ref[...], preferred_element_type=F32)
    logit_ref[...] = logits + br_ref[...]


def _outproj(o1, o2, o3, l1, l2, l3, ob, gates, x2, wa, wb, wo, g2, wrh, wrl, br, tm=512):
    T = x2.shape[0]
    row = lambda i: (i, 0)
    full = lambda a: pl.BlockSpec(a.shape, lambda i: (0, 0))
    rows = lambda a: pl.BlockSpec((tm, a.shape[1]), row)
    out_shape = (jax.ShapeDtypeStruct((T, D_MODEL), F32),
                 jax.ShapeDtypeStruct((T, D_MODEL), BF16),
                 jax.ShapeDtypeStruct((T, LOGIT_PAD), F32))
    return pl.pallas_call(
        _outproj_kernel,
        grid=(T // tm,),
        in_specs=[rows(o1), rows(o2), rows(o3), rows(l1), rows(l2), rows(l3), rows(ob), rows(gates),
                  rows(x2), full(wa), full(wb), full(wo), full(g2), full(wrh), full(wrl), full(br)],
        out_specs=[pl.BlockSpec((tm, s.shape[1]), row) for s in out_shape],
        out_shape=out_shape,
        compiler_params=_cparams("parallel"),
        name="outproj",
    )(o1, o2, o3, l1, l2, l3, ob, gates, x2, wa, wb, wo, g2, wrh, wrl, br)


def _moe_kernel(be_ref, nu_ref, x_ref, wgu_ref, bgu_ref, wd_ref, bd_ref, y_ref):
    @pl.when(pl.program_id(0) < nu_ref[0])
    def _():
        gu = jnp.dot(x_ref[...], wgu_ref[0], preferred_element_type=F32) + bgu_ref[0]
        gate = jnp.minimum(gu[:, :D_MODEL], SWIGLU_LIMIT)
        up = jnp.clip(gu[:, D_MODEL:], -SWIGLU_LIMIT, SWIGLU_LIMIT)
        act = (up + 1.0) * (gate * jax.nn.sigmoid(SWIGLU_ALPHA * gate))
        y = jnp.dot(act.astype(BF16), wd_ref[0], preferred_element_type=F32) + bd_ref[0]
        y_ref[...] = y.astype(BF16)


def _moe_experts(xb, blk_e, n_used, wgu, bgu, wd, bd, bm):
    NP = xb.shape[0]
    nblk = NP // bm
    grid_spec = pltpu.PrefetchScalarGridSpec(
        num_scalar_prefetch=2,
        grid=(nblk,),
        in_specs=[pl.BlockSpec((bm, D_MODEL), lambda i, be, nu: (jnp.minimum(i, nu[0] - 1), 0)),
                  pl.BlockSpec((1, D_MODEL, 2 * D_MODEL), lambda i, be, nu: (be[i], 0, 0)),
                  pl.BlockSpec((1, 1, 2 * D_MODEL), lambda i, be, nu: (be[i], 0, 0)),
                  pl.BlockSpec((1, D_MODEL, D_MODEL), lambda i, be, nu: (be[i], 0, 0)),
                  pl.BlockSpec((1, 1, D_MODEL), lambda i, be, nu: (be[i], 0, 0))],
        out_specs=pl.BlockSpec((bm, D_MODEL), lambda i, be, nu: (jnp.minimum(i, nu[0] - 1), 0)),
    )
    return pl.pallas_call(
        _moe_kernel,
        grid_spec=grid_spec,
        out_shape=jax.ShapeDtypeStruct((NP, D_MODEL), BF16),
        compiler_params=_cparams("arbitrary"),
        name="moe_experts",
    )(blk_e, n_used, xb, wgu, bgu, wd, bd)


def _final_kernel(x1_ref, y_ref, w_ref, g_ref, o_ref):
    w = w_ref[...]
    acc = x1_ref[...]
    for k in range(TOP_K):
        acc = acc + w[:, k:k + 1] * y_ref[k].astype(F32)
    o_ref[...] = acc * lax.rsqrt(jnp.mean(acc * acc, axis=-1, keepdims=True) + EPS) * g_ref[...]


def _final(x1, yk, wk, g, tm=512):
    T = x1.shape[0]
    return pl.pallas_call(
        _final_kernel,
        grid=(T // tm,),
        in_specs=[pl.BlockSpec((tm, D_MODEL), lambda i: (i, 0)),
                  pl.BlockSpec((TOP_K, tm, D_MODEL), lambda i: (0, i, 0)),
                  pl.BlockSpec((tm, TOP_K), lambda i: (i, 0)),
                  pl.BlockSpec((1, D_MODEL), lambda i: (0, 0))],
        out_specs=pl.BlockSpec((tm, D_MODEL), lambda i: (i, 0)),
        out_shape=jax.ShapeDtypeStruct((T, D_MODEL), F32),
        compiler_params=_cparams("parallel"),
        name="final_norm",
    )(x1, yk, wk, g)


def _pack_w_in(w):
    kr = w[:, 1920:1952]
    half = B_ROPE // 2
    z = lambda n: jnp.zeros((w.shape[0], n), w.dtype)
    kr_placed = jnp.concatenate([z(B_NOPE), kr, z(HEAD_PAD - B_NOPE - B_ROPE)], axis=1)
    kr_rot = jnp.concatenate([z(B_NOPE), -kr[:, half:], kr[:, :half], z(HEAD_PAD - B_NOPE - B_ROPE)], axis=1)
    return jnp.concatenate([w[:, :1920], w[:, 1952:4000], kr_placed, kr_rot], axis=1).astype(BF16)


def _pack_w_uq(w):
    half = B_ROPE // 2
    r = w.shape[0]
    nope, x1, x2 = w[..., :B_NOPE], w[..., B_NOPE:B_NOPE + half], w[..., B_NOPE + half:]
    zpad = jnp.zeros((r, B_HEADS, HEAD_PAD - B_NOPE - B_ROPE), w.dtype)
    main = jnp.concatenate([nope, x1, x2, zpad], axis=-1)
    rot = jnp.concatenate([jnp.zeros_like(nope), -x2, x1, zpad], axis=-1)
    return (main.reshape(r, B_HEADS * HEAD_PAD).astype(BF16), rot.reshape(r, B_HEADS * HEAD_PAD).astype(BF16))


def _pack_w_uk(w):
    r = w.shape[0]
    zpad = jnp.zeros((r, B_HEADS, HEAD_PAD - B_NOPE), w.dtype)
    return jnp.concatenate([w, zpad], axis=-1).reshape(r, B_HEADS * HEAD_PAD).astype(BF16)


def _rope_tables(seq):
    half = B_ROPE // 2
    pos = jnp.arange(seq, dtype=F32)
    inv_freq = ROPE_THETA ** (-jnp.arange(0, B_ROPE, 2, dtype=F32) / B_ROPE)
    ang = pos[:, None] * inv_freq[None, :]
    cos, sin = jnp.cos(ang), jnp.sin(ang)
    one = jnp.ones((seq, B_NOPE), F32)
    zero = jnp.zeros((seq, B_NOPE), F32)
    zpad = jnp.zeros((seq, HEAD_PAD - B_NOPE - B_ROPE), F32)
    qs = ((B_NOPE + B_ROPE) ** -0.5) * LOG2E
    cq = jnp.concatenate([one, cos, cos, zpad], axis=1) * qs
    sq = jnp.concatenate([zero, sin, sin, zpad], axis=1) * qs
    ck = jnp.concatenate([zero, cos, cos, zpad], axis=1)
    sk = jnp.concatenate([zero, sin, sin, zpad], axis=1)
    return cq, sq, ck, sk


def kernel(x, norm1_g, w_in, q_a_norm_g, kv_a_norm_g, w_uq, w_uk, w_uv, w_a_out, w_b_out, w_o,
           norm2_g, w_router, b_router, w_gate_up, b_gate_up, w_down, b_down, final_g):
    B, S, D = x.shape
    T = B * S
    depth = norm1_g.shape[0]
    xt = x.reshape(T, D)
    cq_t, sq_t, ck_t, sk_t = _rope_tables(S)
    biases = [_dilated_bias(d) for _, d in A_PATTERNS]
    bm = 512
    for l in range(depth):
        wq1, wq2 = _pack_w_uq(w_uq[l])
        qa, ka, va, qb, kb, vb, gates = _inproj(
            xt, norm1_g[l][None], _pack_w_in(w_in[l]), q_a_norm_g[l][None], kv_a_norm_g[l][None],
            wq1, wq2, _pack_w_uk(w_uk[l]), w_uv[l].reshape(KV_LORA, B_WIDTH).astype(BF16),
            cq_t, sq_t, ck_t, sk_t, S)
        outs = [_dilated_pattern(qa, ka, va, biases[p], B, S, A_PATTERNS[p][1]) for p in range(3)]
        ob = _mla_attention(qb.reshape(B, S, -1), kb.reshape(B, S, -1), vb.reshape(B, S, -1), B, S)

        wr = jnp.pad(w_router[l], ((0, 0), (0, LOGIT_PAD - N_EXPERTS)))
        wrh = wr.astype(BF16)
        wrl = (wr - wrh.astype(F32)).astype(BF16)
        br = jnp.pad(b_router[l], (0, LOGIT_PAD - N_EXPERTS))[None]
        x1, h2, logits = _outproj(
            outs[0][0], outs[1][0], outs[2][0], outs[0][1], outs[1][1], outs[2][1],
            ob.reshape(T, B_WIDTH), gates, xt,
            w_a_out[l].astype(BF16), w_b_out[l].astype(BF16), w_o[l].astype(BF16),
            norm2_g[l][None], wrh, wrl, br)

        logits = logits[:, :N_EXPERTS]
        top_v, top_i = lax.top_k(logits, TOP_K)
        gate_w = jax.nn.softmax(top_v, axis=-1)
        sel = jnp.sum((top_i[:, :, None] == jnp.arange(N_EXPERTS)[None, None, :]).astype(jnp.int32), axis=1)
        incl = jnp.cumsum(sel, axis=0)
        counts = incl[-1]
        rank = jnp.take_along_axis(incl - sel, top_i, axis=1)
        padded = (counts + bm - 1) // bm * bm
        pend = jnp.cumsum(padded)
        pstart = pend - padded
        dest = pstart[top_i] + rank
        NP = T * TOP_K + N_EXPERTS * bm
        nblk = NP // bm
        tok = jnp.broadcast_to(jnp.arange(T, dtype=jnp.int32)[:, None], (T, TOP_K))
        row_tok = jnp.zeros((NP,), jnp.int32).at[dest.reshape(-1)].set(tok.reshape(-1))
        blk_start = jnp.arange(nblk, dtype=jnp.int32) * bm
        blk_e = jnp.minimum(jnp.sum(pend[None, :] <= blk_start[:, None], axis=1), N_EXPERTS - 1).astype(jnp.int32)
        n_used = (pend[-1] // bm).astype(jnp.int32)[None]

        xb = jnp.take(h2, row_tok, axis=0)
        yb = _moe_experts(xb, blk_e, n_used, w_gate_up[l].astype(BF16), b_gate_up[l][:, None, :],
                          w_down[l].astype(BF16), b_down[l][:, None, :], bm)
        yk = jnp.take(yb, dest.T.reshape(-1), axis=0).reshape(TOP_K, T, D)
        if l + 1 < depth:
            raise NotImplementedError("depth > 1")
        out = _final(x1, yk, gate_w, final_g[None])
    return out.reshape(B, S, D)
```

```python
import functools
import math

import jax
import jax.numpy as jnp
from jax import lax
from jax.experimental import pallas as pl
from jax.experimental.pallas import tpu as pltpu

F32 = jnp.float32
BF16 = jnp.bfloat16

D_MODEL = 1024
EPS = 1e-5
NEG = -1e30

A_HEADS = 8
A_HEAD_DIM = 64
A_WIDTH = 512
A_PATTERNS = ((128, 1), (512, 4), (2048, 16))
A_HALF = 64
A_QB = 64
A_WIN = A_QB + 2 * A_HALF

B_HEADS = 8
B_NOPE = 64
B_ROPE = 32
B_V = 64
B_WIDTH = 512
Q_LORA = 256
KV_LORA = 128
ROPE_THETA = 10000.0
HEAD_PAD = 128

N_EXPERTS = 32
TOP_K = 4
SWIGLU_LIMIT = 7.0
SWIGLU_ALPHA = 1.702
LOGIT_PAD = 128

LOG2E = 1.4426950408889634
LN2 = 0.6931471805599453

VMEM_LIMIT = 56 * 1024 * 1024


def _cparams(*sem):
    return pltpu.CompilerParams(dimension_semantics=sem, vmem_limit_bytes=VMEM_LIMIT)


def _inproj_kernel(x_ref, g1_ref, wp_ref, gq_ref, gkv_ref, wq1_ref, wq2_ref, wuk_ref, wuv_ref,
                   cq_ref, sq_ref, ck_ref, sk_ref,
                   qa_ref, ka_ref, va_ref, qb_ref, kb_ref, vb_ref, gate_ref):
    x = x_ref[...]
    h = x * lax.rsqrt(jnp.mean(x * x, axis=-1, keepdims=True) + EPS) * g1_ref[...]
    h = h.astype(BF16)

    def proj(c0, c1):
        return jnp.dot(h, wp_ref[:, c0:c1], preferred_element_type=F32)

    qa_ref[...] = (proj(0, 512) * (A_HEAD_DIM ** -0.5)).astype(BF16)
    ka_ref[...] = proj(512, 1024).astype(BF16)
    va_ref[...] = proj(1024, 1536).astype(BF16)

    cq = proj(1536, 1792)
    cq = cq * lax.rsqrt(jnp.mean(cq * cq, axis=-1, keepdims=True) + EPS) * gq_ref[...]
    cq = cq.astype(BF16)
    q_main = jnp.dot(cq, wq1_ref[...], preferred_element_type=F32)
    q_rot = jnp.dot(cq, wq2_ref[...], preferred_element_type=F32)
    cq_t = cq_ref[...]
    sq_t = sq_ref[...]
    for hd in range(B_HEADS):
        sl = slice(hd * HEAD_PAD, (hd + 1) * HEAD_PAD)
        qb_ref[:, sl] = (q_main[:, sl] * cq_t + q_rot[:, sl] * sq_t).astype(BF16)

    ckv = proj(1792, 1920)
    ckv = ckv * lax.rsqrt(jnp.mean(ckv * ckv, axis=-1, keepdims=True) + EPS) * gkv_ref[...]
    ckv = ckv.astype(BF16)
    k_nope = jnp.dot(ckv, wuk_ref[...], preferred_element_type=F32)
    vb_ref[...] = lax.dot_general(wuv_ref[...], ckv, (((1,), (1,)), ((), ())),
                                  preferred_element_type=F32).astype(BF16)
    k_rope = proj(3968, 4096) * ck_ref[...] + proj(4096, 4224) * sk_ref[...]
    for hd in range(B_HEADS):
        sl = slice(hd * HEAD_PAD, (hd + 1) * HEAD_PAD)
        kb_ref[:, sl] = (k_nope[:, sl] + k_rope).astype(BF16)

    gate_ref[...] = jax.nn.sigmoid(proj(1920, 3968)).astype(BF16)


def _inproj(x2, g1, wp, gq, gkv, wq1, wq2, wuk, wuv, cq_t, sq_t, ck_t, sk_t, seq, tm=512):
    T = x2.shape[0]
    nseq = seq // tm
    row = lambda i: (i, 0)
    fix = lambda i: (0, 0)
    tab = lambda i: (i % nseq, 0)
    full = lambda a: pl.BlockSpec(a.shape, fix)
    out_shape = (
        jax.ShapeDtypeStruct((T, A_WIDTH), BF16),
        jax.ShapeDtypeStruct((T, A_WIDTH), BF16),
        jax.ShapeDtypeStruct((T, A_WIDTH), BF16),
        jax.ShapeDtypeStruct((T, B_HEADS * HEAD_PAD), BF16),
        jax.ShapeDtypeStruct((T, B_HEADS * HEAD_PAD), BF16),
        jax.ShapeDtypeStruct((B_WIDTH, T), BF16),
        jax.ShapeDtypeStruct((T, 2 * D_MODEL), BF16),
    )
    out_specs = [pl.BlockSpec((tm, s.shape[1]), row) for s in out_shape]
    out_specs[5] = pl.BlockSpec((B_WIDTH, tm), lambda i: (0, i))
    return pl.pallas_call(
        _inproj_kernel,
        grid=(T // tm,),
        in_specs=[pl.BlockSpec((tm, D_MODEL), row), full(g1), full(wp), full(gq), full(gkv),
                  full(wq1), full(wq2), full(wuk), full(wuv),
                  pl.BlockSpec((tm, HEAD_PAD), tab), pl.BlockSpec((tm, HEAD_PAD), tab),
                  pl.BlockSpec((tm, HEAD_PAD), tab), pl.BlockSpec((tm, HEAD_PAD), tab)],
        out_specs=out_specs,
        out_shape=out_shape,
        compiler_params=_cparams("parallel"),
        name="inproj",
    )(x2, g1, wp, gq, gkv, wq1, wq2, wuk, wuv, cq_t, sq_t, ck_t, sk_t)


def _dilated_kernel(q_ref, kc_ref, kp_ref, kn_ref, vc_ref, vp_ref, vn_ref, bias_ref,
                    o_ref, lse_ref, kbuf, vbuf, *, rows):
    i = pl.program_id(2)
    last = pl.num_programs(2) - 1
    kbuf[0:A_HALF, :] = kp_ref[0]
    kbuf[A_HALF:A_HALF + rows, :] = kc_ref[0]
    kbuf[A_HALF + rows:, :] = kn_ref[0]
    vbuf[0:A_HALF, :] = vp_ref[0]
    vbuf[A_HALF:A_HALF + rows, :] = vc_ref[0]
    vbuf[A_HALF + rows:, :] = vn_ref[0]

    nj = rows // A_QB
    col = lax.broadcasted_iota(jnp.int32, (1, A_WIN), 1)
    lane = lax.broadcasted_iota(jnp.int32, (A_QB, 128), 1)
    lo_half = lane < A_HEAD_DIM

    def body(j, carry):
        r0 = pl.multiple_of(j * A_QB, A_QB)
        lo_cut = jnp.where(jnp.logical_and(i == 0, j == 0), A_HALF, 0)
        hi_cut = jnp.where(jnp.logical_and(i == last, j == nj - 1), A_HALF + A_QB, A_WIN)
        edge = jnp.where(jnp.logical_or(col < lo_cut, col >= hi_cut), NEG, 0.0)
        for pr in range(A_HEADS // 2):
            ls = slice(pr * 128, (pr + 1) * 128)
            qp = q_ref[0, pl.ds(r0, A_QB), ls]
            kw = kbuf[pl.ds(r0, A_WIN), ls]
            vw = vbuf[pl.ds(r0, A_WIN), ls]
            outs, lses = [], []
            for hh in range(2):
                qh = jnp.where(lo_half if hh == 0 else jnp.logical_not(lo_half), qp, jnp.zeros_like(qp))
                s = lax.dot_general(qh, kw, (((1,), (1,)), ((), ())), preferred_element_type=F32)
                s = s + bias_ref[2 * pr + hh] + edge
                m = jnp.max(s, axis=-1, keepdims=True)
                p = jnp.exp(s - m)
                l = jnp.sum(p, axis=-1, keepdims=True)
                o = jnp.dot(p.astype(BF16), vw, preferred_element_type=F32) / l
                outs.append(o)
                lses.append(jnp.broadcast_to(m + jnp.log(l), (A_QB, 128)))
            o_ref[0, pl.ds(r0, A_QB), ls] = jnp.where(lo_half, outs[0], outs[1])
            lse_ref[0, pl.ds(r0, A_QB), ls] = jnp.where(lo_half, lses[0], lses[1])
        return carry

    lax.fori_loop(0, nj, body, 0)


def _dilated_pattern(qa, ka, va, bias, batch, seq, dil, rows=1024):
    L = seq // dil
    rows = min(rows, L)
    nblk = L // rows
    hb = rows // A_HALF
    nh = L // A_HALF
    view = lambda a: a.reshape(batch, L, dil * A_WIDTH)
    cur = lambda b, r, i: (b, i, r)
    prev = lambda b, r, i: (b, jnp.maximum(i * hb - 1, 0), r)
    nxt = lambda b, r, i: (b, jnp.minimum((i + 1) * hb, nh - 1), r)
    cur_spec = pl.BlockSpec((1, rows, A_WIDTH), cur)
    out_sds = jax.ShapeDtypeStruct((batch, L, dil * A_WIDTH), F32)
    o, lse = pl.pallas_call(
        functools.partial(_dilated_kernel, rows=rows),
        grid=(batch, dil, nblk),
        in_specs=[cur_spec, cur_spec,
                  pl.BlockSpec((1, A_HALF, A_WIDTH), prev), pl.BlockSpec((1, A_HALF, A_WIDTH), nxt),
                  cur_spec,
                  pl.BlockSpec((1, A_HALF, A_WIDTH), prev), pl.BlockSpec((1, A_HALF, A_WIDTH), nxt),
                  pl.BlockSpec(bias.shape, lambda b, r, i: (0, 0, 0))],
        out_specs=[cur_spec, cur_spec],
        out_shape=(out_sds, out_sds),
        scratch_shapes=[pltpu.VMEM((rows + 2 * A_HALF, A_WIDTH), BF16),
                        pltpu.VMEM((rows + 2 * A_HALF, A_WIDTH), BF16)],
        compiler_params=_cparams("parallel", "parallel", "parallel"),
        name=f"dilated_d{dil}",
    )(view(qa), view(ka), view(ka), view(ka), view(va), view(va), view(va), bias)
    T = batch * seq
    return o.reshape(T, A_WIDTH), lse.reshape(T, A_WIDTH)


def _dilated_bias(dil):
    slopes = 2.0 ** (-8.0 * (jnp.arange(A_HEADS, dtype=F32) + 1.0) / A_HEADS)
    rel = (jnp.arange(A_WIN)[None, :] - A_HALF) - jnp.arange(A_QB)[:, None]
    dist = (jnp.abs(rel) * dil).astype(F32)
    bias = -slopes[:, None, None] * dist[None]
    return jnp.where((jnp.abs(rel) <= A_HALF)[None], bias, NEG)


def _mla_kernel(q_ref, k_ref, vt_ref, o_ref, s_even, s_odd, *, tk):
    tq = q_ref.shape[1]
    nk = k_ref.shape[1] // tk
    qs = [q_ref[0, :, hh * HEAD_PAD:(hh + 1) * HEAD_PAD] for hh in range(2)]

    def scores(j, s_ref):
        off = pl.multiple_of(j * tk, tk)
        for hh in range(2):
            s_ref[hh] = lax.dot_general(k_ref[0, pl.ds(off, tk), hh * HEAD_PAD:(hh + 1) * HEAD_PAD], qs[hh],
                                        (((1,), (1,)), ((), ())), preferred_element_type=F32)

    def softmax_pv(j, s_ref, stats):
        off = pl.multiple_of(j * tk, tk)
        new = []
        for hh in range(2):
            m, l, acc = stats[hh]
            s = s_ref[hh]
            vt = vt_ref[hh * B_V:(hh + 1) * B_V, pl.ds(off, tk)]
            m_new = jnp.maximum(m, jnp.max(s, axis=0, keepdims=True))
            alpha = jnp.exp2(m - m_new)
            p = jnp.exp2(s - m_new)
            l = alpha * l + jnp.sum(p, axis=0, keepdims=True)
            acc = alpha * acc + jnp.dot(vt, p.astype(BF16), preferred_element_type=F32)
            new.append((m_new, l, acc))
        return tuple(new)

    def body(jj, stats):
        j = 2 * jj
        scores(j + 1, s_odd)
        stats = softmax_pv(j, s_even, stats)
        scores(jnp.minimum(j + 2, nk - 1), s_even)
        return softmax_pv(j + 1, s_odd, stats)

    init = (jnp.full((1, tq), NEG, F32), jnp.zeros((1, tq), F32), jnp.zeros((B_V, tq), F32))
    scores(0, s_even)
    (_, l0, acc0), (_, l1, acc1) = lax.fori_loop(0, nk // 2, body, (init, init))
    o_t = jnp.concatenate([acc0 / l0, acc1 / l1], axis=0)
    o_ref[0] = o_t.T.astype(BF16)


def _mla_attention(qb, kb, vbt, batch, seq, tq=256, tk=512):
    return pl.pallas_call(
        functools.partial(_mla_kernel, tk=tk),
        grid=(batch, B_HEADS // 2, seq // tq),
        in_specs=[pl.BlockSpec((1, tq, 2 * HEAD_PAD), lambda b, p, i: (b, i, p)),
                  pl.BlockSpec((1, seq, 2 * HEAD_PAD), lambda b, p, i: (b, 0, p)),
                  pl.BlockSpec((2 * B_V, seq), lambda b, p, i: (p, b))],
        out_specs=pl.BlockSpec((1, tq, 128), lambda b, p, i: (b, i, p)),
        out_shape=jax.ShapeDtypeStruct((batch, seq, B_WIDTH), BF16),
        scratch_shapes=[pltpu.VMEM((2, tk, tq), F32), pltpu.VMEM((2, tk, tq), F32)],
        compiler_params=_cparams("parallel", "parallel", "parallel"),
        name="mla_attention",
    )(qb, kb, vbt)


def _outproj_kernel(o1_ref, o2_ref, o3_ref, l1_ref, l2_ref, l3_ref, ob_ref, gate_ref, x_ref,
                    wa_ref, wb_ref, wo_ref, g2_ref, wrh_ref, wrl_ref, br_ref,
                    x1_ref, h2_ref, logit_ref):
    l1, l2, l3 = l1_ref[...], l2_ref[...], l3_ref[...]
    mx = jnp.maximum(jnp.maximum(l1, l2), l3)
    e1, e2, e3 = jnp.exp(l1 - mx), jnp.exp(l2 - mx), jnp.exp(l3 - mx)
    oa = (e1 * o1_ref[...] + e2 * o2_ref[...] + e3 * o3_ref[...]) / (e1 + e2 + e3)
    ya = jnp.dot(oa.astype(BF16), wa_ref[...], preferred_element_type=F32)
    yb = jnp.dot(ob_ref[...], wb_ref[...], preferred_element_type=F32)
    merged = gate_ref[:, :D_MODEL].astype(F32) * ya + gate_ref[:, D_MODEL:].astype(F32) * yb
    x1 = x_ref[...] + jnp.dot(merged.astype(BF16), wo_ref[...], preferred_element_type=F32)
    x1_ref[...] = x1
    h2 = x1 * lax.rsqrt(jnp.mean(x1 * x1, axis=-1, keepdims=True) + EPS) * g2_ref[...]
    hi = h2.astype(BF16)
    h2_ref[...] = hi
    lo = (h2 - hi.astype(F32)).astype(BF16)
    logits = jnp.dot(hi, wrh_ref[...], preferred_element_type=F32)
    logits = logits + jnp.dot(lo, wrh_ref[...], preferred_element_type=F32)
    logits = logits + jnp.dot(hi, wrl_ref[...], preferred_element_type=F32)
    logit_ref[...] = logits + br_ref[...]


def _outproj(o1, o2, o3, l1, l2, l3, ob, gates, x2, wa, wb, wo, g2, wrh, wrl, br, tm=512):
    T = x2.shape[0]
    row = lambda i: (i, 0)
    full = lambda a: pl.BlockSpec(a.shape, lambda i: (0, 0))
    rows = lambda a: pl.BlockSpec((tm, a.shape[1]), row)
    out_shape = (jax.ShapeDtypeStruct((T, D_MODEL), F32),
                 jax.ShapeDtypeStruct((T, D_MODEL), BF16),
                 jax.ShapeDtypeStruct((T, LOGIT_PAD), F32))
    return pl.pallas_call(
        _outproj_kernel,
        grid=(T // tm,),
        in_specs=[rows(o1), rows(o2), rows(o3), rows(l1), rows(l2), rows(l3), rows(ob), rows(gates),
                  rows(x2), full(wa), full(wb), full(wo), full(g2), full(wrh), full(wrl), full(br)],
        out_specs=[pl.BlockSpec((tm, s.shape[1]), row) for s in out_shape],
        out_shape=out_shape,
        compiler_params=_cparams("parallel"),
        name="outproj",
    )(o1, o2, o3, l1, l2, l3, ob, gates, x2, wa, wb, wo, g2, wrh, wrl, br)


def _moe_kernel(be_ref, nu_ref, x_ref, wgu_ref, bgu_ref, wd_ref, bd_ref, y_ref):
    @pl.when(pl.program_id(0) < nu_ref[0])
    def _():
        gu = jnp.dot(x_ref[...], wgu_ref[0], preferred_element_type=F32) + bgu_ref[0]
        gate = jnp.minimum(gu[:, :D_MODEL], SWIGLU_LIMIT)
        up = jnp.clip(gu[:, D_MODEL:], -SWIGLU_LIMIT, SWIGLU_LIMIT)
        act = (up + 1.0) * (gate * jax.nn.sigmoid(SWIGLU_ALPHA * gate))
        y = jnp.dot(act.astype(BF16), wd_ref[0], preferred_element_type=F32) + bd_ref[0]
        y_ref[...] = y.astype(BF16)


def _moe_experts(xb, blk_e, n_used, wgu, bgu, wd, bd, bm):
    NP = xb.shape[0]
    nblk = NP // bm
    grid_spec = pltpu.PrefetchScalarGridSpec(
        num_scalar_prefetch=2,
        grid=(nblk,),
        in_specs=[pl.BlockSpec((bm, D_MODEL), lambda i, be, nu: (jnp.minimum(i, nu[0] - 1), 0)),
                  pl.BlockSpec((1, D_MODEL, 2 * D_MODEL), lambda i, be, nu: (be[i], 0, 0)),
                  pl.BlockSpec((1, 1, 2 * D_MODEL), lambda i, be, nu: (be[i], 0, 0)),
                  pl.BlockSpec((1, D_MODEL, D_MODEL), lambda i, be, nu: (be[i], 0, 0)),
                  pl.BlockSpec((1, 1, D_MODEL), lambda i, be, nu: (be[i], 0, 0))],
        out_specs=pl.BlockSpec((bm, D_MODEL), lambda i, be, nu: (jnp.minimum(i, nu[0] - 1), 0)),
    )
    return pl.pallas_call(
        _moe_kernel,
        grid_spec=grid_spec,
        out_shape=jax.ShapeDtypeStruct((NP, D_MODEL), BF16),
        compiler_params=_cparams("arbitrary"),
        name="moe_experts",
    )(blk_e, n_used, xb, wgu, bgu, wd, bd)


def _final_kernel(x1_ref, y_ref, w_ref, g_ref, o_ref):
    w = w_ref[...]
    acc = x1_ref[...]
    for k in range(TOP_K):
        acc = acc + w[:, k:k + 1] * y_ref[k].astype(F32)
    o_ref[...] = acc * lax.rsqrt(jnp.mean(acc * acc, axis=-1, keepdims=True) + EPS) * g_ref[...]


def _final(x1, yk, wk, g, tm=512):
    T = x1.shape[0]
    return pl.pallas_call(
        _final_kernel,
        grid=(T // tm,),
        in_specs=[pl.BlockSpec((tm, D_MODEL), lambda i: (i, 0)),
                  pl.BlockSpec((TOP_K, tm, D_MODEL), lambda i: (0, i, 0)),
                  pl.BlockSpec((tm, TOP_K), lambda i: (i, 0)),
                  pl.BlockSpec((1, D_MODEL), lambda i: (0, 0))],
        out_specs=pl.BlockSpec((tm, D_MODEL), lambda i: (i, 0)),
        out_shape=jax.ShapeDtypeStruct((T, D_MODEL), F32),
        compiler_params=_cparams("parallel"),
        name="final_norm",
    )(x1, yk, wk, g)


def _pack_w_in(w):
    kr = w[:, 1920:1952]
    half = B_ROPE // 2
    z = lambda n: jnp.zeros((w.shape[0], n), w.dtype)
    kr_placed = jnp.concatenate([z(B_NOPE), kr, z(HEAD_PAD - B_NOPE - B_ROPE)], axis=1)
    kr_rot = jnp.concatenate([z(B_NOPE), -kr[:, half:], kr[:, :half], z(HEAD_PAD - B_NOPE - B_ROPE)], axis=1)
    return jnp.concatenate([w[:, :1920], w[:, 1952:4000], kr_placed, kr_rot], axis=1).astype(BF16)


def _pack_w_uq(w):
    half = B_ROPE // 2
    r = w.shape[0]
    nope, x1, x2 = w[..., :B_NOPE], w[..., B_NOPE:B_NOPE + half], w[..., B_NOPE + half:]
    zpad = jnp.zeros((r, B_HEADS, HEAD_PAD - B_NOPE - B_ROPE), w.dtype)
    main = jnp.concatenate([nope, x1, x2, zpad], axis=-1)
    rot = jnp.concatenate([jnp.zeros_like(nope), -x2, x1, zpad], axis=-1)
    return (main.reshape(r, B_HEADS * HEAD_PAD).astype(BF16), rot.reshape(r, B_HEADS * HEAD_PAD).astype(BF16))


def _pack_w_uk(w):
    r = w.shape[0]
    zpad = jnp.zeros((r, B_HEADS, HEAD_PAD - B_NOPE), w.dtype)
    return jnp.concatenate([w, zpad], axis=-1).reshape(r, B_HEADS * HEAD_PAD).astype(BF16)


def _rope_tables(seq):
    half = B_ROPE // 2
    pos = jnp.arange(seq, dtype=F32)
    inv_freq = ROPE_THETA ** (-jnp.arange(0, B_ROPE, 2, dtype=F32) / B_ROPE)
    ang = pos[:, None] * inv_freq[None, :]
    cos, sin = jnp.cos(ang), jnp.sin(ang)
    one = jnp.ones((seq, B_NOPE), F32)
    zero = jnp.zeros((seq, B_NOPE), F32)
    zpad = jnp.zeros((seq, HEAD_PAD - B_NOPE - B_ROPE), F32)
    qs = ((B_NOPE + B_ROPE) ** -0.5) * LOG2E
    cq = jnp.concatenate([one, cos, cos, zpad], axis=1) * qs
    sq = jnp.concatenate([zero, sin, sin, zpad], axis=1) * qs
    ck = jnp.concatenate([zero, cos, cos, zpad], axis=1)
    sk = jnp.concatenate([zero, sin, sin, zpad], axis=1)
    return cq, sq, ck, sk


def kernel(x, norm1_g, w_in, q_a_norm_g, kv_a_norm_g, w_uq, w_uk, w_uv, w_a_out, w_b_out, w_o,
           norm2_g, w_router, b_router, w_gate_up, b_gate_up, w_down, b_down, final_g):
    B, S, D = x.shape
    T = B * S
    depth = norm1_g.shape[0]
    xt = x.reshape(T, D)
    cq_t, sq_t, ck_t, sk_t = _rope_tables(S)
    biases = [_dilated_bias(d) for _, d in A_PATTERNS]
    bm = 512
    for l in range(depth):
        wq1, wq2 = _pack_w_uq(w_uq[l])
        qa, ka, va, qb, kb, vb, gates = _inproj(
            xt, norm1_g[l][None], _pack_w_in(w_in[l]), q_a_norm_g[l][None], kv_a_norm_g[l][None],
            wq1, wq2, _pack_w_uk(w_uk[l]), w_uv[l].reshape(KV_LORA, B_WIDTH).T.astype(BF16),
            cq_t, sq_t, ck_t, sk_t, S)
        outs = [_dilated_pattern(qa, ka, va, biases[p], B, S, A_PATTERNS[p][1]) for p in range(3)]
        ob = _mla_attention(qb.reshape(B, S, -1), kb.reshape(B, S, -1), vb, B, S)

        wr = jnp.pad(w_router[l], ((0, 0), (0, LOGIT_PAD - N_EXPERTS)))
        wrh = wr.astype(BF16)
        wrl = (wr - wrh.astype(F32)).astype(BF16)
        br = jnp.pad(b_router[l], (0, LOGIT_PAD - N_EXPERTS))[None]
        x1, h2, logits = _outproj(
            outs[0][0], outs[1][0], outs[2][0], outs[0][1], outs[1][1], outs[2][1],
            ob.reshape(T, B_WIDTH), gates, xt,
            w_a_out[l].astype(BF16), w_b_out[l].astype(BF16), w_o[l].astype(BF16),
            norm2_g[l][None], wrh, wrl, br)

        logits = logits[:, :N_EXPERTS]
        top_v, top_i = lax.top_k(logits, TOP_K)
        gate_w = jax.nn.softmax(top_v, axis=-1)
        sel = jnp.sum((top_i[:, :, None] == jnp.arange(N_EXPERTS)[None, None, :]).astype(jnp.int32), axis=1)
        incl = jnp.cumsum(sel, axis=0)
        counts = incl[-1]
        rank = jnp.take_along_axis(incl - sel, top_i, axis=1)
        padded = (counts + bm - 1) // bm * bm
        pend = jnp.cumsum(padded)
        pstart = pend - padded
        dest = pstart[top_i] + rank
        NP = T * TOP_K + N_EXPERTS * bm
        nblk = NP // bm
        tok = jnp.broadcast_to(jnp.arange(T, dtype=jnp.int32)[:, None], (T, TOP_K))
        row_tok = jnp.zeros((NP,), jnp.int32).at[dest.reshape(-1)].set(tok.reshape(-1))
        blk_start = jnp.arange(nblk, dtype=jnp.int32) * bm
        blk_e = jnp.minimum(jnp.sum(pend[None, :] <= blk_start[:, None], axis=1), N_EXPERTS - 1).astype(jnp.int32)
        n_used = (pend[-1] // bm).astype(jnp.int32)[None]

        xb = jnp.take(h2, row_tok, axis=0)
        yb = _moe_experts(xb, blk_e, n_used, w_gate_up[l].astype(BF16), b_gate_up[l][:, None, :],
                          w_down[l].astype(BF16), b_down[l][:, None, :], bm)
        yk = jnp.take(yb, dest.T.reshape(-1), axis=0).reshape(TOP_K, T, D)
        if l + 1 < depth:
            raise NotImplementedError("depth > 1")
        out = _final(x1, yk, gate_w, final_g[None])
    return out.reshape(B, S, D)
```

```python
import functools
import math

import jax
import jax.numpy as jnp
from jax import lax
from jax.experimental import pallas as pl
from jax.experimental.pallas import tpu as pltpu

F32 = jnp.float32
BF16 = jnp.bfloat16

D_MODEL = 1024
EPS = 1e-5
NEG = -1e30

A_HEADS = 8
A_HEAD_DIM = 64
A_WIDTH = 512
A_PATTERNS = ((128, 1), (512, 4), (2048, 16))
A_HALF = 64
A_QB = 64
A_WIN = A_QB + 2 * A_HALF

B_HEADS = 8
B_NOPE = 64
B_ROPE = 32
B_V = 64
B_WIDTH = 512
Q_LORA = 256
KV_LORA = 128
ROPE_THETA = 10000.0
HEAD_PAD = 128
ONES_ROWS = 16
MLA_SCORE_BUFS = 4

N_EXPERTS = 32
TOP_K = 4
SWIGLU_LIMIT = 7.0
SWIGLU_ALPHA = 1.702
LOGIT_PAD = 128

LOG2E = 1.4426950408889634
LN2 = 0.6931471805599453

VMEM_LIMIT = 56 * 1024 * 1024


def _cparams(*sem):
    return pltpu.CompilerParams(dimension_semantics=sem, vmem_limit_bytes=VMEM_LIMIT)


def _inproj_kernel(x_ref, g1_ref, wp_ref, gq_ref, gkv_ref, wq1_ref, wq2_ref, wuk_ref, wuv_ref,
                   cq_ref, sq_ref, ck_ref, sk_ref,
                   qa_ref, ka_ref, va_ref, qb_ref, kb_ref, vb_ref, gate_ref):
    x = x_ref[...]
    h = x * lax.rsqrt(jnp.mean(x * x, axis=-1, keepdims=True) + EPS) * g1_ref[...]
    h = h.astype(BF16)

    def proj(c0, c1):
        return jnp.dot(h, wp_ref[:, c0:c1], preferred_element_type=F32)

    qa_ref[...] = (proj(0, 512) * (A_HEAD_DIM ** -0.5)).astype(BF16)
    ka_ref[...] = proj(512, 1024).astype(BF16)
    va_ref[...] = proj(1024, 1536).astype(BF16)

    cq = proj(1536, 1792)
    cq = cq * lax.rsqrt(jnp.mean(cq * cq, axis=-1, keepdims=True) + EPS) * gq_ref[...]
    cq = cq.astype(BF16)
    q_main = jnp.dot(cq, wq1_ref[...], preferred_element_type=F32)
    q_rot = jnp.dot(cq, wq2_ref[...], preferred_element_type=F32)
    cq_t = cq_ref[...]
    sq_t = sq_ref[...]
    for hd in range(B_HEADS):
        sl = slice(hd * HEAD_PAD, (hd + 1) * HEAD_PAD)
        qb_ref[:, sl] = (q_main[:, sl] * cq_t + q_rot[:, sl] * sq_t).astype(BF16)

    ckv = proj(1792, 1920)
    ckv = ckv * lax.rsqrt(jnp.mean(ckv * ckv, axis=-1, keepdims=True) + EPS) * gkv_ref[...]
    ckv = ckv.astype(BF16)
    k_nope = jnp.dot(ckv, wuk_ref[...], preferred_element_type=F32)
    vb_ref[...] = lax.dot_general(wuv_ref[...], ckv, (((1,), (1,)), ((), ())),
                                  preferred_element_type=F32).astype(BF16)
    k_rope = proj(3968, 4096) * ck_ref[...] + proj(4096, 4224) * sk_ref[...]
    for hd in range(B_HEADS):
        sl = slice(hd * HEAD_PAD, (hd + 1) * HEAD_PAD)
        kb_ref[:, sl] = (k_nope[:, sl] + k_rope).astype(BF16)

    gate_ref[...] = jax.nn.sigmoid(proj(1920, 3968)).astype(BF16)


def _inproj(x2, g1, wp, gq, gkv, wq1, wq2, wuk, wuv, cq_t, sq_t, ck_t, sk_t, seq, tm=512):
    T = x2.shape[0]
    nseq = seq // tm
    row = lambda i: (i, 0)
    fix = lambda i: (0, 0)
    tab = lambda i: (i % nseq, 0)
    full = lambda a: pl.BlockSpec(a.shape, fix)
    out_shape = (
        jax.ShapeDtypeStruct((T, A_WIDTH), BF16),
        jax.ShapeDtypeStruct((T, A_WIDTH), BF16),
        jax.ShapeDtypeStruct((T, A_WIDTH), BF16),
        jax.ShapeDtypeStruct((T, B_HEADS * HEAD_PAD), BF16),
        jax.ShapeDtypeStruct((T, B_HEADS * HEAD_PAD), BF16),
        jax.ShapeDtypeStruct((B_WIDTH, T), BF16),
        jax.ShapeDtypeStruct((T, 2 * D_MODEL), BF16),
    )
    out_specs = [pl.BlockSpec((tm, s.shape[1]), row) for s in out_shape]
    out_specs[5] = pl.BlockSpec((B_WIDTH, tm), lambda i: (0, i))
    return pl.pallas_call(
        _inproj_kernel,
        grid=(T // tm,),
        in_specs=[pl.BlockSpec((tm, D_MODEL), row), full(g1), full(wp), full(gq), full(gkv),
                  full(wq1), full(wq2), full(wuk), full(wuv),
                  pl.BlockSpec((tm, HEAD_PAD), tab), pl.BlockSpec((tm, HEAD_PAD), tab),
                  pl.BlockSpec((tm, HEAD_PAD), tab), pl.BlockSpec((tm, HEAD_PAD), tab)],
        out_specs=out_specs,
        out_shape=out_shape,
        compiler_params=_cparams("parallel"),
        name="inproj",
    )(x2, g1, wp, gq, gkv, wq1, wq2, wuk, wuv, cq_t, sq_t, ck_t, sk_t)


def _dilated_kernel(q_ref, kc_ref, kp_ref, kn_ref, vc_ref, vp_ref, vn_ref, bias_ref,
                    o_ref, lse_ref, kbuf, vbuf, *, rows):
    i = pl.program_id(2)
    last = pl.num_programs(2) - 1
    kbuf[0:A_HALF, :] = kp_ref[0]
    kbuf[A_HALF:A_HALF + rows, :] = kc_ref[0]
    kbuf[A_HALF + rows:, :] = kn_ref[0]
    vbuf[0:A_HALF, :] = vp_ref[0]
    vbuf[A_HALF:A_HALF + rows, :] = vc_ref[0]
    vbuf[A_HALF + rows:, :] = vn_ref[0]

    nj = rows // A_QB
    col = lax.broadcasted_iota(jnp.int32, (1, A_WIN), 1)
    lane = lax.broadcasted_iota(jnp.int32, (A_QB, 128), 1)
    lo_half = lane < A_HEAD_DIM

    def body(j, carry):
        r0 = pl.multiple_of(j * A_QB, A_QB)
        lo_cut = jnp.where(jnp.logical_and(i == 0, j == 0), A_HALF, 0)
        hi_cut = jnp.where(jnp.logical_and(i == last, j == nj - 1), A_HALF + A_QB, A_WIN)
        edge = jnp.where(jnp.logical_or(col < lo_cut, col >= hi_cut), NEG, 0.0)
        for pr in range(A_HEADS // 2):
            ls = slice(pr * 128, (pr + 1) * 128)
            qp = q_ref[0, pl.ds(r0, A_QB), ls]
            kw = kbuf[pl.ds(r0, A_WIN), ls]
            vw = vbuf[pl.ds(r0, A_WIN), ls]
            outs, lses = [], []
            for hh in range(2):
                qh = jnp.where(lo_half if hh == 0 else jnp.logical_not(lo_half), qp, jnp.zeros_like(qp))
                s = lax.dot_general(qh, kw, (((1,), (1,)), ((), ())), preferred_element_type=F32)
                s = s + bias_ref[2 * pr + hh] + edge
                m = jnp.max(s, axis=-1, keepdims=True)
                p = jnp.exp(s - m)
                l = jnp.sum(p, axis=-1, keepdims=True)
                o = jnp.dot(p.astype(BF16), vw, preferred_element_type=F32) / l
                outs.append(o)
                lses.append(jnp.broadcast_to(m + jnp.log(l), (A_QB, 128)))
            o_ref[0, pl.ds(r0, A_QB), ls] = jnp.where(lo_half, outs[0], outs[1])
            lse_ref[0, pl.ds(r0, A_QB), ls] = jnp.where(lo_half, lses[0], lses[1])
        return carry

    lax.fori_loop(0, nj, body, 0)


def _dilated_pattern(qa, ka, va, bias, batch, seq, dil, rows=1024):
    L = seq // dil
    rows = min(rows, L)
    nblk = L // rows
    hb = rows // A_HALF
    nh = L // A_HALF
    view = lambda a: a.reshape(batch, L, dil * A_WIDTH)
    cur = lambda b, r, i: (b, i, r)
    prev = lambda b, r, i: (b, jnp.maximum(i * hb - 1, 0), r)
    nxt = lambda b, r, i: (b, jnp.minimum((i + 1) * hb, nh - 1), r)
    cur_spec = pl.BlockSpec((1, rows, A_WIDTH), cur)
    out_sds = jax.ShapeDtypeStruct((batch, L, dil * A_WIDTH), F32)
    o, lse = pl.pallas_call(
        functools.partial(_dilated_kernel, rows=rows),
        grid=(batch, dil, nblk),
        in_specs=[cur_spec, cur_spec,
                  pl.BlockSpec((1, A_HALF, A_WIDTH), prev), pl.BlockSpec((1, A_HALF, A_WIDTH), nxt),
                  cur_spec,
                  pl.BlockSpec((1, A_HALF, A_WIDTH), prev), pl.BlockSpec((1, A_HALF, A_WIDTH), nxt),
                  pl.BlockSpec(bias.shape, lambda b, r, i: (0, 0, 0))],
        out_specs=[cur_spec, cur_spec],
        out_shape=(out_sds, out_sds),
        scratch_shapes=[pltpu.VMEM((rows + 2 * A_HALF, A_WIDTH), BF16),
                        pltpu.VMEM((rows + 2 * A_HALF, A_WIDTH), BF16)],
        compiler_params=_cparams("parallel", "parallel", "parallel"),
        name=f"dilated_d{dil}",
    )(view(qa), view(ka), view(ka), view(ka), view(va), view(va), view(va), bias)
    T = batch * seq
    return o.reshape(T, A_WIDTH), lse.reshape(T, A_WIDTH)


def _dilated_bias(dil):
    slopes = 2.0 ** (-8.0 * (jnp.arange(A_HEADS, dtype=F32) + 1.0) / A_HEADS)
    rel = (jnp.arange(A_WIN)[None, :] - A_HALF) - jnp.arange(A_QB)[:, None]
    dist = (jnp.abs(rel) * dil).astype(F32)
    bias = -slopes[:, None, None] * dist[None]
    return jnp.where((jnp.abs(rel) <= A_HALF)[None], bias, NEG)


def _mla_kernel(q_ref, k_ref, vt_ref, o_ref, *s_bufs, tk):
    tq = q_ref.shape[1]
    nk = k_ref.shape[1] // tk
    qs = [q_ref[0, :, hh * HEAD_PAD:(hh + 1) * HEAD_PAD] for hh in range(2)]

    def scores(j, s_ref):
        off = pl.multiple_of(j * tk, tk)
        for hh in range(2):
            s_ref[hh] = lax.dot_general(k_ref[0, pl.ds(off, tk), hh * HEAD_PAD:(hh + 1) * HEAD_PAD], qs[hh],
                                        (((1,), (1,)), ((), ())), preferred_element_type=F32)

    ones = jnp.ones((ONES_ROWS, tk), BF16)

    def softmax_pv(j, s_ref, stats):
        off = pl.multiple_of(j * tk, tk)
        new = []
        for hh in range(2):
            m, acc = stats[hh]
            s = s_ref[hh]
            vt = jnp.concatenate([vt_ref[hh * B_V:(hh + 1) * B_V, pl.ds(off, tk)], ones], axis=0)
            m_new = jnp.maximum(m, jnp.max(s, axis=0, keepdims=True))
            alpha = jnp.exp2(m - m_new)
            p = jnp.exp2(s - m_new)
            acc = alpha * acc + jnp.dot(vt, p.astype(BF16), preferred_element_type=F32)
            new.append((m_new, acc))
        return tuple(new)

    nbuf = len(s_bufs)

    def body(jj, stats):
        j = nbuf * jj
        for u in range(nbuf):
            scores(jnp.minimum(j + u + 1, nk - 1), s_bufs[(u + 1) % nbuf])
            stats = softmax_pv(j + u, s_bufs[u], stats)
        return stats

    init = (jnp.full((1, tq), NEG, F32), jnp.zeros((B_V + ONES_ROWS, tq), F32))
    scores(0, s_bufs[0])
    (_, acc0), (_, acc1) = lax.fori_loop(0, nk // nbuf, body, (init, init))
    o_t = jnp.concatenate([a[:B_V] / a[B_V:B_V + 1] for a in (acc0, acc1)], axis=0)
    o_ref[0] = o_t.T.astype(BF16)


def _mla_attention(qb, kb, vbt, batch, seq, tq=256, tk=512):
    return pl.pallas_call(
        functools.partial(_mla_kernel, tk=tk),
        grid=(batch, B_HEADS // 2, seq // tq),
        in_specs=[pl.BlockSpec((1, tq, 2 * HEAD_PAD), lambda b, p, i: (b, i, p)),
                  pl.BlockSpec((1, seq, 2 * HEAD_PAD), lambda b, p, i: (b, 0, p)),
                  pl.BlockSpec((2 * B_V, seq), lambda b, p, i: (p, b))],
        out_specs=pl.BlockSpec((1, tq, 128), lambda b, p, i: (b, i, p)),
        out_shape=jax.ShapeDtypeStruct((batch, seq, B_WIDTH), BF16),
        scratch_shapes=[pltpu.VMEM((2, tk, tq), F32) for _ in range(MLA_SCORE_BUFS)],
        compiler_params=_cparams("parallel", "parallel", "parallel"),
        name="mla_attention",
    )(qb, kb, vbt)


def _outproj_kernel(o1_ref, o2_ref, o3_ref, l1_ref, l2_ref, l3_ref, ob_ref, gate_ref, x_ref,
                    wa_ref, wb_ref, wo_ref, g2_ref, wrh_ref, wrl_ref, br_ref,
                    x1_ref, h2_ref, logit_ref):
    l1, l2, l3 = l1_ref[...], l2_ref[...], l3_ref[...]
    mx = jnp.maximum(jnp.maximum(l1, l2), l3)
    e1, e2, e3 = jnp.exp(l1 - mx), jnp.exp(l2 - mx), jnp.exp(l3 - mx)
    oa = (e1 * o1_ref[...] + e2 * o2_ref[...] + e3 * o3_ref[...]) / (e1 + e2 + e3)
    ya = jnp.dot(oa.astype(BF16), wa_ref[...], preferred_element_type=F32)
    yb = jnp.dot(ob_ref[...], wb_ref[...], preferred_element_type=F32)
    merged = gate_ref[:, :D_MODEL].astype(F32) * ya + gate_ref[:, D_MODEL:].astype(F32) * yb
    x1 = x_ref[...] + jnp.dot(merged.astype(BF16), wo_ref[...], preferred_element_type=F32)
    x1_ref[...] = x1
    h2 = x1 * lax.rsqrt(jnp.mean(x1 * x1, axis=-1, keepdims=True) + EPS) * g2_ref[...]
    hi = h2.astype(BF16)
    h2_ref[...] = hi
    lo = (h2 - hi.astype(F32)).astype(BF16)
    logits = jnp.dot(hi, wrh_ref[...], preferred_element_type=F32)
    logits = logits + jnp.dot(lo, wrh_ref[...], preferred_element_type=F32)
    logits = logits + jnp.dot(hi, wrl_ref[...], preferred_element_type=F32)
    logit_ref[...] = logits + br_ref[...]


def _outproj(o1, o2, o3, l1, l2, l3, ob, gates, x2, wa, wb, wo, g2, wrh, wrl, br, tm=512):
    T = x2.shape[0]
    row = lambda i: (i, 0)
    full = lambda a: pl.BlockSpec(a.shape, lambda i: (0, 0))
    rows = lambda a: pl.BlockSpec((tm, a.shape[1]), row)
    out_shape = (jax.ShapeDtypeStruct((T, D_MODEL), F32),
                 jax.ShapeDtypeStruct((T, D_MODEL), BF16),
                 jax.ShapeDtypeStruct((T, LOGIT_PAD), F32))
    return pl.pallas_call(
        _outproj_kernel,
        grid=(T // tm,),
        in_specs=[rows(o1), rows(o2), rows(o3), rows(l1), rows(l2), rows(l3), rows(ob), rows(gates),
                  rows(x2), full(wa), full(wb), full(wo), full(g2), full(wrh), full(wrl), full(br)],
        out_specs=[pl.BlockSpec((tm, s.shape[1]), row) for s in out_shape],
        out_shape=out_shape,
        compiler_params=_cparams("parallel"),
        name="outproj",
    )(o1, o2, o3, l1, l2, l3, ob, gates, x2, wa, wb, wo, g2, wrh, wrl, br)


def _moe_kernel(be_ref, nu_ref, x_ref, wgu_ref, bgu_ref, wd_ref, bd_ref, y_ref):
    @pl.when(pl.program_id(0) >= nu_ref[0])
    def _():
        y_ref[...] = jnp.zeros_like(y_ref)

    @pl.when(pl.program_id(0) < nu_ref[0])
    def _():
        gu = jnp.dot(x_ref[...], wgu_ref[0], preferred_element_type=F32) + bgu_ref[0]
        gate = jnp.minimum(gu[:, :D_MODEL], SWIGLU_LIMIT)
        up = jnp.clip(gu[:, D_MODEL:], -SWIGLU_LIMIT, SWIGLU_LIMIT)
        act = (up + 1.0) * (gate * jax.nn.sigmoid(SWIGLU_ALPHA * gate))
        y = jnp.dot(act.astype(BF16), wd_ref[0], preferred_element_type=F32) + bd_ref[0]
        y_ref[...] = y.astype(BF16)


def _moe_experts(xb, blk_e, n_used, wgu, bgu, wd, bd, bm):
    NP = xb.shape[0]
    nblk = NP // bm
    grid_spec = pltpu.PrefetchScalarGridSpec(
        num_scalar_prefetch=2,
        grid=(nblk,),
        in_specs=[pl.BlockSpec((bm, D_MODEL), lambda i, be, nu: (jnp.minimum(i, nu[0] - 1), 0)),
                  pl.BlockSpec((1, D_MODEL, 2 * D_MODEL), lambda i, be, nu: (be[i], 0, 0)),
                  pl.BlockSpec((1, 1, 2 * D_MODEL), lambda i, be, nu: (be[i], 0, 0)),
                  pl.BlockSpec((1, D_MODEL, D_MODEL), lambda i, be, nu: (be[i], 0, 0)),
                  pl.BlockSpec((1, 1, D_MODEL), lambda i, be, nu: (be[i], 0, 0))],
        out_specs=pl.BlockSpec((bm, D_MODEL), lambda i, be, nu: (i, 0)),
    )
    return pl.pallas_call(
        _moe_kernel,
        grid_spec=grid_spec,
        out_shape=jax.ShapeDtypeStruct((NP, D_MODEL), BF16),
        compiler_params=_cparams("arbitrary"),
        name="moe_experts",
    )(blk_e, n_used, xb, wgu, bgu, wd, bd)


def _final_kernel(x1_ref, y_ref, w_ref, g_ref, o_ref):
    w = w_ref[...]
    acc = x1_ref[...]
    for k in range(TOP_K):
        acc = acc + w[:, k:k + 1] * y_ref[k].astype(F32)
    o_ref[...] = acc * lax.rsqrt(jnp.mean(acc * acc, axis=-1, keepdims=True) + EPS) * g_ref[...]


def _final(x1, yk, wk, g, tm=512):
    T = x1.shape[0]
    return pl.pallas_call(
        _final_kernel,
        grid=(T // tm,),
        in_specs=[pl.BlockSpec((tm, D_MODEL), lambda i: (i, 0)),
                  pl.BlockSpec((TOP_K, tm, D_MODEL), lambda i: (0, i, 0)),
                  pl.BlockSpec((tm, TOP_K), lambda i: (i, 0)),
                  pl.BlockSpec((1, D_MODEL), lambda i: (0, 0))],
        out_specs=pl.BlockSpec((tm, D_MODEL), lambda i: (i, 0)),
        out_shape=jax.ShapeDtypeStruct((T, D_MODEL), F32),
        compiler_params=_cparams("parallel"),
        name="final_norm",
    )(x1, yk, wk, g)


def _pack_w_in(w):
    kr = w[:, 1920:1952]
    half = B_ROPE // 2
    z = lambda n: jnp.zeros((w.shape[0], n), w.dtype)
    kr_placed = jnp.concatenate([z(B_NOPE), kr, z(HEAD_PAD - B_NOPE - B_ROPE)], axis=1)
    kr_rot = jnp.concatenate([z(B_NOPE), -kr[:, half:], kr[:, :half], z(HEAD_PAD - B_NOPE - B_ROPE)], axis=1)
    return jnp.concatenate([w[:, :1920], w[:, 1952:4000], kr_placed, kr_rot], axis=1).astype(BF16)


def _pack_w_uq(w):
    half = B_ROPE // 2
    r = w.shape[0]
    nope, x1, x2 = w[..., :B_NOPE], w[..., B_NOPE:B_NOPE + half], w[..., B_NOPE + half:]
    zpad = jnp.zeros((r, B_HEADS, HEAD_PAD - B_NOPE - B_ROPE), w.dtype)
    main = jnp.concatenate([nope, x1, x2, zpad], axis=-1)
    rot = jnp.concatenate([jnp.zeros_like(nope), -x2, x1, zpad], axis=-1)
    return (main.reshape(r, B_HEADS * HEAD_PAD).astype(BF16), rot.reshape(r, B_HEADS * HEAD_PAD).astype(BF16))


def _pack_w_uk(w):
    r = w.shape[0]
    zpad = jnp.zeros((r, B_HEADS, HEAD_PAD - B_NOPE), w.dtype)
    return jnp.concatenate([w, zpad], axis=-1).reshape(r, B_HEADS * HEAD_PAD).astype(BF16)


def _rope_tables(seq):
    half = B_ROPE // 2
    pos = jnp.arange(seq, dtype=F32)
    inv_freq = ROPE_THETA ** (-jnp.arange(0, B_ROPE, 2, dtype=F32) / B_ROPE)
    ang = pos[:, None] * inv_freq[None, :]
    cos, sin = jnp.cos(ang), jnp.sin(ang)
    one = jnp.ones((seq, B_NOPE), F32)
    zero = jnp.zeros((seq, B_NOPE), F32)
    zpad = jnp.zeros((seq, HEAD_PAD - B_NOPE - B_ROPE), F32)
    qs = ((B_NOPE + B_ROPE) ** -0.5) * LOG2E
    cq = jnp.concatenate([one, cos, cos, zpad], axis=1) * qs
    sq = jnp.concatenate([zero, sin, sin, zpad], axis=1) * qs
    ck = jnp.concatenate([zero, cos, cos, zpad], axis=1)
    sk = jnp.concatenate([zero, sin, sin, zpad], axis=1)
    return cq, sq, ck, sk


def kernel(x, norm1_g, w_in, q_a_norm_g, kv_a_norm_g, w_uq, w_uk, w_uv, w_a_out, w_b_out, w_o,
           norm2_g, w_router, b_router, w_gate_up, b_gate_up, w_down, b_down, final_g):
    B, S, D = x.shape
    T = B * S
    depth = norm1_g.shape[0]
    xt = x.reshape(T, D)
    cq_t, sq_t, ck_t, sk_t = _rope_tables(S)
    biases = [_dilated_bias(d) for _, d in A_PATTERNS]
    bm = 512
    for l in range(depth):
        wq1, wq2 = _pack_w_uq(w_uq[l])
        qa, ka, va, qb, kb, vb, gates = _inproj(
            xt, norm1_g[l][None], _pack_w_in(w_in[l]), q_a_norm_g[l][None], kv_a_norm_g[l][None],
            wq1, wq2, _pack_w_uk(w_uk[l]), w_uv[l].reshape(KV_LORA, B_WIDTH).T.astype(BF16),
            cq_t, sq_t, ck_t, sk_t, S)
        outs = [_dilated_pattern(qa, ka, va, biases[p], B, S, A_PATTERNS[p][1]) for p in range(3)]
        ob = _mla_attention(qb.reshape(B, S, -1), kb.reshape(B, S, -1), vb, B, S)

        wr = jnp.pad(w_router[l], ((0, 0), (0, LOGIT_PAD - N_EXPERTS)))
        wrh = wr.astype(BF16)
        wrl = (wr - wrh.astype(F32)).astype(BF16)
        br = jnp.pad(b_router[l], (0, LOGIT_PAD - N_EXPERTS))[None]
        x1, h2, logits = _outproj(
            outs[0][0], outs[1][0], outs[2][0], outs[0][1], outs[1][1], outs[2][1],
            ob.reshape(T, B_WIDTH), gates, xt,
            w_a_out[l].astype(BF16), w_b_out[l].astype(BF16), w_o[l].astype(BF16),
            norm2_g[l][None], wrh, wrl, br)

        logits = logits[:, :N_EXPERTS]
        top_v, top_i = lax.top_k(logits, TOP_K)
        gate_w = jax.nn.softmax(top_v, axis=-1)
        sel = jnp.sum((top_i[:, :, None] == jnp.arange(N_EXPERTS)[None, None, :]).astype(jnp.int32), axis=1)
        incl = jnp.cumsum(sel, axis=0)
        counts = incl[-1]
        rank = jnp.take_along_axis(incl - sel, top_i, axis=1)
        padded = (counts + bm - 1) // bm * bm
        pend = jnp.cumsum(padded)
        pstart = pend - padded
        dest = pstart[top_i] + rank
        NP = T * TOP_K + N_EXPERTS * bm
        nblk = NP // bm
        tok = jnp.broadcast_to(jnp.arange(T, dtype=jnp.int32)[:, None], (T, TOP_K))
        row_tok = jnp.zeros((NP,), jnp.int32).at[dest.reshape(-1)].set(tok.reshape(-1))
        blk_start = jnp.arange(nblk, dtype=jnp.int32) * bm
        blk_e = jnp.minimum(jnp.sum(pend[None, :] <= blk_start[:, None], axis=1), N_EXPERTS - 1).astype(jnp.int32)
        n_used = (pend[-1] // bm).astype(jnp.int32)[None]

        xb = jnp.take(h2, row_tok, axis=0)
        yb = _moe_experts(xb, blk_e, n_used, w_gate_up[l].astype(BF16), b_gate_up[l][:, None, :],
                          w_down[l].astype(BF16), b_down[l][:, None, :], bm)
        yk = jnp.take(yb, dest.T.reshape(-1), axis=0).reshape(TOP_K, T, D)
        if l + 1 < depth:
            raise NotImplementedError("depth > 1")
        out = _final(x1, yk, gate_w, final_g[None])
    return out.reshape(B, S, D)
```

```python
import functools
import math

import jax
import jax.numpy as jnp
from jax import lax
from jax.experimental import pallas as pl
from jax.experimental.pallas import tpu as pltpu

F32 = jnp.float32
BF16 = jnp.bfloat16

D_MODEL = 1024
EPS = 1e-5
NEG = -1e30

A_HEADS = 8
A_HEAD_DIM = 64
A_WIDTH = 512
A_PATTERNS = ((128, 1), (512, 4), (2048, 16))
A_HALF = 64
A_QB = 128
A_WIN = A_QB + 2 * A_HALF
assert all(w // (2 * d) == A_HALF for w, d in A_PATTERNS) and [d for _, d in A_PATTERNS] == [1, 4, 16]

B_HEADS = 8
B_NOPE = 64
B_ROPE = 32
B_V = 64
B_WIDTH = 512
Q_LORA = 256
KV_LORA = 128
ROPE_THETA = 10000.0
HEAD_PAD = 128
ONES_ROWS = 16
MLA_SCORE_BUFS = 4

N_EXPERTS = 32
TOP_K = 4
SWIGLU_LIMIT = 7.0
SWIGLU_ALPHA = 1.702
LOGIT_PAD = 128

LOG2E = 1.4426950408889634
LN2 = 0.6931471805599453

VMEM_LIMIT = 56 * 1024 * 1024


def _cparams(*sem):
    return pltpu.CompilerParams(dimension_semantics=sem, vmem_limit_bytes=VMEM_LIMIT)


def _inproj_kernel(x_ref, g1_ref, wp_ref, gq_ref, gkv_ref, wq1_ref, wq2_ref, wuk_ref, wuv_ref,
                   cq_ref, sq_ref, ck_ref, sk_ref,
                   qa_ref, ka_ref, va_ref, qa4_ref, ka4_ref, va4_ref, qa16_ref, ka16_ref, va16_ref,
                   qb_ref, kb_ref, vb_ref, gate_ref, perm_ref):
    tm = x_ref.shape[0]
    x = x_ref[...]
    h = x * lax.rsqrt(jnp.mean(x * x, axis=-1, keepdims=True) + EPS) * g1_ref[...]
    h = h.astype(BF16)

    def proj(c0, c1):
        return jnp.dot(h, wp_ref[:, c0:c1], preferred_element_type=F32)

    def emit(which, z, nat_ref, views):
        nat_ref[...] = z.astype(BF16)
        for c in range(A_WIDTH // 128):
            perm_ref[which, c] = z[:, c * 128:(c + 1) * 128]
        for d, ref in views:
            for r in range(d):
                for c in range(A_WIDTH // 128):
                    ref[:, r * A_WIDTH + c * 128:r * A_WIDTH + (c + 1) * 128] = (
                        perm_ref[which, c, pl.ds(r, tm // d, stride=d), :].astype(BF16))

    emit(0, proj(0, 512) * (A_HEAD_DIM ** -0.5), qa_ref, ((4, qa4_ref), (16, qa16_ref)))
    emit(1, proj(512, 1024), ka_ref, ((4, ka4_ref), (16, ka16_ref)))
    emit(2, proj(1024, 1536), va_ref, ((4, va4_ref), (16, va16_ref)))

    cq = proj(1536, 1792)
    cq = cq * lax.rsqrt(jnp.mean(cq * cq, axis=-1, keepdims=True) + EPS) * gq_ref[...]
    cq = cq.astype(BF16)
    q_main = jnp.dot(cq, wq1_ref[...], preferred_element_type=F32)
    q_rot = jnp.dot(cq, wq2_ref[...], preferred_element_type=F32)
    cq_t = cq_ref[...]
    sq_t = sq_ref[...]
    for hd in range(B_HEADS):
        sl = slice(hd * HEAD_PAD, (hd + 1) * HEAD_PAD)
        qb_ref[:, sl] = (q_main[:, sl] * cq_t + q_rot[:, sl] * sq_t).astype(BF16)

    ckv = proj(1792, 1920)
    ckv = ckv * lax.rsqrt(jnp.mean(ckv * ckv, axis=-1, keepdims=True) + EPS) * gkv_ref[...]
    ckv = ckv.astype(BF16)
    k_nope = jnp.dot(ckv, wuk_ref[...], preferred_element_type=F32)
    vb_ref[...] = lax.dot_general(wuv_ref[...], ckv, (((1,), (1,)), ((), ())),
                                  preferred_element_type=F32).astype(BF16)
    k_rope = proj(3968, 4096) * ck_ref[...] + proj(4096, 4224) * sk_ref[...]
    for hd in range(B_HEADS):
        sl = slice(hd * HEAD_PAD, (hd + 1) * HEAD_PAD)
        kb_ref[:, sl] = (k_nope[:, sl] + k_rope).astype(BF16)

    gate_ref[...] = jax.nn.sigmoid(proj(1920, 3968)).astype(BF16)


def _inproj(x2, g1, wp, gq, gkv, wq1, wq2, wuk, wuv, cq_t, sq_t, ck_t, sk_t, seq, tm=512):
    T = x2.shape[0]
    nseq = seq // tm
    row = lambda i: (i, 0)
    fix = lambda i: (0, 0)
    tab = lambda i: (i % nseq, 0)
    full = lambda a: pl.BlockSpec(a.shape, fix)
    sds = lambda r, c: jax.ShapeDtypeStruct((r, c), BF16)
    out_shape = (
        sds(T, A_WIDTH), sds(T, A_WIDTH), sds(T, A_WIDTH),
        sds(T // 4, 4 * A_WIDTH), sds(T // 4, 4 * A_WIDTH), sds(T // 4, 4 * A_WIDTH),
        sds(T // 16, 16 * A_WIDTH), sds(T // 16, 16 * A_WIDTH), sds(T // 16, 16 * A_WIDTH),
        sds(T, B_HEADS * HEAD_PAD), sds(T, B_HEADS * HEAD_PAD),
        sds(B_WIDTH, T),
        sds(T, 2 * D_MODEL),
    )
    out_specs = [pl.BlockSpec((tm * s.shape[0] // T, s.shape[1]), row) for s in out_shape]
    out_specs[11] = pl.BlockSpec((B_WIDTH, tm), lambda i: (0, i))
    return pl.pallas_call(
        _inproj_kernel,
        grid=(T // tm,),
        in_specs=[pl.BlockSpec((tm, D_MODEL), row), full(g1), full(wp), full(gq), full(gkv),
                  full(wq1), full(wq2), full(wuk), full(wuv),
                  pl.BlockSpec((tm, HEAD_PAD), tab), pl.BlockSpec((tm, HEAD_PAD), tab),
                  pl.BlockSpec((tm, HEAD_PAD), tab), pl.BlockSpec((tm, HEAD_PAD), tab)],
        out_specs=out_specs,
        out_shape=out_shape,
        scratch_shapes=[pltpu.VMEM((3, A_WIDTH // 128, tm, 128), F32)],
        compiler_params=_cparams("parallel"),
        name="inproj",
    )(x2, g1, wp, gq, gkv, wq1, wq2, wuk, wuv, cq_t, sq_t, ck_t, sk_t)


def _dilated_kernel(q_ref, kc_ref, kp_ref, kn_ref, vc_ref, vp_ref, vn_ref, bias_ref,
                    o_ref, lse_ref, kbuf, vbuf, *, rows):
    i = pl.program_id(2)
    last = pl.num_programs(2) - 1
    kbuf[0:A_HALF, :] = kp_ref[0]
    kbuf[A_HALF:A_HALF + rows, :] = kc_ref[0]
    kbuf[A_HALF + rows:, :] = kn_ref[0]
    vbuf[0:A_HALF, :] = vp_ref[0]
    vbuf[A_HALF:A_HALF + rows, :] = vc_ref[0]
    vbuf[A_HALF + rows:, :] = vn_ref[0]

    nj = rows // A_QB
    npair = A_HEADS // 2
    lane = lax.broadcasted_iota(jnp.int32, (A_QB, 128), 1)
    lo_half = lane < A_HEAD_DIM

    def body(j, carry):
        r0 = pl.multiple_of(j * A_QB, A_QB)
        variant = (jnp.logical_and(i == 0, j == 0).astype(jnp.int32)
                   + 2 * jnp.logical_and(i == last, j == nj - 1).astype(jnp.int32))
        scores = []
        for pr in range(npair):
            ls = slice(pr * 128, (pr + 1) * 128)
            qp = q_ref[0, pl.ds(r0, A_QB), ls]
            zero = jnp.zeros_like(qp)
            q2 = jnp.concatenate([jnp.where(lo_half, qp, zero), jnp.where(lo_half, zero, qp)], axis=0)
            s = lax.dot_general(q2, kbuf[pl.ds(r0, A_WIN), ls], (((1,), (1,)), ((), ())),
                                preferred_element_type=F32)
            scores.append(s + bias_ref[variant * npair + pr])
        probs = []
        for s in scores:
            m = jnp.max(s, axis=-1, keepdims=True)
            p = jnp.exp(s - m)
            l = jnp.sum(p, axis=-1, keepdims=True)
            probs.append((p.astype(BF16), l, m + jnp.log(l)))
        for pr, (p, l, lse) in enumerate(probs):
            ls = slice(pr * 128, (pr + 1) * 128)
            o = jnp.dot(p, vbuf[pl.ds(r0, A_WIN), ls], preferred_element_type=F32) / l
            lse = jnp.broadcast_to(lse, (2 * A_QB, 128))
            o_ref[0, pl.ds(r0, A_QB), ls] = jnp.where(lo_half, o[:A_QB], o[A_QB:])
            lse_ref[0, pl.ds(r0, A_QB), ls] = jnp.where(lo_half, lse[:A_QB], lse[A_QB:])
        return carry

    lax.fori_loop(0, nj, body, 0)


def _dilated_pattern(qa, ka, va, bias, batch, seq, dil, rows=1024):
    L = seq // dil
    rows = min(rows, L)
    nblk = L // rows
    hb = rows // A_HALF
    nh = L // A_HALF
    view = lambda a: a.reshape(batch, L, dil * A_WIDTH)
    cur = lambda b, r, i: (b, i, r)
    prev = lambda b, r, i: (b, jnp.maximum(i * hb - 1, 0), r)
    nxt = lambda b, r, i: (b, jnp.minimum((i + 1) * hb, nh - 1), r)
    cur_spec = pl.BlockSpec((1, rows, A_WIDTH), cur)
    out_sds = jax.ShapeDtypeStruct((batch, L, dil * A_WIDTH), F32)
    o, lse = pl.pallas_call(
        functools.partial(_dilated_kernel, rows=rows),
        grid=(batch, dil, nblk),
        in_specs=[cur_spec, cur_spec,
                  pl.BlockSpec((1, A_HALF, A_WIDTH), prev), pl.BlockSpec((1, A_HALF, A_WIDTH), nxt),
                  cur_spec,
                  pl.BlockSpec((1, A_HALF, A_WIDTH), prev), pl.BlockSpec((1, A_HALF, A_WIDTH), nxt),
                  pl.BlockSpec(bias.shape, lambda b, r, i: (0, 0, 0))],
        out_specs=[cur_spec, cur_spec],
        out_shape=(out_sds, out_sds),
        scratch_shapes=[pltpu.VMEM((rows + 2 * A_HALF, A_WIDTH), BF16),
                        pltpu.VMEM((rows + 2 * A_HALF, A_WIDTH), BF16)],
        compiler_params=_cparams("parallel", "parallel", "parallel"),
        name=f"dilated_d{dil}",
    )(view(qa), view(ka), view(ka), view(ka), view(va), view(va), view(va), bias)
    return o.reshape(batch * L, dil * A_WIDTH), lse.reshape(batch * L, dil * A_WIDTH)


def _dilated_bias(dil):
    slopes = 2.0 ** (-8.0 * (jnp.arange(A_HEADS, dtype=F32) + 1.0) / A_HEADS)
    col = jnp.arange(A_WIN)[None, :]
    rel = (col - A_HALF) - jnp.arange(A_QB)[:, None]
    dist = (jnp.abs(rel) * dil).astype(F32)
    bias = -slopes[:, None, None] * dist[None]
    bias = jnp.where((jnp.abs(rel) <= A_HALF)[None], bias, NEG)
    variants = []
    for v in range(4):
        ok = jnp.ones((1, A_WIN), bool)
        if v & 1:
            ok = jnp.logical_and(ok, col >= A_HALF)
        if v & 2:
            ok = jnp.logical_and(ok, col < A_HALF + A_QB)
        variants.append(jnp.where(ok[None], bias, NEG).reshape(A_HEADS // 2, 2 * A_QB, A_WIN))
    return jnp.concatenate(variants, axis=0)


def _mla_kernel(q_ref, k_ref, vt_ref, o_ref, *s_bufs, tk):
    tq = q_ref.shape[1]
    nk = k_ref.shape[1] // tk
    qs = [q_ref[0, :, hh * HEAD_PAD:(hh + 1) * HEAD_PAD] for hh in range(2)]

    def scores(j, s_ref):
        off = pl.multiple_of(j * tk, tk)
        for hh in range(2):
            s_ref[hh] = lax.dot_general(k_ref[0, pl.ds(off, tk), hh * HEAD_PAD:(hh + 1) * HEAD_PAD], qs[hh],
                                        (((1,), (1,)), ((), ())), preferred_element_type=F32)

    ones = jnp.ones((ONES_ROWS, tk), BF16)

    def softmax_pv(j, s_ref, stats):
        off = pl.multiple_of(j * tk, tk)
        new = []
        for hh in range(2):
            m, acc = stats[hh]
            s = s_ref[hh]
            vt = jnp.concatenate([vt_ref[hh * B_V:(hh + 1) * B_V, pl.ds(off, tk)], ones], axis=0)
            m_new = jnp.maximum(m, jnp.max(s, axis=0, keepdims=True))
            alpha = jnp.exp2(m - m_new)
            p = jnp.exp2(s - m_new)
            acc = alpha * acc + jnp.dot(vt, p.astype(BF16), preferred_element_type=F32)
            new.append((m_new, acc))
        return tuple(new)

    nbuf = len(s_bufs)

    def body(jj, stats):
        j = nbuf * jj
        for u in range(nbuf):
            scores(jnp.minimum(j + u + 1, nk - 1), s_bufs[(u + 1) % nbuf])
            stats = softmax_pv(j + u, s_bufs[u], stats)
        return stats

    init = (jnp.full((1, tq), NEG, F32), jnp.zeros((B_V + ONES_ROWS, tq), F32))
    scores(0, s_bufs[0])
    (_, acc0), (_, acc1) = lax.fori_loop(0, nk // nbuf, body, (init, init))
    o_t = jnp.concatenate([a[:B_V] / a[B_V:B_V + 1] for a in (acc0, acc1)], axis=0)
    o_ref[0] = o_t.T.astype(BF16)


def _mla_attention(qb, kb, vbt, batch, seq, tq=256, tk=512):
    return pl.pallas_call(
        functools.partial(_mla_kernel, tk=tk),
        grid=(batch, B_HEADS // 2, seq // tq),
        in_specs=[pl.BlockSpec((1, tq, 2 * HEAD_PAD), lambda b, p, i: (b, i, p)),
                  pl.BlockSpec((1, seq, 2 * HEAD_PAD), lambda b, p, i: (b, 0, p)),
                  pl.BlockSpec((2 * B_V, seq), lambda b, p, i: (p, b))],
        out_specs=pl.BlockSpec((1, tq, 128), lambda b, p, i: (b, i, p)),
        out_shape=jax.ShapeDtypeStruct((batch, seq, B_WIDTH), BF16),
        scratch_shapes=[pltpu.VMEM((2, tk, tq), F32) for _ in range(MLA_SCORE_BUFS)],
        compiler_params=_cparams("parallel", "parallel", "parallel"),
        name="mla_attention",
    )(qb, kb, vbt)


def _outproj_kernel(o1_ref, o2_ref, o3_ref, l1_ref, l2_ref, l3_ref, ob_ref, gate_ref, x_ref,
                    wa_ref, wb_ref, wo_ref, g2_ref, wrh_ref, wrl_ref, br_ref,
                    x1_ref, h2_ref, logit_ref, perm_ref):
    tm = x_ref.shape[0]

    def token_order(which, ref, d):
        for r in range(d):
            for c in range(A_WIDTH // 128):
                perm_ref[which, c, pl.ds(r, tm // d, stride=d), :] = (
                    ref[:, r * A_WIDTH + c * 128:r * A_WIDTH + (c + 1) * 128])
        return jnp.concatenate([perm_ref[which, c] for c in range(A_WIDTH // 128)], axis=1)

    l1 = l1_ref[...]
    l2, l3 = token_order(0, l2_ref, 4), token_order(1, l3_ref, 16)
    o2, o3 = token_order(2, o2_ref, 4), token_order(3, o3_ref, 16)
    mx = jnp.maximum(jnp.maximum(l1, l2), l3)
    e1, e2, e3 = jnp.exp(l1 - mx), jnp.exp(l2 - mx), jnp.exp(l3 - mx)
    oa = (e1 * o1_ref[...] + e2 * o2 + e3 * o3) / (e1 + e2 + e3)
    ya = jnp.dot(oa.astype(BF16), wa_ref[...], preferred_element_type=F32)
    yb = jnp.dot(ob_ref[...], wb_ref[...], preferred_element_type=F32)
    merged = gate_ref[:, :D_MODEL].astype(F32) * ya + gate_ref[:, D_MODEL:].astype(F32) * yb
    x1 = x_ref[...] + jnp.dot(merged.astype(BF16), wo_ref[...], preferred_element_type=F32)
    x1_ref[...] = x1
    h2 = x1 * lax.rsqrt(jnp.mean(x1 * x1, axis=-1, keepdims=True) + EPS) * g2_ref[...]
    hi = h2.astype(BF16)
    h2_ref[...] = hi
    lo = (h2 - hi.astype(F32)).astype(BF16)
    logits = jnp.dot(hi, wrh_ref[...], preferred_element_type=F32)
    logits = logits + jnp.dot(lo, wrh_ref[...], preferred_element_type=F32)
    logits = logits + jnp.dot(hi, wrl_ref[...], preferred_element_type=F32)
    logit_ref[...] = logits + br_ref[...]


def _outproj(o1, o2, o3, l1, l2, l3, ob, gates, x2, wa, wb, wo, g2, wrh, wrl, br, tm=512):
    T = x2.shape[0]
    row = lambda i: (i, 0)
    full = lambda a: pl.BlockSpec(a.shape, lambda i: (0, 0))
    rows = lambda a: pl.BlockSpec((tm * a.shape[0] // T, a.shape[1]), row)
    out_shape = (jax.ShapeDtypeStruct((T, D_MODEL), F32),
                 jax.ShapeDtypeStruct((T, D_MODEL), BF16),
                 jax.ShapeDtypeStruct((T, LOGIT_PAD), F32))
    return pl.pallas_call(
        _outproj_kernel,
        grid=(T // tm,),
        in_specs=[rows(o1), rows(o2), rows(o3), rows(l1), rows(l2), rows(l3), rows(ob), rows(gates),
                  rows(x2), full(wa), full(wb), full(wo), full(g2), full(wrh), full(wrl), full(br)],
        out_specs=[pl.BlockSpec((tm, s.shape[1]), row) for s in out_shape],
        out_shape=out_shape,
        scratch_shapes=[pltpu.VMEM((4, A_WIDTH // 128, tm, 128), F32)],
        compiler_params=_cparams("parallel"),
        name="outproj",
    )(o1, o2, o3, l1, l2, l3, ob, gates, x2, wa, wb, wo, g2, wrh, wrl, br)


def _moe_kernel(be_ref, nu_ref, x_ref, wgu_ref, bgu_ref, wd_ref, bd_ref, y_ref):
    @pl.when(pl.program_id(0) >= nu_ref[0])
    def _():
        y_ref[...] = jnp.zeros_like(y_ref)

    @pl.when(pl.program_id(0) < nu_ref[0])
    def _():
        gu = jnp.dot(x_ref[...], wgu_ref[0], preferred_element_type=F32) + bgu_ref[0]
        gate = jnp.minimum(gu[:, :D_MODEL], SWIGLU_LIMIT)
        up = jnp.clip(gu[:, D_MODEL:], -SWIGLU_LIMIT, SWIGLU_LIMIT)
        act = (up + 1.0) * (gate * jax.nn.sigmoid(SWIGLU_ALPHA * gate))
        y = jnp.dot(act.astype(BF16), wd_ref[0], preferred_element_type=F32) + bd_ref[0]
        y_ref[...] = y.astype(BF16)


def _moe_experts(xb, blk_e, n_used, wgu, bgu, wd, bd, bm):
    NP = xb.shape[0]
    nblk = NP // bm
    grid_spec = pltpu.PrefetchScalarGridSpec(
        num_scalar_prefetch=2,
        grid=(nblk,),
        in_specs=[pl.BlockSpec((bm, D_MODEL), lambda i, be, nu: (jnp.minimum(i, nu[0] - 1), 0)),
                  pl.BlockSpec((1, D_MODEL, 2 * D_MODEL), lambda i, be, nu: (be[i], 0, 0)),
                  pl.BlockSpec((1, 1, 2 * D_MODEL), lambda i, be, nu: (be[i], 0, 0)),
                  pl.BlockSpec((1, D_MODEL, D_MODEL), lambda i, be, nu: (be[i], 0, 0)),
                  pl.BlockSpec((1, 1, D_MODEL), lambda i, be, nu: (be[i], 0, 0))],
        out_specs=pl.BlockSpec((bm, D_MODEL), lambda i, be, nu: (i, 0)),
    )
    return pl.pallas_call(
        _moe_kernel,
        grid_spec=grid_spec,
        out_shape=jax.ShapeDtypeStruct((NP, D_MODEL), BF16),
        compiler_params=_cparams("arbitrary"),
        name="moe_experts",
    )(blk_e, n_used, xb, wgu, bgu, wd, bd)


def _final_kernel(x1_ref, y_ref, w_ref, g_ref, o_ref):
    w = w_ref[...]
    acc = x1_ref[...]
    for k in range(TOP_K):
        acc = acc + w[:, k:k + 1] * y_ref[k].astype(F32)
    o_ref[...] = acc * lax.rsqrt(jnp.mean(acc * acc, axis=-1, keepdims=True) + EPS) * g_ref[...]


def _final(x1, yk, wk, g, tm=512):
    T = x1.shape[0]
    return pl.pallas_call(
        _final_kernel,
        grid=(T // tm,),
        in_specs=[pl.BlockSpec((tm, D_MODEL), lambda i: (i, 0)),
                  pl.BlockSpec((TOP_K, tm, D_MODEL), lambda i: (0, i, 0)),
                  pl.BlockSpec((tm, TOP_K), lambda i: (i, 0)),
                  pl.BlockSpec((1, D_MODEL), lambda i: (0, 0))],
        out_specs=pl.BlockSpec((tm, D_MODEL), lambda i: (i, 0)),
        out_shape=jax.ShapeDtypeStruct((T, D_MODEL), F32),
        compiler_params=_cparams("parallel"),
        name="final_norm",
    )(x1, yk, wk, g)


def _pack_w_in(w):
    kr = w[:, 1920:1952]
    half = B_ROPE // 2
    z = lambda n: jnp.zeros((w.shape[0], n), w.dtype)
    kr_placed = jnp.concatenate([z(B_NOPE), kr, z(HEAD_PAD - B_NOPE - B_ROPE)], axis=1)
    kr_rot = jnp.concatenate([z(B_NOPE), -kr[:, half:], kr[:, :half], z(HEAD_PAD - B_NOPE - B_ROPE)], axis=1)
    return jnp.concatenate([w[:, :1920], w[:, 1952:4000], kr_placed, kr_rot], axis=1).astype(BF16)


def _pack_w_uq(w):
    half = B_ROPE // 2
    r = w.shape[0]
    nope, x1, x2 = w[..., :B_NOPE], w[..., B_NOPE:B_NOPE + half], w[..., B_NOPE + half:]
    zpad = jnp.zeros((r, B_HEADS, HEAD_PAD - B_NOPE - B_ROPE), w.dtype)
    main = jnp.concatenate([nope, x1, x2, zpad], axis=-1)
    rot = jnp.concatenate([jnp.zeros_like(nope), -x2, x1, zpad], axis=-1)
    return (main.reshape(r, B_HEADS * HEAD_PAD).astype(BF16), rot.reshape(r, B_HEADS * HEAD_PAD).astype(BF16))


def _pack_w_uk(w):
    r = w.shape[0]
    zpad = jnp.zeros((r, B_HEADS, HEAD_PAD - B_NOPE), w.dtype)
    return jnp.concatenate([w, zpad], axis=-1).reshape(r, B_HEADS * HEAD_PAD).astype(BF16)


def _rope_tables(seq):
    half = B_ROPE // 2
    pos = jnp.arange(seq, dtype=F32)
    inv_freq = ROPE_THETA ** (-jnp.arange(0, B_ROPE, 2, dtype=F32) / B_ROPE)
    ang = pos[:, None] * inv_freq[None, :]
    cos, sin = jnp.cos(ang), jnp.sin(ang)
    one = jnp.ones((seq, B_NOPE), F32)
    zero = jnp.zeros((seq, B_NOPE), F32)
    zpad = jnp.zeros((seq, HEAD_PAD - B_NOPE - B_ROPE), F32)
    qs = ((B_NOPE + B_ROPE) ** -0.5) * LOG2E
    cq = jnp.concatenate([one, cos, cos, zpad], axis=1) * qs
    sq = jnp.concatenate([zero, sin, sin, zpad], axis=1) * qs
    ck = jnp.concatenate([zero, cos, cos, zpad], axis=1)
    sk = jnp.concatenate([zero, sin, sin, zpad], axis=1)
    return cq, sq, ck, sk


def kernel(x, norm1_g, w_in, q_a_norm_g, kv_a_norm_g, w_uq, w_uk, w_uv, w_a_out, w_b_out, w_o,
           norm2_g, w_router, b_router, w_gate_up, b_gate_up, w_down, b_down, final_g):
    B, S, D = x.shape
    T = B * S
    depth = norm1_g.shape[0]
    xt = x.reshape(T, D)
    cq_t, sq_t, ck_t, sk_t = _rope_tables(S)
    biases = [_dilated_bias(d) for _, d in A_PATTERNS]
    bm = 512
    for l in range(depth):
        wq1, wq2 = _pack_w_uq(w_uq[l])
        qa, ka, va, qa4, ka4, va4, qa16, ka16, va16, qb, kb, vb, gates = _inproj(
            xt, norm1_g[l][None], _pack_w_in(w_in[l]), q_a_norm_g[l][None], kv_a_norm_g[l][None],
            wq1, wq2, _pack_w_uk(w_uk[l]), w_uv[l].reshape(KV_LORA, B_WIDTH).T.astype(BF16),
            cq_t, sq_t, ck_t, sk_t, S)
        qkv = ((qa, ka, va), (qa4, ka4, va4), (qa16, ka16, va16))
        outs = [_dilated_pattern(*qkv[p], biases[p], B, S, A_PATTERNS[p][1]) for p in range(3)]
        ob = _mla_attention(qb.reshape(B, S, -1), kb.reshape(B, S, -1), vb, B, S)

        wr = jnp.pad(w_router[l], ((0, 0), (0, LOGIT_PAD - N_EXPERTS)))
        wrh = wr.astype(BF16)
        wrl = (wr - wrh.astype(F32)).astype(BF16)
        br = jnp.pad(b_router[l], (0, LOGIT_PAD - N_EXPERTS))[None]
        x1, h2, logits = _outproj(
            outs[0][0], outs[1][0], outs[2][0], outs[0][1], outs[1][1], outs[2][1],
            ob.reshape(T, B_WIDTH), gates, xt,
            w_a_out[l].astype(BF16), w_b_out[l].astype(BF16), w_o[l].astype(BF16),
            norm2_g[l][None], wrh, wrl, br)

        logits = logits[:, :N_EXPERTS]
        top_v, top_i = lax.top_k(logits, TOP_K)
        gate_w = jax.nn.softmax(top_v, axis=-1)
        sel = jnp.sum((top_i[:, :, None] == jnp.arange(N_EXPERTS)[None, None, :]).astype(jnp.int32), axis=1)
        incl = jnp.cumsum(sel, axis=0)
        counts = incl[-1]
        rank = jnp.take_along_axis(incl - sel, top_i, axis=1)
        padded = (counts + bm - 1) // bm * bm
        pend = jnp.cumsum(padded)
        pstart = pend - padded
        dest = pstart[top_i] + rank
        NP = T * TOP_K + N_EXPERTS * bm
        nblk = NP // bm
        tok = jnp.broadcast_to(jnp.arange(T, dtype=jnp.int32)[:, None], (T, TOP_K))
        row_tok = jnp.zeros((NP,), jnp.int32).at[dest.reshape(-1)].set(tok.reshape(-1))
        blk_start = jnp.arange(nblk, dtype=jnp.int32) * bm
        blk_e = jnp.minimum(jnp.sum(pend[None, :] <= blk_start[:, None], axis=1), N_EXPERTS - 1).astype(jnp.int32)
        n_used = (pend[-1] // bm).astype(jnp.int32)[None]

        xb = jnp.take(h2, row_tok, axis=0)
        yb = _moe_experts(xb, blk_e, n_used, w_gate_up[l].astype(BF16), b_gate_up[l][:, None, :],
                          w_down[l].astype(BF16), b_down[l][:, None, :], bm)
        yk = jnp.take(yb, dest.T.reshape(-1), axis=0).reshape(TOP_K, T, D)
        if l + 1 < depth:
            raise NotImplementedError("depth > 1")
        out = _final(x1, yk, gate_w, final_g[None])
    return out.reshape(B, S, D)
```

```python
import functools
import math

import jax
import jax.numpy as jnp
from jax import lax
from jax.experimental import pallas as pl
from jax.experimental.pallas import tpu as pltpu

F32 = jnp.float32
BF16 = jnp.bfloat16

D_MODEL = 1024
EPS = 1e-5
NEG = -1e30

A_HEADS = 8
A_HEAD_DIM = 64
A_WIDTH = 512
A_PATTERNS = ((128, 1), (512, 4), (2048, 16))
A_HALF = 64
A_QB = 128
A_WIN = A_QB + 2 * A_HALF
assert all(w // (2 * d) == A_HALF for w, d in A_PATTERNS) and [d for _, d in A_PATTERNS] == [1, 4, 16]

B_HEADS = 8
B_NOPE = 64
B_ROPE = 32
B_V = 64
B_WIDTH = 512
Q_LORA = 256
KV_LORA = 128
ROPE_THETA = 10000.0
HEAD_PAD = 128
ONES_ROWS = 16
MLA_SCORE_BUFS = 4

N_EXPERTS = 32
TOP_K = 4
SWIGLU_LIMIT = 7.0
SWIGLU_ALPHA = 1.702
LOGIT_PAD = 128

LOG2E = 1.4426950408889634
LN2 = 0.6931471805599453

VMEM_LIMIT = 56 * 1024 * 1024


def _cparams(*sem):
    return pltpu.CompilerParams(dimension_semantics=sem, vmem_limit_bytes=VMEM_LIMIT)


def _inproj_kernel(x_ref, g1_ref, wp_ref, gq_ref, gkv_ref, wq1_ref, wq2_ref, wuk_ref, wuv_ref,
                   cq_ref, sq_ref, ck_ref, sk_ref,
                   qa_ref, ka_ref, va_ref, qa4_ref, ka4_ref, va4_ref, qa16_ref, ka16_ref, va16_ref,
                   qb_ref, kb_ref, vb_ref, gate_ref, perm_ref):
    tm = x_ref.shape[0]
    x = x_ref[...]
    h = x * lax.rsqrt(jnp.mean(x * x, axis=-1, keepdims=True) + EPS) * g1_ref[...]
    h = h.astype(BF16)

    def proj(c0, c1):
        return jnp.dot(h, wp_ref[:, c0:c1], preferred_element_type=F32)

    def emit(which, z, nat_ref, views):
        nat_ref[...] = z.astype(BF16)
        for c in range(A_WIDTH // 128):
            perm_ref[which, c] = z[:, c * 128:(c + 1) * 128]
        for d, ref in views:
            for r in range(d):
                for c in range(A_WIDTH // 128):
                    ref[:, r * A_WIDTH + c * 128:r * A_WIDTH + (c + 1) * 128] = (
                        perm_ref[which, c, pl.ds(r, tm // d, stride=d), :].astype(BF16))

    emit(0, proj(0, 512) * (A_HEAD_DIM ** -0.5), qa_ref, ((4, qa4_ref), (16, qa16_ref)))
    emit(1, proj(512, 1024), ka_ref, ((4, ka4_ref), (16, ka16_ref)))
    emit(2, proj(1024, 1536), va_ref, ((4, va4_ref), (16, va16_ref)))

    cq = proj(1536, 1792)
    cq = cq * lax.rsqrt(jnp.mean(cq * cq, axis=-1, keepdims=True) + EPS) * gq_ref[...]
    cq = cq.astype(BF16)
    q_main = jnp.dot(cq, wq1_ref[...], preferred_element_type=F32)
    q_rot = jnp.dot(cq, wq2_ref[...], preferred_element_type=F32)
    cq_t = cq_ref[...]
    sq_t = sq_ref[...]
    for hd in range(B_HEADS):
        sl = slice(hd * HEAD_PAD, (hd + 1) * HEAD_PAD)
        qb_ref[:, sl] = (q_main[:, sl] * cq_t + q_rot[:, sl] * sq_t).astype(BF16)

    ckv = proj(1792, 1920)
    ckv = ckv * lax.rsqrt(jnp.mean(ckv * ckv, axis=-1, keepdims=True) + EPS) * gkv_ref[...]
    ckv = ckv.astype(BF16)
    k_nope = jnp.dot(ckv, wuk_ref[...], preferred_element_type=F32)
    vb_ref[...] = lax.dot_general(wuv_ref[...], ckv, (((1,), (1,)), ((), ())),
                                  preferred_element_type=F32).astype(BF16)
    k_rope = proj(3968, 4096) * ck_ref[...] + proj(4096, 4224) * sk_ref[...]
    for hd in range(B_HEADS):
        sl = slice(hd * HEAD_PAD, (hd + 1) * HEAD_PAD)
        kb_ref[:, sl] = (k_nope[:, sl] + k_rope).astype(BF16)

    gate_ref[...] = jax.nn.sigmoid(proj(1920, 3968)).astype(BF16)


def _inproj(x2, g1, wp, gq, gkv, wq1, wq2, wuk, wuv, cq_t, sq_t, ck_t, sk_t, seq, tm=512):
    T = x2.shape[0]
    nseq = seq // tm
    row = lambda i: (i, 0)
    fix = lambda i: (0, 0)
    tab = lambda i: (i % nseq, 0)
    full = lambda a: pl.BlockSpec(a.shape, fix)
    sds = lambda r, c: jax.ShapeDtypeStruct((r, c), BF16)
    out_shape = (
        sds(T, A_WIDTH), sds(T, A_WIDTH), sds(T, A_WIDTH),
        sds(T // 4, 4 * A_WIDTH), sds(T // 4, 4 * A_WIDTH), sds(T // 4, 4 * A_WIDTH),
        sds(T // 16, 16 * A_WIDTH), sds(T // 16, 16 * A_WIDTH), sds(T // 16, 16 * A_WIDTH),
        sds(T, B_HEADS * HEAD_PAD), sds(T, B_HEADS * HEAD_PAD),
        sds(B_WIDTH, T),
        sds(T, 2 * D_MODEL),
    )
    out_specs = [pl.BlockSpec((tm * s.shape[0] // T, s.shape[1]), row) for s in out_shape]
    out_specs[11] = pl.BlockSpec((B_WIDTH, tm), lambda i: (0, i))
    return pl.pallas_call(
        _inproj_kernel,
        grid=(T // tm,),
        in_specs=[pl.BlockSpec((tm, D_MODEL), row), full(g1), full(wp), full(gq), full(gkv),
                  full(wq1), full(wq2), full(wuk), full(wuv),
                  pl.BlockSpec((tm, HEAD_PAD), tab), pl.BlockSpec((tm, HEAD_PAD), tab),
                  pl.BlockSpec((tm, HEAD_PAD), tab), pl.BlockSpec((tm, HEAD_PAD), tab)],
        out_specs=out_specs,
        out_shape=out_shape,
        scratch_shapes=[pltpu.VMEM((3, A_WIDTH // 128, tm, 128), F32)],
        compiler_params=_cparams("parallel"),
        name="inproj",
    )(x2, g1, wp, gq, gkv, wq1, wq2, wuk, wuv, cq_t, sq_t, ck_t, sk_t)


def _dilated_kernel(q_ref, kc_ref, kp_ref, kn_ref, vc_ref, vp_ref, vn_ref, bias_ref,
                    o_ref, lse_ref, kbuf, vbuf, *, rows):
    i = pl.program_id(2)
    last = pl.num_programs(2) - 1
    kbuf[0:A_HALF, :] = kp_ref[0]
    kbuf[A_HALF:A_HALF + rows, :] = kc_ref[0]
    kbuf[A_HALF + rows:, :] = kn_ref[0]
    vbuf[0:A_HALF, :] = vp_ref[0]
    vbuf[A_HALF:A_HALF + rows, :] = vc_ref[0]
    vbuf[A_HALF + rows:, :] = vn_ref[0]

    nj = rows // A_QB
    npair = A_HEADS // 2
    lane = lax.broadcasted_iota(jnp.int32, (A_QB, 128), 1)
    lo_half = lane < A_HEAD_DIM

    def body(j, carry):
        r0 = pl.multiple_of(j * A_QB, A_QB)
        variant = (jnp.logical_and(i == 0, j == 0).astype(jnp.int32)
                   + 2 * jnp.logical_and(i == last, j == nj - 1).astype(jnp.int32))
        scores = []
        for pr in range(npair):
            ls = slice(pr * 128, (pr + 1) * 128)
            qp = q_ref[0, pl.ds(r0, A_QB), ls]
            zero = jnp.zeros_like(qp)
            q2 = jnp.concatenate([jnp.where(lo_half, qp, zero), jnp.where(lo_half, zero, qp)], axis=0)
            s = lax.dot_general(q2, kbuf[pl.ds(r0, A_WIN), ls], (((1,), (1,)), ((), ())),
                                preferred_element_type=F32)
            scores.append(s + bias_ref[variant * npair + pr])
        probs = []
        for s in scores:
            m = jnp.max(s, axis=-1, keepdims=True)
            p = jnp.exp(s - m)
            l = jnp.sum(p, axis=-1, keepdims=True)
            probs.append((p.astype(BF16), l, m + jnp.log(l)))
        for pr, (p, l, lse) in enumerate(probs):
            ls = slice(pr * 128, (pr + 1) * 128)
            o = jnp.dot(p, vbuf[pl.ds(r0, A_WIN), ls], preferred_element_type=F32) / l
            lse = jnp.broadcast_to(lse, (2 * A_QB, 128))
            o_ref[0, pl.ds(r0, A_QB), ls] = jnp.where(lo_half, o[:A_QB], o[A_QB:])
            lse_ref[0, pl.ds(r0, A_QB), ls] = jnp.where(lo_half, lse[:A_QB], lse[A_QB:])
        return carry

    lax.fori_loop(0, nj, body, 0)


def _dilated_pattern(qa, ka, va, bias, batch, seq, dil, rows=1024):
    L = seq // dil
    rows = min(rows, L)
    nblk = L // rows
    hb = rows // A_HALF
    nh = L // A_HALF
    view = lambda a: a.reshape(batch, L, dil * A_WIDTH)
    cur = lambda b, r, i: (b, i, r)
    prev = lambda b, r, i: (b, jnp.maximum(i * hb - 1, 0), r)
    nxt = lambda b, r, i: (b, jnp.minimum((i + 1) * hb, nh - 1), r)
    cur_spec = pl.BlockSpec((1, rows, A_WIDTH), cur)
    out_sds = jax.ShapeDtypeStruct((batch, L, dil * A_WIDTH), F32)
    o, lse = pl.pallas_call(
        functools.partial(_dilated_kernel, rows=rows),
        grid=(batch, dil, nblk),
        in_specs=[cur_spec, cur_spec,
                  pl.BlockSpec((1, A_HALF, A_WIDTH), prev), pl.BlockSpec((1, A_HALF, A_WIDTH), nxt),
                  cur_spec,
                  pl.BlockSpec((1, A_HALF, A_WIDTH), prev), pl.BlockSpec((1, A_HALF, A_WIDTH), nxt),
                  pl.BlockSpec(bias.shape, lambda b, r, i: (0, 0, 0))],
        out_specs=[cur_spec, cur_spec],
        out_shape=(out_sds, out_sds),
        scratch_shapes=[pltpu.VMEM((rows + 2 * A_HALF, A_WIDTH), BF16),
                        pltpu.VMEM((rows + 2 * A_HALF, A_WIDTH), BF16)],
        compiler_params=_cparams("parallel", "parallel", "parallel"),
        name=f"dilated_d{dil}",
    )(view(qa), view(ka), view(ka), view(ka), view(va), view(va), view(va), bias)
    return o.reshape(batch * L, dil * A_WIDTH), lse.reshape(batch * L, dil * A_WIDTH)


def _dilated_bias(dil):
    slopes = 2.0 ** (-8.0 * (jnp.arange(A_HEADS, dtype=F32) + 1.0) / A_HEADS)
    col = jnp.arange(A_WIN)[None, :]
    rel = (col - A_HALF) - jnp.arange(A_QB)[:, None]
    dist = (jnp.abs(rel) * dil).astype(F32)
    bias = -slopes[:, None, None] * dist[None]
    bias = jnp.where((jnp.abs(rel) <= A_HALF)[None], bias, NEG)
    variants = []
    for v in range(4):
        ok = jnp.ones((1, A_WIN), bool)
        if v & 1:
            ok = jnp.logical_and(ok, col >= A_HALF)
        if v & 2:
            ok = jnp.logical_and(ok, col < A_HALF + A_QB)
        variants.append(jnp.where(ok[None], bias, NEG).reshape(A_HEADS // 2, 2 * A_QB, A_WIN))
    return jnp.concatenate(variants, axis=0)


def _mla_kernel(q_ref, k_ref, vt_ref, o_ref, *s_bufs, tk):
    tq = q_ref.shape[1]
    nk = k_ref.shape[1] // tk
    qs = [q_ref[0, :, hh * HEAD_PAD:(hh + 1) * HEAD_PAD] for hh in range(2)]

    def scores(j, s_ref):
        off = pl.multiple_of(j * tk, tk)
        for hh in range(2):
            s_ref[hh] = lax.dot_general(k_ref[0, pl.ds(off, tk), hh * HEAD_PAD:(hh + 1) * HEAD_PAD], qs[hh],
                                        (((1,), (1,)), ((), ())), preferred_element_type=F32)

    ones = jnp.ones((ONES_ROWS, tk), BF16)

    def softmax_pv(j, s_ref, stats):
        off = pl.multiple_of(j * tk, tk)
        new = []
        for hh in range(2):
            m, acc = stats[hh]
            s = s_ref[hh]
            vt = jnp.concatenate([vt_ref[hh * B_V:(hh + 1) * B_V, pl.ds(off, tk)], ones], axis=0)
            m_new = jnp.maximum(m, jnp.max(s, axis=0, keepdims=True))
            alpha = jnp.exp2(m - m_new)
            p = jnp.exp2(s - m_new)
            acc = alpha * acc + jnp.dot(vt, p.astype(BF16), preferred_element_type=F32)
            new.append((m_new, acc))
        return tuple(new)

    nbuf = len(s_bufs)

    def body(jj, stats):
        j = nbuf * jj
        for u in range(nbuf):
            scores(jnp.minimum(j + u + 1, nk - 1), s_bufs[(u + 1) % nbuf])
            stats = softmax_pv(j + u, s_bufs[u], stats)
        return stats

    init = (jnp.full((1, tq), NEG, F32), jnp.zeros((B_V + ONES_ROWS, tq), F32))
    scores(0, s_bufs[0])
    (_, acc0), (_, acc1) = lax.fori_loop(0, nk // nbuf, body, (init, init))
    o_t = jnp.concatenate([a[:B_V] / a[B_V:B_V + 1] for a in (acc0, acc1)], axis=0)
    o_ref[0] = o_t.T.astype(BF16)


def _mla_attention(qb, kb, vbt, batch, seq, tq=256, tk=512):
    return pl.pallas_call(
        functools.partial(_mla_kernel, tk=tk),
        grid=(batch, B_HEADS // 2, seq // tq),
        in_specs=[pl.BlockSpec((1, tq, 2 * HEAD_PAD), lambda b, p, i: (b, i, p)),
                  pl.BlockSpec((1, seq, 2 * HEAD_PAD), lambda b, p, i: (b, 0, p)),
                  pl.BlockSpec((2 * B_V, seq), lambda b, p, i: (p, b))],
        out_specs=pl.BlockSpec((1, tq, 128), lambda b, p, i: (b, i, p)),
        out_shape=jax.ShapeDtypeStruct((batch, seq, B_WIDTH), BF16),
        scratch_shapes=[pltpu.VMEM((2, tk, tq), F32) for _ in range(MLA_SCORE_BUFS)],
        compiler_params=_cparams("parallel", "parallel", "parallel"),
        name="mla_attention",
    )(qb, kb, vbt)


def _outproj_kernel(o1_ref, o2_ref, o3_ref, l1_ref, l2_ref, l3_ref, ob_ref, gate_ref, x_ref,
                    wa_ref, wb_ref, wo_ref, g2_ref, wrh_ref, wrl_ref, br_ref,
                    x1_ref, h2_ref, logit_ref, perm_ref):
    tm = x_ref.shape[0]

    def token_order(which, ref, d):
        for r in range(d):
            for c in range(A_WIDTH // 128):
                perm_ref[which, c, pl.ds(r, tm // d, stride=d), :] = (
                    ref[:, r * A_WIDTH + c * 128:r * A_WIDTH + (c + 1) * 128])
        return jnp.concatenate([perm_ref[which, c] for c in range(A_WIDTH // 128)], axis=1)

    l1 = l1_ref[...]
    l2, l3 = token_order(0, l2_ref, 4), token_order(1, l3_ref, 16)
    o2, o3 = token_order(2, o2_ref, 4), token_order(3, o3_ref, 16)
    mx = jnp.maximum(jnp.maximum(l1, l2), l3)
    e1, e2, e3 = jnp.exp(l1 - mx), jnp.exp(l2 - mx), jnp.exp(l3 - mx)
    oa = (e1 * o1_ref[...] + e2 * o2 + e3 * o3) / (e1 + e2 + e3)
    ya = jnp.dot(oa.astype(BF16), wa_ref[...], preferred_element_type=F32)
    yb = jnp.dot(ob_ref[...], wb_ref[...], preferred_element_type=F32)
    merged = gate_ref[:, :D_MODEL].astype(F32) * ya + gate_ref[:, D_MODEL:].astype(F32) * yb
    x1 = x_ref[...] + jnp.dot(merged.astype(BF16), wo_ref[...], preferred_element_type=F32)
    x1_ref[...] = x1
    h2 = x1 * lax.rsqrt(jnp.mean(x1 * x1, axis=-1, keepdims=True) + EPS) * g2_ref[...]
    hi = h2.astype(BF16)
    h2_ref[...] = hi
    lo = (h2 - hi.astype(F32)).astype(BF16)
    logits = jnp.dot(hi, wrh_ref[...], preferred_element_type=F32)
    logits = logits + jnp.dot(lo, wrh_ref[...], preferred_element_type=F32)
    logits = logits + jnp.dot(hi, wrl_ref[...], preferred_element_type=F32)
    logit_ref[...] = logits + br_ref[...]


def _outproj(o1, o2, o3, l1, l2, l3, ob, gates, x2, wa, wb, wo, g2, wrh, wrl, br, tm=512):
    T = x2.shape[0]
    row = lambda i: (i, 0)
    full = lambda a: pl.BlockSpec(a.shape, lambda i: (0, 0))
    rows = lambda a: pl.BlockSpec((tm * a.shape[0] // T, a.shape[1]), row)
    out_shape = (jax.ShapeDtypeStruct((T, D_MODEL), F32),
                 jax.ShapeDtypeStruct((T, D_MODEL), BF16),
                 jax.ShapeDtypeStruct((T, LOGIT_PAD), F32))
    return pl.pallas_call(
        _outproj_kernel,
        grid=(T // tm,),
        in_specs=[rows(o1), rows(o2), rows(o3), rows(l1), rows(l2), rows(l3), rows(ob), rows(gates),
                  rows(x2), full(wa), full(wb), full(wo), full(g2), full(wrh), full(wrl), full(br)],
        out_specs=[pl.BlockSpec((tm, s.shape[1]), row) for s in out_shape],
        out_shape=out_shape,
        scratch_shapes=[pltpu.VMEM((4, A_WIDTH // 128, tm, 128), F32)],
        compiler_params=_cparams("parallel"),
        name="outproj",
    )(o1, o2, o3, l1, l2, l3, ob, gates, x2, wa, wb, wo, g2, wrh, wrl, br)


def _moe_kernel(be_ref, nu_ref, x_ref, wgu_ref, bgu_ref, wd_ref, bd_ref, y_ref, wgu_bf, wd_bf):
    i = pl.program_id(0)

    @pl.when(i >= nu_ref[0])
    def _():
        y_ref[...] = jnp.zeros_like(y_ref)

    @pl.when(jnp.logical_or(i == 0, be_ref[i] != be_ref[jnp.maximum(i - 1, 0)]))
    def _():
        rows = 128

        def cast(c, carry):
            r0 = pl.multiple_of(c * rows, rows)
            wgu_bf[pl.ds(r0, rows), :] = wgu_ref[0, pl.ds(r0, rows), :].astype(BF16)
            wd_bf[pl.ds(r0, rows), :] = wd_ref[0, pl.ds(r0, rows), :].astype(BF16)
            return carry

        lax.fori_loop(0, D_MODEL // rows, cast, 0)

    @pl.when(i < nu_ref[0])
    def _():
        gu = jnp.dot(x_ref[...], wgu_bf[...], preferred_element_type=F32) + bgu_ref[0]
        gate = jnp.minimum(gu[:, :D_MODEL], SWIGLU_LIMIT)
        up = jnp.clip(gu[:, D_MODEL:], -SWIGLU_LIMIT, SWIGLU_LIMIT)
        act = (up + 1.0) * (gate * jax.nn.sigmoid(SWIGLU_ALPHA * gate))
        y = jnp.dot(act.astype(BF16), wd_bf[...], preferred_element_type=F32) + bd_ref[0]
        y_ref[...] = y.astype(BF16)


def _moe_experts(xb, blk_e, n_used, wgu, bgu, wd, bd, bm):
    NP = xb.shape[0]
    nblk = NP // bm
    grid_spec = pltpu.PrefetchScalarGridSpec(
        num_scalar_prefetch=2,
        grid=(nblk,),
        in_specs=[pl.BlockSpec((bm, D_MODEL), lambda i, be, nu: (jnp.minimum(i, nu[0] - 1), 0)),
                  pl.BlockSpec((1, D_MODEL, 2 * D_MODEL), lambda i, be, nu: (be[i], 0, 0)),
                  pl.BlockSpec((1, 1, 2 * D_MODEL), lambda i, be, nu: (be[i], 0, 0)),
                  pl.BlockSpec((1, D_MODEL, D_MODEL), lambda i, be, nu: (be[i], 0, 0)),
                  pl.BlockSpec((1, 1, D_MODEL), lambda i, be, nu: (be[i], 0, 0))],
        out_specs=pl.BlockSpec((bm, D_MODEL), lambda i, be, nu: (i, 0)),
        scratch_shapes=[pltpu.VMEM((D_MODEL, 2 * D_MODEL), BF16), pltpu.VMEM((D_MODEL, D_MODEL), BF16)],
    )
    return pl.pallas_call(
        _moe_kernel,
        grid_spec=grid_spec,
        out_shape=jax.ShapeDtypeStruct((NP, D_MODEL), BF16),
        compiler_params=_cparams("arbitrary"),
        name="moe_experts",
    )(blk_e, n_used, xb, wgu, bgu, wd, bd)


def _final_kernel(x1_ref, y_ref, w_ref, g_ref, o_ref):
    w = w_ref[...]
    acc = x1_ref[...]
    for k in range(TOP_K):
        acc = acc + w[:, k:k + 1] * y_ref[k].astype(F32)
    o_ref[...] = acc * lax.rsqrt(jnp.mean(acc * acc, axis=-1, keepdims=True) + EPS) * g_ref[...]


def _final(x1, yk, wk, g, tm=512):
    T = x1.shape[0]
    return pl.pallas_call(
        _final_kernel,
        grid=(T // tm,),
        in_specs=[pl.BlockSpec((tm, D_MODEL), lambda i: (i, 0)),
                  pl.BlockSpec((TOP_K, tm, D_MODEL), lambda i: (0, i, 0)),
                  pl.BlockSpec((tm, TOP_K), lambda i: (i, 0)),
                  pl.BlockSpec((1, D_MODEL), lambda i: (0, 0))],
        out_specs=pl.BlockSpec((tm, D_MODEL), lambda i: (i, 0)),
        out_shape=jax.ShapeDtypeStruct((T, D_MODEL), F32),
        compiler_params=_cparams("parallel"),
        name="final_norm",
    )(x1, yk, wk, g)


def _pack_w_in(w):
    kr = w[:, 1920:1952]
    half = B_ROPE // 2
    z = lambda n: jnp.zeros((w.shape[0], n), w.dtype)
    kr_placed = jnp.concatenate([z(B_NOPE), kr, z(HEAD_PAD - B_NOPE - B_ROPE)], axis=1)
    kr_rot = jnp.concatenate([z(B_NOPE), -kr[:, half:], kr[:, :half], z(HEAD_PAD - B_NOPE - B_ROPE)], axis=1)
    return jnp.concatenate([w[:, :1920], w[:, 1952:4000], kr_placed, kr_rot], axis=1).astype(BF16)


def _pack_w_uq(w):
    half = B_ROPE // 2
    r = w.shape[0]
    nope, x1, x2 = w[..., :B_NOPE], w[..., B_NOPE:B_NOPE + half], w[..., B_NOPE + half:]
    zpad = jnp.zeros((r, B_HEADS, HEAD_PAD - B_NOPE - B_ROPE), w.dtype)
    main = jnp.concatenate([nope, x1, x2, zpad], axis=-1)
    rot = jnp.concatenate([jnp.zeros_like(nope), -x2, x1, zpad], axis=-1)
    return (main.reshape(r, B_HEADS * HEAD_PAD).astype(BF16), rot.reshape(r, B_HEADS * HEAD_PAD).astype(BF16))


def _pack_w_uk(w):
    r = w.shape[0]
    zpad = jnp.zeros((r, B_HEADS, HEAD_PAD - B_NOPE), w.dtype)
    return jnp.concatenate([w, zpad], axis=-1).reshape(r, B_HEADS * HEAD_PAD).astype(BF16)


def _rope_tables(seq):
    half = B_ROPE // 2
    pos = jnp.arange(seq, dtype=F32)
    inv_freq = ROPE_THETA ** (-jnp.arange(0, B_ROPE, 2, dtype=F32) / B_ROPE)
    ang = pos[:, None] * inv_freq[None, :]
    cos, sin = jnp.cos(ang), jnp.sin(ang)
    one = jnp.ones((seq, B_NOPE), F32)
    zero = jnp.zeros((seq, B_NOPE), F32)
    zpad = jnp.zeros((seq, HEAD_PAD - B_NOPE - B_ROPE), F32)
    qs = ((B_NOPE + B_ROPE) ** -0.5) * LOG2E
    cq = jnp.concatenate([one, cos, cos, zpad], axis=1) * qs
    sq = jnp.concatenate([zero, sin, sin, zpad], axis=1) * qs
    ck = jnp.concatenate([zero, cos, cos, zpad], axis=1)
    sk = jnp.concatenate([zero, sin, sin, zpad], axis=1)
    return cq, sq, ck, sk


def kernel(x, norm1_g, w_in, q_a_norm_g, kv_a_norm_g, w_uq, w_uk, w_uv, w_a_out, w_b_out, w_o,
           norm2_g, w_router, b_router, w_gate_up, b_gate_up, w_down, b_down, final_g):
    B, S, D = x.shape
    T = B * S
    depth = norm1_g.shape[0]
    xt = x.reshape(T, D)
    cq_t, sq_t, ck_t, sk_t = _rope_tables(S)
    biases = [_dilated_bias(d) for _, d in A_PATTERNS]
    bm = 512
    for l in range(depth):
        wq1, wq2 = _pack_w_uq(w_uq[l])
        qa, ka, va, qa4, ka4, va4, qa16, ka16, va16, qb, kb, vb, gates = _inproj(
            xt, norm1_g[l][None], _pack_w_in(w_in[l]), q_a_norm_g[l][None], kv_a_norm_g[l][None],
            wq1, wq2, _pack_w_uk(w_uk[l]), w_uv[l].reshape(KV_LORA, B_WIDTH).T.astype(BF16),
            cq_t, sq_t, ck_t, sk_t, S)
        qkv = ((qa, ka, va), (qa4, ka4, va4), (qa16, ka16, va16))
        outs = [_dilated_pattern(*qkv[p], biases[p], B, S, A_PATTERNS[p][1]) for p in range(3)]
        ob = _mla_attention(qb.reshape(B, S, -1), kb.reshape(B, S, -1), vb, B, S)

        wr = jnp.pad(w_router[l], ((0, 0), (0, LOGIT_PAD - N_EXPERTS)))
        wrh = wr.astype(BF16)
        wrl = (wr - wrh.astype(F32)).astype(BF16)
        br = jnp.pad(b_router[l], (0, LOGIT_PAD - N_EXPERTS))[None]
        x1, h2, logits = _outproj(
            outs[0][0], outs[1][0], outs[2][0], outs[0][1], outs[1][1], outs[2][1],
            ob.reshape(T, B_WIDTH), gates, xt,
            w_a_out[l].astype(BF16), w_b_out[l].astype(BF16), w_o[l].astype(BF16),
            norm2_g[l][None], wrh, wrl, br)

        logits = logits[:, :N_EXPERTS]
        top_v, top_i = lax.top_k(logits, TOP_K)
        gate_w = jax.nn.softmax(top_v, axis=-1)
        sel = jnp.sum((top_i[:, :, None] == jnp.arange(N_EXPERTS)[None, None, :]).astype(jnp.int32), axis=1)
        incl = jnp.cumsum(sel, axis=0)
        counts = incl[-1]
        rank = jnp.take_along_axis(incl - sel, top_i, axis=1)
        padded = (counts + bm - 1) // bm * bm
        pend = jnp.cumsum(padded)
        pstart = pend - padded
        dest = pstart[top_i] + rank
        NP = T * TOP_K + N_EXPERTS * bm
        nblk = NP // bm
        tok = jnp.broadcast_to(jnp.arange(T, dtype=jnp.int32)[:, None], (T, TOP_K))
        row_tok = (jnp.arange(NP, dtype=jnp.int32) % T).at[dest.reshape(-1)].set(tok.reshape(-1))
        blk_start = jnp.arange(nblk, dtype=jnp.int32) * bm
        blk_e = jnp.minimum(jnp.sum(pend[None, :] <= blk_start[:, None], axis=1), N_EXPERTS - 1).astype(jnp.int32)
        n_used = (pend[-1] // bm).astype(jnp.int32)[None]

        xb = jnp.take(h2, row_tok, axis=0)
        yb = _moe_experts(xb, blk_e, n_used, w_gate_up[l], b_gate_up[l][:, None, :],
                          w_down[l], b_down[l][:, None, :], bm)
        yk = jnp.take(yb, dest.T.reshape(-1), axis=0).reshape(TOP_K, T, D)
        if l + 1 < depth:
            raise NotImplementedError("depth > 1")
        out = _final(x1, yk, gate_w, final_g[None])
    return out.reshape(B, S, D)
```

```python
import functools
import math

import jax
import jax.numpy as jnp
from jax import lax
from jax.experimental import pallas as pl
from jax.experimental.pallas import tpu as pltpu

F32 = jnp.float32
BF16 = jnp.bfloat16

D_MODEL = 1024
EPS = 1e-5
NEG = -1e30

A_HEADS = 8
A_HEAD_DIM = 64
A_WIDTH = 512
A_PATTERNS = ((128, 1), (512, 4), (2048, 16))
A_HALF = 64
A_QB = 128
A_WIN = A_QB + 2 * A_HALF
assert all(w // (2 * d) == A_HALF for w, d in A_PATTERNS) and [d for _, d in A_PATTERNS] == [1, 4, 16]

B_HEADS = 8
B_NOPE = 64
B_ROPE = 32
B_V = 64
B_WIDTH = 512
Q_LORA = 256
KV_LORA = 128
ROPE_THETA = 10000.0
HEAD_PAD = 128
ONES_ROWS = 16
MLA_SCORE_BUFS = 4

N_EXPERTS = 32
TOP_K = 4
SWIGLU_LIMIT = 7.0
SWIGLU_ALPHA = 1.702
LOGIT_PAD = 128

LOG2E = 1.4426950408889634
LN2 = 0.6931471805599453

VMEM_LIMIT = 56 * 1024 * 1024


def _cparams(*sem):
    return pltpu.CompilerParams(dimension_semantics=sem, vmem_limit_bytes=VMEM_LIMIT)


def _inproj_kernel(x_ref, g1_ref, wp_ref, gq_ref, gkv_ref, wq1_ref, wq2_ref, wuk_ref, wuv_ref,
                   cq_ref, sq_ref, ck_ref, sk_ref,
                   qa_ref, ka_ref, va_ref, qa4_ref, ka4_ref, va4_ref, qa16_ref, ka16_ref, va16_ref,
                   qb_ref, kb_ref, vb_ref, gate_ref, perm_ref):
    tm = x_ref.shape[0]
    x = x_ref[...]
    h = x * lax.rsqrt(jnp.mean(x * x, axis=-1, keepdims=True) + EPS) * g1_ref[...]
    h = h.astype(BF16)

    def proj(c0, c1):
        return jnp.dot(h, wp_ref[:, c0:c1], preferred_element_type=F32)

    def emit(which, z, nat_ref, views):
        nat_ref[...] = z.astype(BF16)
        for c in range(A_WIDTH // 128):
            perm_ref[which, c] = z[:, c * 128:(c + 1) * 128]
        for d, ref in views:
            for r in range(d):
                for c in range(A_WIDTH // 128):
                    ref[:, r * A_WIDTH + c * 128:r * A_WIDTH + (c + 1) * 128] = (
                        perm_ref[which, c, pl.ds(r, tm // d, stride=d), :].astype(BF16))

    emit(0, proj(0, 512) * (A_HEAD_DIM ** -0.5), qa_ref, ((4, qa4_ref), (16, qa16_ref)))
    emit(1, proj(512, 1024), ka_ref, ((4, ka4_ref), (16, ka16_ref)))
    emit(2, proj(1024, 1536), va_ref, ((4, va4_ref), (16, va16_ref)))

    cq = proj(1536, 1792)
    cq = cq * lax.rsqrt(jnp.mean(cq * cq, axis=-1, keepdims=True) + EPS) * gq_ref[...]
    cq = cq.astype(BF16)
    q_main = jnp.dot(cq, wq1_ref[...], preferred_element_type=F32)
    q_rot = jnp.dot(cq, wq2_ref[...], preferred_element_type=F32)
    cq_t = cq_ref[...]
    sq_t = sq_ref[...]
    for hd in range(B_HEADS):
        sl = slice(hd * HEAD_PAD, (hd + 1) * HEAD_PAD)
        qb_ref[:, sl] = (q_main[:, sl] * cq_t + q_rot[:, sl] * sq_t).astype(BF16)

    ckv = proj(1792, 1920)
    ckv = ckv * lax.rsqrt(jnp.mean(ckv * ckv, axis=-1, keepdims=True) + EPS) * gkv_ref[...]
    ckv = ckv.astype(BF16)
    k_nope = jnp.dot(ckv, wuk_ref[...], preferred_element_type=F32)
    vb_ref[...] = lax.dot_general(wuv_ref[...], ckv, (((1,), (1,)), ((), ())),
                                  preferred_element_type=F32).astype(BF16)
    k_rope = proj(3968, 4096) * ck_ref[...] + proj(4096, 4224) * sk_ref[...]
    for hd in range(B_HEADS):
        sl = slice(hd * HEAD_PAD, (hd + 1) * HEAD_PAD)
        kb_ref[:, sl] = (k_nope[:, sl] + k_rope).astype(BF16)

    gate_ref[...] = jax.nn.sigmoid(proj(1920, 3968)).astype(BF16)


def _inproj(x2, g1, wp, gq, gkv, wq1, wq2, wuk, wuv, cq_t, sq_t, ck_t, sk_t, seq, tm=512):
    T = x2.shape[0]
    nseq = seq // tm
    row = lambda i: (i, 0)
    fix = lambda i: (0, 0)
    tab = lambda i: (i % nseq, 0)
    full = lambda a: pl.BlockSpec(a.shape, fix)
    sds = lambda r, c: jax.ShapeDtypeStruct((r, c), BF16)
    out_shape = (
        sds(T, A_WIDTH), sds(T, A_WIDTH), sds(T, A_WIDTH),
        sds(T // 4, 4 * A_WIDTH), sds(T // 4, 4 * A_WIDTH), sds(T // 4, 4 * A_WIDTH),
        sds(T // 16, 16 * A_WIDTH), sds(T // 16, 16 * A_WIDTH), sds(T // 16, 16 * A_WIDTH),
        sds(T, B_HEADS * HEAD_PAD), sds(T, B_HEADS * HEAD_PAD),
        sds(B_WIDTH, T),
        sds(T, 2 * D_MODEL),
    )
    out_specs = [pl.BlockSpec((tm * s.shape[0] // T, s.shape[1]), row) for s in out_shape]
    out_specs[11] = pl.BlockSpec((B_WIDTH, tm), lambda i: (0, i))
    return pl.pallas_call(
        _inproj_kernel,
        grid=(T // tm,),
        in_specs=[pl.BlockSpec((tm, D_MODEL), row), full(g1), full(wp), full(gq), full(gkv),
                  full(wq1), full(wq2), full(wuk), full(wuv),
                  pl.BlockSpec((tm, HEAD_PAD), tab), pl.BlockSpec((tm, HEAD_PAD), tab),
                  pl.BlockSpec((tm, HEAD_PAD), tab), pl.BlockSpec((tm, HEAD_PAD), tab)],
        out_specs=out_specs,
        out_shape=out_shape,
        scratch_shapes=[pltpu.VMEM((3, A_WIDTH // 128, tm, 128), F32)],
        compiler_params=_cparams("parallel"),
        name="inproj",
    )(x2, g1, wp, gq, gkv, wq1, wq2, wuk, wuv, cq_t, sq_t, ck_t, sk_t)


def _dilated_kernel(q_ref, kc_ref, kp_ref, kn_ref, vc_ref, vp_ref, vn_ref, bias_ref,
                    o_ref, lse_ref, kbuf, vbuf, *, rows):
    i = pl.program_id(2)
    last = pl.num_programs(2) - 1
    kbuf[0:A_HALF, :] = kp_ref[0]
    kbuf[A_HALF:A_HALF + rows, :] = kc_ref[0]
    kbuf[A_HALF + rows:, :] = kn_ref[0]
    vbuf[0:A_HALF, :] = vp_ref[0]
    vbuf[A_HALF:A_HALF + rows, :] = vc_ref[0]
    vbuf[A_HALF + rows:, :] = vn_ref[0]

    nj = rows // A_QB
    npair = A_HEADS // 2
    lane = lax.broadcasted_iota(jnp.int32, (A_QB, 128), 1)
    lo_half = lane < A_HEAD_DIM

    def body(j, carry):
        r0 = pl.multiple_of(j * A_QB, A_QB)
        variant = (jnp.logical_and(i == 0, j == 0).astype(jnp.int32)
                   + 2 * jnp.logical_and(i == last, j == nj - 1).astype(jnp.int32))
        scores = []
        for pr in range(npair):
            ls = slice(pr * 128, (pr + 1) * 128)
            qp = q_ref[0, pl.ds(r0, A_QB), ls]
            zero = jnp.zeros_like(qp)
            q2 = jnp.concatenate([jnp.where(lo_half, qp, zero), jnp.where(lo_half, zero, qp)], axis=0)
            s = lax.dot_general(q2, kbuf[pl.ds(r0, A_WIN), ls], (((1,), (1,)), ((), ())),
                                preferred_element_type=F32)
            scores.append(s + bias_ref[variant * npair + pr])
        probs = []
        for s in scores:
            m = jnp.max(s, axis=-1, keepdims=True)
            p = jnp.exp(s - m)
            l = jnp.sum(p, axis=-1, keepdims=True)
            probs.append((p.astype(BF16), l, m + jnp.log(l)))
        for pr, (p, l, lse) in enumerate(probs):
            ls = slice(pr * 128, (pr + 1) * 128)
            o = jnp.dot(p, vbuf[pl.ds(r0, A_WIN), ls], preferred_element_type=F32) / l
            lse = jnp.broadcast_to(lse, (2 * A_QB, 128))
            o_ref[0, pl.ds(r0, A_QB), ls] = jnp.where(lo_half, o[:A_QB], o[A_QB:])
            lse_ref[0, pl.ds(r0, A_QB), ls] = jnp.where(lo_half, lse[:A_QB], lse[A_QB:])
        return carry

    lax.fori_loop(0, nj, body, 0)


def _dilated_pattern(qa, ka, va, bias, batch, seq, dil, rows=1024):
    L = seq // dil
    rows = min(rows, L)
    nblk = L // rows
    hb = rows // A_HALF
    nh = L // A_HALF
    view = lambda a: a.reshape(batch, L, dil * A_WIDTH)
    cur = lambda b, r, i: (b, i, r)
    prev = lambda b, r, i: (b, jnp.maximum(i * hb - 1, 0), r)
    nxt = lambda b, r, i: (b, jnp.minimum((i + 1) * hb, nh - 1), r)
    cur_spec = pl.BlockSpec((1, rows, A_WIDTH), cur)
    out_sds = jax.ShapeDtypeStruct((batch, L, dil * A_WIDTH), F32)
    o, lse = pl.pallas_call(
        functools.partial(_dilated_kernel, rows=rows),
        grid=(batch, dil, nblk),
        in_specs=[cur_spec, cur_spec,
                  pl.BlockSpec((1, A_HALF, A_WIDTH), prev), pl.BlockSpec((1, A_HALF, A_WIDTH), nxt),
                  cur_spec,
                  pl.BlockSpec((1, A_HALF, A_WIDTH), prev), pl.BlockSpec((1, A_HALF, A_WIDTH), nxt),
                  pl.BlockSpec(bias.shape, lambda b, r, i: (0, 0, 0))],
        out_specs=[cur_spec, cur_spec],
        out_shape=(out_sds, out_sds),
        scratch_shapes=[pltpu.VMEM((rows + 2 * A_HALF, A_WIDTH), BF16),
                        pltpu.VMEM((rows + 2 * A_HALF, A_WIDTH), BF16)],
        compiler_params=_cparams("parallel", "parallel", "parallel"),
        name=f"dilated_d{dil}",
    )(view(qa), view(ka), view(ka), view(ka), view(va), view(va), view(va), bias)
    return o.reshape(batch * L, dil * A_WIDTH), lse.reshape(batch * L, dil * A_WIDTH)


def _dilated_bias(dil):
    slopes = 2.0 ** (-8.0 * (jnp.arange(A_HEADS, dtype=F32) + 1.0) / A_HEADS)
    col = jnp.arange(A_WIN)[None, :]
    rel = (col - A_HALF) - jnp.arange(A_QB)[:, None]
    dist = (jnp.abs(rel) * dil).astype(F32)
    bias = -slopes[:, None, None] * dist[None]
    bias = jnp.where((jnp.abs(rel) <= A_HALF)[None], bias, NEG)
    variants = []
    for v in range(4):
        ok = jnp.ones((1, A_WIN), bool)
        if v & 1:
            ok = jnp.logical_and(ok, col >= A_HALF)
        if v & 2:
            ok = jnp.logical_and(ok, col < A_HALF + A_QB)
        variants.append(jnp.where(ok[None], bias, NEG).reshape(A_HEADS // 2, 2 * A_QB, A_WIN))
    return jnp.concatenate(variants, axis=0)


def _mla_kernel(q_ref, k_ref, vt_ref, o_ref, *s_bufs, tk):
    tq = q_ref.shape[1]
    nk = k_ref.shape[1] // tk
    qs = [q_ref[0, :, hh * HEAD_PAD:(hh + 1) * HEAD_PAD] for hh in range(2)]

    def scores(j, s_ref):
        off = pl.multiple_of(j * tk, tk)
        for hh in range(2):
            s_ref[hh] = lax.dot_general(k_ref[0, pl.ds(off, tk), hh * HEAD_PAD:(hh + 1) * HEAD_PAD], qs[hh],
                                        (((1,), (1,)), ((), ())), preferred_element_type=F32)

    ones = jnp.ones((ONES_ROWS, tk), BF16)

    def softmax_pv(j, s_ref, stats):
        off = pl.multiple_of(j * tk, tk)
        new = []
        for hh in range(2):
            m, acc = stats[hh]
            s = s_ref[hh]
            vt = jnp.concatenate([vt_ref[hh * B_V:(hh + 1) * B_V, pl.ds(off, tk)], ones], axis=0)
            m_new = jnp.maximum(m, jnp.max(s, axis=0, keepdims=True))
            alpha = jnp.exp2(m - m_new)
            p = jnp.exp2(s - m_new)
            acc = alpha * acc + jnp.dot(vt, p.astype(BF16), preferred_element_type=F32)
            new.append((m_new, acc))
        return tuple(new)

    nbuf = len(s_bufs)

    def body(jj, stats):
        j = nbuf * jj
        for u in range(nbuf):
            scores(jnp.minimum(j + u + 1, nk - 1), s_bufs[(u + 1) % nbuf])
            stats = softmax_pv(j + u, s_bufs[u], stats)
        return stats

    init = (jnp.full((1, tq), NEG, F32), jnp.zeros((B_V + ONES_ROWS, tq), F32))
    scores(0, s_bufs[0])
    (_, acc0), (_, acc1) = lax.fori_loop(0, nk // nbuf, body, (init, init))
    o_t = jnp.concatenate([a[:B_V] / a[B_V:B_V + 1] for a in (acc0, acc1)], axis=0)
    o_ref[0] = o_t.T.astype(BF16)


def _mla_attention(qb, kb, vbt, batch, seq, tq=256, tk=512):
    return pl.pallas_call(
        functools.partial(_mla_kernel, tk=tk),
        grid=(batch, B_HEADS // 2, seq // tq),
        in_specs=[pl.BlockSpec((1, tq, 2 * HEAD_PAD), lambda b, p, i: (b, i, p)),
                  pl.BlockSpec((1, seq, 2 * HEAD_PAD), lambda b, p, i: (b, 0, p)),
                  pl.BlockSpec((2 * B_V, seq), lambda b, p, i: (p, b))],
        out_specs=pl.BlockSpec((1, tq, 128), lambda b, p, i: (b, i, p)),
        out_shape=jax.ShapeDtypeStruct((batch, seq, B_WIDTH), BF16),
        scratch_shapes=[pltpu.VMEM((2, tk, tq), F32) for _ in range(MLA_SCORE_BUFS)],
        compiler_params=_cparams("parallel", "parallel", "parallel"),
        name="mla_attention",
    )(qb, kb, vbt)


def _outproj_kernel(o1_ref, o2_ref, o3_ref, l1_ref, l2_ref, l3_ref, ob_ref, gate_ref, x_ref,
                    wa_ref, wb_ref, wo_ref, g2_ref, wrh_ref, wrl_ref, br_ref,
                    tri_ref, x1_ref, h2_ref, route_ref, count_ref, perm_ref, carry_ref):
    tm = x_ref.shape[0]

    @pl.when(pl.program_id(0) == 0)
    def _():
        carry_ref[...] = jnp.zeros_like(carry_ref)

    def token_order(which, ref, d):
        for r in range(d):
            for c in range(A_WIDTH // 128):
                perm_ref[which, c, pl.ds(r, tm // d, stride=d), :] = (
                    ref[:, r * A_WIDTH + c * 128:r * A_WIDTH + (c + 1) * 128])
        return jnp.concatenate([perm_ref[which, c] for c in range(A_WIDTH // 128)], axis=1)

    l1 = l1_ref[...]
    l2, l3 = token_order(0, l2_ref, 4), token_order(1, l3_ref, 16)
    o2, o3 = token_order(2, o2_ref, 4), token_order(3, o3_ref, 16)
    mx = jnp.maximum(jnp.maximum(l1, l2), l3)
    e1, e2, e3 = jnp.exp(l1 - mx), jnp.exp(l2 - mx), jnp.exp(l3 - mx)
    oa = (e1 * o1_ref[...] + e2 * o2 + e3 * o3) / (e1 + e2 + e3)
    ya = jnp.dot(oa.astype(BF16), wa_ref[...], preferred_element_type=F32)
    yb = jnp.dot(ob_ref[...], wb_ref[...], preferred_element_type=F32)
    merged = gate_ref[:, :D_MODEL].astype(F32) * ya + gate_ref[:, D_MODEL:].astype(F32) * yb
    x1 = x_ref[...] + jnp.dot(merged.astype(BF16), wo_ref[...], preferred_element_type=F32)
    x1_ref[...] = x1
    h2 = x1 * lax.rsqrt(jnp.mean(x1 * x1, axis=-1, keepdims=True) + EPS) * g2_ref[...]
    hi = h2.astype(BF16)
    h2_ref[...] = hi
    lo = (h2 - hi.astype(F32)).astype(BF16)
    logits = jnp.dot(hi, wrh_ref[...], preferred_element_type=F32)
    logits = logits + jnp.dot(lo, wrh_ref[...], preferred_element_type=F32)
    logits = logits + jnp.dot(hi, wrl_ref[...], preferred_element_type=F32)
    logits = logits + br_ref[...]

    lane = lax.broadcasted_iota(jnp.int32, (tm, LOGIT_PAD), 1)
    work = logits
    sel = jnp.zeros((tm, LOGIT_PAD), F32)
    idx_cols, val_cols = [], []
    for _ in range(TOP_K):
        mx = jnp.max(work, axis=-1, keepdims=True)
        idx = jnp.min(jnp.where(work == mx, lane, LOGIT_PAD), axis=-1, keepdims=True)
        hit = lane == idx
        sel = sel + hit.astype(F32)
        work = jnp.where(hit, 2.0 * NEG, work)
        idx_cols.append(idx)
        val_cols.append(mx)
    exps = [jnp.exp(v - val_cols[0]) for v in val_cols]
    denom = exps[0] + exps[1] + exps[2] + exps[3]
    gate_cols = [e / denom for e in exps]

    before = jnp.dot(tri_ref[...], sel.astype(BF16), preferred_element_type=F32) + carry_ref[0:1, :]
    rank_cols = [jnp.sum(jnp.where(lane == idx, before, 0.0), axis=-1, keepdims=True) for idx in idx_cols]
    carry_ref[...] = carry_ref[...] + jnp.sum(sel, axis=0, keepdims=True)
    count_ref[...] = carry_ref[...]

    cols = [c.astype(F32) for c in idx_cols] + rank_cols + gate_cols
    route = jnp.zeros((tm, LOGIT_PAD), F32)
    for c, col in enumerate(cols):
        route = jnp.where(lane == c, col, route)
    route_ref[...] = route


def _outproj(o1, o2, o3, l1, l2, l3, ob, gates, x2, wa, wb, wo, g2, wrh, wrl, br, tm=512):
    T = x2.shape[0]
    row = lambda i: (i, 0)
    fix = lambda i: (0, 0)
    full = lambda a: pl.BlockSpec(a.shape, fix)
    rows = lambda a: pl.BlockSpec((tm * a.shape[0] // T, a.shape[1]), row)
    tri = (jnp.arange(tm)[:, None] > jnp.arange(tm)[None, :]).astype(BF16)
    out_shape = (jax.ShapeDtypeStruct((T, D_MODEL), F32),
                 jax.ShapeDtypeStruct((T, D_MODEL), BF16),
                 jax.ShapeDtypeStruct((T, LOGIT_PAD), F32),
                 jax.ShapeDtypeStruct((8, LOGIT_PAD), F32))
    return pl.pallas_call(
        _outproj_kernel,
        grid=(T // tm,),
        in_specs=[rows(o1), rows(o2), rows(o3), rows(l1), rows(l2), rows(l3), rows(ob), rows(gates),
                  rows(x2), full(wa), full(wb), full(wo), full(g2), full(wrh), full(wrl), full(br), full(tri)],
        out_specs=[pl.BlockSpec((tm, D_MODEL), row), pl.BlockSpec((tm, D_MODEL), row),
                   pl.BlockSpec((tm, LOGIT_PAD), row), pl.BlockSpec((8, LOGIT_PAD), fix)],
        out_shape=out_shape,
        scratch_shapes=[pltpu.VMEM((4, A_WIDTH // 128, tm, 128), F32), pltpu.VMEM((8, LOGIT_PAD), F32)],
        compiler_params=_cparams("arbitrary"),
        name="outproj",
    )(o1, o2, o3, l1, l2, l3, ob, gates, x2, wa, wb, wo, g2, wrh, wrl, br, tri)


def _moe_kernel(be_ref, nu_ref, x_ref, wgu_ref, bgu_ref, wd_ref, bd_ref, y_ref, wgu_bf, wd_bf):
    i = pl.program_id(0)

    @pl.when(i >= nu_ref[0])
    def _():
        y_ref[...] = jnp.zeros_like(y_ref)

    @pl.when(jnp.logical_or(i == 0, be_ref[i] != be_ref[jnp.maximum(i - 1, 0)]))
    def _():
        rows = 128

        def cast(c, carry):
            r0 = pl.multiple_of(c * rows, rows)
            wgu_bf[pl.ds(r0, rows), :] = wgu_ref[0, pl.ds(r0, rows), :].astype(BF16)
            wd_bf[pl.ds(r0, rows), :] = wd_ref[0, pl.ds(r0, rows), :].astype(BF16)
            return carry

        lax.fori_loop(0, D_MODEL // rows, cast, 0)

    @pl.when(i < nu_ref[0])
    def _():
        gu = jnp.dot(x_ref[...], wgu_bf[...], preferred_element_type=F32) + bgu_ref[0]
        gate = jnp.minimum(gu[:, :D_MODEL], SWIGLU_LIMIT)
        up = jnp.clip(gu[:, D_MODEL:], -SWIGLU_LIMIT, SWIGLU_LIMIT)
        act = (up + 1.0) * (gate * jax.nn.sigmoid(SWIGLU_ALPHA * gate))
        y = jnp.dot(act.astype(BF16), wd_bf[...], preferred_element_type=F32) + bd_ref[0]
        y_ref[...] = y.astype(BF16)


def _moe_experts(xb, blk_e, n_used, wgu, bgu, wd, bd, bm):
    NP = xb.shape[0]
    nblk = NP // bm
    grid_spec = pltpu.PrefetchScalarGridSpec(
        num_scalar_prefetch=2,
        grid=(nblk,),
        in_specs=[pl.BlockSpec((bm, D_MODEL), lambda i, be, nu: (jnp.minimum(i, nu[0] - 1), 0)),
                  pl.BlockSpec((1, D_MODEL, 2 * D_MODEL), lambda i, be, nu: (be[i], 0, 0)),
                  pl.BlockSpec((1, 1, 2 * D_MODEL), lambda i, be, nu: (be[i], 0, 0)),
                  pl.BlockSpec((1, D_MODEL, D_MODEL), lambda i, be, nu: (be[i], 0, 0)),
                  pl.BlockSpec((1, 1, D_MODEL), lambda i, be, nu: (be[i], 0, 0))],
        out_specs=pl.BlockSpec((bm, D_MODEL), lambda i, be, nu: (i, 0)),
        scratch_shapes=[pltpu.VMEM((D_MODEL, 2 * D_MODEL), BF16), pltpu.VMEM((D_MODEL, D_MODEL), BF16)],
    )
    return pl.pallas_call(
        _moe_kernel,
        grid_spec=grid_spec,
        out_shape=jax.ShapeDtypeStruct((NP, D_MODEL), BF16),
        compiler_params=_cparams("arbitrary"),
        name="moe_experts",
    )(blk_e, n_used, xb, wgu, bgu, wd, bd)


def _final_kernel(x1_ref, y_ref, w_ref, g_ref, o_ref):
    w = w_ref[...]
    acc = x1_ref[...]
    for k in range(TOP_K):
        acc = acc + w[:, k:k + 1] * y_ref[k].astype(F32)
    o_ref[...] = acc * lax.rsqrt(jnp.mean(acc * acc, axis=-1, keepdims=True) + EPS) * g_ref[...]


def _final(x1, yk, wk, g, tm=512):
    T = x1.shape[0]
    return pl.pallas_call(
        _final_kernel,
        grid=(T // tm,),
        in_specs=[pl.BlockSpec((tm, D_MODEL), lambda i: (i, 0)),
                  pl.BlockSpec((TOP_K, tm, D_MODEL), lambda i: (0, i, 0)),
                  pl.BlockSpec((tm, TOP_K), lambda i: (i, 0)),
                  pl.BlockSpec((1, D_MODEL), lambda i: (0, 0))],
        out_specs=pl.BlockSpec((tm, D_MODEL), lambda i: (i, 0)),
        out_shape=jax.ShapeDtypeStruct((T, D_MODEL), F32),
        compiler_params=_cparams("parallel"),
        name="final_norm",
    )(x1, yk, wk, g)


def _pack_w_in(w):
    kr = w[:, 1920:1952]
    half = B_ROPE // 2
    z = lambda n: jnp.zeros((w.shape[0], n), w.dtype)
    kr_placed = jnp.concatenate([z(B_NOPE), kr, z(HEAD_PAD - B_NOPE - B_ROPE)], axis=1)
    kr_rot = jnp.concatenate([z(B_NOPE), -kr[:, half:], kr[:, :half], z(HEAD_PAD - B_NOPE - B_ROPE)], axis=1)
    return jnp.concatenate([w[:, :1920], w[:, 1952:4000], kr_placed, kr_rot], axis=1).astype(BF16)


def _pack_w_uq(w):
    half = B_ROPE // 2
    r = w.shape[0]
    nope, x1, x2 = w[..., :B_NOPE], w[..., B_NOPE:B_NOPE + half], w[..., B_NOPE + half:]
    zpad = jnp.zeros((r, B_HEADS, HEAD_PAD - B_NOPE - B_ROPE), w.dtype)
    main = jnp.concatenate([nope, x1, x2, zpad], axis=-1)
    rot = jnp.concatenate([jnp.zeros_like(nope), -x2, x1, zpad], axis=-1)
    return (main.reshape(r, B_HEADS * HEAD_PAD).astype(BF16), rot.reshape(r, B_HEADS * HEAD_PAD).astype(BF16))


def _pack_w_uk(w):
    r = w.shape[0]
    zpad = jnp.zeros((r, B_HEADS, HEAD_PAD - B_NOPE), w.dtype)
    return jnp.concatenate([w, zpad], axis=-1).reshape(r, B_HEADS * HEAD_PAD).astype(BF16)


def _rope_tables(seq):
    half = B_ROPE // 2
    pos = jnp.arange(seq, dtype=F32)
    inv_freq = ROPE_THETA ** (-jnp.arange(0, B_ROPE, 2, dtype=F32) / B_ROPE)
    ang = pos[:, None] * inv_freq[None, :]
    cos, sin = jnp.cos(ang), jnp.sin(ang)
    one = jnp.ones((seq, B_NOPE), F32)
    zero = jnp.zeros((seq, B_NOPE), F32)
    zpad = jnp.zeros((seq, HEAD_PAD - B_NOPE - B_ROPE), F32)
    qs = ((B_NOPE + B_ROPE) ** -0.5) * LOG2E
    cq = jnp.concatenate([one, cos, cos, zpad], axis=1) * qs
    sq = jnp.concatenate([zero, sin, sin, zpad], axis=1) * qs
    ck = jnp.concatenate([zero, cos, cos, zpad], axis=1)
    sk = jnp.concatenate([zero, sin, sin, zpad], axis=1)
    return cq, sq, ck, sk


def kernel(x, norm1_g, w_in, q_a_norm_g, kv_a_norm_g, w_uq, w_uk, w_uv, w_a_out, w_b_out, w_o,
           norm2_g, w_router, b_router, w_gate_up, b_gate_up, w_down, b_down, final_g):
    B, S, D = x.shape
    T = B * S
    depth = norm1_g.shape[0]
    xt = x.reshape(T, D)
    cq_t, sq_t, ck_t, sk_t = _rope_tables(S)
    biases = [_dilated_bias(d) for _, d in A_PATTERNS]
    bm = 512
    for l in range(depth):
        wq1, wq2 = _pack_w_uq(w_uq[l])
        qa, ka, va, qa4, ka4, va4, qa16, ka16, va16, qb, kb, vb, gates = _inproj(
            xt, norm1_g[l][None], _pack_w_in(w_in[l]), q_a_norm_g[l][None], kv_a_norm_g[l][None],
            wq1, wq2, _pack_w_uk(w_uk[l]), w_uv[l].reshape(KV_LORA, B_WIDTH).T.astype(BF16),
            cq_t, sq_t, ck_t, sk_t, S)
        qkv = ((qa, ka, va), (qa4, ka4, va4), (qa16, ka16, va16))
        outs = [_dilated_pattern(*qkv[p], biases[p], B, S, A_PATTERNS[p][1]) for p in range(3)]
        ob = _mla_attention(qb.reshape(B, S, -1), kb.reshape(B, S, -1), vb, B, S)

        wr = jnp.pad(w_router[l], ((0, 0), (0, LOGIT_PAD - N_EXPERTS)))
        wrh = wr.astype(BF16)
        wrl = (wr - wrh.astype(F32)).astype(BF16)
        br = jnp.pad(b_router[l], (0, LOGIT_PAD - N_EXPERTS), constant_values=NEG)[None]
        x1, h2, route, count_rows = _outproj(
            outs[0][0], outs[1][0], outs[2][0], outs[0][1], outs[1][1], outs[2][1],
            ob.reshape(T, B_WIDTH), gates, xt,
            w_a_out[l].astype(BF16), w_b_out[l].astype(BF16), w_o[l].astype(BF16),
            norm2_g[l][None], wrh, wrl, br)

        top_i = route[:, 0:TOP_K].astype(jnp.int32)
        rank = route[:, TOP_K:2 * TOP_K].astype(jnp.int32)
        gate_w = route[:, 2 * TOP_K:3 * TOP_K]
        counts = count_rows[0, :N_EXPERTS].astype(jnp.int32)
        padded = (counts + bm - 1) // bm * bm
        pend = jnp.cumsum(padded)
        pstart = pend - padded
        dest = pstart[top_i] + rank
        NP = T * TOP_K + N_EXPERTS * bm
        nblk = NP // bm
        tok = jnp.broadcast_to(jnp.arange(T, dtype=jnp.int32)[:, None], (T, TOP_K))
        row_tok = (jnp.arange(NP, dtype=jnp.int32) % T).at[dest.reshape(-1)].set(tok.reshape(-1))
        blk_start = jnp.arange(nblk, dtype=jnp.int32) * bm
        blk_e = jnp.minimum(jnp.sum(pend[None, :] <= blk_start[:, None], axis=1), N_EXPERTS - 1).astype(jnp.int32)
        n_used = (pend[-1] // bm).astype(jnp.int32)[None]

        xb = jnp.take(h2, row_tok, axis=0)
        yb = _moe_experts(xb, blk_e, n_used, w_gate_up[l], b_gate_up[l][:, None, :],
                          w_down[l], b_down[l][:, None, :], bm)
        yk = jnp.take(yb, dest.T.reshape(-1), axis=0).reshape(TOP_K, T, D)
        if l + 1 < depth:
            raise NotImplementedError("depth > 1")
        out = _final(x1, yk, gate_w, final_g[None])
    return out.reshape(B, S, D)
```

```python
import functools
import math

import jax
import jax.numpy as jnp
from jax import lax
from jax.experimental import pallas as pl
from jax.experimental.pallas import tpu as pltpu
from jax.experimental.pallas import tpu_sc as plsc

F32 = jnp.float32
BF16 = jnp.bfloat16

D_MODEL = 1024
EPS = 1e-5
NEG = -1e30

A_HEADS = 8
A_HEAD_DIM = 64
A_WIDTH = 512
A_PATTERNS = ((128, 1), (512, 4), (2048, 16))
A_HALF = 64
A_QB = 128
A_WIN = A_QB + 2 * A_HALF
assert all(w // (2 * d) == A_HALF for w, d in A_PATTERNS) and [d for _, d in A_PATTERNS] == [1, 4, 16]

B_HEADS = 8
B_NOPE = 64
B_ROPE = 32
B_V = 64
B_WIDTH = 512
Q_LORA = 256
KV_LORA = 128
ROPE_THETA = 10000.0
HEAD_PAD = 128
ONES_ROWS = 16
MLA_SCORE_BUFS = 4

N_EXPERTS = 32
TOP_K = 4
SWIGLU_LIMIT = 7.0
SWIGLU_ALPHA = 1.702
LOGIT_PAD = 128

LOG2E = 1.4426950408889634
LN2 = 0.6931471805599453

VMEM_LIMIT = 56 * 1024 * 1024


def _cparams(*sem):
    return pltpu.CompilerParams(dimension_semantics=sem, vmem_limit_bytes=VMEM_LIMIT)


def _inproj_kernel(x_ref, g1_ref, wp_ref, gq_ref, gkv_ref, wq1_ref, wq2_ref, wuk_ref, wuv_ref,
                   cq_ref, sq_ref, ck_ref, sk_ref,
                   qa_ref, ka_ref, va_ref, qa4_ref, ka4_ref, va4_ref, qa16_ref, ka16_ref, va16_ref,
                   qb_ref, kb_ref, vb_ref, gate_ref, perm_ref):
    tm = x_ref.shape[0]
    x = x_ref[...]
    h = x * lax.rsqrt(jnp.mean(x * x, axis=-1, keepdims=True) + EPS) * g1_ref[...]
    h = h.astype(BF16)

    def proj(c0, c1):
        return jnp.dot(h, wp_ref[:, c0:c1], preferred_element_type=F32)

    def emit(which, z, nat_ref, views):
        nat_ref[...] = z.astype(BF16)
        for c in range(A_WIDTH // 128):
            perm_ref[which, c] = z[:, c * 128:(c + 1) * 128]
        for d, ref in views:
            for r in range(d):
                for c in range(A_WIDTH // 128):
                    ref[:, r * A_WIDTH + c * 128:r * A_WIDTH + (c + 1) * 128] = (
                        perm_ref[which, c, pl.ds(r, tm // d, stride=d), :].astype(BF16))

    emit(0, proj(0, 512) * (A_HEAD_DIM ** -0.5), qa_ref, ((4, qa4_ref), (16, qa16_ref)))
    emit(1, proj(512, 1024), ka_ref, ((4, ka4_ref), (16, ka16_ref)))
    emit(2, proj(1024, 1536), va_ref, ((4, va4_ref), (16, va16_ref)))

    cq = proj(1536, 1792)
    cq = cq * lax.rsqrt(jnp.mean(cq * cq, axis=-1, keepdims=True) + EPS) * gq_ref[...]
    cq = cq.astype(BF16)
    q_main = jnp.dot(cq, wq1_ref[...], preferred_element_type=F32)
    q_rot = jnp.dot(cq, wq2_ref[...], preferred_element_type=F32)
    cq_t = cq_ref[...]
    sq_t = sq_ref[...]
    for hd in range(B_HEADS):
        sl = slice(hd * HEAD_PAD, (hd + 1) * HEAD_PAD)
        qb_ref[:, sl] = (q_main[:, sl] * cq_t + q_rot[:, sl] * sq_t).astype(BF16)

    ckv = proj(1792, 1920)
    ckv = ckv * lax.rsqrt(jnp.mean(ckv * ckv, axis=-1, keepdims=True) + EPS) * gkv_ref[...]
    ckv = ckv.astype(BF16)
    k_nope = jnp.dot(ckv, wuk_ref[...], preferred_element_type=F32)
    vb_ref[...] = lax.dot_general(wuv_ref[...], ckv, (((1,), (1,)), ((), ())),
                                  preferred_element_type=F32).astype(BF16)
    k_rope = proj(3968, 4096) * ck_ref[...] + proj(4096, 4224) * sk_ref[...]
    for hd in range(B_HEADS):
        sl = slice(hd * HEAD_PAD, (hd + 1) * HEAD_PAD)
        kb_ref[:, sl] = (k_nope[:, sl] + k_rope).astype(BF16)

    gate_ref[...] = jax.nn.sigmoid(proj(1920, 3968)).astype(BF16)


def _inproj(x2, g1, wp, gq, gkv, wq1, wq2, wuk, wuv, cq_t, sq_t, ck_t, sk_t, seq, tm=512):
    T = x2.shape[0]
    nseq = seq // tm
    row = lambda i: (i, 0)
    fix = lambda i: (0, 0)
    tab = lambda i: (i % nseq, 0)
    full = lambda a: pl.BlockSpec(a.shape, fix)
    sds = lambda r, c: jax.ShapeDtypeStruct((r, c), BF16)
    out_shape = (
        sds(T, A_WIDTH), sds(T, A_WIDTH), sds(T, A_WIDTH),
        sds(T // 4, 4 * A_WIDTH), sds(T // 4, 4 * A_WIDTH), sds(T // 4, 4 * A_WIDTH),
        sds(T // 16, 16 * A_WIDTH), sds(T // 16, 16 * A_WIDTH), sds(T // 16, 16 * A_WIDTH),
        sds(T, B_HEADS * HEAD_PAD), sds(T, B_HEADS * HEAD_PAD),
        sds(B_WIDTH, T),
        sds(T, 2 * D_MODEL),
    )
    out_specs = [pl.BlockSpec((tm * s.shape[0] // T, s.shape[1]), row) for s in out_shape]
    out_specs[11] = pl.BlockSpec((B_WIDTH, tm), lambda i: (0, i))
    return pl.pallas_call(
        _inproj_kernel,
        grid=(T // tm,),
        in_specs=[pl.BlockSpec((tm, D_MODEL), row), full(g1), full(wp), full(gq), full(gkv),
                  full(wq1), full(wq2), full(wuk), full(wuv),
                  pl.BlockSpec((tm, HEAD_PAD), tab), pl.BlockSpec((tm, HEAD_PAD), tab),
                  pl.BlockSpec((tm, HEAD_PAD), tab), pl.BlockSpec((tm, HEAD_PAD), tab)],
        out_specs=out_specs,
        out_shape=out_shape,
        scratch_shapes=[pltpu.VMEM((3, A_WIDTH // 128, tm, 128), F32)],
        compiler_params=_cparams("parallel"),
        name="inproj",
    )(x2, g1, wp, gq, gkv, wq1, wq2, wuk, wuv, cq_t, sq_t, ck_t, sk_t)


def _dilated_kernel(q_ref, kc_ref, kp_ref, kn_ref, vc_ref, vp_ref, vn_ref, bias_ref,
                    o_ref, lse_ref, kbuf, vbuf, *, rows):
    i = pl.program_id(2)
    last = pl.num_programs(2) - 1
    kbuf[0:A_HALF, :] = kp_ref[0]
    kbuf[A_HALF:A_HALF + rows, :] = kc_ref[0]
    kbuf[A_HALF + rows:, :] = kn_ref[0]
    vbuf[0:A_HALF, :] = vp_ref[0]
    vbuf[A_HALF:A_HALF + rows, :] = vc_ref[0]
    vbuf[A_HALF + rows:, :] = vn_ref[0]

    nj = rows // A_QB
    npair = A_HEADS // 2
    lane = lax.broadcasted_iota(jnp.int32, (A_QB, 128), 1)
    lo_half = lane < A_HEAD_DIM

    def body(j, carry):
        r0 = pl.multiple_of(j * A_QB, A_QB)
        variant = (jnp.logical_and(i == 0, j == 0).astype(jnp.int32)
                   + 2 * jnp.logical_and(i == last, j == nj - 1).astype(jnp.int32))
        scores = []
        for pr in range(npair):
            ls = slice(pr * 128, (pr + 1) * 128)
            qp = q_ref[0, pl.ds(r0, A_QB), ls]
            zero = jnp.zeros_like(qp)
            q2 = jnp.concatenate([jnp.where(lo_half, qp, zero), jnp.where(lo_half, zero, qp)], axis=0)
            s = lax.dot_general(q2, kbuf[pl.ds(r0, A_WIN), ls], (((1,), (1,)), ((), ())),
                                preferred_element_type=F32)
            scores.append(s + bias_ref[variant * npair + pr])
        probs = []
        for s in scores:
            m = jnp.max(s, axis=-1, keepdims=True)
            p = jnp.exp(s - m)
            l = jnp.sum(p, axis=-1, keepdims=True)
            probs.append((p.astype(BF16), l, m + jnp.log(l)))
        for pr, (p, l, lse) in enumerate(probs):
            ls = slice(pr * 128, (pr + 1) * 128)
            o = jnp.dot(p, vbuf[pl.ds(r0, A_WIN), ls], preferred_element_type=F32) / l
            lse = jnp.broadcast_to(lse, (2 * A_QB, 128))
            o_ref[0, pl.ds(r0, A_QB), ls] = jnp.where(lo_half, o[:A_QB], o[A_QB:])
            lse_ref[0, pl.ds(r0, A_QB), ls] = jnp.where(lo_half, lse[:A_QB], lse[A_QB:])
        return carry

    lax.fori_loop(0, nj, body, 0)


def _dilated_pattern(qa, ka, va, bias, batch, seq, dil, rows=1024):
    L = seq // dil
    rows = min(rows, L)
    nblk = L // rows
    hb = rows // A_HALF
    nh = L // A_HALF
    view = lambda a: a.reshape(batch, L, dil * A_WIDTH)
    cur = lambda b, r, i: (b, i, r)
    prev = lambda b, r, i: (b, jnp.maximum(i * hb - 1, 0), r)
    nxt = lambda b, r, i: (b, jnp.minimum((i + 1) * hb, nh - 1), r)
    cur_spec = pl.BlockSpec((1, rows, A_WIDTH), cur)
    out_sds = jax.ShapeDtypeStruct((batch, L, dil * A_WIDTH), F32)
    o, lse = pl.pallas_call(
        functools.partial(_dilated_kernel, rows=rows),
        grid=(batch, dil, nblk),
        in_specs=[cur_spec, cur_spec,
                  pl.BlockSpec((1, A_HALF, A_WIDTH), prev), pl.BlockSpec((1, A_HALF, A_WIDTH), nxt),
                  cur_spec,
                  pl.BlockSpec((1, A_HALF, A_WIDTH), prev), pl.BlockSpec((1, A_HALF, A_WIDTH), nxt),
                  pl.BlockSpec(bias.shape, lambda b, r, i: (0, 0, 0))],
        out_specs=[cur_spec, cur_spec],
        out_shape=(out_sds, out_sds),
        scratch_shapes=[pltpu.VMEM((rows + 2 * A_HALF, A_WIDTH), BF16),
                        pltpu.VMEM((rows + 2 * A_HALF, A_WIDTH), BF16)],
        compiler_params=_cparams("parallel", "parallel", "parallel"),
        name=f"dilated_d{dil}",
    )(view(qa), view(ka), view(ka), view(ka), view(va), view(va), view(va), bias)
    return o.reshape(batch * L, dil * A_WIDTH), lse.reshape(batch * L, dil * A_WIDTH)


def _dilated_bias(dil):
    slopes = 2.0 ** (-8.0 * (jnp.arange(A_HEADS, dtype=F32) + 1.0) / A_HEADS)
    col = jnp.arange(A_WIN)[None, :]
    rel = (col - A_HALF) - jnp.arange(A_QB)[:, None]
    dist = (jnp.abs(rel) * dil).astype(F32)
    bias = -slopes[:, None, None] * dist[None]
    bias = jnp.where((jnp.abs(rel) <= A_HALF)[None], bias, NEG)
    variants = []
    for v in range(4):
        ok = jnp.ones((1, A_WIN), bool)
        if v & 1:
            ok = jnp.logical_and(ok, col >= A_HALF)
        if v & 2:
            ok = jnp.logical_and(ok, col < A_HALF + A_QB)
        variants.append(jnp.where(ok[None], bias, NEG).reshape(A_HEADS // 2, 2 * A_QB, A_WIN))
    return jnp.concatenate(variants, axis=0)


def _mla_kernel(q_ref, k_ref, vt_ref, o_ref, *s_bufs, tk):
    tq = q_ref.shape[1]
    nk = k_ref.shape[1] // tk
    qs = [q_ref[0, :, hh * HEAD_PAD:(hh + 1) * HEAD_PAD] for hh in range(2)]

    def scores(j, s_ref):
        off = pl.multiple_of(j * tk, tk)
        for hh in range(2):
            s_ref[hh] = lax.dot_general(k_ref[0, pl.ds(off, tk), hh * HEAD_PAD:(hh + 1) * HEAD_PAD], qs[hh],
                                        (((1,), (1,)), ((), ())), preferred_element_type=F32)

    ones = jnp.ones((ONES_ROWS, tk), BF16)

    def softmax_pv(j, s_ref, stats):
        off = pl.multiple_of(j * tk, tk)
        new = []
        for hh in range(2):
            m, acc = stats[hh]
            s = s_ref[hh]
            vt = jnp.concatenate([vt_ref[hh * B_V:(hh + 1) * B_V, pl.ds(off, tk)], ones], axis=0)
            m_new = jnp.maximum(m, jnp.max(s, axis=0, keepdims=True))
            alpha = jnp.exp2(m - m_new)
            p = jnp.exp2(s - m_new)
            acc = alpha * acc + jnp.dot(vt, p.astype(BF16), preferred_element_type=F32)
            new.append((m_new, acc))
        return tuple(new)

    nbuf = len(s_bufs)

    def body(jj, stats):
        j = nbuf * jj
        for u in range(nbuf):
            scores(jnp.minimum(j + u + 1, nk - 1), s_bufs[(u + 1) % nbuf])
            stats = softmax_pv(j + u, s_bufs[u], stats)
        return stats

    init = (jnp.full((1, tq), NEG, F32), jnp.zeros((B_V + ONES_ROWS, tq), F32))
    scores(0, s_bufs[0])
    (_, acc0), (_, acc1) = lax.fori_loop(0, nk // nbuf, body, (init, init))
    o_t = jnp.concatenate([a[:B_V] / a[B_V:B_V + 1] for a in (acc0, acc1)], axis=0)
    o_ref[0] = o_t.T.astype(BF16)


def _mla_attention(qb, kb, vbt, batch, seq, tq=256, tk=512):
    return pl.pallas_call(
        functools.partial(_mla_kernel, tk=tk),
        grid=(batch, B_HEADS // 2, seq // tq),
        in_specs=[pl.BlockSpec((1, tq, 2 * HEAD_PAD), lambda b, p, i: (b, i, p)),
                  pl.BlockSpec((1, seq, 2 * HEAD_PAD), lambda b, p, i: (b, 0, p)),
                  pl.BlockSpec((2 * B_V, seq), lambda b, p, i: (p, b))],
        out_specs=pl.BlockSpec((1, tq, 128), lambda b, p, i: (b, i, p)),
        out_shape=jax.ShapeDtypeStruct((batch, seq, B_WIDTH), BF16),
        scratch_shapes=[pltpu.VMEM((2, tk, tq), F32) for _ in range(MLA_SCORE_BUFS)],
        compiler_params=_cparams("parallel", "parallel", "parallel"),
        name="mla_attention",
    )(qb, kb, vbt)


def _outproj_kernel(o1_ref, o2_ref, o3_ref, l1_ref, l2_ref, l3_ref, ob_ref, gate_ref, x_ref,
                    wa_ref, wb_ref, wo_ref, g2_ref, wrh_ref, wrl_ref, br_ref,
                    tri_ref, x1_ref, h2_ref, route_ref, count_ref, perm_ref, carry_ref):
    tm = x_ref.shape[0]

    @pl.when(pl.program_id(0) == 0)
    def _():
        carry_ref[...] = jnp.zeros_like(carry_ref)

    def token_order(which, ref, d):
        for r in range(d):
            for c in range(A_WIDTH // 128):
                perm_ref[which, c, pl.ds(r, tm // d, stride=d), :] = (
                    ref[:, r * A_WIDTH + c * 128:r * A_WIDTH + (c + 1) * 128])
        return jnp.concatenate([perm_ref[which, c] for c in range(A_WIDTH // 128)], axis=1)

    l1 = l1_ref[...]
    l2, l3 = token_order(0, l2_ref, 4), token_order(1, l3_ref, 16)
    o2, o3 = token_order(2, o2_ref, 4), token_order(3, o3_ref, 16)
    mx = jnp.maximum(jnp.maximum(l1, l2), l3)
    e1, e2, e3 = jnp.exp(l1 - mx), jnp.exp(l2 - mx), jnp.exp(l3 - mx)
    oa = (e1 * o1_ref[...] + e2 * o2 + e3 * o3) / (e1 + e2 + e3)
    ya = jnp.dot(oa.astype(BF16), wa_ref[...], preferred_element_type=F32)
    yb = jnp.dot(ob_ref[...], wb_ref[...], preferred_element_type=F32)
    merged = gate_ref[:, :D_MODEL].astype(F32) * ya + gate_ref[:, D_MODEL:].astype(F32) * yb
    x1 = x_ref[...] + jnp.dot(merged.astype(BF16), wo_ref[...], preferred_element_type=F32)
    x1_ref[...] = x1
    h2 = x1 * lax.rsqrt(jnp.mean(x1 * x1, axis=-1, keepdims=True) + EPS) * g2_ref[...]
    hi = h2.astype(BF16)
    h2_ref[...] = _pack_rows(h2)
    lo = (h2 - hi.astype(F32)).astype(BF16)
    logits = jnp.dot(hi, wrh_ref[...], preferred_element_type=F32)
    logits = logits + jnp.dot(lo, wrh_ref[...], preferred_element_type=F32)
    logits = logits + jnp.dot(hi, wrl_ref[...], preferred_element_type=F32)
    logits = logits + br_ref[...]

    lane = lax.broadcasted_iota(jnp.int32, (tm, LOGIT_PAD), 1)
    work = logits
    sel = jnp.zeros((tm, LOGIT_PAD), F32)
    idx_cols, val_cols = [], []
    for _ in range(TOP_K):
        mx = jnp.max(work, axis=-1, keepdims=True)
        idx = jnp.min(jnp.where(work == mx, lane, LOGIT_PAD), axis=-1, keepdims=True)
        hit = lane == idx
        sel = sel + hit.astype(F32)
        work = jnp.where(hit, 2.0 * NEG, work)
        idx_cols.append(idx)
        val_cols.append(mx)
    exps = [jnp.exp(v - val_cols[0]) for v in val_cols]
    denom = exps[0] + exps[1] + exps[2] + exps[3]
    gate_cols = [e / denom for e in exps]

    before = jnp.dot(tri_ref[...], sel.astype(BF16), preferred_element_type=F32) + carry_ref[0:1, :]
    rank_cols = [jnp.sum(jnp.where(lane == idx, before, 0.0), axis=-1, keepdims=True) for idx in idx_cols]
    carry_ref[...] = carry_ref[...] + jnp.sum(sel, axis=0, keepdims=True)
    count_ref[...] = carry_ref[...]

    cols = [c.astype(F32) for c in idx_cols] + rank_cols + gate_cols
    route = jnp.zeros((tm, LOGIT_PAD), F32)
    for c, col in enumerate(cols):
        route = jnp.where(lane == c, col, route)
    route_ref[...] = route


def _outproj(o1, o2, o3, l1, l2, l3, ob, gates, x2, wa, wb, wo, g2, wrh, wrl, br, tm=512):
    T = x2.shape[0]
    row = lambda i: (i, 0)
    fix = lambda i: (0, 0)
    full = lambda a: pl.BlockSpec(a.shape, fix)
    rows = lambda a: pl.BlockSpec((tm * a.shape[0] // T, a.shape[1]), row)
    tri = (jnp.arange(tm)[:, None] > jnp.arange(tm)[None, :]).astype(BF16)
    out_shape = (jax.ShapeDtypeStruct((T, D_MODEL), F32),
                 jax.ShapeDtypeStruct((T, HALF_D), jnp.int32),
                 jax.ShapeDtypeStruct((T, LOGIT_PAD), F32),
                 jax.ShapeDtypeStruct((8, LOGIT_PAD), F32))
    return pl.pallas_call(
        _outproj_kernel,
        grid=(T // tm,),
        in_specs=[rows(o1), rows(o2), rows(o3), rows(l1), rows(l2), rows(l3), rows(ob), rows(gates),
                  rows(x2), full(wa), full(wb), full(wo), full(g2), full(wrh), full(wrl), full(br), full(tri)],
        out_specs=[pl.BlockSpec((tm, D_MODEL), row), pl.BlockSpec((tm, HALF_D), row),
                   pl.BlockSpec((tm, LOGIT_PAD), row), pl.BlockSpec((8, LOGIT_PAD), fix)],
        out_shape=out_shape,
        scratch_shapes=[pltpu.VMEM((4, A_WIDTH // 128, tm, 128), F32), pltpu.VMEM((8, LOGIT_PAD), F32)],
        compiler_params=_cparams("arbitrary"),
        name="outproj",
    )(o1, o2, o3, l1, l2, l3, ob, gates, x2, wa, wb, wo, g2, wrh, wrl, br, tri)


HALF_D = D_MODEL // 2


def _pack_rows(x):
    lo = pltpu.bitcast(x[:, :HALF_D].astype(BF16).astype(F32), jnp.uint32)
    hi = pltpu.bitcast(x[:, HALF_D:].astype(BF16).astype(F32), jnp.uint32)
    word = jnp.bitwise_or(lax.shift_right_logical(lo, jnp.uint32(16)), jnp.bitwise_and(hi, jnp.uint32(0xFFFF0000)))
    return pltpu.bitcast(word, jnp.int32)


def _unpack_rows(w):
    u = pltpu.bitcast(w, jnp.uint32)
    lo = pltpu.bitcast(lax.shift_left(u, jnp.uint32(16)), F32)
    hi = pltpu.bitcast(jnp.bitwise_and(u, jnp.uint32(0xFFFF0000)), F32)
    return jnp.concatenate([lo, hi], axis=1)


SC_WINDOW = 128


def _gather_rows(table, idx):
    M = idx.shape[0]
    width = table.shape[1]
    mesh = plsc.VectorSubcoreMesh(core_axis_name="core", subcore_axis_name="subcore")

    @functools.partial(pl.kernel, out_type=jax.ShapeDtypeStruct((M, width), table.dtype), mesh=mesh,
                       name="gather_rows")
    def gather(table_hbm, idx_hbm, out_hbm):
        def body(idx_vmem, out_vmem):
            pltpu.sync_copy(table_hbm.at[idx_vmem.at[0]], out_vmem)

        pltpu.emit_pipeline(
            body,
            grid=(M // SC_WINDOW,),
            in_specs=[pl.BlockSpec((1, SC_WINDOW), index_map=lambda i: (0, i))],
            out_specs=[pl.BlockSpec((SC_WINDOW, width), index_map=lambda i: (i, 0),
                                    pipeline_mode=pl.Buffered(1))],
            core_axis_name=("core", "subcore"),
            dimension_semantics=(pltpu.PARALLEL,),
        )(idx_hbm, out_hbm)

    return gather(table, idx.reshape(1, M))


def _moe_kernel(be_ref, nu_ref, x_ref, wgu_ref, bgu_ref, wd_ref, bd_ref, y_ref, wgu_bf, wd_bf):
    i = pl.program_id(0)

    @pl.when(i >= nu_ref[0])
    def _():
        y_ref[...] = jnp.zeros_like(y_ref)

    @pl.when(jnp.logical_or(i == 0, be_ref[i] != be_ref[jnp.maximum(i - 1, 0)]))
    def _():
        rows = 128

        def cast(c, carry):
            r0 = pl.multiple_of(c * rows, rows)
            wgu_bf[pl.ds(r0, rows), :] = wgu_ref[0, pl.ds(r0, rows), :].astype(BF16)
            wd_bf[pl.ds(r0, rows), :] = wd_ref[0, pl.ds(r0, rows), :].astype(BF16)
            return carry

        lax.fori_loop(0, D_MODEL // rows, cast, 0)

    @pl.when(i < nu_ref[0])
    def _():
        x = _unpack_rows(x_ref[...]).astype(BF16)
        gu = jnp.dot(x, wgu_bf[...], preferred_element_type=F32) + bgu_ref[0]
        gate = jnp.minimum(gu[:, :D_MODEL], SWIGLU_LIMIT)
        up = jnp.clip(gu[:, D_MODEL:], -SWIGLU_LIMIT, SWIGLU_LIMIT)
        act = (up + 1.0) * (gate * jax.nn.sigmoid(SWIGLU_ALPHA * gate))
        y = jnp.dot(act.astype(BF16), wd_bf[...], preferred_element_type=F32) + bd_ref[0]
        y_ref[...] = _pack_rows(y)


def _moe_experts(xb, blk_e, n_used, wgu, bgu, wd, bd, bm):
    NP = xb.shape[0]
    nblk = NP // bm
    grid_spec = pltpu.PrefetchScalarGridSpec(
        num_scalar_prefetch=2,
        grid=(nblk,),
        in_specs=[pl.BlockSpec((bm, HALF_D), lambda i, be, nu: (jnp.minimum(i, nu[0] - 1), 0)),
                  pl.BlockSpec((1, D_MODEL, 2 * D_MODEL), lambda i, be, nu: (be[i], 0, 0)),
                  pl.BlockSpec((1, 1, 2 * D_MODEL), lambda i, be, nu: (be[i], 0, 0)),
                  pl.BlockSpec((1, D_MODEL, D_MODEL), lambda i, be, nu: (be[i], 0, 0)),
                  pl.BlockSpec((1, 1, D_MODEL), lambda i, be, nu: (be[i], 0, 0))],
        out_specs=pl.BlockSpec((bm, HALF_D), lambda i, be, nu: (i, 0)),
        scratch_shapes=[pltpu.VMEM((D_MODEL, 2 * D_MODEL), BF16), pltpu.VMEM((D_MODEL, D_MODEL), BF16)],
    )
    return pl.pallas_call(
        _moe_kernel,
        grid_spec=grid_spec,
        out_shape=jax.ShapeDtypeStruct((NP, HALF_D), jnp.int32),
        compiler_params=_cparams("arbitrary"),
        name="moe_experts",
    )(blk_e, n_used, xb, wgu, bgu, wd, bd)


def _final_kernel(x1_ref, y_ref, w_ref, g_ref, o_ref):
    w = w_ref[...]
    acc = x1_ref[...]
    for k in range(TOP_K):
        acc = acc + w[:, k:k + 1] * _unpack_rows(y_ref[k])
    o_ref[...] = acc * lax.rsqrt(jnp.mean(acc * acc, axis=-1, keepdims=True) + EPS) * g_ref[...]


def _final(x1, yk, wk, g, tm=512):
    T = x1.shape[0]
    return pl.pallas_call(
        _final_kernel,
        grid=(T // tm,),
        in_specs=[pl.BlockSpec((tm, D_MODEL), lambda i: (i, 0)),
                  pl.BlockSpec((TOP_K, tm, HALF_D), lambda i: (0, i, 0)),
                  pl.BlockSpec((tm, TOP_K), lambda i: (i, 0)),
                  pl.BlockSpec((1, D_MODEL), lambda i: (0, 0))],
        out_specs=pl.BlockSpec((tm, D_MODEL), lambda i: (i, 0)),
        out_shape=jax.ShapeDtypeStruct((T, D_MODEL), F32),
        compiler_params=_cparams("parallel"),
        name="final_norm",
    )(x1, yk, wk, g)


def _pack_w_in(w):
    kr = w[:, 1920:1952]
    half = B_ROPE // 2
    z = lambda n: jnp.zeros((w.shape[0], n), w.dtype)
    kr_placed = jnp.concatenate([z(B_NOPE), kr, z(HEAD_PAD - B_NOPE - B_ROPE)], axis=1)
    kr_rot = jnp.concatenate([z(B_NOPE), -kr[:, half:], kr[:, :half], z(HEAD_PAD - B_NOPE - B_ROPE)], axis=1)
    return jnp.concatenate([w[:, :1920], w[:, 1952:4000], kr_placed, kr_rot], axis=1).astype(BF16)


def _pack_w_uq(w):
    half = B_ROPE // 2
    r = w.shape[0]
    nope, x1, x2 = w[..., :B_NOPE], w[..., B_NOPE:B_NOPE + half], w[..., B_NOPE + half:]
    zpad = jnp.zeros((r, B_HEADS, HEAD_PAD - B_NOPE - B_ROPE), w.dtype)
    main = jnp.concatenate([nope, x1, x2, zpad], axis=-1)
    rot = jnp.concatenate([jnp.zeros_like(nope), -x2, x1, zpad], axis=-1)
    return (main.reshape(r, B_HEADS * HEAD_PAD).astype(BF16), rot.reshape(r, B_HEADS * HEAD_PAD).astype(BF16))


def _pack_w_uk(w):
    r = w.shape[0]
    zpad = jnp.zeros((r, B_HEADS, HEAD_PAD - B_NOPE), w.dtype)
    return jnp.concatenate([w, zpad], axis=-1).reshape(r, B_HEADS * HEAD_PAD).astype(BF16)


def _rope_tables(seq):
    half = B_ROPE // 2
    pos = jnp.arange(seq, dtype=F32)
    inv_freq = ROPE_THETA ** (-jnp.arange(0, B_ROPE, 2, dtype=F32) / B_ROPE)
    ang = pos[:, None] * inv_freq[None, :]
    cos, sin = jnp.cos(ang), jnp.sin(ang)
    one = jnp.ones((seq, B_NOPE), F32)
    zero = jnp.zeros((seq, B_NOPE), F32)
    zpad = jnp.zeros((seq, HEAD_PAD - B_NOPE - B_ROPE), F32)
    qs = ((B_NOPE + B_ROPE) ** -0.5) * LOG2E
    cq = jnp.concatenate([one, cos, cos, zpad], axis=1) * qs
    sq = jnp.concatenate([zero, sin, sin, zpad], axis=1) * qs
    ck = jnp.concatenate([zero, cos, cos, zpad], axis=1)
    sk = jnp.concatenate([zero, sin, sin, zpad], axis=1)
    return cq, sq, ck, sk


def kernel(x, norm1_g, w_in, q_a_norm_g, kv_a_norm_g, w_uq, w_uk, w_uv, w_a_out, w_b_out, w_o,
           norm2_g, w_router, b_router, w_gate_up, b_gate_up, w_down, b_down, final_g):
    B, S, D = x.shape
    T = B * S
    depth = norm1_g.shape[0]
    xt = x.reshape(T, D)
    cq_t, sq_t, ck_t, sk_t = _rope_tables(S)
    biases = [_dilated_bias(d) for _, d in A_PATTERNS]
    bm = 512
    for l in range(depth):
        wq1, wq2 = _pack_w_uq(w_uq[l])
        qa, ka, va, qa4, ka4, va4, qa16, ka16, va16, qb, kb, vb, gates = _inproj(
            xt, norm1_g[l][None], _pack_w_in(w_in[l]), q_a_norm_g[l][None], kv_a_norm_g[l][None],
            wq1, wq2, _pack_w_uk(w_uk[l]), w_uv[l].reshape(KV_LORA, B_WIDTH).T.astype(BF16),
            cq_t, sq_t, ck_t, sk_t, S)
        qkv = ((qa, ka, va), (qa4, ka4, va4), (qa16, ka16, va16))
        outs = [_dilated_pattern(*qkv[p], biases[p], B, S, A_PATTERNS[p][1]) for p in range(3)]
        ob = _mla_attention(qb.reshape(B, S, -1), kb.reshape(B, S, -1), vb, B, S)

        wr = jnp.pad(w_router[l], ((0, 0), (0, LOGIT_PAD - N_EXPERTS)))
        wrh = wr.astype(BF16)
        wrl = (wr - wrh.astype(F32)).astype(BF16)
        br = jnp.pad(b_router[l], (0, LOGIT_PAD - N_EXPERTS), constant_values=NEG)[None]
        x1, h2, route, count_rows = _outproj(
            outs[0][0], outs[1][0], outs[2][0], outs[0][1], outs[1][1], outs[2][1],
            ob.reshape(T, B_WIDTH), gates, xt,
            w_a_out[l].astype(BF16), w_b_out[l].astype(BF16), w_o[l].astype(BF16),
            norm2_g[l][None], wrh, wrl, br)

        top_i = route[:, 0:TOP_K].astype(jnp.int32)
        rank = route[:, TOP_K:2 * TOP_K].astype(jnp.int32)
        gate_w = route[:, 2 * TOP_K:3 * TOP_K]
        counts = count_rows[0, :N_EXPERTS].astype(jnp.int32)
        padded = (counts + bm - 1) // bm * bm
        pend = jnp.cumsum(padded)
        pstart = pend - padded
        dest = pstart[top_i] + rank
        NP = T * TOP_K + N_EXPERTS * bm
        nblk = NP // bm
        tok = jnp.broadcast_to(jnp.arange(T, dtype=jnp.int32)[:, None], (T, TOP_K))
        row_tok = (jnp.arange(NP, dtype=jnp.int32) % T).at[dest.reshape(-1)].set(tok.reshape(-1))
        blk_start = jnp.arange(nblk, dtype=jnp.int32) * bm
        blk_e = jnp.minimum(jnp.sum(pend[None, :] <= blk_start[:, None], axis=1), N_EXPERTS - 1).astype(jnp.int32)
        n_used = (pend[-1] // bm).astype(jnp.int32)[None]

        xb = _gather_rows(h2, row_tok)
        yb = _moe_experts(xb, blk_e, n_used, w_gate_up[l], b_gate_up[l][:, None, :],
                          w_down[l], b_down[l][:, None, :], bm)
        yk = _gather_rows(yb, dest.T.reshape(-1)).reshape(TOP_K, T, HALF_D)
        if l + 1 < depth:
            raise NotImplementedError("depth > 1")
        out = _final(x1, yk, gate_w, final_g[None])
    return out.reshape(B, S, D)
```

```python
import functools
import math

import jax
import jax.numpy as jnp
from jax import lax
from jax.experimental import pallas as pl
from jax.experimental.pallas import tpu as pltpu
from jax.experimental.pallas import tpu_sc as plsc

F32 = jnp.float32
BF16 = jnp.bfloat16

D_MODEL = 1024
EPS = 1e-5
NEG = -1e30

A_HEADS = 8
A_HEAD_DIM = 64
A_WIDTH = 512
A_PATTERNS = ((128, 1), (512, 4), (2048, 16))
A_HALF = 64
A_QB = 128
A_WIN = A_QB + 2 * A_HALF
assert all(w // (2 * d) == A_HALF for w, d in A_PATTERNS) and [d for _, d in A_PATTERNS] == [1, 4, 16]

B_HEADS = 8
B_NOPE = 64
B_ROPE = 32
B_V = 64
B_WIDTH = 512
Q_LORA = 256
KV_LORA = 128
ROPE_THETA = 10000.0
HEAD_PAD = 128
ONES_ROWS = 16
MLA_SCORE_BUFS = 4

N_EXPERTS = 32
TOP_K = 4
SWIGLU_LIMIT = 7.0
SWIGLU_ALPHA = 1.702
LOGIT_PAD = 128

LOG2E = 1.4426950408889634
LN2 = 0.6931471805599453

VMEM_LIMIT = 56 * 1024 * 1024


def _cparams(*sem):
    return pltpu.CompilerParams(dimension_semantics=sem, vmem_limit_bytes=VMEM_LIMIT)


def _inproj_kernel(x_ref, g1_ref, wp_ref, gq_ref, gkv_ref, wq1_ref, wq2_ref, wuk_ref, wuv_ref,
                   cq_ref, sq_ref, ck_ref, sk_ref,
                   qa_ref, ka_ref, va_ref, qa4_ref, ka4_ref, va4_ref, qa16_ref, ka16_ref, va16_ref,
                   qb_ref, kb_ref, vb_ref, gate_ref, perm_ref):
    tm = x_ref.shape[0]
    x = x_ref[...]
    h = x * lax.rsqrt(jnp.mean(x * x, axis=-1, keepdims=True) + EPS) * g1_ref[...]
    h = h.astype(BF16)

    def proj(c0, c1):
        return jnp.dot(h, wp_ref[:, c0:c1], preferred_element_type=F32)

    def emit(which, z, nat_ref, views):
        nat_ref[...] = z.astype(BF16)
        for c in range(A_WIDTH // 128):
            perm_ref[which, c] = z[:, c * 128:(c + 1) * 128]
        for d, ref in views:
            for r in range(d):
                for c in range(A_WIDTH // 128):
                    ref[:, r * A_WIDTH + c * 128:r * A_WIDTH + (c + 1) * 128] = (
                        perm_ref[which, c, pl.ds(r, tm // d, stride=d), :].astype(BF16))

    emit(0, proj(0, 512) * (A_HEAD_DIM ** -0.5), qa_ref, ((4, qa4_ref), (16, qa16_ref)))
    emit(1, proj(512, 1024), ka_ref, ((4, ka4_ref), (16, ka16_ref)))
    emit(2, proj(1024, 1536), va_ref, ((4, va4_ref), (16, va16_ref)))

    cq = proj(1536, 1792)
    cq = cq * lax.rsqrt(jnp.mean(cq * cq, axis=-1, keepdims=True) + EPS) * gq_ref[...]
    cq = cq.astype(BF16)
    q_main = jnp.dot(cq, wq1_ref[...], preferred_element_type=F32)
    q_rot = jnp.dot(cq, wq2_ref[...], preferred_element_type=F32)
    cq_t = cq_ref[...]
    sq_t = sq_ref[...]
    for hd in range(B_HEADS):
        sl = slice(hd * HEAD_PAD, (hd + 1) * HEAD_PAD)
        qb_ref[:, sl] = (q_main[:, sl] * cq_t + q_rot[:, sl] * sq_t).astype(BF16)

    ckv = proj(1792, 1920)
    ckv = ckv * lax.rsqrt(jnp.mean(ckv * ckv, axis=-1, keepdims=True) + EPS) * gkv_ref[...]
    ckv = ckv.astype(BF16)
    k_nope = jnp.dot(ckv, wuk_ref[...], preferred_element_type=F32)
    vb_ref[...] = lax.dot_general(wuv_ref[...], ckv, (((1,), (1,)), ((), ())),
                                  preferred_element_type=F32).astype(BF16)
    k_rope = proj(3968, 4096) * ck_ref[...] + proj(4096, 4224) * sk_ref[...]
    for hd in range(B_HEADS):
        sl = slice(hd * HEAD_PAD, (hd + 1) * HEAD_PAD)
        kb_ref[:, sl] = (k_nope[:, sl] + k_rope).astype(BF16)

    gate_ref[...] = jax.nn.sigmoid(proj(1920, 3968)).astype(BF16)


def _inproj(x2, g1, wp, gq, gkv, wq1, wq2, wuk, wuv, cq_t, sq_t, ck_t, sk_t, seq, tm=512):
    T = x2.shape[0]
    nseq = seq // tm
    row = lambda i: (i, 0)
    fix = lambda i: (0, 0)
    tab = lambda i: (i % nseq, 0)
    full = lambda a: pl.BlockSpec(a.shape, fix)
    sds = lambda r, c: jax.ShapeDtypeStruct((r, c), BF16)
    out_shape = (
        sds(T, A_WIDTH), sds(T, A_WIDTH), sds(T, A_WIDTH),
        sds(T // 4, 4 * A_WIDTH), sds(T // 4, 4 * A_WIDTH), sds(T // 4, 4 * A_WIDTH),
        sds(T // 16, 16 * A_WIDTH), sds(T // 16, 16 * A_WIDTH), sds(T // 16, 16 * A_WIDTH),
        sds(T, B_HEADS * HEAD_PAD), sds(T, B_HEADS * HEAD_PAD),
        sds(B_WIDTH, T),
        sds(T, 2 * D_MODEL),
    )
    out_specs = [pl.BlockSpec((tm * s.shape[0] // T, s.shape[1]), row) for s in out_shape]
    out_specs[11] = pl.BlockSpec((B_WIDTH, tm), lambda i: (0, i))
    return pl.pallas_call(
        _inproj_kernel,
        grid=(T // tm,),
        in_specs=[pl.BlockSpec((tm, D_MODEL), row), full(g1), full(wp), full(gq), full(gkv),
                  full(wq1), full(wq2), full(wuk), full(wuv),
                  pl.BlockSpec((tm, HEAD_PAD), tab), pl.BlockSpec((tm, HEAD_PAD), tab),
                  pl.BlockSpec((tm, HEAD_PAD), tab), pl.BlockSpec((tm, HEAD_PAD), tab)],
        out_specs=out_specs,
        out_shape=out_shape,
        scratch_shapes=[pltpu.VMEM((3, A_WIDTH // 128, tm, 128), F32)],
        compiler_params=_cparams("parallel"),
        name="inproj",
    )(x2, g1, wp, gq, gkv, wq1, wq2, wuk, wuv, cq_t, sq_t, ck_t, sk_t)


def _dilated_kernel(q_ref, kc_ref, kp_ref, kn_ref, vc_ref, vp_ref, vn_ref, bias_ref,
                    o_ref, lse_ref, kbuf, vbuf, *, rows):
    i = pl.program_id(2)
    last = pl.num_programs(2) - 1
    kbuf[0:A_HALF, :] = kp_ref[0]
    kbuf[A_HALF:A_HALF + rows, :] = kc_ref[0]
    kbuf[A_HALF + rows:, :] = kn_ref[0]
    vbuf[0:A_HALF, :] = vp_ref[0]
    vbuf[A_HALF:A_HALF + rows, :] = vc_ref[0]
    vbuf[A_HALF + rows:, :] = vn_ref[0]

    nj = rows // A_QB
    npair = A_HEADS // 2
    lane = lax.broadcasted_iota(jnp.int32, (A_QB, 128), 1)
    lo_half = lane < A_HEAD_DIM

    def body(j, carry):
        r0 = pl.multiple_of(j * A_QB, A_QB)
        variant = (jnp.logical_and(i == 0, j == 0).astype(jnp.int32)
                   + 2 * jnp.logical_and(i == last, j == nj - 1).astype(jnp.int32))
        scores = []
        for pr in range(npair):
            ls = slice(pr * 128, (pr + 1) * 128)
            qp = q_ref[0, pl.ds(r0, A_QB), ls]
            zero = jnp.zeros_like(qp)
            q2 = jnp.concatenate([jnp.where(lo_half, qp, zero), jnp.where(lo_half, zero, qp)], axis=0)
            s = lax.dot_general(q2, kbuf[pl.ds(r0, A_WIN), ls], (((1,), (1,)), ((), ())),
                                preferred_element_type=F32)
            scores.append(s + bias_ref[variant * npair + pr])
        probs = []
        for s in scores:
            m = jnp.max(s, axis=-1, keepdims=True)
            p = jnp.exp(s - m)
            l = jnp.sum(p, axis=-1, keepdims=True)
            probs.append((p.astype(BF16), l, m + jnp.log(l)))
        for pr, (p, l, lse) in enumerate(probs):
            ls = slice(pr * 128, (pr + 1) * 128)
            o = jnp.dot(p, vbuf[pl.ds(r0, A_WIN), ls], preferred_element_type=F32) / l
            lse = jnp.broadcast_to(lse, (2 * A_QB, 128))
            o_ref[0, pl.ds(r0, A_QB), ls] = jnp.where(lo_half, o[:A_QB], o[A_QB:])
            lse_ref[0, pl.ds(r0, A_QB), ls] = jnp.where(lo_half, lse[:A_QB], lse[A_QB:])
        return carry

    lax.fori_loop(0, nj, body, 0)


def _dilated_pattern(qa, ka, va, bias, batch, seq, dil, rows=1024):
    L = seq // dil
    rows = min(rows, L)
    nblk = L // rows
    hb = rows // A_HALF
    nh = L // A_HALF
    view = lambda a: a.reshape(batch, L, dil * A_WIDTH)
    cur = lambda b, r, i: (b, i, r)
    prev = lambda b, r, i: (b, jnp.maximum(i * hb - 1, 0), r)
    nxt = lambda b, r, i: (b, jnp.minimum((i + 1) * hb, nh - 1), r)
    cur_spec = pl.BlockSpec((1, rows, A_WIDTH), cur)
    out_sds = jax.ShapeDtypeStruct((batch, L, dil * A_WIDTH), F32)
    o, lse = pl.pallas_call(
        functools.partial(_dilated_kernel, rows=rows),
        grid=(batch, dil, nblk),
        in_specs=[cur_spec, cur_spec,
                  pl.BlockSpec((1, A_HALF, A_WIDTH), prev), pl.BlockSpec((1, A_HALF, A_WIDTH), nxt),
                  cur_spec,
                  pl.BlockSpec((1, A_HALF, A_WIDTH), prev), pl.BlockSpec((1, A_HALF, A_WIDTH), nxt),
                  pl.BlockSpec(bias.shape, lambda b, r, i: (0, 0, 0))],
        out_specs=[cur_spec, cur_spec],
        out_shape=(out_sds, out_sds),
        scratch_shapes=[pltpu.VMEM((rows + 2 * A_HALF, A_WIDTH), BF16),
                        pltpu.VMEM((rows + 2 * A_HALF, A_WIDTH), BF16)],
        compiler_params=_cparams("parallel", "parallel", "parallel"),
        name=f"dilated_d{dil}",
    )(view(qa), view(ka), view(ka), view(ka), view(va), view(va), view(va), bias)
    return o.reshape(batch * L, dil * A_WIDTH), lse.reshape(batch * L, dil * A_WIDTH)


def _dilated_bias(dil):
    slopes = 2.0 ** (-8.0 * (jnp.arange(A_HEADS, dtype=F32) + 1.0) / A_HEADS)
    col = jnp.arange(A_WIN)[None, :]
    rel = (col - A_HALF) - jnp.arange(A_QB)[:, None]
    dist = (jnp.abs(rel) * dil).astype(F32)
    bias = -slopes[:, None, None] * dist[None]
    bias = jnp.where((jnp.abs(rel) <= A_HALF)[None], bias, NEG)
    variants = []
    for v in range(4):
        ok = jnp.ones((1, A_WIN), bool)
        if v & 1:
            ok = jnp.logical_and(ok, col >= A_HALF)
        if v & 2:
            ok = jnp.logical_and(ok, col < A_HALF + A_QB)
        variants.append(jnp.where(ok[None], bias, NEG).reshape(A_HEADS // 2, 2 * A_QB, A_WIN))
    return jnp.concatenate(variants, axis=0)


def _mla_kernel(q_ref, k_ref, vt_ref, o_ref, *s_bufs, tk):
    tq = q_ref.shape[1]
    nk = k_ref.shape[1] // tk
    qs = [q_ref[0, :, hh * HEAD_PAD:(hh + 1) * HEAD_PAD] for hh in range(2)]

    def scores(j, s_ref):
        off = pl.multiple_of(j * tk, tk)
        for hh in range(2):
            s_ref[hh] = lax.dot_general(k_ref[0, pl.ds(off, tk), hh * HEAD_PAD:(hh + 1) * HEAD_PAD], qs[hh],
                                        (((1,), (1,)), ((), ())), preferred_element_type=F32)

    ones = jnp.ones((ONES_ROWS, tk), BF16)

    def softmax_pv(j, s_ref, stats):
        off = pl.multiple_of(j * tk, tk)
        new = []
        for hh in range(2):
            m, acc = stats[hh]
            s = s_ref[hh]
            vt = jnp.concatenate([vt_ref[hh * B_V:(hh + 1) * B_V, pl.ds(off, tk)], ones], axis=0)
            m_new = jnp.maximum(m, jnp.max(s, axis=0, keepdims=True))
            alpha = jnp.exp2(m - m_new)
            p = jnp.exp2(s - m_new)
            acc = alpha * acc + jnp.dot(vt, p.astype(BF16), preferred_element_type=F32)
            new.append((m_new, acc))
        return tuple(new)

    nbuf = len(s_bufs)

    def body(jj, stats):
        j = nbuf * jj
        for u in range(nbuf):
            scores(jnp.minimum(j + u + 1, nk - 1), s_bufs[(u + 1) % nbuf])
            stats = softmax_pv(j + u, s_bufs[u], stats)
        return stats

    init = (jnp.full((1, tq), NEG, F32), jnp.zeros((B_V + ONES_ROWS, tq), F32))
    scores(0, s_bufs[0])
    (_, acc0), (_, acc1) = lax.fori_loop(0, nk // nbuf, body, (init, init))
    o_t = jnp.concatenate([a[:B_V] / a[B_V:B_V + 1] for a in (acc0, acc1)], axis=0)
    o_ref[0] = o_t.T.astype(BF16)


def _mla_attention(qb, kb, vbt, batch, seq, tq=256, tk=512):
    return pl.pallas_call(
        functools.partial(_mla_kernel, tk=tk),
        grid=(batch, B_HEADS // 2, seq // tq),
        in_specs=[pl.BlockSpec((1, tq, 2 * HEAD_PAD), lambda b, p, i: (b, i, p)),
                  pl.BlockSpec((1, seq, 2 * HEAD_PAD), lambda b, p, i: (b, 0, p)),
                  pl.BlockSpec((2 * B_V, seq), lambda b, p, i: (p, b))],
        out_specs=pl.BlockSpec((1, tq, 128), lambda b, p, i: (b, i, p)),
        out_shape=jax.ShapeDtypeStruct((batch, seq, B_WIDTH), BF16),
        scratch_shapes=[pltpu.VMEM((2, tk, tq), F32) for _ in range(MLA_SCORE_BUFS)],
        compiler_params=_cparams("parallel", "parallel", "parallel"),
        name="mla_attention",
    )(qb, kb, vbt)


def _outproj_kernel(o1_ref, o2_ref, o3_ref, l1_ref, l2_ref, l3_ref, ob_ref, gate_ref, x_ref,
                    wa_ref, wb_ref, wo_ref, g2_ref, wrh_ref, wrl_ref, br_ref,
                    tri_ref, x1_ref, h2_ref, route_ref, count_ref, perm_ref, carry_ref):
    tm = x_ref.shape[0]

    @pl.when(pl.program_id(0) == 0)
    def _():
        carry_ref[...] = jnp.zeros_like(carry_ref)

    def token_order(which, ref, d):
        for r in range(d):
            for c in range(A_WIDTH // 128):
                perm_ref[which, c, pl.ds(r, tm // d, stride=d), :] = (
                    ref[:, r * A_WIDTH + c * 128:r * A_WIDTH + (c + 1) * 128])
        return jnp.concatenate([perm_ref[which, c] for c in range(A_WIDTH // 128)], axis=1)

    l1 = l1_ref[...]
    l2, l3 = token_order(0, l2_ref, 4), token_order(1, l3_ref, 16)
    o2, o3 = token_order(2, o2_ref, 4), token_order(3, o3_ref, 16)
    mx = jnp.maximum(jnp.maximum(l1, l2), l3)
    e1, e2, e3 = jnp.exp(l1 - mx), jnp.exp(l2 - mx), jnp.exp(l3 - mx)
    oa = (e1 * o1_ref[...] + e2 * o2 + e3 * o3) / (e1 + e2 + e3)
    ya = jnp.dot(oa.astype(BF16), wa_ref[...], preferred_element_type=F32)
    yb = jnp.dot(ob_ref[...], wb_ref[...], preferred_element_type=F32)
    merged = gate_ref[:, :D_MODEL].astype(F32) * ya + gate_ref[:, D_MODEL:].astype(F32) * yb
    x1 = x_ref[...] + jnp.dot(merged.astype(BF16), wo_ref[...], preferred_element_type=F32)
    x1_ref[...] = x1
    h2 = x1 * lax.rsqrt(jnp.mean(x1 * x1, axis=-1, keepdims=True) + EPS) * g2_ref[...]
    hi = h2.astype(BF16)
    h2_ref[...] = _pack_rows(h2)
    lo = (h2 - hi.astype(F32)).astype(BF16)
    logits = jnp.dot(hi, wrh_ref[...], preferred_element_type=F32)
    logits = logits + jnp.dot(lo, wrh_ref[...], preferred_element_type=F32)
    logits = logits + jnp.dot(hi, wrl_ref[...], preferred_element_type=F32)
    logits = logits + br_ref[...]

    lane = lax.broadcasted_iota(jnp.int32, (tm, LOGIT_PAD), 1)
    work = logits
    sel = jnp.zeros((tm, LOGIT_PAD), F32)
    idx_cols, val_cols = [], []
    for _ in range(TOP_K):
        mx = jnp.max(work, axis=-1, keepdims=True)
        idx = jnp.min(jnp.where(work == mx, lane, LOGIT_PAD), axis=-1, keepdims=True)
        hit = lane == idx
        sel = sel + hit.astype(F32)
        work = jnp.where(hit, 2.0 * NEG, work)
        idx_cols.append(idx)
        val_cols.append(mx)
    exps = [jnp.exp(v - val_cols[0]) for v in val_cols]
    denom = exps[0] + exps[1] + exps[2] + exps[3]
    gate_cols = [e / denom for e in exps]

    before = jnp.dot(tri_ref[...], sel.astype(BF16), preferred_element_type=F32) + carry_ref[0:1, :]
    rank_cols = [jnp.sum(jnp.where(lane == idx, before, 0.0), axis=-1, keepdims=True) for idx in idx_cols]
    carry_ref[...] = carry_ref[...] + jnp.sum(sel, axis=0, keepdims=True)
    count_ref[...] = carry_ref[...]

    cols = [c.astype(F32) for c in idx_cols] + rank_cols + gate_cols
    route = jnp.zeros((tm, LOGIT_PAD), F32)
    for c, col in enumerate(cols):
        route = jnp.where(lane == c, col, route)
    route_ref[...] = route


def _outproj(o1, o2, o3, l1, l2, l3, ob, gates, x2, wa, wb, wo, g2, wrh, wrl, br, tm=512):
    T = x2.shape[0]
    row = lambda i: (i, 0)
    fix = lambda i: (0, 0)
    full = lambda a: pl.BlockSpec(a.shape, fix)
    rows = lambda a: pl.BlockSpec((tm * a.shape[0] // T, a.shape[1]), row)
    tri = (jnp.arange(tm)[:, None] > jnp.arange(tm)[None, :]).astype(BF16)
    out_shape = (jax.ShapeDtypeStruct((T, D_MODEL), F32),
                 jax.ShapeDtypeStruct((T, HALF_D), jnp.int32),
                 jax.ShapeDtypeStruct((T, LOGIT_PAD), F32),
                 jax.ShapeDtypeStruct((8, LOGIT_PAD), F32))
    return pl.pallas_call(
        _outproj_kernel,
        grid=(T // tm,),
        in_specs=[rows(o1), rows(o2), rows(o3), rows(l1), rows(l2), rows(l3), rows(ob), rows(gates),
                  rows(x2), full(wa), full(wb), full(wo), full(g2), full(wrh), full(wrl), full(br), full(tri)],
        out_specs=[pl.BlockSpec((tm, D_MODEL), row), pl.BlockSpec((tm, HALF_D), row),
                   pl.BlockSpec((tm, LOGIT_PAD), row), pl.BlockSpec((8, LOGIT_PAD), fix)],
        out_shape=out_shape,
        scratch_shapes=[pltpu.VMEM((4, A_WIDTH // 128, tm, 128), F32), pltpu.VMEM((8, LOGIT_PAD), F32)],
        compiler_params=_cparams("arbitrary"),
        name="outproj",
    )(o1, o2, o3, l1, l2, l3, ob, gates, x2, wa, wb, wo, g2, wrh, wrl, br, tri)


HALF_D = D_MODEL // 2


def _pack_rows(x):
    lo = pltpu.bitcast(x[:, :HALF_D].astype(BF16).astype(F32), jnp.uint32)
    hi = pltpu.bitcast(x[:, HALF_D:].astype(BF16).astype(F32), jnp.uint32)
    word = jnp.bitwise_or(lax.shift_right_logical(lo, jnp.uint32(16)), jnp.bitwise_and(hi, jnp.uint32(0xFFFF0000)))
    return pltpu.bitcast(word, jnp.int32)


def _unpack_rows(w):
    u = pltpu.bitcast(w, jnp.uint32)
    lo = pltpu.bitcast(lax.shift_left(u, jnp.uint32(16)), F32)
    hi = pltpu.bitcast(jnp.bitwise_and(u, jnp.uint32(0xFFFF0000)), F32)
    return jnp.concatenate([lo, hi], axis=1)


SC_WINDOW = 128


def _gather_rows(table, idx):
    M = idx.shape[0]
    width = table.shape[1]
    mesh = plsc.VectorSubcoreMesh(core_axis_name="core", subcore_axis_name="subcore")

    @functools.partial(pl.kernel, out_type=jax.ShapeDtypeStruct((M, width), table.dtype), mesh=mesh,
                       name="gather_rows")
    def gather(table_hbm, idx_hbm, out_hbm):
        def body(idx_vmem, out_vmem):
            pltpu.sync_copy(table_hbm.at[idx_vmem.at[0]], out_vmem)

        pltpu.emit_pipeline(
            body,
            grid=(M // SC_WINDOW,),
            in_specs=[pl.BlockSpec((1, SC_WINDOW), index_map=lambda i: (0, i))],
            out_specs=[pl.BlockSpec((SC_WINDOW, width), index_map=lambda i: (i, 0),
                                    pipeline_mode=pl.Buffered(1))],
            core_axis_name=("core", "subcore"),
            dimension_semantics=(pltpu.PARALLEL,),
        )(idx_hbm, out_hbm)

    return gather(table, idx.reshape(1, M))


def _scatter_rows(rows, idx, n_out):
    M = idx.shape[0]
    R, width = rows.shape
    mesh = plsc.VectorSubcoreMesh(core_axis_name="core", subcore_axis_name="subcore")

    @functools.partial(pl.kernel, out_type=jax.ShapeDtypeStruct((n_out, width), rows.dtype), mesh=mesh,
                       name="scatter_rows")
    def scatter(rows_hbm, idx_hbm, out_hbm):
        def body(rows_vmem, idx_vmem):
            pltpu.sync_copy(rows_vmem, out_hbm.at[idx_vmem.at[0]])

        pltpu.emit_pipeline(
            body,
            grid=(M // SC_WINDOW,),
            in_specs=[pl.BlockSpec((SC_WINDOW, width), index_map=lambda i: (i % (R // SC_WINDOW), 0),
                                   pipeline_mode=pl.Buffered(1)),
                      pl.BlockSpec((1, SC_WINDOW), index_map=lambda i: (0, i))],
            out_specs=[],
            core_axis_name=("core", "subcore"),
            dimension_semantics=(pltpu.PARALLEL,),
        )(rows_hbm, idx_hbm)

    return scatter(rows, idx.reshape(1, M))


def _moe_kernel(be_ref, nu_ref, nv_ref, x_ref, wgu_ref, bgu_ref, wd_ref, bd_ref, y_ref, wgu_bf, wd_bf):
    i = pl.program_id(0)

    @pl.when(i >= nu_ref[0])
    def _():
        y_ref[...] = jnp.zeros_like(y_ref)

    @pl.when(jnp.logical_or(i == 0, be_ref[i] != be_ref[jnp.maximum(i - 1, 0)]))
    def _():
        rows = 128

        def cast(c, carry):
            r0 = pl.multiple_of(c * rows, rows)
            wgu_bf[pl.ds(r0, rows), :] = wgu_ref[0, pl.ds(r0, rows), :].astype(BF16)
            wd_bf[pl.ds(r0, rows), :] = wd_ref[0, pl.ds(r0, rows), :].astype(BF16)
            return carry

        lax.fori_loop(0, D_MODEL // rows, cast, 0)

    @pl.when(i < nu_ref[0])
    def _():
        live = lax.broadcasted_iota(jnp.int32, (x_ref.shape[0], 1), 0) < nv_ref[i]
        x = jnp.where(live, _unpack_rows(x_ref[...]), 0.0).astype(BF16)
        gu = jnp.dot(x, wgu_bf[...], preferred_element_type=F32) + bgu_ref[0]
        gate = jnp.minimum(gu[:, :D_MODEL], SWIGLU_LIMIT)
        up = jnp.clip(gu[:, D_MODEL:], -SWIGLU_LIMIT, SWIGLU_LIMIT)
        act = (up + 1.0) * (gate * jax.nn.sigmoid(SWIGLU_ALPHA * gate))
        y = jnp.dot(act.astype(BF16), wd_bf[...], preferred_element_type=F32) + bd_ref[0]
        y_ref[...] = _pack_rows(y)


def _moe_experts(xb, blk_e, n_used, n_valid, wgu, bgu, wd, bd, bm):
    NP = xb.shape[0]
    nblk = NP // bm
    grid_spec = pltpu.PrefetchScalarGridSpec(
        num_scalar_prefetch=3,
        grid=(nblk,),
        in_specs=[pl.BlockSpec((bm, HALF_D), lambda i, be, nu, nv: (jnp.minimum(i, nu[0] - 1), 0)),
                  pl.BlockSpec((1, D_MODEL, 2 * D_MODEL), lambda i, be, nu, nv: (be[i], 0, 0)),
                  pl.BlockSpec((1, 1, 2 * D_MODEL), lambda i, be, nu, nv: (be[i], 0, 0)),
                  pl.BlockSpec((1, D_MODEL, D_MODEL), lambda i, be, nu, nv: (be[i], 0, 0)),
                  pl.BlockSpec((1, 1, D_MODEL), lambda i, be, nu, nv: (be[i], 0, 0))],
        out_specs=pl.BlockSpec((bm, HALF_D), lambda i, be, nu, nv: (i, 0)),
        scratch_shapes=[pltpu.VMEM((D_MODEL, 2 * D_MODEL), BF16), pltpu.VMEM((D_MODEL, D_MODEL), BF16)],
    )
    return pl.pallas_call(
        _moe_kernel,
        grid_spec=grid_spec,
        out_shape=jax.ShapeDtypeStruct((NP, HALF_D), jnp.int32),
        compiler_params=_cparams("arbitrary"),
        name="moe_experts",
    )(blk_e, n_used, n_valid, xb, wgu, bgu, wd, bd)


def _final_kernel(x1_ref, y_ref, w_ref, g_ref, o_ref):
    w = w_ref[...]
    acc = x1_ref[...]
    for k in range(TOP_K):
        acc = acc + w[:, k:k + 1] * _unpack_rows(y_ref[k])
    o_ref[...] = acc * lax.rsqrt(jnp.mean(acc * acc, axis=-1, keepdims=True) + EPS) * g_ref[...]


def _final(x1, yk, wk, g, tm=512):
    T = x1.shape[0]
    return pl.pallas_call(
        _final_kernel,
        grid=(T // tm,),
        in_specs=[pl.BlockSpec((tm, D_MODEL), lambda i: (i, 0)),
                  pl.BlockSpec((TOP_K, tm, HALF_D), lambda i: (0, i, 0)),
                  pl.BlockSpec((tm, TOP_K), lambda i: (i, 0)),
                  pl.BlockSpec((1, D_MODEL), lambda i: (0, 0))],
        out_specs=pl.BlockSpec((tm, D_MODEL), lambda i: (i, 0)),
        out_shape=jax.ShapeDtypeStruct((T, D_MODEL), F32),
        compiler_params=_cparams("parallel"),
        name="final_norm",
    )(x1, yk, wk, g)


def _pack_w_in(w):
    kr = w[:, 1920:1952]
    half = B_ROPE // 2
    z = lambda n: jnp.zeros((w.shape[0], n), w.dtype)
    kr_placed = jnp.concatenate([z(B_NOPE), kr, z(HEAD_PAD - B_NOPE - B_ROPE)], axis=1)
    kr_rot = jnp.concatenate([z(B_NOPE), -kr[:, half:], kr[:, :half], z(HEAD_PAD - B_NOPE - B_ROPE)], axis=1)
    return jnp.concatenate([w[:, :1920], w[:, 1952:4000], kr_placed, kr_rot], axis=1).astype(BF16)


def _pack_w_uq(w):
    half = B_ROPE // 2
    r = w.shape[0]
    nope, x1, x2 = w[..., :B_NOPE], w[..., B_NOPE:B_NOPE + half], w[..., B_NOPE + half:]
    zpad = jnp.zeros((r, B_HEADS, HEAD_PAD - B_NOPE - B_ROPE), w.dtype)
    main = jnp.concatenate([nope, x1, x2, zpad], axis=-1)
    rot = jnp.concatenate([jnp.zeros_like(nope), -x2, x1, zpad], axis=-1)
    return (main.reshape(r, B_HEADS * HEAD_PAD).astype(BF16), rot.reshape(r, B_HEADS * HEAD_PAD).astype(BF16))


def _pack_w_uk(w):
    r = w.shape[0]
    zpad = jnp.zeros((r, B_HEADS, HEAD_PAD - B_NOPE), w.dtype)
    return jnp.concatenate([w, zpad], axis=-1).reshape(r, B_HEADS * HEAD_PAD).astype(BF16)


def _rope_tables(seq):
    half = B_ROPE // 2
    pos = jnp.arange(seq, dtype=F32)
    inv_freq = ROPE_THETA ** (-jnp.arange(0, B_ROPE, 2, dtype=F32) / B_ROPE)
    ang = pos[:, None] * inv_freq[None, :]
    cos, sin = jnp.cos(ang), jnp.sin(ang)
    one = jnp.ones((seq, B_NOPE), F32)
    zero = jnp.zeros((seq, B_NOPE), F32)
    zpad = jnp.zeros((seq, HEAD_PAD - B_NOPE - B_ROPE), F32)
    qs = ((B_NOPE + B_ROPE) ** -0.5) * LOG2E
    cq = jnp.concatenate([one, cos, cos, zpad], axis=1) * qs
    sq = jnp.concatenate([zero, sin, sin, zpad], axis=1) * qs
    ck = jnp.concatenate([zero, cos, cos, zpad], axis=1)
    sk = jnp.concatenate([zero, sin, sin, zpad], axis=1)
    return cq, sq, ck, sk


def kernel(x, norm1_g, w_in, q_a_norm_g, kv_a_norm_g, w_uq, w_uk, w_uv, w_a_out, w_b_out, w_o,
           norm2_g, w_router, b_router, w_gate_up, b_gate_up, w_down, b_down, final_g):
    B, S, D = x.shape
    T = B * S
    depth = norm1_g.shape[0]
    xt = x.reshape(T, D)
    cq_t, sq_t, ck_t, sk_t = _rope_tables(S)
    biases = [_dilated_bias(d) for _, d in A_PATTERNS]
    bm = 512
    for l in range(depth):
        wq1, wq2 = _pack_w_uq(w_uq[l])
        qa, ka, va, qa4, ka4, va4, qa16, ka16, va16, qb, kb, vb, gates = _inproj(
            xt, norm1_g[l][None], _pack_w_in(w_in[l]), q_a_norm_g[l][None], kv_a_norm_g[l][None],
            wq1, wq2, _pack_w_uk(w_uk[l]), w_uv[l].reshape(KV_LORA, B_WIDTH).T.astype(BF16),
            cq_t, sq_t, ck_t, sk_t, S)
        qkv = ((qa, ka, va), (qa4, ka4, va4), (qa16, ka16, va16))
        outs = [_dilated_pattern(*qkv[p], biases[p], B, S, A_PATTERNS[p][1]) for p in range(3)]
        ob = _mla_attention(qb.reshape(B, S, -1), kb.reshape(B, S, -1), vb, B, S)

        wr = jnp.pad(w_router[l], ((0, 0), (0, LOGIT_PAD - N_EXPERTS)))
        wrh = wr.astype(BF16)
        wrl = (wr - wrh.astype(F32)).astype(BF16)
        br = jnp.pad(b_router[l], (0, LOGIT_PAD - N_EXPERTS), constant_values=NEG)[None]
        x1, h2, route, count_rows = _outproj(
            outs[0][0], outs[1][0], outs[2][0], outs[0][1], outs[1][1], outs[2][1],
            ob.reshape(T, B_WIDTH), gates, xt,
            w_a_out[l].astype(BF16), w_b_out[l].astype(BF16), w_o[l].astype(BF16),
            norm2_g[l][None], wrh, wrl, br)

        top_i = route[:, 0:TOP_K].astype(jnp.int32)
        rank = route[:, TOP_K:2 * TOP_K].astype(jnp.int32)
        gate_w = route[:, 2 * TOP_K:3 * TOP_K]
        counts = count_rows[0, :N_EXPERTS].astype(jnp.int32)
        padded = (counts + bm - 1) // bm * bm
        pend = jnp.cumsum(padded)
        pstart = pend - padded
        dest = pstart[top_i] + rank
        NP = T * TOP_K + N_EXPERTS * bm
        nblk = NP // bm
        blk_start = jnp.arange(nblk, dtype=jnp.int32) * bm
        blk_e = jnp.minimum(jnp.sum(pend[None, :] <= blk_start[:, None], axis=1), N_EXPERTS - 1).astype(jnp.int32)
        n_used = (pend[-1] // bm).astype(jnp.int32)[None]
        n_valid = jnp.clip(pstart[blk_e] + counts[blk_e] - blk_start, 0, bm).astype(jnp.int32)

        dest_kt = dest.T.reshape(-1)
        xb = _scatter_rows(h2, dest_kt, NP)
        yb = _moe_experts(xb, blk_e, n_used, n_valid, w_gate_up[l], b_gate_up[l][:, None, :],
                          w_down[l], b_down[l][:, None, :], bm)
        yk = _gather_rows(yb, dest_kt).reshape(TOP_K, T, HALF_D)
        if l + 1 < depth:
            raise NotImplementedError("depth > 1")
        out = _final(x1, yk, gate_w, final_g[None])
    return out.reshape(B, S, D)
```

```python
import functools
import math

import jax
import jax.numpy as jnp
from jax import lax
from jax.experimental import pallas as pl
from jax.experimental.pallas import tpu as pltpu
from jax.experimental.pallas import tpu_sc as plsc

F32 = jnp.float32
BF16 = jnp.bfloat16

D_MODEL = 1024
EPS = 1e-5
NEG = -1e30

A_HEADS = 8
A_HEAD_DIM = 64
A_WIDTH = 512
A_PATTERNS = ((128, 1), (512, 4), (2048, 16))
A_HALF = 64
A_QB = 128
A_WIN = A_QB + 2 * A_HALF
assert all(w // (2 * d) == A_HALF for w, d in A_PATTERNS) and [d for _, d in A_PATTERNS] == [1, 4, 16]

B_HEADS = 8
B_NOPE = 64
B_ROPE = 32
B_V = 64
B_WIDTH = 512
Q_LORA = 256
KV_LORA = 128
ROPE_THETA = 10000.0
HEAD_PAD = 128
ONES_ROWS = 16
MLA_SCORE_BUFS = 4
MLA_HEADS_PER_STEP = 4

N_EXPERTS = 32
TOP_K = 4
SWIGLU_LIMIT = 7.0
SWIGLU_ALPHA = 1.702
LOGIT_PAD = 128

LOG2E = 1.4426950408889634
LN2 = 0.6931471805599453

VMEM_LIMIT = 56 * 1024 * 1024


def _cparams(*sem):
    return pltpu.CompilerParams(dimension_semantics=sem, vmem_limit_bytes=VMEM_LIMIT)


def _inproj_kernel(x_ref, g1_ref, wp_ref, gq_ref, gkv_ref, wq1_ref, wq2_ref, wuk_ref, wuv_ref,
                   cq_ref, sq_ref, ck_ref, sk_ref,
                   qa_ref, ka_ref, va_ref, qa4_ref, ka4_ref, va4_ref, qa16_ref, ka16_ref, va16_ref,
                   qb_ref, kb_ref, vb_ref, gate_ref, perm_ref):
    tm = x_ref.shape[0]
    x = x_ref[...]
    h = x * lax.rsqrt(jnp.mean(x * x, axis=-1, keepdims=True) + EPS) * g1_ref[...]
    h = h.astype(BF16)

    def proj(c0, c1):
        return jnp.dot(h, wp_ref[:, c0:c1], preferred_element_type=F32)

    def emit(which, z, nat_ref, views):
        nat_ref[...] = z.astype(BF16)
        for c in range(A_WIDTH // 128):
            perm_ref[which, c] = z[:, c * 128:(c + 1) * 128]
        for d, ref in views:
            for r in range(d):
                for c in range(A_WIDTH // 128):
                    ref[:, r * A_WIDTH + c * 128:r * A_WIDTH + (c + 1) * 128] = (
                        perm_ref[which, c, pl.ds(r, tm // d, stride=d), :].astype(BF16))

    emit(0, proj(0, 512) * (A_HEAD_DIM ** -0.5), qa_ref, ((4, qa4_ref), (16, qa16_ref)))
    emit(1, proj(512, 1024), ka_ref, ((4, ka4_ref), (16, ka16_ref)))
    emit(2, proj(1024, 1536), va_ref, ((4, va4_ref), (16, va16_ref)))

    cq = proj(1536, 1792)
    cq = cq * lax.rsqrt(jnp.mean(cq * cq, axis=-1, keepdims=True) + EPS) * gq_ref[...]
    cq = cq.astype(BF16)
    q_main = jnp.dot(cq, wq1_ref[...], preferred_element_type=F32)
    q_rot = jnp.dot(cq, wq2_ref[...], preferred_element_type=F32)
    cq_t = cq_ref[...]
    sq_t = sq_ref[...]
    for hd in range(B_HEADS):
        sl = slice(hd * HEAD_PAD, (hd + 1) * HEAD_PAD)
        qb_ref[:, sl] = (q_main[:, sl] * cq_t + q_rot[:, sl] * sq_t).astype(BF16)

    ckv = proj(1792, 1920)
    ckv = ckv * lax.rsqrt(jnp.mean(ckv * ckv, axis=-1, keepdims=True) + EPS) * gkv_ref[...]
    ckv = ckv.astype(BF16)
    k_nope = jnp.dot(ckv, wuk_ref[...], preferred_element_type=F32)
    vb_ref[...] = lax.dot_general(wuv_ref[...], ckv, (((1,), (1,)), ((), ())),
                                  preferred_element_type=F32).astype(BF16)
    k_rope = proj(3968, 4096) * ck_ref[...] + proj(4096, 4224) * sk_ref[...]
    for hd in range(B_HEADS):
        sl = slice(hd * HEAD_PAD, (hd + 1) * HEAD_PAD)
        kb_ref[:, sl] = (k_nope[:, sl] + k_rope).astype(BF16)

    gate_ref[...] = jax.nn.sigmoid(proj(1920, 3968)).astype(BF16)


def _inproj(x2, g1, wp, gq, gkv, wq1, wq2, wuk, wuv, cq_t, sq_t, ck_t, sk_t, seq, tm=512):
    T = x2.shape[0]
    nseq = seq // tm
    row = lambda i: (i, 0)
    fix = lambda i: (0, 0)
    tab = lambda i: (i % nseq, 0)
    full = lambda a: pl.BlockSpec(a.shape, fix)
    sds = lambda r, c: jax.ShapeDtypeStruct((r, c), BF16)
    out_shape = (
        sds(T, A_WIDTH), sds(T, A_WIDTH), sds(T, A_WIDTH),
        sds(T // 4, 4 * A_WIDTH), sds(T // 4, 4 * A_WIDTH), sds(T // 4, 4 * A_WIDTH),
        sds(T // 16, 16 * A_WIDTH), sds(T // 16, 16 * A_WIDTH), sds(T // 16, 16 * A_WIDTH),
        sds(T, B_HEADS * HEAD_PAD), sds(T, B_HEADS * HEAD_PAD),
        sds(B_WIDTH, T),
        sds(T, 2 * D_MODEL),
    )
    out_specs = [pl.BlockSpec((tm * s.shape[0] // T, s.shape[1]), row) for s in out_shape]
    out_specs[11] = pl.BlockSpec((B_WIDTH, tm), lambda i: (0, i))
    return pl.pallas_call(
        _inproj_kernel,
        grid=(T // tm,),
        in_specs=[pl.BlockSpec((tm, D_MODEL), row), full(g1), full(wp), full(gq), full(gkv),
                  full(wq1), full(wq2), full(wuk), full(wuv),
                  pl.BlockSpec((tm, HEAD_PAD), tab), pl.BlockSpec((tm, HEAD_PAD), tab),
                  pl.BlockSpec((tm, HEAD_PAD), tab), pl.BlockSpec((tm, HEAD_PAD), tab)],
        out_specs=out_specs,
        out_shape=out_shape,
        scratch_shapes=[pltpu.VMEM((3, A_WIDTH // 128, tm, 128), F32)],
        compiler_params=_cparams("parallel"),
        name="inproj",
    )(x2, g1, wp, gq, gkv, wq1, wq2, wuk, wuv, cq_t, sq_t, ck_t, sk_t)


def _dilated_kernel(q_ref, kc_ref, kp_ref, kn_ref, vc_ref, vp_ref, vn_ref, bias_ref,
                    o_ref, lse_ref, kbuf, vbuf, *, rows):
    i = pl.program_id(2)
    last = pl.num_programs(2) - 1
    kbuf[0:A_HALF, :] = kp_ref[0]
    kbuf[A_HALF:A_HALF + rows, :] = kc_ref[0]
    kbuf[A_HALF + rows:, :] = kn_ref[0]
    vbuf[0:A_HALF, :] = vp_ref[0]
    vbuf[A_HALF:A_HALF + rows, :] = vc_ref[0]
    vbuf[A_HALF + rows:, :] = vn_ref[0]

    nj = rows // A_QB
    npair = A_HEADS // 2
    lane = lax.broadcasted_iota(jnp.int32, (A_QB, 128), 1)
    lo_half = lane < A_HEAD_DIM

    def body(j, carry):
        r0 = pl.multiple_of(j * A_QB, A_QB)
        variant = (jnp.logical_and(i == 0, j == 0).astype(jnp.int32)
                   + 2 * jnp.logical_and(i == last, j == nj - 1).astype(jnp.int32))
        scores = []
        for pr in range(npair):
            ls = slice(pr * 128, (pr + 1) * 128)
            qp = q_ref[0, pl.ds(r0, A_QB), ls]
            zero = jnp.zeros_like(qp)
            q2 = jnp.concatenate([jnp.where(lo_half, qp, zero), jnp.where(lo_half, zero, qp)], axis=0)
            s = lax.dot_general(q2, kbuf[pl.ds(r0, A_WIN), ls], (((1,), (1,)), ((), ())),
                                preferred_element_type=F32)
            scores.append(s + bias_ref[variant * npair + pr])
        probs = []
        for s in scores:
            m = jnp.max(s, axis=-1, keepdims=True)
            p = jnp.exp(s - m)
            l = jnp.sum(p, axis=-1, keepdims=True)
            probs.append((p.astype(BF16), l, m + jnp.log(l)))
        for pr, (p, l, lse) in enumerate(probs):
            ls = slice(pr * 128, (pr + 1) * 128)
            o = jnp.dot(p, vbuf[pl.ds(r0, A_WIN), ls], preferred_element_type=F32) / l
            lse = jnp.broadcast_to(lse, (2 * A_QB, 128))
            o_ref[0, pl.ds(r0, A_QB), ls] = jnp.where(lo_half, o[:A_QB], o[A_QB:])
            lse_ref[0, pl.ds(r0, A_QB), ls] = jnp.where(lo_half, lse[:A_QB], lse[A_QB:])
        return carry

    lax.fori_loop(0, nj, body, 0)


def _dilated_pattern(qa, ka, va, bias, batch, seq, dil, rows=1024):
    L = seq // dil
    rows = min(rows, L)
    nblk = L // rows
    hb = rows // A_HALF
    nh = L // A_HALF
    view = lambda a: a.reshape(batch, L, dil * A_WIDTH)
    cur = lambda b, r, i: (b, i, r)
    prev = lambda b, r, i: (b, jnp.maximum(i * hb - 1, 0), r)
    nxt = lambda b, r, i: (b, jnp.minimum((i + 1) * hb, nh - 1), r)
    cur_spec = pl.BlockSpec((1, rows, A_WIDTH), cur)
    out_sds = jax.ShapeDtypeStruct((batch, L, dil * A_WIDTH), F32)
    o, lse = pl.pallas_call(
        functools.partial(_dilated_kernel, rows=rows),
        grid=(batch, dil, nblk),
        in_specs=[cur_spec, cur_spec,
                  pl.BlockSpec((1, A_HALF, A_WIDTH), prev), pl.BlockSpec((1, A_HALF, A_WIDTH), nxt),
                  cur_spec,
                  pl.BlockSpec((1, A_HALF, A_WIDTH), prev), pl.BlockSpec((1, A_HALF, A_WIDTH), nxt),
                  pl.BlockSpec(bias.shape, lambda b, r, i: (0, 0, 0))],
        out_specs=[cur_spec, cur_spec],
        out_shape=(out_sds, out_sds),
        scratch_shapes=[pltpu.VMEM((rows + 2 * A_HALF, A_WIDTH), BF16),
                        pltpu.VMEM((rows + 2 * A_HALF, A_WIDTH), BF16)],
        compiler_params=_cparams("parallel", "parallel", "parallel"),
        name=f"dilated_d{dil}",
    )(view(qa), view(ka), view(ka), view(ka), view(va), view(va), view(va), bias)
    return o.reshape(batch * L, dil * A_WIDTH), lse.reshape(batch * L, dil * A_WIDTH)


def _dilated_bias(dil):
    slopes = 2.0 ** (-8.0 * (jnp.arange(A_HEADS, dtype=F32) + 1.0) / A_HEADS)
    col = jnp.arange(A_WIN)[None, :]
    rel = (col - A_HALF) - jnp.arange(A_QB)[:, None]
    dist = (jnp.abs(rel) * dil).astype(F32)
    bias = -slopes[:, None, None] * dist[None]
    bias = jnp.where((jnp.abs(rel) <= A_HALF)[None], bias, NEG)
    variants = []
    for v in range(4):
        ok = jnp.ones((1, A_WIN), bool)
        if v & 1:
            ok = jnp.logical_and(ok, col >= A_HALF)
        if v & 2:
            ok = jnp.logical_and(ok, col < A_HALF + A_QB)
        variants.append(jnp.where(ok[None], bias, NEG).reshape(A_HEADS // 2, 2 * A_QB, A_WIN))
    return jnp.concatenate(variants, axis=0)


def _mla_kernel(q_ref, k_ref, vt_ref, o_ref, *s_bufs, tk):
    tq = q_ref.shape[1]
    nk = k_ref.shape[1] // tk
    nh = q_ref.shape[2] // HEAD_PAD
    qs = [q_ref[0, :, hh * HEAD_PAD:(hh + 1) * HEAD_PAD] for hh in range(nh)]

    def scores(j, s_ref):
        off = pl.multiple_of(j * tk, tk)
        for hh in range(nh):
            s_ref[hh] = lax.dot_general(k_ref[0, pl.ds(off, tk), hh * HEAD_PAD:(hh + 1) * HEAD_PAD], qs[hh],
                                        (((1,), (1,)), ((), ())), preferred_element_type=F32)

    ones = jnp.ones((ONES_ROWS, tk), BF16)

    def softmax_pv(j, s_ref, stats):
        off = pl.multiple_of(j * tk, tk)
        new = []
        for hh in range(nh):
            m, acc = stats[hh]
            s = s_ref[hh]
            vt = jnp.concatenate([vt_ref[hh * B_V:(hh + 1) * B_V, pl.ds(off, tk)], ones], axis=0)
            m_new = jnp.maximum(m, jnp.max(s, axis=0, keepdims=True))
            alpha = jnp.exp2(m - m_new)
            p = jnp.exp2(s - m_new)
            acc = alpha * acc + jnp.dot(vt, p.astype(BF16), preferred_element_type=F32)
            new.append((m_new, acc))
        return tuple(new)

    nbuf = len(s_bufs)

    def body(jj, stats):
        j = nbuf * jj
        for u in range(nbuf):
            scores(jnp.minimum(j + u + 1, nk - 1), s_bufs[(u + 1) % nbuf])
            stats = softmax_pv(j + u, s_bufs[u], stats)
        return stats

    init = (jnp.full((1, tq), NEG, F32), jnp.zeros((B_V + ONES_ROWS, tq), F32))
    scores(0, s_bufs[0])
    stats = lax.fori_loop(0, nk // nbuf, body, (init,) * nh)
    o_t = jnp.concatenate([a[:B_V] / a[B_V:B_V + 1] for _, a in stats], axis=0)
    o_ref[0] = o_t.T.astype(BF16)


def _mla_attention(qb, kb, vbt, batch, seq, tq=256, tk=512, nh=MLA_HEADS_PER_STEP):
    once = dict(pipeline_mode=pl.Buffered(1))
    return pl.pallas_call(
        functools.partial(_mla_kernel, tk=tk),
        grid=(batch, B_HEADS // nh, seq // tq),
        in_specs=[pl.BlockSpec((1, tq, nh * HEAD_PAD), lambda b, p, i: (b, i, p)),
                  pl.BlockSpec((1, seq, nh * HEAD_PAD), lambda b, p, i: (b, 0, p), **once),
                  pl.BlockSpec((nh * B_V, seq), lambda b, p, i: (p, b), **once)],
        out_specs=pl.BlockSpec((1, tq, nh * B_V), lambda b, p, i: (b, i, p)),
        out_shape=jax.ShapeDtypeStruct((batch, seq, B_WIDTH), BF16),
        scratch_shapes=[pltpu.VMEM((nh, tk, tq), F32) for _ in range(MLA_SCORE_BUFS)],
        compiler_params=_cparams("parallel", "parallel", "parallel"),
        name="mla_attention",
    )(qb, kb, vbt)


def _outproj_kernel(o1_ref, o2_ref, o3_ref, l1_ref, l2_ref, l3_ref, ob_ref, gate_ref, x_ref,
                    wa_ref, wb_ref, wo_ref, g2_ref, wrh_ref, wrl_ref, br_ref,
                    tri_ref, x1_ref, h2_ref, route_ref, count_ref, perm_ref, carry_ref):
    tm = x_ref.shape[0]

    @pl.when(pl.program_id(0) == 0)
    def _():
        carry_ref[...] = jnp.zeros_like(carry_ref)

    def token_order(which, ref, d):
        for r in range(d):
            for c in range(A_WIDTH // 128):
                perm_ref[which, c, pl.ds(r, tm // d, stride=d), :] = (
                    ref[:, r * A_WIDTH + c * 128:r * A_WIDTH + (c + 1) * 128])
        return jnp.concatenate([perm_ref[which, c] for c in range(A_WIDTH // 128)], axis=1)

    l1 = l1_ref[...]
    l2, l3 = token_order(0, l2_ref, 4), token_order(1, l3_ref, 16)
    o2, o3 = token_order(2, o2_ref, 4), token_order(3, o3_ref, 16)
    mx = jnp.maximum(jnp.maximum(l1, l2), l3)
    e1, e2, e3 = jnp.exp(l1 - mx), jnp.exp(l2 - mx), jnp.exp(l3 - mx)
    oa = (e1 * o1_ref[...] + e2 * o2 + e3 * o3) / (e1 + e2 + e3)
    ya = jnp.dot(oa.astype(BF16), wa_ref[...], preferred_element_type=F32)
    yb = jnp.dot(ob_ref[...], wb_ref[...], preferred_element_type=F32)
    merged = gate_ref[:, :D_MODEL].astype(F32) * ya + gate_ref[:, D_MODEL:].astype(F32) * yb
    x1 = x_ref[...] + jnp.dot(merged.astype(BF16), wo_ref[...], preferred_element_type=F32)
    x1_ref[...] = x1
    h2 = x1 * lax.rsqrt(jnp.mean(x1 * x1, axis=-1, keepdims=True) + EPS) * g2_ref[...]
    hi = h2.astype(BF16)
    h2_ref[...] = _pack_rows(h2)
    lo = (h2 - hi.astype(F32)).astype(BF16)
    logits = jnp.dot(hi, wrh_ref[...], preferred_element_type=F32)
    logits = logits + jnp.dot(lo, wrh_ref[...], preferred_element_type=F32)
    logits = logits + jnp.dot(hi, wrl_ref[...], preferred_element_type=F32)
    logits = logits + br_ref[...]

    lane = lax.broadcasted_iota(jnp.int32, (tm, LOGIT_PAD), 1)
    work = logits
    sel = jnp.zeros((tm, LOGIT_PAD), F32)
    idx_cols, val_cols = [], []
    for _ in range(TOP_K):
        mx = jnp.max(work, axis=-1, keepdims=True)
        idx = jnp.min(jnp.where(work == mx, lane, LOGIT_PAD), axis=-1, keepdims=True)
        hit = lane == idx
        sel = sel + hit.astype(F32)
        work = jnp.where(hit, 2.0 * NEG, work)
        idx_cols.append(idx)
        val_cols.append(mx)
    exps = [jnp.exp(v - val_cols[0]) for v in val_cols]
    denom = exps[0] + exps[1] + exps[2] + exps[3]
    gate_cols = [e / denom for e in exps]

    before = jnp.dot(tri_ref[...], sel.astype(BF16), preferred_element_type=F32) + carry_ref[0:1, :]
    rank_cols = [jnp.sum(jnp.where(lane == idx, before, 0.0), axis=-1, keepdims=True) for idx in idx_cols]
    carry_ref[...] = carry_ref[...] + jnp.sum(sel, axis=0, keepdims=True)
    count_ref[...] = carry_ref[...]

    cols = [c.astype(F32) for c in idx_cols] + rank_cols + gate_cols
    route = jnp.zeros((tm, LOGIT_PAD), F32)
    for c, col in enumerate(cols):
        route = jnp.where(lane == c, col, route)
    route_ref[...] = route


def _outproj(o1, o2, o3, l1, l2, l3, ob, gates, x2, wa, wb, wo, g2, wrh, wrl, br, tm=512):
    T = x2.shape[0]
    row = lambda i: (i, 0)
    fix = lambda i: (0, 0)
    full = lambda a: pl.BlockSpec(a.shape, fix)
    rows = lambda a: pl.BlockSpec((tm * a.shape[0] // T, a.shape[1]), row)
    tri = (jnp.arange(tm)[:, None] > jnp.arange(tm)[None, :]).astype(BF16)
    out_shape = (jax.ShapeDtypeStruct((T, D_MODEL), F32),
                 jax.ShapeDtypeStruct((T, HALF_D), jnp.int32),
                 jax.ShapeDtypeStruct((T, LOGIT_PAD), F32),
                 jax.ShapeDtypeStruct((8, LOGIT_PAD), F32))
    return pl.pallas_call(
        _outproj_kernel,
        grid=(T // tm,),
        in_specs=[rows(o1), rows(o2), rows(o3), rows(l1), rows(l2), rows(l3), rows(ob), rows(gates),
                  rows(x2), full(wa), full(wb), full(wo), full(g2), full(wrh), full(wrl), full(br), full(tri)],
        out_specs=[pl.BlockSpec((tm, D_MODEL), row), pl.BlockSpec((tm, HALF_D), row),
                   pl.BlockSpec((tm, LOGIT_PAD), row), pl.BlockSpec((8, LOGIT_PAD), fix)],
        out_shape=out_shape,
        scratch_shapes=[pltpu.VMEM((4, A_WIDTH // 128, tm, 128), F32), pltpu.VMEM((8, LOGIT_PAD), F32)],
        compiler_params=_cparams("arbitrary"),
        name="outproj",
    )(o1, o2, o3, l1, l2, l3, ob, gates, x2, wa, wb, wo, g2, wrh, wrl, br, tri)


HALF_D = D_MODEL // 2


def _pack_rows(x):
    lo = pltpu.bitcast(x[:, :HALF_D].astype(BF16).astype(F32), jnp.uint32)
    hi = pltpu.bitcast(x[:, HALF_D:].astype(BF16).astype(F32), jnp.uint32)
    word = jnp.bitwise_or(lax.shift_right_logical(lo, jnp.uint32(16)), jnp.bitwise_and(hi, jnp.uint32(0xFFFF0000)))
    return pltpu.bitcast(word, jnp.int32)


def _unpack_rows(w):
    u = pltpu.bitcast(w, jnp.uint32)
    lo = pltpu.bitcast(lax.shift_left(u, jnp.uint32(16)), F32)
    hi = pltpu.bitcast(jnp.bitwise_and(u, jnp.uint32(0xFFFF0000)), F32)
    return jnp.concatenate([lo, hi], axis=1)


SC_WINDOW = 128


def _gather_rows(table, idx):
    M = idx.shape[0]
    width = table.shape[1]
    mesh = plsc.VectorSubcoreMesh(core_axis_name="core", subcore_axis_name="subcore")

    @functools.partial(pl.kernel, out_type=jax.ShapeDtypeStruct((M, width), table.dtype), mesh=mesh,
                       name="gather_rows")
    def gather(table_hbm, idx_hbm, out_hbm):
        def body(idx_vmem, out_vmem):
            pltpu.sync_copy(table_hbm.at[idx_vmem.at[0]], out_vmem)

        pltpu.emit_pipeline(
            body,
            grid=(M // SC_WINDOW,),
            in_specs=[pl.BlockSpec((1, SC_WINDOW), index_map=lambda i: (0, i))],
            out_specs=[pl.BlockSpec((SC_WINDOW, width), index_map=lambda i: (i, 0),
                                    pipeline_mode=pl.Buffered(1))],
            core_axis_name=("core", "subcore"),
            dimension_semantics=(pltpu.PARALLEL,),
        )(idx_hbm, out_hbm)

    return gather(table, idx.reshape(1, M))


def _scatter_rows(rows, idx, n_out):
    M = idx.shape[0]
    R, width = rows.shape
    mesh = plsc.VectorSubcoreMesh(core_axis_name="core", subcore_axis_name="subcore")

    @functools.partial(pl.kernel, out_type=jax.ShapeDtypeStruct((n_out, width), rows.dtype), mesh=mesh,
                       name="scatter_rows")
    def scatter(rows_hbm, idx_hbm, out_hbm):
        def body(rows_vmem, idx_vmem):
            pltpu.sync_copy(rows_vmem, out_hbm.at[idx_vmem.at[0]])

        pltpu.emit_pipeline(
            body,
            grid=(M // SC_WINDOW,),
            in_specs=[pl.BlockSpec((SC_WINDOW, width), index_map=lambda i: (i % (R // SC_WINDOW), 0),
                                   pipeline_mode=pl.Buffered(1)),
                      pl.BlockSpec((1, SC_WINDOW), index_map=lambda i: (0, i))],
            out_specs=[],
            core_axis_name=("core", "subcore"),
            dimension_semantics=(pltpu.PARALLEL,),
        )(rows_hbm, idx_hbm)

    return scatter(rows, idx.reshape(1, M))


def _moe_kernel(be_ref, nu_ref, nv_ref, x_ref, wgu_ref, bgu_ref, wd_ref, bd_ref, y_ref, wgu_bf, wd_bf):
    i = pl.program_id(0)

    @pl.when(i >= nu_ref[0])
    def _():
        y_ref[...] = jnp.zeros_like(y_ref)

    @pl.when(jnp.logical_or(i == 0, be_ref[i] != be_ref[jnp.maximum(i - 1, 0)]))
    def _():
        rows = 128

        def cast(c, carry):
            r0 = pl.multiple_of(c * rows, rows)
            wgu_bf[pl.ds(r0, rows), :] = wgu_ref[0, pl.ds(r0, rows), :].astype(BF16)
            wd_bf[pl.ds(r0, rows), :] = wd_ref[0, pl.ds(r0, rows), :].astype(BF16)
            return carry

        lax.fori_loop(0, D_MODEL // rows, cast, 0)

    @pl.when(i < nu_ref[0])
    def _():
        live = lax.broadcasted_iota(jnp.int32, (x_ref.shape[0], 1), 0) < nv_ref[i]
        x = jnp.where(live, _unpack_rows(x_ref[...]), 0.0).astype(BF16)
        gu = jnp.dot(x, wgu_bf[...], preferred_element_type=F32) + bgu_ref[0]
        gate = jnp.minimum(gu[:, :D_MODEL], SWIGLU_LIMIT)
        up = jnp.clip(gu[:, D_MODEL:], -SWIGLU_LIMIT, SWIGLU_LIMIT)
        act = (up + 1.0) * (gate * jax.nn.sigmoid(SWIGLU_ALPHA * gate))
        y = jnp.dot(act.astype(BF16), wd_bf[...], preferred_element_type=F32) + bd_ref[0]
        y_ref[...] = _pack_rows(y)


def _moe_experts(xb, blk_e, n_used, n_valid, wgu, bgu, wd, bd, bm):
    NP = xb.shape[0]
    nblk = NP // bm
    grid_spec = pltpu.PrefetchScalarGridSpec(
        num_scalar_prefetch=3,
        grid=(nblk,),
        in_specs=[pl.BlockSpec((bm, HALF_D), lambda i, be, nu, nv: (jnp.minimum(i, nu[0] - 1), 0)),
                  pl.BlockSpec((1, D_MODEL, 2 * D_MODEL), lambda i, be, nu, nv: (be[i], 0, 0)),
                  pl.BlockSpec((1, 1, 2 * D_MODEL), lambda i, be, nu, nv: (be[i], 0, 0)),
                  pl.BlockSpec((1, D_MODEL, D_MODEL), lambda i, be, nu, nv: (be[i], 0, 0)),
                  pl.BlockSpec((1, 1, D_MODEL), lambda i, be, nu, nv: (be[i], 0, 0))],
        out_specs=pl.BlockSpec((bm, HALF_D), lambda i, be, nu, nv: (i, 0)),
        scratch_shapes=[pltpu.VMEM((D_MODEL, 2 * D_MODEL), BF16), pltpu.VMEM((D_MODEL, D_MODEL), BF16)],
    )
    return pl.pallas_call(
        _moe_kernel,
        grid_spec=grid_spec,
        out_shape=jax.ShapeDtypeStruct((NP, HALF_D), jnp.int32),
        compiler_params=_cparams("arbitrary"),
        name="moe_experts",
    )(blk_e, n_used, n_valid, xb, wgu, bgu, wd, bd)


def _final_kernel(x1_ref, y_ref, w_ref, g_ref, o_ref):
    w = w_ref[...]
    acc = x1_ref[...]
    for k in range(TOP_K):
        acc = acc + w[:, k:k + 1] * _unpack_rows(y_ref[k])
    o_ref[...] = acc * lax.rsqrt(jnp.mean(acc * acc, axis=-1, keepdims=True) + EPS) * g_ref[...]


def _final(x1, yk, wk, g, tm=512):
    T = x1.shape[0]
    return pl.pallas_call(
        _final_kernel,
        grid=(T // tm,),
        in_specs=[pl.BlockSpec((tm, D_MODEL), lambda i: (i, 0)),
                  pl.BlockSpec((TOP_K, tm, HALF_D), lambda i: (0, i, 0)),
                  pl.BlockSpec((tm, TOP_K), lambda i: (i, 0)),
                  pl.BlockSpec((1, D_MODEL), lambda i: (0, 0))],
        out_specs=pl.BlockSpec((tm, D_MODEL), lambda i: (i, 0)),
        out_shape=jax.ShapeDtypeStruct((T, D_MODEL), F32),
        compiler_params=_cparams("parallel"),
        name="final_norm",
    )(x1, yk, wk, g)


def _pack_w_in(w):
    kr = w[:, 1920:1952]
    half = B_ROPE // 2
    z = lambda n: jnp.zeros((w.shape[0], n), w.dtype)
    kr_placed = jnp.concatenate([z(B_NOPE), kr, z(HEAD_PAD - B_NOPE - B_ROPE)], axis=1)
    kr_rot = jnp.concatenate([z(B_NOPE), -kr[:, half:], kr[:, :half], z(HEAD_PAD - B_NOPE - B_ROPE)], axis=1)
    return jnp.concatenate([w[:, :1920], w[:, 1952:4000], kr_placed, kr_rot], axis=1).astype(BF16)


def _pack_w_uq(w):
    half = B_ROPE // 2
    r = w.shape[0]
    nope, x1, x2 = w[..., :B_NOPE], w[..., B_NOPE:B_NOPE + half], w[..., B_NOPE + half:]
    zpad = jnp.zeros((r, B_HEADS, HEAD_PAD - B_NOPE - B_ROPE), w.dtype)
    main = jnp.concatenate([nope, x1, x2, zpad], axis=-1)
    rot = jnp.concatenate([jnp.zeros_like(nope), -x2, x1, zpad], axis=-1)
    return (main.reshape(r, B_HEADS * HEAD_PAD).astype(BF16), rot.reshape(r, B_HEADS * HEAD_PAD).astype(BF16))


def _pack_w_uk(w):
    r = w.shape[0]
    zpad = jnp.zeros((r, B_HEADS, HEAD_PAD - B_NOPE), w.dtype)
    return jnp.concatenate([w, zpad], axis=-1).reshape(r, B_HEADS * HEAD_PAD).astype(BF16)


def _rope_tables(seq):
    half = B_ROPE // 2
    pos = jnp.arange(seq, dtype=F32)
    inv_freq = ROPE_THETA ** (-jnp.arange(0, B_ROPE, 2, dtype=F32) / B_ROPE)
    ang = pos[:, None] * inv_freq[None, :]
    cos, sin = jnp.cos(ang), jnp.sin(ang)
    one = jnp.ones((seq, B_NOPE), F32)
    zero = jnp.zeros((seq, B_NOPE), F32)
    zpad = jnp.zeros((seq, HEAD_PAD - B_NOPE - B_ROPE), F32)
    qs = ((B_NOPE + B_ROPE) ** -0.5) * LOG2E
    cq = jnp.concatenate([one, cos, cos, zpad], axis=1) * qs
    sq = jnp.concatenate([zero, sin, sin, zpad], axis=1) * qs
    ck = jnp.concatenate([zero, cos, cos, zpad], axis=1)
    sk = jnp.concatenate([zero, sin, sin, zpad], axis=1)
    return cq, sq, ck, sk


def kernel(x, norm1_g, w_in, q_a_norm_g, kv_a_norm_g, w_uq, w_uk, w_uv, w_a_out, w_b_out, w_o,
           norm2_g, w_router, b_router, w_gate_up, b_gate_up, w_down, b_down, final_g):
    B, S, D = x.shape
    T = B * S
    depth = norm1_g.shape[0]
    xt = x.reshape(T, D)
    cq_t, sq_t, ck_t, sk_t = _rope_tables(S)
    biases = [_dilated_bias(d) for _, d in A_PATTERNS]
    bm = 512
    for l in range(depth):
        wq1, wq2 = _pack_w_uq(w_uq[l])
        qa, ka, va, qa4, ka4, va4, qa16, ka16, va16, qb, kb, vb, gates = _inproj(
            xt, norm1_g[l][None], _pack_w_in(w_in[l]), q_a_norm_g[l][None], kv_a_norm_g[l][None],
            wq1, wq2, _pack_w_uk(w_uk[l]), w_uv[l].reshape(KV_LORA, B_WIDTH).T.astype(BF16),
            cq_t, sq_t, ck_t, sk_t, S)
        qkv = ((qa, ka, va), (qa4, ka4, va4), (qa16, ka16, va16))
        outs = [_dilated_pattern(*qkv[p], biases[p], B, S, A_PATTERNS[p][1]) for p in range(3)]
        ob = _mla_attention(qb.reshape(B, S, -1), kb.reshape(B, S, -1), vb, B, S)

        wr = jnp.pad(w_router[l], ((0, 0), (0, LOGIT_PAD - N_EXPERTS)))
        wrh = wr.astype(BF16)
        wrl = (wr - wrh.astype(F32)).astype(BF16)
        br = jnp.pad(b_router[l], (0, LOGIT_PAD - N_EXPERTS), constant_values=NEG)[None]
        x1, h2, route, count_rows = _outproj(
            outs[0][0], outs[1][0], outs[2][0], outs[0][1], outs[1][1], outs[2][1],
            ob.reshape(T, B_WIDTH), gates, xt,
            w_a_out[l].astype(BF16), w_b_out[l].astype(BF16), w_o[l].astype(BF16),
            norm2_g[l][None], wrh, wrl, br)

        top_i = route[:, 0:TOP_K].astype(jnp.int32)
        rank = route[:, TOP_K:2 * TOP_K].astype(jnp.int32)
        gate_w = route[:, 2 * TOP_K:3 * TOP_K]
        counts = count_rows[0, :N_EXPERTS].astype(jnp.int32)
        padded = (counts + bm - 1) // bm * bm
        pend = jnp.cumsum(padded)
        pstart = pend - padded
        dest = pstart[top_i] + rank
        NP = T * TOP_K + N_EXPERTS * bm
        nblk = NP // bm
        blk_start = jnp.arange(nblk, dtype=jnp.int32) * bm
        blk_e = jnp.minimum(jnp.sum(pend[None, :] <= blk_start[:, None], axis=1), N_EXPERTS - 1).astype(jnp.int32)
        n_used = (pend[-1] // bm).astype(jnp.int32)[None]
        n_valid = jnp.clip(pstart[blk_e] + counts[blk_e] - blk_start, 0, bm).astype(jnp.int32)

        dest_kt = dest.T.reshape(-1)
        xb = _scatter_rows(h2, dest_kt, NP)
        yb = _moe_experts(xb, blk_e, n_used, n_valid, w_gate_up[l], b_gate_up[l][:, None, :],
                          w_down[l], b_down[l][:, None, :], bm)
        yk = _gather_rows(yb, dest_kt).reshape(TOP_K, T, HALF_D)
        if l + 1 < depth:
            raise NotImplementedError("depth > 1")
        out = _final(x1, yk, gate_w, final_g[None])
    return out.reshape(B, S, D)
```

```python
import functools

import jax
import jax.numpy as jnp
from jax import lax
from jax.experimental import pallas as pl
from jax.experimental.pallas import tpu as pltpu
from jax.experimental.pallas import tpu_sc as plsc

F32 = jnp.float32
BF16 = jnp.bfloat16

D_MODEL = 1024
EPS = 1e-5
NEG = -1e30

A_HEADS = 8
A_HEAD_DIM = 64
A_WIDTH = 512
A_PATTERNS = ((128, 1), (512, 4), (2048, 16))
A_HALF = 64
A_QB = 128
A_WIN = A_QB + 2 * A_HALF
assert all(w // (2 * d) == A_HALF for w, d in A_PATTERNS) and [d for _, d in A_PATTERNS] == [1, 4, 16]

B_HEADS = 8
B_NOPE = 64
B_ROPE = 32
B_V = 64
B_WIDTH = 512
Q_LORA = 256
KV_LORA = 128
ROPE_THETA = 10000.0
HEAD_PAD = 128
ONES_ROWS = 16
MLA_SCORE_BUFS = 4
MLA_HEADS_PER_STEP = 4

N_EXPERTS = 32
TOP_K = 4
SWIGLU_LIMIT = 7.0
SWIGLU_ALPHA = 1.702
LOGIT_PAD = 128

LOG2E = 1.4426950408889634

VMEM_LIMIT = 56 * 1024 * 1024


def _cparams(*sem):
    return pltpu.CompilerParams(dimension_semantics=sem, vmem_limit_bytes=VMEM_LIMIT)


def _inproj_kernel(x_ref, g1_ref, wp_ref, gq_ref, gkv_ref, wq1_ref, wq2_ref, wuk_ref, wuv_ref,
                   cq_ref, sq_ref, ck_ref, sk_ref,
                   qa_ref, ka_ref, va_ref, qa4_ref, ka4_ref, va4_ref, qa16_ref, ka16_ref, va16_ref,
                   qb_ref, kb_ref, vb_ref, gate_ref, perm_ref, perm4_ref):
    tm = x_ref.shape[0]
    x = x_ref[...]
    h = x * lax.rsqrt(jnp.mean(x * x, axis=-1, keepdims=True) + EPS) * g1_ref[...]
    h = h.astype(BF16)

    def proj(c0, c1):
        return jnp.dot(h, wp_ref[:, c0:c1], preferred_element_type=F32)

    def emit(which, z, nat_ref, v4_ref, v16_ref):
        nat_ref[...] = z.astype(BF16)
        nc = A_WIDTH // 128
        for c in range(nc):
            perm_ref[which, c] = z[:, c * 128:(c + 1) * 128]
        for r1 in range(4):
            for c in range(nc):
                quarter = perm_ref[which, c, pl.ds(r1, tm // 4, stride=4), :]
                v4_ref[:, r1 * A_WIDTH + c * 128:r1 * A_WIDTH + (c + 1) * 128] = quarter.astype(BF16)
                perm4_ref[which, r1, c] = quarter
        for r1 in range(4):
            for r2 in range(4):
                r = 4 * r2 + r1
                for c in range(nc):
                    v16_ref[:, r * A_WIDTH + c * 128:r * A_WIDTH + (c + 1) * 128] = (
                        perm4_ref[which, r1, c, pl.ds(r2, tm // 16, stride=4), :].astype(BF16))

    emit(0, proj(0, 512) * (A_HEAD_DIM ** -0.5), qa_ref, qa4_ref, qa16_ref)
    emit(1, proj(512, 1024), ka_ref, ka4_ref, ka16_ref)
    emit(2, proj(1024, 1536), va_ref, va4_ref, va16_ref)

    cq = proj(1536, 1792)
    cq = cq * lax.rsqrt(jnp.mean(cq * cq, axis=-1, keepdims=True) + EPS) * gq_ref[...]
    cq = cq.astype(BF16)
    q_main = jnp.dot(cq, wq1_ref[...], preferred_element_type=F32)
    q_rot = jnp.dot(cq, wq2_ref[...], preferred_element_type=F32)
    cq_t = cq_ref[...]
    sq_t = sq_ref[...]
    for hd in range(B_HEADS):
        sl = slice(hd * HEAD_PAD, (hd + 1) * HEAD_PAD)
        qb_ref[:, sl] = (q_main[:, sl] * cq_t + q_rot[:, sl] * sq_t).astype(BF16)

    ckv = proj(1792, 1920)
    ckv = ckv * lax.rsqrt(jnp.mean(ckv * ckv, axis=-1, keepdims=True) + EPS) * gkv_ref[...]
    ckv = ckv.astype(BF16)
    k_nope = jnp.dot(ckv, wuk_ref[...], preferred_element_type=F32)
    vb_ref[...] = lax.dot_general(wuv_ref[...], ckv, (((1,), (1,)), ((), ())),
                                  preferred_element_type=F32).astype(BF16)
    k_rope = proj(3968, 4096) * ck_ref[...] + proj(4096, 4224) * sk_ref[...]
    for hd in range(B_HEADS):
        sl = slice(hd * HEAD_PAD, (hd + 1) * HEAD_PAD)
        kb_ref[:, sl] = (k_nope[:, sl] + k_rope).astype(BF16)

    gate_ref[...] = jax.nn.sigmoid(proj(1920, 3968)).astype(BF16)


def _inproj(x2, g1, wp, gq, gkv, wq1, wq2, wuk, wuv, cq_t, sq_t, ck_t, sk_t, seq, tm=512):
    T = x2.shape[0]
    nseq = seq // tm
    row = lambda i: (i, 0)
    fix = lambda i: (0, 0)
    tab = lambda i: (i % nseq, 0)
    full = lambda a: pl.BlockSpec(a.shape, fix)
    sds = lambda r, c: jax.ShapeDtypeStruct((r, c), BF16)
    out_shape = (
        sds(T, A_WIDTH), sds(T, A_WIDTH), sds(T, A_WIDTH),
        sds(T // 4, 4 * A_WIDTH), sds(T // 4, 4 * A_WIDTH), sds(T // 4, 4 * A_WIDTH),
        sds(T // 16, 16 * A_WIDTH), sds(T // 16, 16 * A_WIDTH), sds(T // 16, 16 * A_WIDTH),
        sds(T, B_HEADS * HEAD_PAD), sds(T, B_HEADS * HEAD_PAD),
        sds(B_WIDTH, T),
        sds(T, 2 * D_MODEL),
    )
    out_specs = [pl.BlockSpec((tm * s.shape[0] // T, s.shape[1]), row) for s in out_shape]
    out_specs[11] = pl.BlockSpec((B_WIDTH, tm), lambda i: (0, i))
    return pl.pallas_call(
        _inproj_kernel,
        grid=(T // tm,),
        in_specs=[pl.BlockSpec((tm, D_MODEL), row), full(g1), full(wp), full(gq), full(gkv),
                  full(wq1), full(wq2), full(wuk), full(wuv),
                  pl.BlockSpec((tm, HEAD_PAD), tab), pl.BlockSpec((tm, HEAD_PAD), tab),
                  pl.BlockSpec((tm, HEAD_PAD), tab), pl.BlockSpec((tm, HEAD_PAD), tab)],
        out_specs=out_specs,
        out_shape=out_shape,
        scratch_shapes=[pltpu.VMEM((3, A_WIDTH // 128, tm, 128), F32),
                        pltpu.VMEM((3, 4, A_WIDTH // 128, tm // 4, 128), F32)],
        compiler_params=_cparams("parallel"),
        name="inproj",
    )(x2, g1, wp, gq, gkv, wq1, wq2, wuk, wuv, cq_t, sq_t, ck_t, sk_t)


def _dilated_kernel(q_ref, kc_ref, kp_ref, kn_ref, vc_ref, vp_ref, vn_ref, bias_ref,
                    o_ref, lse_ref, kbuf, vbuf, *, rows):
    i = pl.program_id(2)
    last = pl.num_programs(2) - 1
    kbuf[0:A_HALF, :] = kp_ref[0]
    kbuf[A_HALF:A_HALF + rows, :] = kc_ref[0]
    kbuf[A_HALF + rows:, :] = kn_ref[0]
    vbuf[0:A_HALF, :] = vp_ref[0]
    vbuf[A_HALF:A_HALF + rows, :] = vc_ref[0]
    vbuf[A_HALF + rows:, :] = vn_ref[0]

    nj = rows // A_QB
    npair = A_HEADS // 2
    lane = lax.broadcasted_iota(jnp.int32, (A_QB, 128), 1)
    lo_half = lane < A_HEAD_DIM

    def body(j, carry):
        r0 = pl.multiple_of(j * A_QB, A_QB)
        variant = (jnp.logical_and(i == 0, j == 0).astype(jnp.int32)
                   + 2 * jnp.logical_and(i == last, j == nj - 1).astype(jnp.int32))
        scores = []
        for pr in range(npair):
            ls = slice(pr * 128, (pr + 1) * 128)
            qp = q_ref[0, pl.ds(r0, A_QB), ls]
            zero = jnp.zeros_like(qp)
            q2 = jnp.concatenate([jnp.where(lo_half, qp, zero), jnp.where(lo_half, zero, qp)], axis=0)
            s = lax.dot_general(q2, kbuf[pl.ds(r0, A_WIN), ls], (((1,), (1,)), ((), ())),
                                preferred_element_type=F32)
            scores.append(s + bias_ref[variant * npair + pr])
        probs = []
        for s in scores:
            m = jnp.max(s, axis=-1, keepdims=True)
            p = jnp.exp(s - m)
            l = jnp.sum(p, axis=-1, keepdims=True)
            probs.append((p.astype(BF16), l, m + jnp.log(l)))
        for pr, (p, l, lse) in enumerate(probs):
            ls = slice(pr * 128, (pr + 1) * 128)
            o = jnp.dot(p, vbuf[pl.ds(r0, A_WIN), ls], preferred_element_type=F32) / l
            lse = jnp.broadcast_to(lse, (2 * A_QB, 128))
            o_ref[0, pl.ds(r0, A_QB), ls] = jnp.where(lo_half, o[:A_QB], o[A_QB:]).astype(o_ref.dtype)
            lse_ref[0, pl.ds(r0, A_QB), ls] = jnp.where(lo_half, lse[:A_QB], lse[A_QB:])
        return carry

    lax.fori_loop(0, nj, body, 0)


def _dilated_pattern(qa, ka, va, bias, batch, seq, dil, rows=1024):
    L = seq // dil
    rows = min(rows, L)
    nblk = L // rows
    hb = rows // A_HALF
    nh = L // A_HALF
    view = lambda a: a.reshape(batch, L, dil * A_WIDTH)
    cur = lambda b, r, i: (b, i, r)
    prev = lambda b, r, i: (b, jnp.maximum(i * hb - 1, 0), r)
    nxt = lambda b, r, i: (b, jnp.minimum((i + 1) * hb, nh - 1), r)
    cur_spec = pl.BlockSpec((1, rows, A_WIDTH), cur)
    out_sds = lambda dt: jax.ShapeDtypeStruct((batch, L, dil * A_WIDTH), dt)
    o, lse = pl.pallas_call(
        functools.partial(_dilated_kernel, rows=rows),
        grid=(batch, dil, nblk),
        in_specs=[cur_spec, cur_spec,
                  pl.BlockSpec((1, A_HALF, A_WIDTH), prev), pl.BlockSpec((1, A_HALF, A_WIDTH), nxt),
                  cur_spec,
                  pl.BlockSpec((1, A_HALF, A_WIDTH), prev), pl.BlockSpec((1, A_HALF, A_WIDTH), nxt),
                  pl.BlockSpec(bias.shape, lambda b, r, i: (0, 0, 0))],
        out_specs=[cur_spec, cur_spec],
        out_shape=(out_sds(BF16), out_sds(F32)),
        scratch_shapes=[pltpu.VMEM((rows + 2 * A_HALF, A_WIDTH), BF16),
                        pltpu.VMEM((rows + 2 * A_HALF, A_WIDTH), BF16)],
        compiler_params=_cparams("parallel", "parallel", "parallel"),
        name=f"dilated_d{dil}",
    )(view(qa), view(ka), view(ka), view(ka), view(va), view(va), view(va), bias)
    return o.reshape(batch * L, dil * A_WIDTH), lse.reshape(batch * L, dil * A_WIDTH)


def _dilated_bias(dil):
    slopes = 2.0 ** (-8.0 * (jnp.arange(A_HEADS, dtype=F32) + 1.0) / A_HEADS)
    col = jnp.arange(A_WIN)[None, :]
    rel = (col - A_HALF) - jnp.arange(A_QB)[:, None]
    dist = (jnp.abs(rel) * dil).astype(F32)
    bias = -slopes[:, None, None] * dist[None]
    bias = jnp.where((jnp.abs(rel) <= A_HALF)[None], bias, NEG)
    variants = []
    for v in range(4):
        ok = jnp.ones((1, A_WIN), bool)
        if v & 1:
            ok = jnp.logical_and(ok, col >= A_HALF)
        if v & 2:
            ok = jnp.logical_and(ok, col < A_HALF + A_QB)
        variants.append(jnp.where(ok[None], bias, NEG).reshape(A_HEADS // 2, 2 * A_QB, A_WIN))
    return jnp.concatenate(variants, axis=0)


def _mla_kernel(q_ref, k_ref, vt_ref, o_ref, *s_bufs, tk):
    tq = q_ref.shape[1]
    nk = k_ref.shape[1] // tk
    nh = q_ref.shape[2] // HEAD_PAD
    qs = [q_ref[0, :, hh * HEAD_PAD:(hh + 1) * HEAD_PAD] for hh in range(nh)]

    def scores(j, s_ref):
        off = pl.multiple_of(j * tk, tk)
        for hh in range(nh):
            s_ref[hh] = lax.dot_general(k_ref[0, pl.ds(off, tk), hh * HEAD_PAD:(hh + 1) * HEAD_PAD], qs[hh],
                                        (((1,), (1,)), ((), ())), preferred_element_type=F32)

    ones = jnp.ones((ONES_ROWS, tk), BF16)

    def softmax_pv(j, s_ref, stats):
        off = pl.multiple_of(j * tk, tk)
        new = []
        for hh in range(nh):
            m, acc = stats[hh]
            s = s_ref[hh]
            vt = jnp.concatenate([vt_ref[hh * B_V:(hh + 1) * B_V, pl.ds(off, tk)], ones], axis=0)
            m_new = jnp.maximum(m, jnp.max(s, axis=0, keepdims=True))
            alpha = jnp.exp2(m - m_new)
            p = jnp.exp2(s - m_new)
            acc = alpha * acc + jnp.dot(vt, p.astype(BF16), preferred_element_type=F32)
            new.append((m_new, acc))
        return tuple(new)

    nbuf = len(s_bufs)

    def body(jj, stats):
        j = nbuf * jj
        for u in range(nbuf):
            scores(jnp.minimum(j + u + 1, nk - 1), s_bufs[(u + 1) % nbuf])
            stats = softmax_pv(j + u, s_bufs[u], stats)
        return stats

    init = (jnp.full((1, tq), NEG, F32), jnp.zeros((B_V + ONES_ROWS, tq), F32))
    scores(0, s_bufs[0])
    stats = lax.fori_loop(0, nk // nbuf, body, (init,) * nh)
    o_t = jnp.concatenate([a[:B_V] / a[B_V:B_V + 1] for _, a in stats], axis=0)
    o_ref[0] = o_t.T.astype(BF16)


def _mla_attention(qb, kb, vbt, batch, seq, tq=256, tk=512, nh=MLA_HEADS_PER_STEP):
    once = dict(pipeline_mode=pl.Buffered(1))
    return pl.pallas_call(
        functools.partial(_mla_kernel, tk=tk),
        grid=(batch, B_HEADS // nh, seq // tq),
        in_specs=[pl.BlockSpec((1, tq, nh * HEAD_PAD), lambda b, p, i: (b, i, p)),
                  pl.BlockSpec((1, seq, nh * HEAD_PAD), lambda b, p, i: (b, 0, p), **once),
                  pl.BlockSpec((nh * B_V, seq), lambda b, p, i: (p, b), **once)],
        out_specs=pl.BlockSpec((1, tq, nh * B_V), lambda b, p, i: (b, i, p)),
        out_shape=jax.ShapeDtypeStruct((batch, seq, B_WIDTH), BF16),
        scratch_shapes=[pltpu.VMEM((nh, tk, tq), F32) for _ in range(MLA_SCORE_BUFS)],
        compiler_params=_cparams("parallel", "parallel", "parallel"),
        name="mla_attention",
    )(qb, kb, vbt)


def _outproj_kernel(o1_ref, o2_ref, o3_ref, l1_ref, l2_ref, l3_ref, ob_ref, gate_ref, x_ref,
                    wa_ref, wb_ref, wo_ref, g2_ref, wrh_ref, wrl_ref, br_ref,
                    tri_ref, x1_ref, h2_ref, route_ref, count_ref, perm_ref, perm4_ref, carry_ref):
    tm = x_ref.shape[0]

    @pl.when(pl.program_id(0) == 0)
    def _():
        carry_ref[...] = jnp.zeros_like(carry_ref)

    nc = A_WIDTH // 128

    def lanes(r, c):
        return slice(r * A_WIDTH + c * 128, r * A_WIDTH + (c + 1) * 128)

    def token_order(which, ref, d):
        if d == 4:
            for r in range(4):
                for c in range(nc):
                    perm_ref[which, c, pl.ds(r, tm // 4, stride=4), :] = ref[:, lanes(r, c)].astype(F32)
        else:
            for r1 in range(4):
                for r2 in range(4):
                    for c in range(nc):
                        perm4_ref[which - 2, r1, c, pl.ds(r2, tm // 16, stride=4), :] = (
                            ref[:, lanes(4 * r2 + r1, c)].astype(F32))
            for r1 in range(4):
                for c in range(nc):
                    perm_ref[which, c, pl.ds(r1, tm // 4, stride=4), :] = perm4_ref[which - 2, r1, c]
        return jnp.concatenate([perm_ref[which, c] for c in range(nc)], axis=1)

    l1 = l1_ref[...]
    l2, o2 = token_order(0, l2_ref, 4), token_order(1, o2_ref, 4)
    l3, o3 = token_order(2, l3_ref, 16), token_order(3, o3_ref, 16)
    mx = jnp.maximum(jnp.maximum(l1, l2), l3)
    e1, e2, e3 = jnp.exp(l1 - mx), jnp.exp(l2 - mx), jnp.exp(l3 - mx)
    oa = (e1 * o1_ref[...] + e2 * o2 + e3 * o3) / (e1 + e2 + e3)
    ya = jnp.dot(oa.astype(BF16), wa_ref[...], preferred_element_type=F32)
    yb = jnp.dot(ob_ref[...], wb_ref[...], preferred_element_type=F32)
    merged = gate_ref[:, :D_MODEL].astype(F32) * ya + gate_ref[:, D_MODEL:].astype(F32) * yb
    x1 = x_ref[...] + jnp.dot(merged.astype(BF16), wo_ref[...], preferred_element_type=F32)
    x1_ref[...] = x1
    h2 = x1 * lax.rsqrt(jnp.mean(x1 * x1, axis=-1, keepdims=True) + EPS) * g2_ref[...]
    hi = h2.astype(BF16)
    h2_ref[...] = _pack_rows(h2)
    lo = (h2 - hi.astype(F32)).astype(BF16)
    logits = jnp.dot(hi, wrh_ref[...], preferred_element_type=F32)
    logits = logits + jnp.dot(lo, wrh_ref[...], preferred_element_type=F32)
    logits = logits + jnp.dot(hi, wrl_ref[...], preferred_element_type=F32)
    logits = logits + br_ref[...]

    lane = lax.broadcasted_iota(jnp.int32, (tm, LOGIT_PAD), 1)
    work = logits
    sel = jnp.zeros((tm, LOGIT_PAD), F32)
    idx_cols, val_cols = [], []
    for _ in range(TOP_K):
        mx = jnp.max(work, axis=-1, keepdims=True)
        idx = jnp.min(jnp.where(work == mx, lane, LOGIT_PAD), axis=-1, keepdims=True)
        hit = lane == idx
        sel = sel + hit.astype(F32)
        work = jnp.where(hit, 2.0 * NEG, work)
        idx_cols.append(idx)
        val_cols.append(mx)
    exps = [jnp.exp(v - val_cols[0]) for v in val_cols]
    denom = exps[0] + exps[1] + exps[2] + exps[3]
    gate_cols = [e / denom for e in exps]

    before = jnp.dot(tri_ref[...], sel.astype(BF16), preferred_element_type=F32) + carry_ref[0:1, :]
    rank_cols = [jnp.sum(jnp.where(lane == idx, before, 0.0), axis=-1, keepdims=True) for idx in idx_cols]
    carry_ref[...] = carry_ref[...] + jnp.sum(sel, axis=0, keepdims=True)
    count_ref[...] = carry_ref[...]

    cols = [c.astype(F32) for c in idx_cols] + rank_cols + gate_cols
    route = jnp.zeros((tm, LOGIT_PAD), F32)
    for c, col in enumerate(cols):
        route = jnp.where(lane == c, col, route)
    route_ref[...] = route


def _outproj(o1, o2, o3, l1, l2, l3, ob, gates, x2, wa, wb, wo, g2, wrh, wrl, br, tm=512):
    T = x2.shape[0]
    row = lambda i: (i, 0)
    fix = lambda i: (0, 0)
    full = lambda a: pl.BlockSpec(a.shape, fix)
    rows = lambda a: pl.BlockSpec((tm * a.shape[0] // T, a.shape[1]), row)
    tri = (jnp.arange(tm)[:, None] > jnp.arange(tm)[None, :]).astype(BF16)
    out_shape = (jax.ShapeDtypeStruct((T, D_MODEL), F32),
                 jax.ShapeDtypeStruct((T, HALF_D), jnp.int32),
                 jax.ShapeDtypeStruct((T, LOGIT_PAD), F32),
                 jax.ShapeDtypeStruct((8, LOGIT_PAD), F32))
    return pl.pallas_call(
        _outproj_kernel,
        grid=(T // tm,),
        in_specs=[rows(o1), rows(o2), rows(o3), rows(l1), rows(l2), rows(l3), rows(ob), rows(gates),
                  rows(x2), full(wa), full(wb), full(wo), full(g2), full(wrh), full(wrl), full(br), full(tri)],
        out_specs=[pl.BlockSpec((tm, D_MODEL), row), pl.BlockSpec((tm, HALF_D), row),
                   pl.BlockSpec((tm, LOGIT_PAD), row), pl.BlockSpec((8, LOGIT_PAD), fix)],
        out_shape=out_shape,
        scratch_shapes=[pltpu.VMEM((4, A_WIDTH // 128, tm, 128), F32),
                        pltpu.VMEM((2, 4, A_WIDTH // 128, tm // 4, 128), F32),
                        pltpu.VMEM((8, LOGIT_PAD), F32)],
        compiler_params=_cparams("arbitrary"),
        name="outproj",
    )(o1, o2, o3, l1, l2, l3, ob, gates, x2, wa, wb, wo, g2, wrh, wrl, br, tri)


HALF_D = D_MODEL // 2


def _pack_rows(x):
    lo = pltpu.bitcast(x[:, :HALF_D].astype(BF16).astype(F32), jnp.uint32)
    hi = pltpu.bitcast(x[:, HALF_D:].astype(BF16).astype(F32), jnp.uint32)
    word = jnp.bitwise_or(lax.shift_right_logical(lo, jnp.uint32(16)), jnp.bitwise_and(hi, jnp.uint32(0xFFFF0000)))
    return pltpu.bitcast(word, jnp.int32)


def _unpack_rows(w):
    u = pltpu.bitcast(w, jnp.uint32)
    lo = pltpu.bitcast(lax.shift_left(u, jnp.uint32(16)), F32)
    hi = pltpu.bitcast(jnp.bitwise_and(u, jnp.uint32(0xFFFF0000)), F32)
    return jnp.concatenate([lo, hi], axis=1)


SC_WINDOW = 128


def _gather_rows(table, idx):
    M = idx.shape[0]
    width = table.shape[1]
    mesh = plsc.VectorSubcoreMesh(core_axis_name="core", subcore_axis_name="subcore")

    @functools.partial(pl.kernel, out_type=jax.ShapeDtypeStruct((M, width), table.dtype), mesh=mesh,
                       name="gather_rows")
    def gather(table_hbm, idx_hbm, out_hbm):
        def body(idx_vmem, out_vmem):
            pltpu.sync_copy(table_hbm.at[idx_vmem.at[0]], out_vmem)

        pltpu.emit_pipeline(
            body,
            grid=(M // SC_WINDOW,),
            in_specs=[pl.BlockSpec((1, SC_WINDOW), index_map=lambda i: (0, i))],
            out_specs=[pl.BlockSpec((SC_WINDOW, width), index_map=lambda i: (i, 0),
                                    pipeline_mode=pl.Buffered(1))],
            core_axis_name=("core", "subcore"),
            dimension_semantics=(pltpu.PARALLEL,),
        )(idx_hbm, out_hbm)

    return gather(table, idx.reshape(1, M))


def _scatter_rows(rows, idx, n_out):
    M = idx.shape[0]
    R, width = rows.shape
    mesh = plsc.VectorSubcoreMesh(core_axis_name="core", subcore_axis_name="subcore")

    @functools.partial(pl.kernel, out_type=jax.ShapeDtypeStruct((n_out, width), rows.dtype), mesh=mesh,
                       name="scatter_rows")
    def scatter(rows_hbm, idx_hbm, out_hbm):
        def body(rows_vmem, idx_vmem):
            pltpu.sync_copy(rows_vmem, out_hbm.at[idx_vmem.at[0]])

        pltpu.emit_pipeline(
            body,
            grid=(M // SC_WINDOW,),
            in_specs=[pl.BlockSpec((SC_WINDOW, width), index_map=lambda i: (i % (R // SC_WINDOW), 0),
                                   pipeline_mode=pl.Buffered(1)),
                      pl.BlockSpec((1, SC_WINDOW), index_map=lambda i: (0, i))],
            out_specs=[],
            core_axis_name=("core", "subcore"),
            dimension_semantics=(pltpu.PARALLEL,),
        )(rows_hbm, idx_hbm)

    return scatter(rows, idx.reshape(1, M))


def _moe_kernel(be_ref, nu_ref, nv_ref, x_ref, wgu_ref, bgu_ref, wd_ref, bd_ref, y_ref, wgu_bf, wd_bf):
    i = pl.program_id(0)

    @pl.when(i >= nu_ref[0])
    def _():
        y_ref[...] = jnp.zeros_like(y_ref)

    @pl.when(jnp.logical_or(i == 0, be_ref[i] != be_ref[jnp.maximum(i - 1, 0)]))
    def _():
        rows = 128

        def cast(c, carry):
            r0 = pl.multiple_of(c * rows, rows)
            wgu_bf[pl.ds(r0, rows), :] = wgu_ref[0, pl.ds(r0, rows), :].astype(BF16)
            wd_bf[pl.ds(r0, rows), :] = wd_ref[0, pl.ds(r0, rows), :].astype(BF16)
            return carry

        lax.fori_loop(0, D_MODEL // rows, cast, 0)

    @pl.when(i < nu_ref[0])
    def _():
        live = lax.broadcasted_iota(jnp.int32, (x_ref.shape[0], 1), 0) < nv_ref[i]
        x = jnp.where(live, _unpack_rows(x_ref[...]), 0.0).astype(BF16)
        gu = jnp.dot(x, wgu_bf[...], preferred_element_type=F32) + bgu_ref[0]
        gate = jnp.minimum(gu[:, :D_MODEL], SWIGLU_LIMIT)
        up = jnp.clip(gu[:, D_MODEL:], -SWIGLU_LIMIT, SWIGLU_LIMIT)
        act = (up + 1.0) * (gate * jax.nn.sigmoid(SWIGLU_ALPHA * gate))
        y = jnp.dot(act.astype(BF16), wd_bf[...], preferred_element_type=F32) + bd_ref[0]
        y_ref[...] = _pack_rows(y)


def _moe_experts(xb, blk_e, n_used, n_valid, wgu, bgu, wd, bd, bm):
    NP = xb.shape[0]
    nblk = NP // bm
    grid_spec = pltpu.PrefetchScalarGridSpec(
        num_scalar_prefetch=3,
        grid=(nblk,),
        in_specs=[pl.BlockSpec((bm, HALF_D), lambda i, be, nu, nv: (jnp.minimum(i, nu[0] - 1), 0)),
                  pl.BlockSpec((1, D_MODEL, 2 * D_MODEL), lambda i, be, nu, nv: (be[i], 0, 0)),
                  pl.BlockSpec((1, 1, 2 * D_MODEL), lambda i, be, nu, nv: (be[i], 0, 0)),
                  pl.BlockSpec((1, D_MODEL, D_MODEL), lambda i, be, nu, nv: (be[i], 0, 0)),
                  pl.BlockSpec((1, 1, D_MODEL), lambda i, be, nu, nv: (be[i], 0, 0))],
        out_specs=pl.BlockSpec((bm, HALF_D), lambda i, be, nu, nv: (i, 0)),
        scratch_shapes=[pltpu.VMEM((D_MODEL, 2 * D_MODEL), BF16), pltpu.VMEM((D_MODEL, D_MODEL), BF16)],
    )
    return pl.pallas_call(
        _moe_kernel,
        grid_spec=grid_spec,
        out_shape=jax.ShapeDtypeStruct((NP, HALF_D), jnp.int32),
        compiler_params=_cparams("arbitrary"),
        name="moe_experts",
    )(blk_e, n_used, n_valid, xb, wgu, bgu, wd, bd)


def _final_kernel(x1_ref, y_ref, w_ref, g_ref, o_ref):
    w = w_ref[...]
    acc = x1_ref[...]
    for k in range(TOP_K):
        acc = acc + w[:, k:k + 1] * _unpack_rows(y_ref[k])
    o_ref[...] = acc * lax.rsqrt(jnp.mean(acc * acc, axis=-1, keepdims=True) + EPS) * g_ref[...]


def _final(x1, yk, wk, g, tm=512):
    T = x1.shape[0]
    return pl.pallas_call(
        _final_kernel,
        grid=(T // tm,),
        in_specs=[pl.BlockSpec((tm, D_MODEL), lambda i: (i, 0)),
                  pl.BlockSpec((TOP_K, tm, HALF_D), lambda i: (0, i, 0)),
                  pl.BlockSpec((tm, TOP_K), lambda i: (i, 0)),
                  pl.BlockSpec((1, D_MODEL), lambda i: (0, 0))],
        out_specs=pl.BlockSpec((tm, D_MODEL), lambda i: (i, 0)),
        out_shape=jax.ShapeDtypeStruct((T, D_MODEL), F32),
        compiler_params=_cparams("parallel"),
        name="final_norm",
    )(x1, yk, wk, g)


def _pack_w_in(w):
    kr = w[:, 1920:1952]
    half = B_ROPE // 2
    z = lambda n: jnp.zeros((w.shape[0], n), w.dtype)
    kr_placed = jnp.concatenate([z(B_NOPE), kr, z(HEAD_PAD - B_NOPE - B_ROPE)], axis=1)
    kr_rot = jnp.concatenate([z(B_NOPE), -kr[:, half:], kr[:, :half], z(HEAD_PAD - B_NOPE - B_ROPE)], axis=1)
    return jnp.concatenate([w[:, :1920], w[:, 1952:4000], kr_placed, kr_rot], axis=1).astype(BF16)


def _pack_w_uq(w):
    half = B_ROPE // 2
    r = w.shape[0]
    nope, x1, x2 = w[..., :B_NOPE], w[..., B_NOPE:B_NOPE + half], w[..., B_NOPE + half:]
    zpad = jnp.zeros((r, B_HEADS, HEAD_PAD - B_NOPE - B_ROPE), w.dtype)
    main = jnp.concatenate([nope, x1, x2, zpad], axis=-1)
    rot = jnp.concatenate([jnp.zeros_like(nope), -x2, x1, zpad], axis=-1)
    return (main.reshape(r, B_HEADS * HEAD_PAD).astype(BF16), rot.reshape(r, B_HEADS * HEAD_PAD).astype(BF16))


def _pack_w_uk(w):
    r = w.shape[0]
    zpad = jnp.zeros((r, B_HEADS, HEAD_PAD - B_NOPE), w.dtype)
    return jnp.concatenate([w, zpad], axis=-1).reshape(r, B_HEADS * HEAD_PAD).astype(BF16)


def _rope_tables(seq):
    half = B_ROPE // 2
    pos = jnp.arange(seq, dtype=F32)
    inv_freq = ROPE_THETA ** (-jnp.arange(0, B_ROPE, 2, dtype=F32) / B_ROPE)
    ang = pos[:, None] * inv_freq[None, :]
    cos, sin = jnp.cos(ang), jnp.sin(ang)
    one = jnp.ones((seq, B_NOPE), F32)
    zero = jnp.zeros((seq, B_NOPE), F32)
    zpad = jnp.zeros((seq, HEAD_PAD - B_NOPE - B_ROPE), F32)
    qs = ((B_NOPE + B_ROPE) ** -0.5) * LOG2E
    cq = jnp.concatenate([one, cos, cos, zpad], axis=1) * qs
    sq = jnp.concatenate([zero, sin, sin, zpad], axis=1) * qs
    ck = jnp.concatenate([zero, cos, cos, zpad], axis=1)
    sk = jnp.concatenate([zero, sin, sin, zpad], axis=1)
    return cq, sq, ck, sk


def kernel(x, norm1_g, w_in, q_a_norm_g, kv_a_norm_g, w_uq, w_uk, w_uv, w_a_out, w_b_out, w_o,
           norm2_g, w_router, b_router, w_gate_up, b_gate_up, w_down, b_down, final_g):
    B, S, D = x.shape
    T = B * S
    depth = norm1_g.shape[0]
    xt = x.reshape(T, D)
    cq_t, sq_t, ck_t, sk_t = _rope_tables(S)
    biases = [_dilated_bias(d) for _, d in A_PATTERNS]
    bm = 512
    for l in range(depth):
        wq1, wq2 = _pack_w_uq(w_uq[l])
        qa, ka, va, qa4, ka4, va4, qa16, ka16, va16, qb, kb, vb, gates = _inproj(
            xt, norm1_g[l][None], _pack_w_in(w_in[l]), q_a_norm_g[l][None], kv_a_norm_g[l][None],
            wq1, wq2, _pack_w_uk(w_uk[l]), w_uv[l].reshape(KV_LORA, B_WIDTH).T.astype(BF16),
            cq_t, sq_t, ck_t, sk_t, S)
        qkv = ((qa, ka, va), (qa4, ka4, va4), (qa16, ka16, va16))
        outs = [_dilated_pattern(*qkv[p], biases[p], B, S, A_PATTERNS[p][1]) for p in range(3)]
        ob = _mla_attention(qb.reshape(B, S, -1), kb.reshape(B, S, -1), vb, B, S)

        wr = jnp.pad(w_router[l], ((0, 0), (0, LOGIT_PAD - N_EXPERTS)))
        wrh = wr.astype(BF16)
        wrl = (wr - wrh.astype(F32)).astype(BF16)
        br = jnp.pad(b_router[l], (0, LOGIT_PAD - N_EXPERTS), constant_values=NEG)[None]
        x1, h2, route, count_rows = _outproj(
            outs[0][0], outs[1][0], outs[2][0], outs[0][1], outs[1][1], outs[2][1],
            ob.reshape(T, B_WIDTH), gates, xt,
            w_a_out[l].astype(BF16), w_b_out[l].astype(BF16), w_o[l].astype(BF16),
            norm2_g[l][None], wrh, wrl, br)

        top_i = route[:, 0:TOP_K].astype(jnp.int32)
        rank = route[:, TOP_K:2 * TOP_K].astype(jnp.int32)
        gate_w = route[:, 2 * TOP_K:3 * TOP_K]
        counts = count_rows[0, :N_EXPERTS].astype(jnp.int32)
        padded = (counts + bm - 1) // bm * bm
        pend = jnp.cumsum(padded)
        pstart = pend - padded
        dest = pstart[top_i] + rank
        NP = T * TOP_K + N_EXPERTS * bm
        nblk = NP // bm
        blk_start = jnp.arange(nblk, dtype=jnp.int32) * bm
        blk_e = jnp.minimum(jnp.sum(pend[None, :] <= blk_start[:, None], axis=1), N_EXPERTS - 1).astype(jnp.int32)
        n_used = (pend[-1] // bm).astype(jnp.int32)[None]
        n_valid = jnp.clip(pstart[blk_e] + counts[blk_e] - blk_start, 0, bm).astype(jnp.int32)

        dest_kt = dest.T.reshape(-1)
        xb = _scatter_rows(h2, dest_kt, NP)
        yb = _moe_experts(xb, blk_e, n_used, n_valid, w_gate_up[l], b_gate_up[l][:, None, :],
                          w_down[l], b_down[l][:, None, :], bm)
        yk = _gather_rows(yb, dest_kt).reshape(TOP_K, T, HALF_D)
        if l + 1 < depth:
            raise NotImplementedError("depth > 1")
        out = _final(x1, yk, gate_w, final_g[None])
    return out.reshape(B, S, D)
```

```python
import functools

import jax
import jax.numpy as jnp
import numpy as np
from jax import lax
from jax.experimental import pallas as pl
from jax.experimental.pallas import tpu as pltpu
from jax.experimental.pallas import tpu_sc as plsc

F32 = jnp.float32
BF16 = jnp.bfloat16

D_MODEL = 1024
EPS = 1e-5
NEG = -1e30

A_HEADS = 8
A_HEAD_DIM = 64
A_WIDTH = 512
A_PATTERNS = ((128, 1), (512, 4), (2048, 16))
A_HALF = 64
A_QB = 128
A_WIN = A_QB + 2 * A_HALF
assert all(w // (2 * d) == A_HALF for w, d in A_PATTERNS) and [d for _, d in A_PATTERNS] == [1, 4, 16]

B_HEADS = 8
B_NOPE = 64
B_ROPE = 32
B_V = 64
B_WIDTH = 512
Q_LORA = 256
KV_LORA = 128
ROPE_THETA = 10000.0
HEAD_PAD = 128
ONES_ROWS = 16
MLA_SCORE_BUFS = 4
MLA_HEADS_PER_STEP = 4

N_EXPERTS = 32
TOP_K = 4
SWIGLU_LIMIT = 7.0
SWIGLU_ALPHA = 1.702
LOGIT_PAD = 128

LOG2E = 1.4426950408889634

VMEM_LIMIT = 56 * 1024 * 1024


def _cparams(*sem):
    return pltpu.CompilerParams(dimension_semantics=sem, vmem_limit_bytes=VMEM_LIMIT)


def _inproj_kernel(x_ref, g1_ref, wp_ref, gq_ref, gkv_ref, wq1_ref, wq2_ref, wuk_ref, wuv_ref,
                   cq_ref, sq_ref, ck_ref, sk_ref,
                   qa_ref, ka_ref, va_ref, qa4_ref, ka4_ref, va4_ref, qa16_ref, ka16_ref, va16_ref,
                   qb_ref, kb_ref, vb_ref, gate_ref, perm_ref, perm4_ref):
    tm = x_ref.shape[0]
    x = x_ref[...]
    h = x * lax.rsqrt(jnp.mean(x * x, axis=-1, keepdims=True) + EPS) * g1_ref[...]
    h = h.astype(BF16)

    def proj(c0, c1):
        return jnp.dot(h, wp_ref[:, c0:c1], preferred_element_type=F32)

    def emit(which, z, nat_ref, v4_ref, v16_ref):
        nat_ref[...] = z.astype(BF16)
        nc = A_WIDTH // 128
        for c in range(nc):
            perm_ref[which, c] = z[:, c * 128:(c + 1) * 128]
        for r1 in range(4):
            for c in range(nc):
                quarter = perm_ref[which, c, pl.ds(r1, tm // 4, stride=4), :]
                v4_ref[:, r1 * A_WIDTH + c * 128:r1 * A_WIDTH + (c + 1) * 128] = quarter.astype(BF16)
                perm4_ref[which, r1, c] = quarter
        for r1 in range(4):
            for r2 in range(4):
                r = 4 * r2 + r1
                for c in range(nc):
                    v16_ref[:, r * A_WIDTH + c * 128:r * A_WIDTH + (c + 1) * 128] = (
                        perm4_ref[which, r1, c, pl.ds(r2, tm // 16, stride=4), :].astype(BF16))

    emit(0, proj(0, 512) * (A_HEAD_DIM ** -0.5), qa_ref, qa4_ref, qa16_ref)
    emit(1, proj(512, 1024), ka_ref, ka4_ref, ka16_ref)
    emit(2, proj(1024, 1536), va_ref, va4_ref, va16_ref)

    cq = proj(1536, 1792)
    cq = cq * lax.rsqrt(jnp.mean(cq * cq, axis=-1, keepdims=True) + EPS) * gq_ref[...]
    cq = cq.astype(BF16)
    q_main = jnp.dot(cq, wq1_ref[...], preferred_element_type=F32)
    q_rot = jnp.dot(cq, wq2_ref[...], preferred_element_type=F32)
    cq_t = cq_ref[...]
    sq_t = sq_ref[...]
    for hd in range(B_HEADS):
        sl = slice(hd * HEAD_PAD, (hd + 1) * HEAD_PAD)
        qb_ref[:, sl] = (q_main[:, sl] * cq_t + q_rot[:, sl] * sq_t).astype(BF16)

    ckv = proj(1792, 1920)
    ckv = ckv * lax.rsqrt(jnp.mean(ckv * ckv, axis=-1, keepdims=True) + EPS) * gkv_ref[...]
    ckv = ckv.astype(BF16)
    k_nope = jnp.dot(ckv, wuk_ref[...], preferred_element_type=F32)
    vb_ref[...] = lax.dot_general(wuv_ref[...], ckv, (((1,), (1,)), ((), ())),
                                  preferred_element_type=F32).astype(BF16)
    k_rope = proj(3968, 4096) * ck_ref[...] + proj(4096, 4224) * sk_ref[...]
    for hd in range(B_HEADS):
        sl = slice(hd * HEAD_PAD, (hd + 1) * HEAD_PAD)
        kb_ref[:, sl] = (k_nope[:, sl] + k_rope).astype(BF16)

    gate_ref[...] = jax.nn.sigmoid(proj(1920, 3968)).astype(BF16)


def _inproj(x2, g1, wp, gq, gkv, wq1, wq2, wuk, wuv, cq_t, sq_t, ck_t, sk_t, seq, tm=512):
    T = x2.shape[0]
    nseq = seq // tm
    row = lambda i: (i, 0)
    fix = lambda i: (0, 0)
    tab = lambda i: (i % nseq, 0)
    full = lambda a: pl.BlockSpec(a.shape, fix)
    sds = lambda r, c: jax.ShapeDtypeStruct((r, c), BF16)
    out_shape = (
        sds(T, A_WIDTH), sds(T, A_WIDTH), sds(T, A_WIDTH),
        sds(T // 4, 4 * A_WIDTH), sds(T // 4, 4 * A_WIDTH), sds(T // 4, 4 * A_WIDTH),
        sds(T // 16, 16 * A_WIDTH), sds(T // 16, 16 * A_WIDTH), sds(T // 16, 16 * A_WIDTH),
        sds(T, B_HEADS * HEAD_PAD), sds(T, B_HEADS * HEAD_PAD),
        sds(B_WIDTH, T),
        sds(T, 2 * D_MODEL),
    )
    out_specs = [pl.BlockSpec((tm * s.shape[0] // T, s.shape[1]), row) for s in out_shape]
    out_specs[11] = pl.BlockSpec((B_WIDTH, tm), lambda i: (0, i))
    return pl.pallas_call(
        _inproj_kernel,
        grid=(T // tm,),
        in_specs=[pl.BlockSpec((tm, D_MODEL), row), full(g1), full(wp), full(gq), full(gkv),
                  full(wq1), full(wq2), full(wuk), full(wuv),
                  pl.BlockSpec((tm, HEAD_PAD), tab), pl.BlockSpec((tm, HEAD_PAD), tab),
                  pl.BlockSpec((tm, HEAD_PAD), tab), pl.BlockSpec((tm, HEAD_PAD), tab)],
        out_specs=out_specs,
        out_shape=out_shape,
        scratch_shapes=[pltpu.VMEM((3, A_WIDTH // 128, tm, 128), F32),
                        pltpu.VMEM((3, 4, A_WIDTH // 128, tm // 4, 128), F32)],
        compiler_params=_cparams("parallel"),
        name="inproj",
    )(x2, g1, wp, gq, gkv, wq1, wq2, wuk, wuv, cq_t, sq_t, ck_t, sk_t)


def _dilated_kernel(q_ref, kc_ref, kp_ref, kn_ref, vc_ref, vp_ref, vn_ref, bias_ref,
                    o_ref, lse_ref, kbuf, vbuf, *, rows):
    i = pl.program_id(2)
    last = pl.num_programs(2) - 1
    kbuf[0:A_HALF, :] = kp_ref[0]
    kbuf[A_HALF:A_HALF + rows, :] = kc_ref[0]
    kbuf[A_HALF + rows:, :] = kn_ref[0]
    vbuf[0:A_HALF, :] = vp_ref[0]
    vbuf[A_HALF:A_HALF + rows, :] = vc_ref[0]
    vbuf[A_HALF + rows:, :] = vn_ref[0]

    nj = rows // A_QB
    npair = A_HEADS // 2
    lane = lax.broadcasted_iota(jnp.int32, (A_QB, 128), 1)
    lo_half = lane < A_HEAD_DIM

    def body(j, carry):
        r0 = pl.multiple_of(j * A_QB, A_QB)
        variant = (jnp.logical_and(i == 0, j == 0).astype(jnp.int32)
                   + 2 * jnp.logical_and(i == last, j == nj - 1).astype(jnp.int32))
        scores = []
        for pr in range(npair):
            ls = slice(pr * 128, (pr + 1) * 128)
            qp = q_ref[0, pl.ds(r0, A_QB), ls]
            zero = jnp.zeros_like(qp)
            q2 = jnp.concatenate([jnp.where(lo_half, qp, zero), jnp.where(lo_half, zero, qp)], axis=0)
            s = lax.dot_general(q2, kbuf[pl.ds(r0, A_WIN), ls], (((1,), (1,)), ((), ())),
                                preferred_element_type=F32)
            scores.append(s + bias_ref[variant * npair + pr])
        probs = []
        for s in scores:
            m = jnp.max(s, axis=-1, keepdims=True)
            p = jnp.exp(s - m)
            l = jnp.sum(p, axis=-1, keepdims=True)
            probs.append((p.astype(BF16), l, m + jnp.log(l)))
        for pr, (p, l, lse) in enumerate(probs):
            ls = slice(pr * 128, (pr + 1) * 128)
            o = jnp.dot(p, vbuf[pl.ds(r0, A_WIN), ls], preferred_element_type=F32) / l
            lse = jnp.broadcast_to(lse, (2 * A_QB, 128))
            o_ref[0, pl.ds(r0, A_QB), ls] = jnp.where(lo_half, o[:A_QB], o[A_QB:]).astype(o_ref.dtype)
            lse_ref[0, pl.ds(r0, A_QB), ls] = jnp.where(lo_half, lse[:A_QB], lse[A_QB:])
        return carry

    lax.fori_loop(0, nj, body, 0)


def _dilated_pattern(qa, ka, va, bias, batch, seq, dil, rows=1024):
    L = seq // dil
    rows = min(rows, L)
    nblk = L // rows
    hb = rows // A_HALF
    nh = L // A_HALF
    view = lambda a: a.reshape(batch, L, dil * A_WIDTH)
    cur = lambda b, r, i: (b, i, r)
    prev = lambda b, r, i: (b, jnp.maximum(i * hb - 1, 0), r)
    nxt = lambda b, r, i: (b, jnp.minimum((i + 1) * hb, nh - 1), r)
    cur_spec = pl.BlockSpec((1, rows, A_WIDTH), cur)
    out_sds = lambda dt: jax.ShapeDtypeStruct((batch, L, dil * A_WIDTH), dt)
    o, lse = pl.pallas_call(
        functools.partial(_dilated_kernel, rows=rows),
        grid=(batch, dil, nblk),
        in_specs=[cur_spec, cur_spec,
                  pl.BlockSpec((1, A_HALF, A_WIDTH), prev), pl.BlockSpec((1, A_HALF, A_WIDTH), nxt),
                  cur_spec,
                  pl.BlockSpec((1, A_HALF, A_WIDTH), prev), pl.BlockSpec((1, A_HALF, A_WIDTH), nxt),
                  pl.BlockSpec(bias.shape, lambda b, r, i: (0, 0, 0))],
        out_specs=[cur_spec, cur_spec],
        out_shape=(out_sds(BF16), out_sds(F32)),
        scratch_shapes=[pltpu.VMEM((rows + 2 * A_HALF, A_WIDTH), BF16),
                        pltpu.VMEM((rows + 2 * A_HALF, A_WIDTH), BF16)],
        compiler_params=_cparams("parallel", "parallel", "parallel"),
        name=f"dilated_d{dil}",
    )(view(qa), view(ka), view(ka), view(ka), view(va), view(va), view(va), bias)
    return o.reshape(batch * L, dil * A_WIDTH), lse.reshape(batch * L, dil * A_WIDTH)


def _dilated_bias(dil):
    slopes = 2.0 ** (-8.0 * (np.arange(A_HEADS, dtype=np.float64) + 1.0) / A_HEADS)
    col = np.arange(A_WIN)[None, :]
    rel = (col - A_HALF) - np.arange(A_QB)[:, None]
    bias = -slopes[:, None, None] * (np.abs(rel) * dil).astype(np.float64)[None]
    bias = np.where((np.abs(rel) <= A_HALF)[None], bias, NEG)
    variants = []
    for v in range(4):
        ok = np.ones((1, A_WIN), bool)
        if v & 1:
            ok = np.logical_and(ok, col >= A_HALF)
        if v & 2:
            ok = np.logical_and(ok, col < A_HALF + A_QB)
        variants.append(np.where(ok[None], bias, NEG).reshape(A_HEADS // 2, 2 * A_QB, A_WIN))
    return jnp.asarray(np.concatenate(variants, axis=0), F32)


def _mla_kernel(q_ref, k_ref, vt_ref, o_ref, *s_bufs, tk):
    tq = q_ref.shape[1]
    nk = k_ref.shape[1] // tk
    nh = q_ref.shape[2] // HEAD_PAD
    qs = [q_ref[0, :, hh * HEAD_PAD:(hh + 1) * HEAD_PAD] for hh in range(nh)]

    def scores(j, s_ref):
        off = pl.multiple_of(j * tk, tk)
        for hh in range(nh):
            s_ref[hh] = lax.dot_general(k_ref[0, pl.ds(off, tk), hh * HEAD_PAD:(hh + 1) * HEAD_PAD], qs[hh],
                                        (((1,), (1,)), ((), ())), preferred_element_type=F32)

    ones = jnp.ones((ONES_ROWS, tk), BF16)

    def softmax_pv(j, s_ref, stats):
        off = pl.multiple_of(j * tk, tk)
        new = []
        for hh in range(nh):
            m, acc = stats[hh]
            s = s_ref[hh]
            vt = jnp.concatenate([vt_ref[hh * B_V:(hh + 1) * B_V, pl.ds(off, tk)], ones], axis=0)
            m_new = jnp.maximum(m, jnp.max(s, axis=0, keepdims=True))
            alpha = jnp.exp2(m - m_new)
            p = jnp.exp2(s - m_new)
            acc = alpha * acc + jnp.dot(vt, p.astype(BF16), preferred_element_type=F32)
            new.append((m_new, acc))
        return tuple(new)

    nbuf = len(s_bufs)

    def body(jj, stats):
        j = nbuf * jj
        for u in range(nbuf):
            scores(jnp.minimum(j + u + 1, nk - 1), s_bufs[(u + 1) % nbuf])
            stats = softmax_pv(j + u, s_bufs[u], stats)
        return stats

    init = (jnp.full((1, tq), NEG, F32), jnp.zeros((B_V + ONES_ROWS, tq), F32))
    scores(0, s_bufs[0])
    stats = lax.fori_loop(0, nk // nbuf, body, (init,) * nh)
    o_t = jnp.concatenate([a[:B_V] / a[B_V:B_V + 1] for _, a in stats], axis=0)
    o_ref[0] = o_t.T.astype(BF16)


def _mla_attention(qb, kb, vbt, batch, seq, tq=256, tk=512, nh=MLA_HEADS_PER_STEP):
    once = dict(pipeline_mode=pl.Buffered(1))
    return pl.pallas_call(
        functools.partial(_mla_kernel, tk=tk),
        grid=(batch, B_HEADS // nh, seq // tq),
        in_specs=[pl.BlockSpec((1, tq, nh * HEAD_PAD), lambda b, p, i: (b, i, p)),
                  pl.BlockSpec((1, seq, nh * HEAD_PAD), lambda b, p, i: (b, 0, p), **once),
                  pl.BlockSpec((nh * B_V, seq), lambda b, p, i: (p, b), **once)],
        out_specs=pl.BlockSpec((1, tq, nh * B_V), lambda b, p, i: (b, i, p)),
        out_shape=jax.ShapeDtypeStruct((batch, seq, B_WIDTH), BF16),
        scratch_shapes=[pltpu.VMEM((nh, tk, tq), F32) for _ in range(MLA_SCORE_BUFS)],
        compiler_params=_cparams("parallel", "parallel", "parallel"),
        name="mla_attention",
    )(qb, kb, vbt)


def _outproj_kernel(o1_ref, o2_ref, o3_ref, l1_ref, l2_ref, l3_ref, ob_ref, gate_ref, x_ref,
                    wa_ref, wb_ref, wo_ref, g2_ref, wrh_ref, wrl_ref, br_ref,
                    tri_ref, x1_ref, h2_ref, route_ref, count_ref, perm_ref, perm4_ref, carry_ref):
    tm = x_ref.shape[0]

    @pl.when(pl.program_id(0) == 0)
    def _():
        carry_ref[...] = jnp.zeros_like(carry_ref)

    nc = A_WIDTH // 128

    def lanes(r, c):
        return slice(r * A_WIDTH + c * 128, r * A_WIDTH + (c + 1) * 128)

    def token_order(which, ref, d):
        if d == 4:
            for r in range(4):
                for c in range(nc):
                    perm_ref[which, c, pl.ds(r, tm // 4, stride=4), :] = ref[:, lanes(r, c)].astype(F32)
        else:
            for r1 in range(4):
                for r2 in range(4):
                    for c in range(nc):
                        perm4_ref[which - 2, r1, c, pl.ds(r2, tm // 16, stride=4), :] = (
                            ref[:, lanes(4 * r2 + r1, c)].astype(F32))
            for r1 in range(4):
                for c in range(nc):
                    perm_ref[which, c, pl.ds(r1, tm // 4, stride=4), :] = perm4_ref[which - 2, r1, c]
        return jnp.concatenate([perm_ref[which, c] for c in range(nc)], axis=1)

    l1 = l1_ref[...]
    l2, o2 = token_order(0, l2_ref, 4), token_order(1, o2_ref, 4)
    l3, o3 = token_order(2, l3_ref, 16), token_order(3, o3_ref, 16)
    mx = jnp.maximum(jnp.maximum(l1, l2), l3)
    e1, e2, e3 = jnp.exp(l1 - mx), jnp.exp(l2 - mx), jnp.exp(l3 - mx)
    oa = (e1 * o1_ref[...] + e2 * o2 + e3 * o3) / (e1 + e2 + e3)
    ya = jnp.dot(oa.astype(BF16), wa_ref[...], preferred_element_type=F32)
    yb = jnp.dot(ob_ref[...], wb_ref[...], preferred_element_type=F32)
    merged = gate_ref[:, :D_MODEL].astype(F32) * ya + gate_ref[:, D_MODEL:].astype(F32) * yb
    x1 = x_ref[...] + jnp.dot(merged.astype(BF16), wo_ref[...], preferred_element_type=F32)
    x1_ref[...] = x1
    h2 = x1 * lax.rsqrt(jnp.mean(x1 * x1, axis=-1, keepdims=True) + EPS) * g2_ref[...]
    hi = h2.astype(BF16)
    h2_ref[...] = _pack_rows(h2)
    lo = (h2 - hi.astype(F32)).astype(BF16)
    logits = jnp.dot(hi, wrh_ref[...], preferred_element_type=F32)
    logits = logits + jnp.dot(lo, wrh_ref[...], preferred_element_type=F32)
    logits = logits + jnp.dot(hi, wrl_ref[...], preferred_element_type=F32)
    logits = logits + br_ref[...]

    lane = lax.broadcasted_iota(jnp.int32, (tm, LOGIT_PAD), 1)
    work = logits
    sel = jnp.zeros((tm, LOGIT_PAD), F32)
    idx_cols, val_cols = [], []
    for _ in range(TOP_K):
        mx = jnp.max(work, axis=-1, keepdims=True)
        idx = jnp.min(jnp.where(work == mx, lane, LOGIT_PAD), axis=-1, keepdims=True)
        hit = lane == idx
        sel = sel + hit.astype(F32)
        work = jnp.where(hit, 2.0 * NEG, work)
        idx_cols.append(idx)
        val_cols.append(mx)
    exps = [jnp.exp(v - val_cols[0]) for v in val_cols]
    denom = exps[0] + exps[1] + exps[2] + exps[3]
    gate_cols = [e / denom for e in exps]

    before = jnp.dot(tri_ref[...], sel.astype(BF16), preferred_element_type=F32) + carry_ref[0:1, :]
    rank_cols = [jnp.sum(jnp.where(lane == idx, before, 0.0), axis=-1, keepdims=True) for idx in idx_cols]
    carry_ref[...] = carry_ref[...] + jnp.sum(sel, axis=0, keepdims=True)
    count_ref[...] = carry_ref[...]

    cols = [c.astype(F32) for c in idx_cols] + rank_cols + gate_cols
    route = jnp.zeros((tm, LOGIT_PAD), F32)
    for c, col in enumerate(cols):
        route = jnp.where(lane == c, col, route)
    route_ref[...] = route


def _outproj(o1, o2, o3, l1, l2, l3, ob, gates, x2, wa, wb, wo, g2, wrh, wrl, br, tm=512):
    T = x2.shape[0]
    row = lambda i: (i, 0)
    fix = lambda i: (0, 0)
    full = lambda a: pl.BlockSpec(a.shape, fix)
    rows = lambda a: pl.BlockSpec((tm * a.shape[0] // T, a.shape[1]), row)
    tri = (jnp.arange(tm)[:, None] > jnp.arange(tm)[None, :]).astype(BF16)
    out_shape = (jax.ShapeDtypeStruct((T, D_MODEL), F32),
                 jax.ShapeDtypeStruct((T, HALF_D), jnp.int32),
                 jax.ShapeDtypeStruct((T, LOGIT_PAD), F32),
                 jax.ShapeDtypeStruct((8, LOGIT_PAD), F32))
    return pl.pallas_call(
        _outproj_kernel,
        grid=(T // tm,),
        in_specs=[rows(o1), rows(o2), rows(o3), rows(l1), rows(l2), rows(l3), rows(ob), rows(gates),
                  rows(x2), full(wa), full(wb), full(wo), full(g2), full(wrh), full(wrl), full(br), full(tri)],
        out_specs=[pl.BlockSpec((tm, D_MODEL), row), pl.BlockSpec((tm, HALF_D), row),
                   pl.BlockSpec((tm, LOGIT_PAD), row), pl.BlockSpec((8, LOGIT_PAD), fix)],
        out_shape=out_shape,
        scratch_shapes=[pltpu.VMEM((4, A_WIDTH // 128, tm, 128), F32),
                        pltpu.VMEM((2, 4, A_WIDTH // 128, tm // 4, 128), F32),
                        pltpu.VMEM((8, LOGIT_PAD), F32)],
        compiler_params=_cparams("arbitrary"),
        name="outproj",
    )(o1, o2, o3, l1, l2, l3, ob, gates, x2, wa, wb, wo, g2, wrh, wrl, br, tri)


HALF_D = D_MODEL // 2


def _pack_rows(x):
    lo = pltpu.bitcast(x[:, :HALF_D].astype(BF16).astype(F32), jnp.uint32)
    hi = pltpu.bitcast(x[:, HALF_D:].astype(BF16).astype(F32), jnp.uint32)
    word = jnp.bitwise_or(lax.shift_right_logical(lo, jnp.uint32(16)), jnp.bitwise_and(hi, jnp.uint32(0xFFFF0000)))
    return pltpu.bitcast(word, jnp.int32)


def _unpack_rows(w):
    u = pltpu.bitcast(w, jnp.uint32)
    lo = pltpu.bitcast(lax.shift_left(u, jnp.uint32(16)), F32)
    hi = pltpu.bitcast(jnp.bitwise_and(u, jnp.uint32(0xFFFF0000)), F32)
    return jnp.concatenate([lo, hi], axis=1)


SC_WINDOW = 128


def _gather_rows(table, idx):
    M = idx.shape[0]
    width = table.shape[1]
    mesh = plsc.VectorSubcoreMesh(core_axis_name="core", subcore_axis_name="subcore")

    @functools.partial(pl.kernel, out_type=jax.ShapeDtypeStruct((M, width), table.dtype), mesh=mesh,
                       name="gather_rows")
    def gather(table_hbm, idx_hbm, out_hbm):
        def body(idx_vmem, out_vmem):
            pltpu.sync_copy(table_hbm.at[idx_vmem.at[0]], out_vmem)

        pltpu.emit_pipeline(
            body,
            grid=(M // SC_WINDOW,),
            in_specs=[pl.BlockSpec((1, SC_WINDOW), index_map=lambda i: (0, i))],
            out_specs=[pl.BlockSpec((SC_WINDOW, width), index_map=lambda i: (i, 0),
                                    pipeline_mode=pl.Buffered(1))],
            core_axis_name=("core", "subcore"),
            dimension_semantics=(pltpu.PARALLEL,),
        )(idx_hbm, out_hbm)

    return gather(table, idx.reshape(1, M))


def _scatter_rows(rows, idx, n_out):
    M = idx.shape[0]
    R, width = rows.shape
    mesh = plsc.VectorSubcoreMesh(core_axis_name="core", subcore_axis_name="subcore")

    @functools.partial(pl.kernel, out_type=jax.ShapeDtypeStruct((n_out, width), rows.dtype), mesh=mesh,
                       name="scatter_rows")
    def scatter(rows_hbm, idx_hbm, out_hbm):
        def body(rows_vmem, idx_vmem):
            pltpu.sync_copy(rows_vmem, out_hbm.at[idx_vmem.at[0]])

        pltpu.emit_pipeline(
            body,
            grid=(M // SC_WINDOW,),
            in_specs=[pl.BlockSpec((SC_WINDOW, width), index_map=lambda i: (i % (R // SC_WINDOW), 0),
                                   pipeline_mode=pl.Buffered(1)),
                      pl.BlockSpec((1, SC_WINDOW), index_map=lambda i: (0, i))],
            out_specs=[],
            core_axis_name=("core", "subcore"),
            dimension_semantics=(pltpu.PARALLEL,),
        )(rows_hbm, idx_hbm)

    return scatter(rows, idx.reshape(1, M))


def _moe_kernel(be_ref, nu_ref, nv_ref, x_ref, wgu_ref, bgu_ref, wd_ref, bd_ref, y_ref, wgu_bf, wd_bf):
    i = pl.program_id(0)

    @pl.when(i >= nu_ref[0])
    def _():
        y_ref[...] = jnp.zeros_like(y_ref)

    @pl.when(jnp.logical_or(i == 0, be_ref[i] != be_ref[jnp.maximum(i - 1, 0)]))
    def _():
        rows = 128

        def cast(c, carry):
            r0 = pl.multiple_of(c * rows, rows)
            wgu_bf[pl.ds(r0, rows), :] = wgu_ref[0, pl.ds(r0, rows), :].astype(BF16)
            wd_bf[pl.ds(r0, rows), :] = wd_ref[0, pl.ds(r0, rows), :].astype(BF16)
            return carry

        lax.fori_loop(0, D_MODEL // rows, cast, 0)

    @pl.when(i < nu_ref[0])
    def _():
        live = lax.broadcasted_iota(jnp.int32, (x_ref.shape[0], 1), 0) < nv_ref[i]
        x = jnp.where(live, _unpack_rows(x_ref[...]), 0.0).astype(BF16)
        gu = jnp.dot(x, wgu_bf[...], preferred_element_type=F32) + bgu_ref[0]
        gate = jnp.minimum(gu[:, :D_MODEL], SWIGLU_LIMIT)
        up = jnp.clip(gu[:, D_MODEL:], -SWIGLU_LIMIT, SWIGLU_LIMIT)
        act = (up + 1.0) * (gate * jax.nn.sigmoid(SWIGLU_ALPHA * gate))
        y = jnp.dot(act.astype(BF16), wd_bf[...], preferred_element_type=F32) + bd_ref[0]
        y_ref[...] = _pack_rows(y)


def _moe_experts(xb, blk_e, n_used, n_valid, wgu, bgu, wd, bd, bm):
    NP = xb.shape[0]
    nblk = NP // bm
    grid_spec = pltpu.PrefetchScalarGridSpec(
        num_scalar_prefetch=3,
        grid=(nblk,),
        in_specs=[pl.BlockSpec((bm, HALF_D), lambda i, be, nu, nv: (jnp.minimum(i, nu[0] - 1), 0)),
                  pl.BlockSpec((1, D_MODEL, 2 * D_MODEL), lambda i, be, nu, nv: (be[i], 0, 0)),
                  pl.BlockSpec((1, 1, 2 * D_MODEL), lambda i, be, nu, nv: (be[i], 0, 0)),
                  pl.BlockSpec((1, D_MODEL, D_MODEL), lambda i, be, nu, nv: (be[i], 0, 0)),
                  pl.BlockSpec((1, 1, D_MODEL), lambda i, be, nu, nv: (be[i], 0, 0))],
        out_specs=pl.BlockSpec((bm, HALF_D), lambda i, be, nu, nv: (i, 0)),
        scratch_shapes=[pltpu.VMEM((D_MODEL, 2 * D_MODEL), BF16), pltpu.VMEM((D_MODEL, D_MODEL), BF16)],
    )
    return pl.pallas_call(
        _moe_kernel,
        grid_spec=grid_spec,
        out_shape=jax.ShapeDtypeStruct((NP, HALF_D), jnp.int32),
        compiler_params=_cparams("arbitrary"),
        name="moe_experts",
    )(blk_e, n_used, n_valid, xb, wgu, bgu, wd, bd)


def _final_kernel(x1_ref, y_ref, w_ref, g_ref, o_ref):
    w = w_ref[...]
    acc = x1_ref[...]
    for k in range(TOP_K):
        acc = acc + w[:, k:k + 1] * _unpack_rows(y_ref[k])
    o_ref[...] = acc * lax.rsqrt(jnp.mean(acc * acc, axis=-1, keepdims=True) + EPS) * g_ref[...]


def _final(x1, yk, wk, g, tm=512):
    T = x1.shape[0]
    return pl.pallas_call(
        _final_kernel,
        grid=(T // tm,),
        in_specs=[pl.BlockSpec((tm, D_MODEL), lambda i: (i, 0)),
                  pl.BlockSpec((TOP_K, tm, HALF_D), lambda i: (0, i, 0)),
                  pl.BlockSpec((tm, TOP_K), lambda i: (i, 0)),
                  pl.BlockSpec((1, D_MODEL), lambda i: (0, 0))],
        out_specs=pl.BlockSpec((tm, D_MODEL), lambda i: (i, 0)),
        out_shape=jax.ShapeDtypeStruct((T, D_MODEL), F32),
        compiler_params=_cparams("parallel"),
        name="final_norm",
    )(x1, yk, wk, g)


def _pack_w_in(w):
    kr = w[:, 1920:1952]
    half = B_ROPE // 2
    z = lambda n: jnp.zeros((w.shape[0], n), w.dtype)
    kr_placed = jnp.concatenate([z(B_NOPE), kr, z(HEAD_PAD - B_NOPE - B_ROPE)], axis=1)
    kr_rot = jnp.concatenate([z(B_NOPE), -kr[:, half:], kr[:, :half], z(HEAD_PAD - B_NOPE - B_ROPE)], axis=1)
    return jnp.concatenate([w[:, :1920], w[:, 1952:4000], kr_placed, kr_rot], axis=1).astype(BF16)


def _pack_w_uq(w):
    half = B_ROPE // 2
    r = w.shape[0]
    nope, x1, x2 = w[..., :B_NOPE], w[..., B_NOPE:B_NOPE + half], w[..., B_NOPE + half:]
    zpad = jnp.zeros((r, B_HEADS, HEAD_PAD - B_NOPE - B_ROPE), w.dtype)
    main = jnp.concatenate([nope, x1, x2, zpad], axis=-1)
    rot = jnp.concatenate([jnp.zeros_like(nope), -x2, x1, zpad], axis=-1)
    return (main.reshape(r, B_HEADS * HEAD_PAD).astype(BF16), rot.reshape(r, B_HEADS * HEAD_PAD).astype(BF16))


def _pack_w_uk(w):
    r = w.shape[0]
    zpad = jnp.zeros((r, B_HEADS, HEAD_PAD - B_NOPE), w.dtype)
    return jnp.concatenate([w, zpad], axis=-1).reshape(r, B_HEADS * HEAD_PAD).astype(BF16)


def _rope_tables(seq):
    inv_freq = ROPE_THETA ** (-np.arange(0, B_ROPE, 2, dtype=np.float64) / B_ROPE)
    ang = np.arange(seq, dtype=np.float64)[:, None] * inv_freq[None, :]
    cos, sin = jnp.asarray(np.cos(ang), F32), jnp.asarray(np.sin(ang), F32)
    one = jnp.ones((seq, B_NOPE), F32)
    zero = jnp.zeros((seq, B_NOPE), F32)
    zpad = jnp.zeros((seq, HEAD_PAD - B_NOPE - B_ROPE), F32)
    qs = ((B_NOPE + B_ROPE) ** -0.5) * LOG2E
    cq = jnp.concatenate([one, cos, cos, zpad], axis=1) * qs
    sq = jnp.concatenate([zero, sin, sin, zpad], axis=1) * qs
    ck = jnp.concatenate([zero, cos, cos, zpad], axis=1)
    sk = jnp.concatenate([zero, sin, sin, zpad], axis=1)
    return cq, sq, ck, sk


def kernel(x, norm1_g, w_in, q_a_norm_g, kv_a_norm_g, w_uq, w_uk, w_uv, w_a_out, w_b_out, w_o,
           norm2_g, w_router, b_router, w_gate_up, b_gate_up, w_down, b_down, final_g):
    B, S, D = x.shape
    T = B * S
    depth = norm1_g.shape[0]
    xt = x.reshape(T, D)
    cq_t, sq_t, ck_t, sk_t = _rope_tables(S)
    biases = [_dilated_bias(d) for _, d in A_PATTERNS]
    bm = 512
    for l in range(depth):
        wq1, wq2 = _pack_w_uq(w_uq[l])
        qa, ka, va, qa4, ka4, va4, qa16, ka16, va16, qb, kb, vb, gates = _inproj(
            xt, norm1_g[l][None], _pack_w_in(w_in[l]), q_a_norm_g[l][None], kv_a_norm_g[l][None],
            wq1, wq2, _pack_w_uk(w_uk[l]), w_uv[l].reshape(KV_LORA, B_WIDTH).T.astype(BF16),
            cq_t, sq_t, ck_t, sk_t, S)
        qkv = ((qa, ka, va), (qa4, ka4, va4), (qa16, ka16, va16))
        outs = [_dilated_pattern(*qkv[p], biases[p], B, S, A_PATTERNS[p][1]) for p in range(3)]
        ob = _mla_attention(qb.reshape(B, S, -1), kb.reshape(B, S, -1), vb, B, S)

        wr = jnp.pad(w_router[l], ((0, 0), (0, LOGIT_PAD - N_EXPERTS)))
        wrh = wr.astype(BF16)
        wrl = (wr - wrh.astype(F32)).astype(BF16)
        br = jnp.pad(b_router[l], (0, LOGIT_PAD - N_EXPERTS), constant_values=NEG)[None]
        x1, h2, route, count_rows = _outproj(
            outs[0][0], outs[1][0], outs[2][0], outs[0][1], outs[1][1], outs[2][1],
            ob.reshape(T, B_WIDTH), gates, xt,
            w_a_out[l].astype(BF16), w_b_out[l].astype(BF16), w_o[l].astype(BF16),
            norm2_g[l][None], wrh, wrl, br)

        top_i = route[:, 0:TOP_K].astype(jnp.int32)
        rank = route[:, TOP_K:2 * TOP_K].astype(jnp.int32)
        gate_w = route[:, 2 * TOP_K:3 * TOP_K]
        counts = count_rows[0, :N_EXPERTS].astype(jnp.int32)
        padded = (counts + bm - 1) // bm * bm
        pend = jnp.cumsum(padded)
        pstart = pend - padded
        dest = pstart[top_i] + rank
        NP = T * TOP_K + N_EXPERTS * bm
        nblk = NP // bm
        blk_start = jnp.arange(nblk, dtype=jnp.int32) * bm
        blk_e = jnp.minimum(jnp.sum(pend[None, :] <= blk_start[:, None], axis=1), N_EXPERTS - 1).astype(jnp.int32)
        n_used = (pend[-1] // bm).astype(jnp.int32)[None]
        n_valid = jnp.clip(pstart[blk_e] + counts[blk_e] - blk_start, 0, bm).astype(jnp.int32)

        dest_kt = dest.T.reshape(-1)
        xb = _scatter_rows(h2, dest_kt, NP)
        yb = _moe_experts(xb, blk_e, n_used, n_valid, w_gate_up[l], b_gate_up[l][:, None, :],
                          w_down[l], b_down[l][:, None, :], bm)
        yk = _gather_rows(yb, dest_kt).reshape(TOP_K, T, HALF_D)
        if l + 1 < depth:
            raise NotImplementedError("depth > 1")
        out = _final(x1, yk, gate_w, final_g[None])
    return out.reshape(B, S, D)
```

```python
import functools

import jax
import jax.numpy as jnp
import numpy as np
from jax import lax
from jax.experimental import pallas as pl
from jax.experimental.pallas import tpu as pltpu
from jax.experimental.pallas import tpu_sc as plsc

F32 = jnp.float32
BF16 = jnp.bfloat16

D_MODEL = 1024
EPS = 1e-5
NEG = -1e30

A_HEADS = 8
A_HEAD_DIM = 64
A_WIDTH = 512
A_PATTERNS = ((128, 1), (512, 4), (2048, 16))
A_HALF = 64
A_QB = 128
A_WIN = A_QB + 2 * A_HALF
assert all(w // (2 * d) == A_HALF for w, d in A_PATTERNS) and [d for _, d in A_PATTERNS] == [1, 4, 16]

B_HEADS = 8
B_NOPE = 64
B_ROPE = 32
B_V = 64
B_WIDTH = 512
Q_LORA = 256
KV_LORA = 128
ROPE_THETA = 10000.0
HEAD_PAD = 128
ONES_ROWS = 16
MLA_SCORE_BUFS = 4
MLA_HEADS_PER_STEP = 4

N_EXPERTS = 32
TOP_K = 4
SWIGLU_LIMIT = 7.0
SWIGLU_ALPHA = 1.702
LOGIT_PAD = 128

LOG2E = 1.4426950408889634

VMEM_LIMIT = 56 * 1024 * 1024


def _cparams(*sem):
    return pltpu.CompilerParams(dimension_semantics=sem, vmem_limit_bytes=VMEM_LIMIT)


def _inproj_kernel(x_ref, g1_ref, wp_ref, gq_ref, gkv_ref, wq1_ref, wq2_ref, wuk_ref, wuv_ref,
                   cq_ref, sq_ref, ck_ref, sk_ref,
                   qa_ref, ka_ref, va_ref, qa4_ref, ka4_ref, va4_ref, qa16_ref, ka16_ref, va16_ref,
                   qb_ref, kb_ref, vb_ref, gate_ref, perm_ref, perm4_ref):
    tm = x_ref.shape[0]
    x = x_ref[...]
    h = x * lax.rsqrt(jnp.mean(x * x, axis=-1, keepdims=True) + EPS) * g1_ref[...]
    h = h.astype(BF16)

    def proj(c0, c1):
        return jnp.dot(h, wp_ref[:, c0:c1], preferred_element_type=F32)

    def emit(which, z, nat_ref, v4_ref, v16_ref):
        nat_ref[...] = z.astype(BF16)
        nc = A_WIDTH // 128
        for c in range(nc):
            perm_ref[which, c] = z[:, c * 128:(c + 1) * 128]
        for r1 in range(4):
            for c in range(nc):
                quarter = perm_ref[which, c, pl.ds(r1, tm // 4, stride=4), :]
                v4_ref[:, r1 * A_WIDTH + c * 128:r1 * A_WIDTH + (c + 1) * 128] = quarter.astype(BF16)
                perm4_ref[which, r1, c] = quarter
        for r1 in range(4):
            for r2 in range(4):
                r = 4 * r2 + r1
                for c in range(nc):
                    v16_ref[:, r * A_WIDTH + c * 128:r * A_WIDTH + (c + 1) * 128] = (
                        perm4_ref[which, r1, c, pl.ds(r2, tm // 16, stride=4), :].astype(BF16))

    emit(0, proj(0, 512) * (A_HEAD_DIM ** -0.5), qa_ref, qa4_ref, qa16_ref)
    emit(1, proj(512, 1024), ka_ref, ka4_ref, ka16_ref)
    emit(2, proj(1024, 1536), va_ref, va4_ref, va16_ref)

    cq = proj(1536, 1792)
    cq = cq * lax.rsqrt(jnp.mean(cq * cq, axis=-1, keepdims=True) + EPS) * gq_ref[...]
    cq = cq.astype(BF16)
    q_main = jnp.dot(cq, wq1_ref[...], preferred_element_type=F32)
    q_rot = jnp.dot(cq, wq2_ref[...], preferred_element_type=F32)
    cq_t = cq_ref[...]
    sq_t = sq_ref[...]
    for hd in range(B_HEADS):
        sl = slice(hd * HEAD_PAD, (hd + 1) * HEAD_PAD)
        qb_ref[:, sl] = (q_main[:, sl] * cq_t + q_rot[:, sl] * sq_t).astype(BF16)

    ckv = proj(1792, 1920)
    ckv = ckv * lax.rsqrt(jnp.mean(ckv * ckv, axis=-1, keepdims=True) + EPS) * gkv_ref[...]
    ckv = ckv.astype(BF16)
    k_nope = jnp.dot(ckv, wuk_ref[...], preferred_element_type=F32)
    vb_ref[...] = lax.dot_general(wuv_ref[...], ckv, (((1,), (1,)), ((), ())),
                                  preferred_element_type=F32).astype(BF16)
    k_rope = proj(3968, 4096) * ck_ref[...] + proj(4096, 4224) * sk_ref[...]
    for hd in range(B_HEADS):
        sl = slice(hd * HEAD_PAD, (hd + 1) * HEAD_PAD)
        kb_ref[:, sl] = (k_nope[:, sl] + k_rope).astype(BF16)

    gate_ref[...] = jax.nn.sigmoid(proj(1920, 3968)).astype(BF16)


def _inproj(x2, g1, wp, gq, gkv, wq1, wq2, wuk, wuv, cq_t, sq_t, ck_t, sk_t, seq, tm=512):
    T = x2.shape[0]
    nseq = seq // tm
    row = lambda i: (i, 0)
    fix = lambda i: (0, 0)
    tab = lambda i: (i % nseq, 0)
    full = lambda a: pl.BlockSpec(a.shape, fix)
    sds = lambda r, c: jax.ShapeDtypeStruct((r, c), BF16)
    out_shape = (
        sds(T, A_WIDTH), sds(T, A_WIDTH), sds(T, A_WIDTH),
        sds(T // 4, 4 * A_WIDTH), sds(T // 4, 4 * A_WIDTH), sds(T // 4, 4 * A_WIDTH),
        sds(T // 16, 16 * A_WIDTH), sds(T // 16, 16 * A_WIDTH), sds(T // 16, 16 * A_WIDTH),
        sds(T, B_HEADS * HEAD_PAD), sds(T, B_HEADS * HEAD_PAD),
        sds(B_WIDTH, T),
        sds(T, 2 * D_MODEL),
    )
    out_specs = [pl.BlockSpec((tm * s.shape[0] // T, s.shape[1]), row) for s in out_shape]
    out_specs[11] = pl.BlockSpec((B_WIDTH, tm), lambda i: (0, i))
    return pl.pallas_call(
        _inproj_kernel,
        grid=(T // tm,),
        in_specs=[pl.BlockSpec((tm, D_MODEL), row), full(g1), full(wp), full(gq), full(gkv),
                  full(wq1), full(wq2), full(wuk), full(wuv),
                  pl.BlockSpec((tm, HEAD_PAD), tab), pl.BlockSpec((tm, HEAD_PAD), tab),
                  pl.BlockSpec((tm, HEAD_PAD), tab), pl.BlockSpec((tm, HEAD_PAD), tab)],
        out_specs=out_specs,
        out_shape=out_shape,
        scratch_shapes=[pltpu.VMEM((3, A_WIDTH // 128, tm, 128), F32),
                        pltpu.VMEM((3, 4, A_WIDTH // 128, tm // 4, 128), F32)],
        compiler_params=_cparams("parallel"),
        name="inproj",
    )(x2, g1, wp, gq, gkv, wq1, wq2, wuk, wuv, cq_t, sq_t, ck_t, sk_t)


def _dilated_kernel(q_ref, kc_ref, kp_ref, kn_ref, vc_ref, vp_ref, vn_ref, bias_ref,
                    o_ref, lse_ref, kbuf, vbuf, *, rows):
    i = pl.program_id(2)
    last = pl.num_programs(2) - 1
    kbuf[0:A_HALF, :] = kp_ref[0]
    kbuf[A_HALF:A_HALF + rows, :] = kc_ref[0]
    kbuf[A_HALF + rows:, :] = kn_ref[0]
    vbuf[0:A_HALF, :] = vp_ref[0]
    vbuf[A_HALF:A_HALF + rows, :] = vc_ref[0]
    vbuf[A_HALF + rows:, :] = vn_ref[0]

    nj = rows // A_QB
    npair = A_HEADS // 2
    lane = lax.broadcasted_iota(jnp.int32, (A_QB, 128), 1)
    lo_half = lane < A_HEAD_DIM

    def body(j, carry):
        r0 = pl.multiple_of(j * A_QB, A_QB)
        variant = (jnp.logical_and(i == 0, j == 0).astype(jnp.int32)
                   + 2 * jnp.logical_and(i == last, j == nj - 1).astype(jnp.int32))
        scores = []
        for pr in range(npair):
            ls = slice(pr * 128, (pr + 1) * 128)
            qp = q_ref[0, pl.ds(r0, A_QB), ls]
            zero = jnp.zeros_like(qp)
            q2 = jnp.concatenate([jnp.where(lo_half, qp, zero), jnp.where(lo_half, zero, qp)], axis=0)
            s = lax.dot_general(q2, kbuf[pl.ds(r0, A_WIN), ls], (((1,), (1,)), ((), ())),
                                preferred_element_type=F32)
            scores.append(s + bias_ref[variant * npair + pr])
        probs = []
        for s in scores:
            m = jnp.max(s, axis=-1, keepdims=True)
            p = jnp.exp(s - m)
            l = jnp.sum(p, axis=-1, keepdims=True)
            probs.append((p.astype(BF16), l, m + jnp.log(l)))
        for pr, (p, l, lse) in enumerate(probs):
            ls = slice(pr * 128, (pr + 1) * 128)
            o = jnp.dot(p, vbuf[pl.ds(r0, A_WIN), ls], preferred_element_type=F32) / l
            lse = jnp.broadcast_to(lse, (2 * A_QB, 128))
            o_ref[0, pl.ds(r0, A_QB), ls] = jnp.where(lo_half, o[:A_QB], o[A_QB:]).astype(o_ref.dtype)
            lse_ref[0, pl.ds(r0, A_QB), ls] = jnp.where(lo_half, lse[:A_QB], lse[A_QB:])
        return carry

    lax.fori_loop(0, nj, body, 0)


def _dilated_pattern(qa, ka, va, bias, batch, seq, dil, rows=1024):
    L = seq // dil
    rows = min(rows, L)
    nblk = L // rows
    hb = rows // A_HALF
    nh = L // A_HALF
    view = lambda a: a.reshape(batch, L, dil * A_WIDTH)
    cur = lambda b, r, i: (b, i, r)
    prev = lambda b, r, i: (b, jnp.maximum(i * hb - 1, 0), r)
    nxt = lambda b, r, i: (b, jnp.minimum((i + 1) * hb, nh - 1), r)
    cur_spec = pl.BlockSpec((1, rows, A_WIDTH), cur)
    out_sds = lambda dt: jax.ShapeDtypeStruct((batch, L, dil * A_WIDTH), dt)
    o, lse = pl.pallas_call(
        functools.partial(_dilated_kernel, rows=rows),
        grid=(batch, dil, nblk),
        in_specs=[cur_spec, cur_spec,
                  pl.BlockSpec((1, A_HALF, A_WIDTH), prev), pl.BlockSpec((1, A_HALF, A_WIDTH), nxt),
                  cur_spec,
                  pl.BlockSpec((1, A_HALF, A_WIDTH), prev), pl.BlockSpec((1, A_HALF, A_WIDTH), nxt),
                  pl.BlockSpec(bias.shape, lambda b, r, i: (0, 0, 0))],
        out_specs=[cur_spec, cur_spec],
        out_shape=(out_sds(BF16), out_sds(F32)),
        scratch_shapes=[pltpu.VMEM((rows + 2 * A_HALF, A_WIDTH), BF16),
                        pltpu.VMEM((rows + 2 * A_HALF, A_WIDTH), BF16)],
        compiler_params=_cparams("parallel", "parallel", "parallel"),
        name=f"dilated_d{dil}",
    )(view(qa), view(ka), view(ka), view(ka), view(va), view(va), view(va), bias)
    return o.reshape(batch * L, dil * A_WIDTH), lse.reshape(batch * L, dil * A_WIDTH)


def _dilated_bias(dil):
    slopes = 2.0 ** (-8.0 * (np.arange(A_HEADS, dtype=np.float64) + 1.0) / A_HEADS)
    col = np.arange(A_WIN)[None, :]
    rel = (col - A_HALF) - np.arange(A_QB)[:, None]
    bias = -slopes[:, None, None] * (np.abs(rel) * dil).astype(np.float64)[None]
    bias = np.where((np.abs(rel) <= A_HALF)[None], bias, NEG)
    variants = []
    for v in range(4):
        ok = np.ones((1, A_WIN), bool)
        if v & 1:
            ok = np.logical_and(ok, col >= A_HALF)
        if v & 2:
            ok = np.logical_and(ok, col < A_HALF + A_QB)
        variants.append(np.where(ok[None], bias, NEG).reshape(A_HEADS // 2, 2 * A_QB, A_WIN))
    return jnp.asarray(np.concatenate(variants, axis=0), F32)


def _mla_kernel(q_ref, k_ref, vt_ref, o_ref, *s_bufs, tk):
    tq = q_ref.shape[1]
    nk = k_ref.shape[1] // tk
    nh = q_ref.shape[2] // HEAD_PAD
    qs = [q_ref[0, :, hh * HEAD_PAD:(hh + 1) * HEAD_PAD] for hh in range(nh)]

    def scores(j, s_ref):
        off = pl.multiple_of(j * tk, tk)
        for hh in range(nh):
            s_ref[hh] = lax.dot_general(k_ref[0, pl.ds(off, tk), hh * HEAD_PAD:(hh + 1) * HEAD_PAD], qs[hh],
                                        (((1,), (1,)), ((), ())), preferred_element_type=F32)

    ones = jnp.ones((ONES_ROWS, tk), BF16)

    def softmax_pv(j, s_ref, stats):
        off = pl.multiple_of(j * tk, tk)
        new = []
        for hh in range(nh):
            m, acc = stats[hh]
            s = s_ref[hh]
            vt = jnp.concatenate([vt_ref[hh * B_V:(hh + 1) * B_V, pl.ds(off, tk)], ones], axis=0)
            m_new = jnp.maximum(m, jnp.max(s, axis=0, keepdims=True))
            alpha = jnp.exp2(m - m_new)
            p = jnp.exp2(s - m_new)
            acc = alpha * acc + jnp.dot(vt, p.astype(BF16), preferred_element_type=F32)
            new.append((m_new, acc))
        return tuple(new)

    nbuf = len(s_bufs)

    def body(jj, stats):
        j = nbuf * jj
        for u in range(nbuf):
            scores(jnp.minimum(j + u + 1, nk - 1), s_bufs[(u + 1) % nbuf])
            stats = softmax_pv(j + u, s_bufs[u], stats)
        return stats

    init = (jnp.full((1, tq), NEG, F32), jnp.zeros((B_V + ONES_ROWS, tq), F32))
    scores(0, s_bufs[0])
    stats = lax.fori_loop(0, nk // nbuf, body, (init,) * nh)
    o_t = jnp.concatenate([a[:B_V] / a[B_V:B_V + 1] for _, a in stats], axis=0)
    o_ref[0] = o_t.T.astype(BF16)


def _mla_attention(qb, kb, vbt, batch, seq, tq=256, tk=512, nh=MLA_HEADS_PER_STEP):
    once = dict(pipeline_mode=pl.Buffered(1))
    return pl.pallas_call(
        functools.partial(_mla_kernel, tk=tk),
        grid=(batch, B_HEADS // nh, seq // tq),
        in_specs=[pl.BlockSpec((1, tq, nh * HEAD_PAD), lambda b, p, i: (b, i, p)),
                  pl.BlockSpec((1, seq, nh * HEAD_PAD), lambda b, p, i: (b, 0, p), **once),
                  pl.BlockSpec((nh * B_V, seq), lambda b, p, i: (p, b), **once)],
        out_specs=pl.BlockSpec((1, tq, nh * B_V), lambda b, p, i: (b, i, p)),
        out_shape=jax.ShapeDtypeStruct((batch, seq, B_WIDTH), BF16),
        scratch_shapes=[pltpu.VMEM((nh, tk, tq), F32) for _ in range(MLA_SCORE_BUFS)],
        compiler_params=_cparams("parallel", "parallel", "parallel"),
        name="mla_attention",
    )(qb, kb, vbt)


def _outproj_kernel(o1_ref, o2_ref, o3_ref, l1_ref, l2_ref, l3_ref, ob_ref, gate_ref, x_ref,
                    wa_ref, wb_ref, wo_ref, g2_ref, wrh_ref, wrl_ref, br_ref,
                    tri_ref, x1_ref, h2_ref, route_ref, count_ref, perm_ref, perm4_ref, carry_ref):
    tm = x_ref.shape[0]

    @pl.when(pl.program_id(0) == 0)
    def _():
        carry_ref[...] = jnp.zeros_like(carry_ref)

    nc = A_WIDTH // 128

    def lanes(r, c):
        return slice(r * A_WIDTH + c * 128, r * A_WIDTH + (c + 1) * 128)

    def token_order(which, ref, d):
        if d == 4:
            for r in range(4):
                for c in range(nc):
                    perm_ref[which, c, pl.ds(r, tm // 4, stride=4), :] = ref[:, lanes(r, c)].astype(F32)
        else:
            for r1 in range(4):
                for r2 in range(4):
                    for c in range(nc):
                        perm4_ref[which - 2, r1, c, pl.ds(r2, tm // 16, stride=4), :] = (
                            ref[:, lanes(4 * r2 + r1, c)].astype(F32))
            for r1 in range(4):
                for c in range(nc):
                    perm_ref[which, c, pl.ds(r1, tm // 4, stride=4), :] = perm4_ref[which - 2, r1, c]
        return jnp.concatenate([perm_ref[which, c] for c in range(nc)], axis=1)

    l1 = l1_ref[...]
    l2, o2 = token_order(0, l2_ref, 4), token_order(1, o2_ref, 4)
    l3, o3 = token_order(2, l3_ref, 16), token_order(3, o3_ref, 16)
    mx = jnp.maximum(jnp.maximum(l1, l2), l3)
    e1, e2, e3 = jnp.exp(l1 - mx), jnp.exp(l2 - mx), jnp.exp(l3 - mx)
    oa = (e1 * o1_ref[...] + e2 * o2 + e3 * o3) / (e1 + e2 + e3)
    ya = jnp.dot(oa.astype(BF16), wa_ref[...], preferred_element_type=F32)
    yb = jnp.dot(ob_ref[...], wb_ref[...], preferred_element_type=F32)
    merged = gate_ref[:, :D_MODEL].astype(F32) * ya + gate_ref[:, D_MODEL:].astype(F32) * yb
    x1 = x_ref[...] + jnp.dot(merged.astype(BF16), wo_ref[...], preferred_element_type=F32)
    x1_ref[...] = x1
    h2 = x1 * lax.rsqrt(jnp.mean(x1 * x1, axis=-1, keepdims=True) + EPS) * g2_ref[...]
    hi = h2.astype(BF16)
    h2_ref[...] = _pack_rows(h2)
    lo = (h2 - hi.astype(F32)).astype(BF16)
    logits = jnp.dot(hi, wrh_ref[...], preferred_element_type=F32)
    logits = logits + jnp.dot(lo, wrh_ref[...], preferred_element_type=F32)
    logits = logits + jnp.dot(hi, wrl_ref[...], preferred_element_type=F32)
    logits = logits + br_ref[...]

    lane = lax.broadcasted_iota(jnp.int32, (tm, LOGIT_PAD), 1)
    work = logits
    sel = jnp.zeros((tm, LOGIT_PAD), F32)
    idx_cols, val_cols = [], []
    for _ in range(TOP_K):
        mx = jnp.max(work, axis=-1, keepdims=True)
        idx = jnp.min(jnp.where(work == mx, lane, LOGIT_PAD), axis=-1, keepdims=True)
        hit = lane == idx
        sel = sel + hit.astype(F32)
        work = jnp.where(hit, 2.0 * NEG, work)
        idx_cols.append(idx)
        val_cols.append(mx)
    exps = [jnp.exp(v - val_cols[0]) for v in val_cols]
    denom = exps[0] + exps[1] + exps[2] + exps[3]
    gate_cols = [e / denom for e in exps]

    before = jnp.dot(tri_ref[...], sel.astype(BF16), preferred_element_type=F32) + carry_ref[0:1, :]
    rank_cols = [jnp.sum(jnp.where(lane == idx, before, 0.0), axis=-1, keepdims=True) for idx in idx_cols]
    carry_ref[...] = carry_ref[...] + jnp.sum(sel, axis=0, keepdims=True)
    count_ref[...] = carry_ref[...]

    cols = [c.astype(F32) for c in idx_cols] + rank_cols + gate_cols
    route = jnp.zeros((tm, LOGIT_PAD), F32)
    for c, col in enumerate(cols):
        route = jnp.where(lane == c, col, route)
    route_ref[...] = route


def _outproj(o1, o2, o3, l1, l2, l3, ob, gates, x2, wa, wb, wo, g2, wrh, wrl, br, tm=512):
    T = x2.shape[0]
    row = lambda i: (i, 0)
    fix = lambda i: (0, 0)
    full = lambda a: pl.BlockSpec(a.shape, fix)
    rows = lambda a: pl.BlockSpec((tm * a.shape[0] // T, a.shape[1]), row)
    tri = (jnp.arange(tm)[:, None] > jnp.arange(tm)[None, :]).astype(BF16)
    out_shape = (jax.ShapeDtypeStruct((T, D_MODEL), F32),
                 jax.ShapeDtypeStruct((T, HALF_D), jnp.int32),
                 jax.ShapeDtypeStruct((T, LOGIT_PAD), F32),
                 jax.ShapeDtypeStruct((8, LOGIT_PAD), F32))
    return pl.pallas_call(
        _outproj_kernel,
        grid=(T // tm,),
        in_specs=[rows(o1), rows(o2), rows(o3), rows(l1), rows(l2), rows(l3), rows(ob), rows(gates),
                  rows(x2), full(wa), full(wb), full(wo), full(g2), full(wrh), full(wrl), full(br), full(tri)],
        out_specs=[pl.BlockSpec((tm, D_MODEL), row), pl.BlockSpec((tm, HALF_D), row),
                   pl.BlockSpec((tm, LOGIT_PAD), row), pl.BlockSpec((8, LOGIT_PAD), fix)],
        out_shape=out_shape,
        scratch_shapes=[pltpu.VMEM((4, A_WIDTH // 128, tm, 128), F32),
                        pltpu.VMEM((2, 4, A_WIDTH // 128, tm // 4, 128), F32),
                        pltpu.VMEM((8, LOGIT_PAD), F32)],
        compiler_params=_cparams("arbitrary"),
        name="outproj",
    )(o1, o2, o3, l1, l2, l3, ob, gates, x2, wa, wb, wo, g2, wrh, wrl, br, tri)


HALF_D = D_MODEL // 2


def _pack_rows(x):
    lo = pltpu.bitcast(x[:, :HALF_D].astype(BF16).astype(F32), jnp.uint32)
    hi = pltpu.bitcast(x[:, HALF_D:].astype(BF16).astype(F32), jnp.uint32)
    word = jnp.bitwise_or(lax.shift_right_logical(lo, jnp.uint32(16)), jnp.bitwise_and(hi, jnp.uint32(0xFFFF0000)))
    return pltpu.bitcast(word, jnp.int32)


def _unpack_rows(w):
    u = pltpu.bitcast(w, jnp.uint32)
    lo = pltpu.bitcast(lax.shift_left(u, jnp.uint32(16)), F32)
    hi = pltpu.bitcast(jnp.bitwise_and(u, jnp.uint32(0xFFFF0000)), F32)
    return jnp.concatenate([lo, hi], axis=1)


SC_WINDOW = 128


def _gather_rows(table, idx):
    M = idx.shape[0]
    width = table.shape[1]
    mesh = plsc.VectorSubcoreMesh(core_axis_name="core", subcore_axis_name="subcore")

    @functools.partial(pl.kernel, out_type=jax.ShapeDtypeStruct((M, width), table.dtype), mesh=mesh,
                       name="gather_rows")
    def gather(table_hbm, idx_hbm, out_hbm):
        def body(idx_vmem, out_vmem):
            pltpu.sync_copy(table_hbm.at[idx_vmem.at[0]], out_vmem)

        pltpu.emit_pipeline(
            body,
            grid=(M // SC_WINDOW,),
            in_specs=[pl.BlockSpec((1, SC_WINDOW), index_map=lambda i: (0, i))],
            out_specs=[pl.BlockSpec((SC_WINDOW, width), index_map=lambda i: (i, 0),
                                    pipeline_mode=pl.Buffered(1))],
            core_axis_name=("core", "subcore"),
            dimension_semantics=(pltpu.PARALLEL,),
        )(idx_hbm, out_hbm)

    return gather(table, idx.reshape(1, M))


def _scatter_rows(rows, idx, n_out):
    M = idx.shape[0]
    R, width = rows.shape
    mesh = plsc.VectorSubcoreMesh(core_axis_name="core", subcore_axis_name="subcore")

    @functools.partial(pl.kernel, out_type=jax.ShapeDtypeStruct((n_out, width), rows.dtype), mesh=mesh,
                       name="scatter_rows")
    def scatter(rows_hbm, idx_hbm, out_hbm):
        def body(rows_vmem, idx_vmem):
            pltpu.sync_copy(rows_vmem, out_hbm.at[idx_vmem.at[0]])

        pltpu.emit_pipeline(
            body,
            grid=(M // SC_WINDOW,),
            in_specs=[pl.BlockSpec((SC_WINDOW, width), index_map=lambda i: (i % (R // SC_WINDOW), 0),
                                   pipeline_mode=pl.Buffered(1)),
                      pl.BlockSpec((1, SC_WINDOW), index_map=lambda i: (0, i))],
            out_specs=[],
            core_axis_name=("core", "subcore"),
            dimension_semantics=(pltpu.PARALLEL,),
        )(rows_hbm, idx_hbm)

    return scatter(rows, idx.reshape(1, M))


def _moe_kernel(be_ref, nu_ref, nv_ref, x_ref, wgu_ref, bgu_ref, wd_ref, bd_ref, y_ref, wgu_bf, wd_bf):
    i = pl.program_id(0)

    @pl.when(i >= nu_ref[0])
    def _():
        y_ref[...] = jnp.zeros_like(y_ref)

    @pl.when(jnp.logical_or(i == 0, be_ref[i] != be_ref[jnp.maximum(i - 1, 0)]))
    def _():
        rows = 128

        def cast(c, carry):
            r0 = pl.multiple_of(c * rows, rows)
            wgu_bf[pl.ds(r0, rows), :] = wgu_ref[0, pl.ds(r0, rows), :].astype(BF16)
            wd_bf[pl.ds(r0, rows), :] = wd_ref[0, pl.ds(r0, rows), :].astype(BF16)
            return carry

        lax.fori_loop(0, D_MODEL // rows, cast, 0)

    @pl.when(i < nu_ref[0])
    def _():
        live = lax.broadcasted_iota(jnp.int32, (x_ref.shape[0], 1), 0) < nv_ref[i]
        x = jnp.where(live, _unpack_rows(x_ref[...]), 0.0).astype(BF16)
        gu = jnp.dot(x, wgu_bf[...], preferred_element_type=F32) + bgu_ref[0]
        gate = jnp.minimum(gu[:, :D_MODEL], SWIGLU_LIMIT)
        up = jnp.clip(gu[:, D_MODEL:], -SWIGLU_LIMIT, SWIGLU_LIMIT)
        act = (up + 1.0) * (gate * jax.nn.sigmoid(SWIGLU_ALPHA * gate))
        y = jnp.dot(act.astype(BF16), wd_bf[...], preferred_element_type=F32) + bd_ref[0]
        y_ref[...] = _pack_rows(y)


def _moe_experts(xb, blk_e, n_used, n_valid, wgu, bgu, wd, bd, bm):
    NP = xb.shape[0]
    nblk = NP // bm
    grid_spec = pltpu.PrefetchScalarGridSpec(
        num_scalar_prefetch=3,
        grid=(nblk,),
        in_specs=[pl.BlockSpec((bm, HALF_D), lambda i, be, nu, nv: (jnp.minimum(i, nu[0] - 1), 0)),
                  pl.BlockSpec((1, D_MODEL, 2 * D_MODEL), lambda i, be, nu, nv: (be[i], 0, 0)),
                  pl.BlockSpec((1, 1, 2 * D_MODEL), lambda i, be, nu, nv: (be[i], 0, 0)),
                  pl.BlockSpec((1, D_MODEL, D_MODEL), lambda i, be, nu, nv: (be[i], 0, 0)),
                  pl.BlockSpec((1, 1, D_MODEL), lambda i, be, nu, nv: (be[i], 0, 0))],
        out_specs=pl.BlockSpec((bm, HALF_D), lambda i, be, nu, nv: (i, 0)),
        scratch_shapes=[pltpu.VMEM((D_MODEL, 2 * D_MODEL), BF16), pltpu.VMEM((D_MODEL, D_MODEL), BF16)],
    )
    return pl.pallas_call(
        _moe_kernel,
        grid_spec=grid_spec,
        out_shape=jax.ShapeDtypeStruct((NP, HALF_D), jnp.int32),
        compiler_params=_cparams("arbitrary"),
        name="moe_experts",
    )(blk_e, n_used, n_valid, xb, wgu, bgu, wd, bd)


def _final_kernel(x1_ref, y_ref, route_ref, g_ref, o_ref):
    w = route_ref[...]
    acc = x1_ref[...]
    for k in range(TOP_K):
        acc = acc + w[:, 2 * TOP_K + k:2 * TOP_K + k + 1] * _unpack_rows(y_ref[k])
    o_ref[...] = acc * lax.rsqrt(jnp.mean(acc * acc, axis=-1, keepdims=True) + EPS) * g_ref[...]


def _final(x1, yk, route, g, tm=512):
    T = x1.shape[0]
    return pl.pallas_call(
        _final_kernel,
        grid=(T // tm,),
        in_specs=[pl.BlockSpec((tm, D_MODEL), lambda i: (i, 0)),
                  pl.BlockSpec((TOP_K, tm, HALF_D), lambda i: (0, i, 0)),
                  pl.BlockSpec((tm, LOGIT_PAD), lambda i: (i, 0)),
                  pl.BlockSpec((1, D_MODEL), lambda i: (0, 0))],
        out_specs=pl.BlockSpec((tm, D_MODEL), lambda i: (i, 0)),
        out_shape=jax.ShapeDtypeStruct((T, D_MODEL), F32),
        compiler_params=_cparams("parallel"),
        name="final_norm",
    )(x1, yk, route, g)


def _dest_kernel(route_ref, pstart_ref, dest_ref):
    tm = route_ref.shape[0]
    route = route_ref[...]
    lane = lax.broadcasted_iota(jnp.int32, (tm, LOGIT_PAD), 1)
    rows = jnp.zeros((tm, LOGIT_PAD), F32)
    for k in range(TOP_K):
        expert = route[:, k:k + 1].astype(jnp.int32)
        start = jnp.sum(jnp.where(lane == expert, pstart_ref[...], 0.0), axis=-1, keepdims=True)
        rows = jnp.where(lane == k, start + route[:, TOP_K + k:TOP_K + k + 1], rows)
    dest_ref[...] = rows.T[:TOP_K].astype(jnp.int32)


def _dest_rows(route, pstart, tm=2048):
    T = route.shape[0]
    return pl.pallas_call(
        _dest_kernel,
        grid=(T // tm,),
        in_specs=[pl.BlockSpec((tm, LOGIT_PAD), lambda i: (i, 0)),
                  pl.BlockSpec((1, LOGIT_PAD), lambda i: (0, 0))],
        out_specs=pl.BlockSpec((TOP_K, tm), lambda i: (0, i)),
        out_shape=jax.ShapeDtypeStruct((TOP_K, T), jnp.int32),
        compiler_params=_cparams("parallel"),
        name="dest_rows",
    )(route, pstart)


def _pack_w_in(w):
    kr = w[:, 1920:1952]
    half = B_ROPE // 2
    z = lambda n: jnp.zeros((w.shape[0], n), w.dtype)
    kr_placed = jnp.concatenate([z(B_NOPE), kr, z(HEAD_PAD - B_NOPE - B_ROPE)], axis=1)
    kr_rot = jnp.concatenate([z(B_NOPE), -kr[:, half:], kr[:, :half], z(HEAD_PAD - B_NOPE - B_ROPE)], axis=1)
    return jnp.concatenate([w[:, :1920], w[:, 1952:4000], kr_placed, kr_rot], axis=1).astype(BF16)


def _pack_w_uq(w):
    half = B_ROPE // 2
    r = w.shape[0]
    nope, x1, x2 = w[..., :B_NOPE], w[..., B_NOPE:B_NOPE + half], w[..., B_NOPE + half:]
    zpad = jnp.zeros((r, B_HEADS, HEAD_PAD - B_NOPE - B_ROPE), w.dtype)
    main = jnp.concatenate([nope, x1, x2, zpad], axis=-1)
    rot = jnp.concatenate([jnp.zeros_like(nope), -x2, x1, zpad], axis=-1)
    return (main.reshape(r, B_HEADS * HEAD_PAD).astype(BF16), rot.reshape(r, B_HEADS * HEAD_PAD).astype(BF16))


def _pack_w_uk(w):
    r = w.shape[0]
    zpad = jnp.zeros((r, B_HEADS, HEAD_PAD - B_NOPE), w.dtype)
    return jnp.concatenate([w, zpad], axis=-1).reshape(r, B_HEADS * HEAD_PAD).astype(BF16)


def _rope_tables(seq):
    inv_freq = ROPE_THETA ** (-np.arange(0, B_ROPE, 2, dtype=np.float64) / B_ROPE)
    ang = np.arange(seq, dtype=np.float64)[:, None] * inv_freq[None, :]
    cos, sin = jnp.asarray(np.cos(ang), F32), jnp.asarray(np.sin(ang), F32)
    one = jnp.ones((seq, B_NOPE), F32)
    zero = jnp.zeros((seq, B_NOPE), F32)
    zpad = jnp.zeros((seq, HEAD_PAD - B_NOPE - B_ROPE), F32)
    qs = ((B_NOPE + B_ROPE) ** -0.5) * LOG2E
    cq = jnp.concatenate([one, cos, cos, zpad], axis=1) * qs
    sq = jnp.concatenate([zero, sin, sin, zpad], axis=1) * qs
    ck = jnp.concatenate([zero, cos, cos, zpad], axis=1)
    sk = jnp.concatenate([zero, sin, sin, zpad], axis=1)
    return cq, sq, ck, sk


def kernel(x, norm1_g, w_in, q_a_norm_g, kv_a_norm_g, w_uq, w_uk, w_uv, w_a_out, w_b_out, w_o,
           norm2_g, w_router, b_router, w_gate_up, b_gate_up, w_down, b_down, final_g):
    B, S, D = x.shape
    T = B * S
    depth = norm1_g.shape[0]
    xt = x.reshape(T, D)
    cq_t, sq_t, ck_t, sk_t = _rope_tables(S)
    biases = [_dilated_bias(d) for _, d in A_PATTERNS]
    bm = 512
    for l in range(depth):
        wq1, wq2 = _pack_w_uq(w_uq[l])
        qa, ka, va, qa4, ka4, va4, qa16, ka16, va16, qb, kb, vb, gates = _inproj(
            xt, norm1_g[l][None], _pack_w_in(w_in[l]), q_a_norm_g[l][None], kv_a_norm_g[l][None],
            wq1, wq2, _pack_w_uk(w_uk[l]), w_uv[l].reshape(KV_LORA, B_WIDTH).T.astype(BF16),
            cq_t, sq_t, ck_t, sk_t, S)
        qkv = ((qa, ka, va), (qa4, ka4, va4), (qa16, ka16, va16))
        outs = [_dilated_pattern(*qkv[p], biases[p], B, S, A_PATTERNS[p][1]) for p in range(3)]
        ob = _mla_attention(qb.reshape(B, S, -1), kb.reshape(B, S, -1), vb, B, S)

        wr = jnp.pad(w_router[l], ((0, 0), (0, LOGIT_PAD - N_EXPERTS)))
        wrh = wr.astype(BF16)
        wrl = (wr - wrh.astype(F32)).astype(BF16)
        br = jnp.pad(b_router[l], (0, LOGIT_PAD - N_EXPERTS), constant_values=NEG)[None]
        x1, h2, route, count_rows = _outproj(
            outs[0][0], outs[1][0], outs[2][0], outs[0][1], outs[1][1], outs[2][1],
            ob.reshape(T, B_WIDTH), gates, xt,
            w_a_out[l].astype(BF16), w_b_out[l].astype(BF16), w_o[l].astype(BF16),
            norm2_g[l][None], wrh, wrl, br)

        counts = count_rows[0, :N_EXPERTS].astype(jnp.int32)
        padded = (counts + bm - 1) // bm * bm
        pend = jnp.cumsum(padded)
        pstart = pend - padded
        NP = T * TOP_K + N_EXPERTS * bm
        nblk = NP // bm
        blk_start = jnp.arange(nblk, dtype=jnp.int32) * bm
        blk_e = jnp.minimum(jnp.sum(pend[None, :] <= blk_start[:, None], axis=1), N_EXPERTS - 1).astype(jnp.int32)
        n_used = (pend[-1] // bm).astype(jnp.int32)[None]
        n_valid = jnp.clip(pstart[blk_e] + counts[blk_e] - blk_start, 0, bm).astype(jnp.int32)

        pstart_row = jnp.pad(pstart.astype(F32), (0, LOGIT_PAD - N_EXPERTS))[None]
        dest_kt = _dest_rows(route, pstart_row).reshape(-1)
        xb = _scatter_rows(h2, dest_kt, NP)
        yb = _moe_experts(xb, blk_e, n_used, n_valid, w_gate_up[l], b_gate_up[l][:, None, :],
                          w_down[l], b_down[l][:, None, :], bm)
        yk = _gather_rows(yb, dest_kt).reshape(TOP_K, T, HALF_D)
        if l + 1 < depth:
            raise NotImplementedError("depth > 1")
        out = _final(x1, yk, route, final_g[None])
    return out.reshape(B, S, D)
```

```python
import functools

import jax
import jax.numpy as jnp
import numpy as np
from jax import lax
from jax.experimental import pallas as pl
from jax.experimental.pallas import tpu as pltpu
from jax.experimental.pallas import tpu_sc as plsc

F32 = jnp.float32
BF16 = jnp.bfloat16

D_MODEL = 1024
EPS = 1e-5
NEG = -1e30

A_HEADS = 8
A_HEAD_DIM = 64
A_WIDTH = 512
A_PATTERNS = ((128, 1), (512, 4), (2048, 16))
A_HALF = 64
A_QB = 128
A_WIN = A_QB + 2 * A_HALF
assert all(w // (2 * d) == A_HALF for w, d in A_PATTERNS) and [d for _, d in A_PATTERNS] == [1, 4, 16]

B_HEADS = 8
B_NOPE = 64
B_ROPE = 32
B_V = 64
B_WIDTH = 512
Q_LORA = 256
KV_LORA = 128
ROPE_THETA = 10000.0
HEAD_PAD = 128
ONES_ROWS = 16
MLA_SCORE_BUFS = 4
MLA_HEADS_PER_STEP = 4

N_EXPERTS = 32
TOP_K = 4
SWIGLU_LIMIT = 7.0
SWIGLU_ALPHA = 1.702
LOGIT_PAD = 128

LOG2E = 1.4426950408889634

VMEM_LIMIT = 56 * 1024 * 1024


def _cparams(*sem):
    return pltpu.CompilerParams(dimension_semantics=sem, vmem_limit_bytes=VMEM_LIMIT)


def _inproj_kernel(x_ref, g1_ref, wp_ref, gq_ref, gkv_ref, wq1_ref, wq2_ref, wuk_ref, wuv_ref,
                   cq_ref, sq_ref, ck_ref, sk_ref,
                   qa_ref, ka_ref, va_ref, qa4_ref, ka4_ref, va4_ref, qa16_ref, ka16_ref, va16_ref,
                   qb_ref, kb_ref, vb_ref, gate_ref, perm_ref, perm4_ref):
    tm = x_ref.shape[0]
    x = x_ref[...]
    h = x * lax.rsqrt(jnp.mean(x * x, axis=-1, keepdims=True) + EPS) * g1_ref[...]
    h = h.astype(BF16)

    def proj(c0, c1):
        return jnp.dot(h, wp_ref[:, c0:c1], preferred_element_type=F32)

    def emit(which, z, nat_ref, v4_ref, v16_ref):
        nat_ref[...] = z.astype(BF16)
        nc = A_WIDTH // 128
        for c in range(nc):
            perm_ref[which, c] = z[:, c * 128:(c + 1) * 128]
        for r1 in range(4):
            for c in range(nc):
                quarter = perm_ref[which, c, pl.ds(r1, tm // 4, stride=4), :]
                v4_ref[:, r1 * A_WIDTH + c * 128:r1 * A_WIDTH + (c + 1) * 128] = quarter.astype(BF16)
                perm4_ref[which, r1, c] = quarter
        for r1 in range(4):
            for r2 in range(4):
                r = 4 * r2 + r1
                for c in range(nc):
                    v16_ref[:, r * A_WIDTH + c * 128:r * A_WIDTH + (c + 1) * 128] = (
                        perm4_ref[which, r1, c, pl.ds(r2, tm // 16, stride=4), :].astype(BF16))

    emit(0, proj(0, 512) * (A_HEAD_DIM ** -0.5), qa_ref, qa4_ref, qa16_ref)
    emit(1, proj(512, 1024), ka_ref, ka4_ref, ka16_ref)
    emit(2, proj(1024, 1536), va_ref, va4_ref, va16_ref)

    cq = proj(1536, 1792)
    cq = cq * lax.rsqrt(jnp.mean(cq * cq, axis=-1, keepdims=True) + EPS) * gq_ref[...]
    cq = cq.astype(BF16)
    q_main = jnp.dot(cq, wq1_ref[...], preferred_element_type=F32)
    q_rot = jnp.dot(cq, wq2_ref[...], preferred_element_type=F32)
    cq_t = cq_ref[...]
    sq_t = sq_ref[...]
    for hd in range(B_HEADS):
        sl = slice(hd * HEAD_PAD, (hd + 1) * HEAD_PAD)
        qb_ref[:, sl] = (q_main[:, sl] * cq_t + q_rot[:, sl] * sq_t).astype(BF16)

    ckv = proj(1792, 1920)
    ckv = ckv * lax.rsqrt(jnp.mean(ckv * ckv, axis=-1, keepdims=True) + EPS) * gkv_ref[...]
    ckv = ckv.astype(BF16)
    k_nope = jnp.dot(ckv, wuk_ref[...], preferred_element_type=F32)
    vb_ref[...] = lax.dot_general(wuv_ref[...], ckv, (((1,), (1,)), ((), ())),
                                  preferred_element_type=F32).astype(BF16)
    k_rope = proj(3968, 4096) * ck_ref[...] + proj(4096, 4224) * sk_ref[...]
    for hd in range(B_HEADS):
        sl = slice(hd * HEAD_PAD, (hd + 1) * HEAD_PAD)
        kb_ref[:, sl] = (k_nope[:, sl] + k_rope).astype(BF16)

    gate_ref[...] = jax.nn.sigmoid(proj(1920, 3968)).astype(BF16)


def _inproj(x2, g1, wp, gq, gkv, wq1, wq2, wuk, wuv, cq_t, sq_t, ck_t, sk_t, seq, tm=512):
    T = x2.shape[0]
    nseq = seq // tm
    row = lambda i: (i, 0)
    fix = lambda i: (0, 0)
    tab = lambda i: (i % nseq, 0)
    full = lambda a: pl.BlockSpec(a.shape, fix)
    sds = lambda r, c: jax.ShapeDtypeStruct((r, c), BF16)
    out_shape = (
        sds(T, A_WIDTH), sds(T, A_WIDTH), sds(T, A_WIDTH),
        sds(T // 4, 4 * A_WIDTH), sds(T // 4, 4 * A_WIDTH), sds(T // 4, 4 * A_WIDTH),
        sds(T // 16, 16 * A_WIDTH), sds(T // 16, 16 * A_WIDTH), sds(T // 16, 16 * A_WIDTH),
        sds(T, B_HEADS * HEAD_PAD), sds(T, B_HEADS * HEAD_PAD),
        sds(B_WIDTH, T),
        sds(T, 2 * D_MODEL),
    )
    out_specs = [pl.BlockSpec((tm * s.shape[0] // T, s.shape[1]), row) for s in out_shape]
    out_specs[11] = pl.BlockSpec((B_WIDTH, tm), lambda i: (0, i))
    return pl.pallas_call(
        _inproj_kernel,
        grid=(T // tm,),
        in_specs=[pl.BlockSpec((tm, D_MODEL), row), full(g1), full(wp), full(gq), full(gkv),
                  full(wq1), full(wq2), full(wuk), full(wuv),
                  pl.BlockSpec((tm, HEAD_PAD), tab), pl.BlockSpec((tm, HEAD_PAD), tab),
                  pl.BlockSpec((tm, HEAD_PAD), tab), pl.BlockSpec((tm, HEAD_PAD), tab)],
        out_specs=out_specs,
        out_shape=out_shape,
        scratch_shapes=[pltpu.VMEM((3, A_WIDTH // 128, tm, 128), F32),
                        pltpu.VMEM((3, 4, A_WIDTH // 128, tm // 4, 128), F32)],
        compiler_params=_cparams("parallel"),
        name="inproj",
    )(x2, g1, wp, gq, gkv, wq1, wq2, wuk, wuv, cq_t, sq_t, ck_t, sk_t)


def _dilated_kernel(q_ref, kc_ref, kp_ref, kn_ref, vc_ref, vp_ref, vn_ref, bias_ref,
                    o_ref, lse_ref, kbuf, vbuf, *, rows):
    i = pl.program_id(2)
    last = pl.num_programs(2) - 1
    kbuf[0:A_HALF, :] = kp_ref[0]
    kbuf[A_HALF:A_HALF + rows, :] = kc_ref[0]
    kbuf[A_HALF + rows:, :] = kn_ref[0]
    vbuf[0:A_HALF, :] = vp_ref[0]
    vbuf[A_HALF:A_HALF + rows, :] = vc_ref[0]
    vbuf[A_HALF + rows:, :] = vn_ref[0]

    nj = rows // A_QB
    npair = A_HEADS // 2
    lane = lax.broadcasted_iota(jnp.int32, (A_QB, 128), 1)
    lo_half = lane < A_HEAD_DIM

    def body(j, carry):
        r0 = pl.multiple_of(j * A_QB, A_QB)
        variant = (jnp.logical_and(i == 0, j == 0).astype(jnp.int32)
                   + 2 * jnp.logical_and(i == last, j == nj - 1).astype(jnp.int32))
        scores = []
        for pr in range(npair):
            ls = slice(pr * 128, (pr + 1) * 128)
            qp = q_ref[0, pl.ds(r0, A_QB), ls]
            zero = jnp.zeros_like(qp)
            q2 = jnp.concatenate([jnp.where(lo_half, qp, zero), jnp.where(lo_half, zero, qp)], axis=0)
            s = lax.dot_general(q2, kbuf[pl.ds(r0, A_WIN), ls], (((1,), (1,)), ((), ())),
                                preferred_element_type=F32)
            scores.append(s + bias_ref[variant * npair + pr])
        probs = []
        for s in scores:
            m = jnp.max(s, axis=-1, keepdims=True)
            p = jnp.exp(s - m)
            l = jnp.sum(p, axis=-1, keepdims=True)
            probs.append((p.astype(BF16), l, m + jnp.log(l)))
        for pr, (p, l, lse) in enumerate(probs):
            ls = slice(pr * 128, (pr + 1) * 128)
            o = jnp.dot(p, vbuf[pl.ds(r0, A_WIN), ls], preferred_element_type=F32) / l
            lse = jnp.broadcast_to(lse, (2 * A_QB, 128))
            o_ref[0, pl.ds(r0, A_QB), ls] = jnp.where(lo_half, o[:A_QB], o[A_QB:]).astype(o_ref.dtype)
            lse_ref[0, pl.ds(r0, A_QB), ls] = jnp.where(lo_half, lse[:A_QB], lse[A_QB:])
        return carry

    lax.fori_loop(0, nj, body, 0)


def _dilated_pattern(qa, ka, va, bias, batch, seq, dil, rows=1024):
    L = seq // dil
    rows = min(rows, L)
    nblk = L // rows
    hb = rows // A_HALF
    nh = L // A_HALF
    view = lambda a: a.reshape(batch, L, dil * A_WIDTH)
    cur = lambda b, r, i: (b, i, r)
    prev = lambda b, r, i: (b, jnp.maximum(i * hb - 1, 0), r)
    nxt = lambda b, r, i: (b, jnp.minimum((i + 1) * hb, nh - 1), r)
    cur_spec = pl.BlockSpec((1, rows, A_WIDTH), cur)
    out_sds = lambda dt: jax.ShapeDtypeStruct((batch, L, dil * A_WIDTH), dt)
    o, lse = pl.pallas_call(
        functools.partial(_dilated_kernel, rows=rows),
        grid=(batch, dil, nblk),
        in_specs=[cur_spec, cur_spec,
                  pl.BlockSpec((1, A_HALF, A_WIDTH), prev), pl.BlockSpec((1, A_HALF, A_WIDTH), nxt),
                  cur_spec,
                  pl.BlockSpec((1, A_HALF, A_WIDTH), prev), pl.BlockSpec((1, A_HALF, A_WIDTH), nxt),
                  pl.BlockSpec(bias.shape, lambda b, r, i: (0, 0, 0))],
        out_specs=[cur_spec, cur_spec],
        out_shape=(out_sds(BF16), out_sds(F32)),
        scratch_shapes=[pltpu.VMEM((rows + 2 * A_HALF, A_WIDTH), BF16),
                        pltpu.VMEM((rows + 2 * A_HALF, A_WIDTH), BF16)],
        compiler_params=_cparams("parallel", "parallel", "parallel"),
        name=f"dilated_d{dil}",
    )(view(qa), view(ka), view(ka), view(ka), view(va), view(va), view(va), bias)
    return o.reshape(batch * L, dil * A_WIDTH), lse.reshape(batch * L, dil * A_WIDTH)


def _dilated_bias(dil):
    slopes = 2.0 ** (-8.0 * (np.arange(A_HEADS, dtype=np.float64) + 1.0) / A_HEADS)
    col = np.arange(A_WIN)[None, :]
    rel = (col - A_HALF) - np.arange(A_QB)[:, None]
    bias = -slopes[:, None, None] * (np.abs(rel) * dil).astype(np.float64)[None]
    bias = np.where((np.abs(rel) <= A_HALF)[None], bias, NEG)
    variants = []
    for v in range(4):
        ok = np.ones((1, A_WIN), bool)
        if v & 1:
            ok = np.logical_and(ok, col >= A_HALF)
        if v & 2:
            ok = np.logical_and(ok, col < A_HALF + A_QB)
        variants.append(np.where(ok[None], bias, NEG).reshape(A_HEADS // 2, 2 * A_QB, A_WIN))
    return jnp.asarray(np.concatenate(variants, axis=0), F32)


def _mla_kernel(q_ref, k_ref, vt_ref, o_ref, *s_bufs, tk):
    tq = q_ref.shape[1]
    nk = k_ref.shape[1] // tk
    nh = q_ref.shape[2] // HEAD_PAD
    qs = [q_ref[0, :, hh * HEAD_PAD:(hh + 1) * HEAD_PAD] for hh in range(nh)]

    def scores(j, s_ref):
        off = pl.multiple_of(j * tk, tk)
        for hh in range(nh):
            s_ref[hh] = lax.dot_general(k_ref[0, pl.ds(off, tk), hh * HEAD_PAD:(hh + 1) * HEAD_PAD], qs[hh],
                                        (((1,), (1,)), ((), ())), preferred_element_type=F32)

    ones = jnp.ones((ONES_ROWS, tk), BF16)

    def softmax_pv(j, s_ref, stats):
        off = pl.multiple_of(j * tk, tk)
        new = []
        for hh in range(nh):
            m, acc = stats[hh]
            s = s_ref[hh]
            vt = jnp.concatenate([vt_ref[hh * B_V:(hh + 1) * B_V, pl.ds(off, tk)], ones], axis=0)
            m_new = jnp.maximum(m, jnp.max(s, axis=0, keepdims=True))
            alpha = jnp.exp2(m - m_new)
            p = jnp.exp2(s - m_new)
            acc = alpha * acc + jnp.dot(vt, p.astype(BF16), preferred_element_type=F32)
            new.append((m_new, acc))
        return tuple(new)

    nbuf = len(s_bufs)

    def body(jj, stats):
        j = nbuf * jj
        for u in range(nbuf):
            scores(jnp.minimum(j + u + 1, nk - 1), s_bufs[(u + 1) % nbuf])
            stats = softmax_pv(j + u, s_bufs[u], stats)
        return stats

    init = (jnp.full((1, tq), NEG, F32), jnp.zeros((B_V + ONES_ROWS, tq), F32))
    scores(0, s_bufs[0])
    stats = lax.fori_loop(0, nk // nbuf, body, (init,) * nh)
    o_t = jnp.concatenate([a[:B_V] / a[B_V:B_V + 1] for _, a in stats], axis=0)
    o_ref[0] = o_t.T.astype(BF16)


def _mla_attention(qb, kb, vbt, batch, seq, tq=256, tk=512, nh=MLA_HEADS_PER_STEP):
    once = dict(pipeline_mode=pl.Buffered(1))
    return pl.pallas_call(
        functools.partial(_mla_kernel, tk=tk),
        grid=(batch, B_HEADS // nh, seq // tq),
        in_specs=[pl.BlockSpec((1, tq, nh * HEAD_PAD), lambda b, p, i: (b, i, p)),
                  pl.BlockSpec((1, seq, nh * HEAD_PAD), lambda b, p, i: (b, 0, p), **once),
                  pl.BlockSpec((nh * B_V, seq), lambda b, p, i: (p, b), **once)],
        out_specs=pl.BlockSpec((1, tq, nh * B_V), lambda b, p, i: (b, i, p)),
        out_shape=jax.ShapeDtypeStruct((batch, seq, B_WIDTH), BF16),
        scratch_shapes=[pltpu.VMEM((nh, tk, tq), F32) for _ in range(MLA_SCORE_BUFS)],
        compiler_params=_cparams("parallel", "parallel", "parallel"),
        name="mla_attention",
    )(qb, kb, vbt)


def _outproj_kernel(o1_ref, o2_ref, o3_ref, l1_ref, l2_ref, l3_ref, ob_ref, gate_ref, x_ref,
                    wa_ref, wb_ref, wo_ref, g2_ref, wrh_ref, wrl_ref, br_ref,
                    tri_ref, x1_ref, h2_ref, route_ref, count_ref, perm_ref, perm4_ref, carry_ref):
    tm = x_ref.shape[0]

    @pl.when(pl.program_id(0) == 0)
    def _():
        carry_ref[...] = jnp.zeros_like(carry_ref)

    nc = A_WIDTH // 128

    def lanes(r, c):
        return slice(r * A_WIDTH + c * 128, r * A_WIDTH + (c + 1) * 128)

    def token_order(which, ref, d):
        if d == 4:
            for r in range(4):
                for c in range(nc):
                    perm_ref[which, c, pl.ds(r, tm // 4, stride=4), :] = ref[:, lanes(r, c)].astype(F32)
        else:
            for r1 in range(4):
                for r2 in range(4):
                    for c in range(nc):
                        perm4_ref[which - 2, r1, c, pl.ds(r2, tm // 16, stride=4), :] = (
                            ref[:, lanes(4 * r2 + r1, c)].astype(F32))
            for r1 in range(4):
                for c in range(nc):
                    perm_ref[which, c, pl.ds(r1, tm // 4, stride=4), :] = perm4_ref[which - 2, r1, c]
        return jnp.concatenate([perm_ref[which, c] for c in range(nc)], axis=1)

    l1 = l1_ref[...]
    l2, o2 = token_order(0, l2_ref, 4), token_order(1, o2_ref, 4)
    l3, o3 = token_order(2, l3_ref, 16), token_order(3, o3_ref, 16)
    mx = jnp.maximum(jnp.maximum(l1, l2), l3)
    e1, e2, e3 = jnp.exp(l1 - mx), jnp.exp(l2 - mx), jnp.exp(l3 - mx)
    oa = (e1 * o1_ref[...] + e2 * o2 + e3 * o3) / (e1 + e2 + e3)
    ya = jnp.dot(oa.astype(BF16), wa_ref[...], preferred_element_type=F32)
    yb = jnp.dot(ob_ref[...], wb_ref[...], preferred_element_type=F32)
    merged = gate_ref[:, :D_MODEL].astype(F32) * ya + gate_ref[:, D_MODEL:].astype(F32) * yb
    x1 = x_ref[...] + jnp.dot(merged.astype(BF16), wo_ref[...], preferred_element_type=F32)
    x1_ref[...] = x1
    h2 = x1 * lax.rsqrt(jnp.mean(x1 * x1, axis=-1, keepdims=True) + EPS) * g2_ref[...]
    hi = h2.astype(BF16)
    h2_ref[...] = _pack_rows(h2)
    lo = (h2 - hi.astype(F32)).astype(BF16)
    logits = jnp.dot(hi, wrh_ref[...], preferred_element_type=F32)
    logits = logits + jnp.dot(lo, wrh_ref[...], preferred_element_type=F32)
    logits = logits + jnp.dot(hi, wrl_ref[...], preferred_element_type=F32)
    logits = logits + br_ref[...]

    lane = lax.broadcasted_iota(jnp.int32, (tm, LOGIT_PAD), 1)
    work = logits
    sel = jnp.zeros((tm, LOGIT_PAD), F32)
    idx_cols, val_cols = [], []
    for _ in range(TOP_K):
        mx = jnp.max(work, axis=-1, keepdims=True)
        idx = jnp.min(jnp.where(work == mx, lane, LOGIT_PAD), axis=-1, keepdims=True)
        hit = lane == idx
        sel = sel + hit.astype(F32)
        work = jnp.where(hit, 2.0 * NEG, work)
        idx_cols.append(idx)
        val_cols.append(mx)
    exps = [jnp.exp(v - val_cols[0]) for v in val_cols]
    denom = exps[0] + exps[1] + exps[2] + exps[3]
    gate_cols = [e / denom for e in exps]

    before = jnp.dot(tri_ref[...], sel.astype(BF16), preferred_element_type=F32) + carry_ref[0:1, :]
    rank_cols = [jnp.sum(jnp.where(lane == idx, before, 0.0), axis=-1, keepdims=True) for idx in idx_cols]
    carry_ref[...] = carry_ref[...] + jnp.sum(sel, axis=0, keepdims=True)
    count_ref[...] = carry_ref[...]

    cols = [c.astype(F32) for c in idx_cols] + rank_cols + gate_cols
    route = jnp.zeros((tm, LOGIT_PAD), F32)
    for c, col in enumerate(cols):
        route = jnp.where(lane == c, col, route)
    route_ref[...] = route


def _outproj(o1, o2, o3, l1, l2, l3, ob, gates, x2, wa, wb, wo, g2, wrh, wrl, br, tm=512):
    T = x2.shape[0]
    row = lambda i: (i, 0)
    fix = lambda i: (0, 0)
    full = lambda a: pl.BlockSpec(a.shape, fix)
    rows = lambda a: pl.BlockSpec((tm * a.shape[0] // T, a.shape[1]), row)
    tri = (jnp.arange(tm)[:, None] > jnp.arange(tm)[None, :]).astype(BF16)
    out_shape = (jax.ShapeDtypeStruct((T, D_MODEL), F32),
                 jax.ShapeDtypeStruct((T, HALF_D), jnp.int32),
                 jax.ShapeDtypeStruct((T, LOGIT_PAD), F32),
                 jax.ShapeDtypeStruct((8, LOGIT_PAD), F32))
    return pl.pallas_call(
        _outproj_kernel,
        grid=(T // tm,),
        in_specs=[rows(o1), rows(o2), rows(o3), rows(l1), rows(l2), rows(l3), rows(ob), rows(gates),
                  rows(x2), full(wa), full(wb), full(wo), full(g2), full(wrh), full(wrl), full(br), full(tri)],
        out_specs=[pl.BlockSpec((tm, D_MODEL), row), pl.BlockSpec((tm, HALF_D), row),
                   pl.BlockSpec((tm, LOGIT_PAD), row), pl.BlockSpec((8, LOGIT_PAD), fix)],
        out_shape=out_shape,
        scratch_shapes=[pltpu.VMEM((4, A_WIDTH // 128, tm, 128), F32),
                        pltpu.VMEM((2, 4, A_WIDTH // 128, tm // 4, 128), F32),
                        pltpu.VMEM((8, LOGIT_PAD), F32)],
        compiler_params=_cparams("arbitrary"),
        name="outproj",
    )(o1, o2, o3, l1, l2, l3, ob, gates, x2, wa, wb, wo, g2, wrh, wrl, br, tri)


HALF_D = D_MODEL // 2


def _pack_rows(x):
    lo = pltpu.bitcast(x[:, :HALF_D].astype(BF16).astype(F32), jnp.uint32)
    hi = pltpu.bitcast(x[:, HALF_D:].astype(BF16).astype(F32), jnp.uint32)
    word = jnp.bitwise_or(lax.shift_right_logical(lo, jnp.uint32(16)), jnp.bitwise_and(hi, jnp.uint32(0xFFFF0000)))
    return pltpu.bitcast(word, jnp.int32)


def _unpack_rows(w):
    u = pltpu.bitcast(w, jnp.uint32)
    lo = pltpu.bitcast(lax.shift_left(u, jnp.uint32(16)), F32)
    hi = pltpu.bitcast(jnp.bitwise_and(u, jnp.uint32(0xFFFF0000)), F32)
    return jnp.concatenate([lo, hi], axis=1)


SC_WINDOW = 128


def _gather_rows(table, idx):
    M = idx.shape[0]
    width = table.shape[1]
    mesh = plsc.VectorSubcoreMesh(core_axis_name="core", subcore_axis_name="subcore")

    @functools.partial(pl.kernel, out_type=jax.ShapeDtypeStruct((M, width), table.dtype), mesh=mesh,
                       name="gather_rows")
    def gather(table_hbm, idx_hbm, out_hbm):
        def body(idx_vmem, out_vmem):
            pltpu.sync_copy(table_hbm.at[idx_vmem.at[0]], out_vmem)

        pltpu.emit_pipeline(
            body,
            grid=(M // SC_WINDOW,),
            in_specs=[pl.BlockSpec((1, SC_WINDOW), index_map=lambda i: (0, i))],
            out_specs=[pl.BlockSpec((SC_WINDOW, width), index_map=lambda i: (i, 0),
                                    pipeline_mode=pl.Buffered(1))],
            core_axis_name=("core", "subcore"),
            dimension_semantics=(pltpu.PARALLEL,),
        )(idx_hbm, out_hbm)

    return gather(table, idx.reshape(1, M))


def _scatter_rows(rows, idx, n_out):
    M = idx.shape[0]
    R, width = rows.shape
    mesh = plsc.VectorSubcoreMesh(core_axis_name="core", subcore_axis_name="subcore")

    @functools.partial(pl.kernel, out_type=jax.ShapeDtypeStruct((n_out, width), rows.dtype), mesh=mesh,
                       name="scatter_rows")
    def scatter(rows_hbm, idx_hbm, out_hbm):
        def body(rows_vmem, idx_vmem):
            pltpu.sync_copy(rows_vmem, out_hbm.at[idx_vmem.at[0]])

        pltpu.emit_pipeline(
            body,
            grid=(M // SC_WINDOW,),
            in_specs=[pl.BlockSpec((SC_WINDOW, width), index_map=lambda i: (i % (R // SC_WINDOW), 0),
                                   pipeline_mode=pl.Buffered(1)),
                      pl.BlockSpec((1, SC_WINDOW), index_map=lambda i: (0, i))],
            out_specs=[],
            core_axis_name=("core", "subcore"),
            dimension_semantics=(pltpu.PARALLEL,),
        )(rows_hbm, idx_hbm)

    return scatter(rows, idx.reshape(1, M))


def _moe_kernel(be_ref, nu_ref, nv_ref, x_ref, wgu_ref, bgu_ref, wd_ref, bd_ref, y_ref, wgu_bf, wd_bf):
    i = pl.program_id(0)

    @pl.when(i >= nu_ref[0])
    def _():
        y_ref[...] = jnp.zeros_like(y_ref)

    @pl.when(jnp.logical_or(i == 0, be_ref[i] != be_ref[jnp.maximum(i - 1, 0)]))
    def _():
        rows = 128

        def cast(c, carry):
            r0 = pl.multiple_of(c * rows, rows)
            wgu_bf[pl.ds(r0, rows), :] = wgu_ref[0, pl.ds(r0, rows), :].astype(BF16)
            wd_bf[pl.ds(r0, rows), :] = wd_ref[0, pl.ds(r0, rows), :].astype(BF16)
            return carry

        lax.fori_loop(0, D_MODEL // rows, cast, 0)

    @pl.when(i < nu_ref[0])
    def _():
        live = lax.broadcasted_iota(jnp.int32, (x_ref.shape[0], 1), 0) < nv_ref[i]
        x = jnp.where(live, _unpack_rows(x_ref[...]), 0.0).astype(BF16)
        gu = jnp.dot(x, wgu_bf[...], preferred_element_type=F32) + bgu_ref[0]
        gate = jnp.minimum(gu[:, :D_MODEL], SWIGLU_LIMIT)
        up = jnp.clip(gu[:, D_MODEL:], -SWIGLU_LIMIT, SWIGLU_LIMIT)
        act = (up + 1.0) * (gate * jax.nn.sigmoid(SWIGLU_ALPHA * gate))
        y = jnp.dot(act.astype(BF16), wd_bf[...], preferred_element_type=F32) + bd_ref[0]
        y_ref[...] = _pack_rows(y)


def _moe_experts(xb, blk_e, n_used, n_valid, wgu, bgu, wd, bd, bm):
    NP = xb.shape[0]
    nblk = NP // bm
    grid_spec = pltpu.PrefetchScalarGridSpec(
        num_scalar_prefetch=3,
        grid=(nblk,),
        in_specs=[pl.BlockSpec((bm, HALF_D), lambda i, be, nu, nv: (jnp.minimum(i, nu[0] - 1), 0)),
                  pl.BlockSpec((1, D_MODEL, 2 * D_MODEL), lambda i, be, nu, nv: (be[i], 0, 0)),
                  pl.BlockSpec((1, 1, 2 * D_MODEL), lambda i, be, nu, nv: (be[i], 0, 0)),
                  pl.BlockSpec((1, D_MODEL, D_MODEL), lambda i, be, nu, nv: (be[i], 0, 0)),
                  pl.BlockSpec((1, 1, D_MODEL), lambda i, be, nu, nv: (be[i], 0, 0))],
        out_specs=pl.BlockSpec((bm, HALF_D), lambda i, be, nu, nv: (i, 0)),
        scratch_shapes=[pltpu.VMEM((D_MODEL, 2 * D_MODEL), BF16), pltpu.VMEM((D_MODEL, D_MODEL), BF16)],
    )
    return pl.pallas_call(
        _moe_kernel,
        grid_spec=grid_spec,
        out_shape=jax.ShapeDtypeStruct((NP, HALF_D), jnp.int32),
        compiler_params=_cparams("arbitrary"),
        name="moe_experts",
    )(blk_e, n_used, n_valid, xb, wgu, bgu, wd, bd)


def _final_kernel(x1_ref, y_ref, route_ref, g_ref, o_ref):
    w = route_ref[...]
    acc = x1_ref[...]
    for k in range(TOP_K):
        acc = acc + w[:, 2 * TOP_K + k:2 * TOP_K + k + 1] * _unpack_rows(y_ref[k])
    o_ref[...] = acc * lax.rsqrt(jnp.mean(acc * acc, axis=-1, keepdims=True) + EPS) * g_ref[...]


def _final(x1, yk, route, g, tm=512):
    T = x1.shape[0]
    return pl.pallas_call(
        _final_kernel,
        grid=(T // tm,),
        in_specs=[pl.BlockSpec((tm, D_MODEL), lambda i: (i, 0)),
                  pl.BlockSpec((TOP_K, tm, HALF_D), lambda i: (0, i, 0)),
                  pl.BlockSpec((tm, LOGIT_PAD), lambda i: (i, 0)),
                  pl.BlockSpec((1, D_MODEL), lambda i: (0, 0))],
        out_specs=pl.BlockSpec((tm, D_MODEL), lambda i: (i, 0)),
        out_shape=jax.ShapeDtypeStruct((T, D_MODEL), F32),
        compiler_params=_cparams("parallel"),
        name="final_norm",
    )(x1, yk, route, g)


def _dest_kernel(route_ref, pstart_ref, dest_ref):
    tm = route_ref.shape[0]
    route = route_ref[...]
    lane = lax.broadcasted_iota(jnp.int32, (tm, LOGIT_PAD), 1)
    expert = jnp.where(lane < TOP_K, route, 0.0).astype(jnp.int32)
    start = jnp.take_along_axis(jnp.broadcast_to(pstart_ref[...], (tm, LOGIT_PAD)), expert, axis=1)
    position = pltpu.roll(route, LOGIT_PAD - TOP_K, axis=1)
    rows = start + position
    dest_ref[...] = rows.T[:TOP_K].astype(jnp.int32)


def _dest_rows(route, pstart, tm=2048):
    T = route.shape[0]
    return pl.pallas_call(
        _dest_kernel,
        grid=(T // tm,),
        in_specs=[pl.BlockSpec((tm, LOGIT_PAD), lambda i: (i, 0)),
                  pl.BlockSpec((1, LOGIT_PAD), lambda i: (0, 0))],
        out_specs=pl.BlockSpec((TOP_K, tm), lambda i: (0, i)),
        out_shape=jax.ShapeDtypeStruct((TOP_K, T), jnp.int32),
        compiler_params=_cparams("parallel"),
        name="dest_rows",
    )(route, pstart)


def _pack_w_in(w):
    kr = w[:, 1920:1952]
    half = B_ROPE // 2
    z = lambda n: jnp.zeros((w.shape[0], n), w.dtype)
    kr_placed = jnp.concatenate([z(B_NOPE), kr, z(HEAD_PAD - B_NOPE - B_ROPE)], axis=1)
    kr_rot = jnp.concatenate([z(B_NOPE), -kr[:, half:], kr[:, :half], z(HEAD_PAD - B_NOPE - B_ROPE)], axis=1)
    return jnp.concatenate([w[:, :1920], w[:, 1952:4000], kr_placed, kr_rot], axis=1).astype(BF16)


def _pack_w_uq(w):
    half = B_ROPE // 2
    r = w.shape[0]
    nope, x1, x2 = w[..., :B_NOPE], w[..., B_NOPE:B_NOPE + half], w[..., B_NOPE + half:]
    zpad = jnp.zeros((r, B_HEADS, HEAD_PAD - B_NOPE - B_ROPE), w.dtype)
    main = jnp.concatenate([nope, x1, x2, zpad], axis=-1)
    rot = jnp.concatenate([jnp.zeros_like(nope), -x2, x1, zpad], axis=-1)
    return (main.reshape(r, B_HEADS * HEAD_PAD).astype(BF16), rot.reshape(r, B_HEADS * HEAD_PAD).astype(BF16))


def _pack_w_uk(w):
    r = w.shape[0]
    zpad = jnp.zeros((r, B_HEADS, HEAD_PAD - B_NOPE), w.dtype)
    return jnp.concatenate([w, zpad], axis=-1).reshape(r, B_HEADS * HEAD_PAD).astype(BF16)


def _rope_tables(seq):
    inv_freq = ROPE_THETA ** (-np.arange(0, B_ROPE, 2, dtype=np.float64) / B_ROPE)
    ang = np.arange(seq, dtype=np.float64)[:, None] * inv_freq[None, :]
    cos, sin = jnp.asarray(np.cos(ang), F32), jnp.asarray(np.sin(ang), F32)
    one = jnp.ones((seq, B_NOPE), F32)
    zero = jnp.zeros((seq, B_NOPE), F32)
    zpad = jnp.zeros((seq, HEAD_PAD - B_NOPE - B_ROPE), F32)
    qs = ((B_NOPE + B_ROPE) ** -0.5) * LOG2E
    cq = jnp.concatenate([one, cos, cos, zpad], axis=1) * qs
    sq = jnp.concatenate([zero, sin, sin, zpad], axis=1) * qs
    ck = jnp.concatenate([zero, cos, cos, zpad], axis=1)
    sk = jnp.concatenate([zero, sin, sin, zpad], axis=1)
    return cq, sq, ck, sk


def kernel(x, norm1_g, w_in, q_a_norm_g, kv_a_norm_g, w_uq, w_uk, w_uv, w_a_out, w_b_out, w_o,
           norm2_g, w_router, b_router, w_gate_up, b_gate_up, w_down, b_down, final_g):
    B, S, D = x.shape
    T = B * S
    depth = norm1_g.shape[0]
    xt = x.reshape(T, D)
    cq_t, sq_t, ck_t, sk_t = _rope_tables(S)
    biases = [_dilated_bias(d) for _, d in A_PATTERNS]
    bm = 512
    for l in range(depth):
        wq1, wq2 = _pack_w_uq(w_uq[l])
        qa, ka, va, qa4, ka4, va4, qa16, ka16, va16, qb, kb, vb, gates = _inproj(
            xt, norm1_g[l][None], _pack_w_in(w_in[l]), q_a_norm_g[l][None], kv_a_norm_g[l][None],
            wq1, wq2, _pack_w_uk(w_uk[l]), w_uv[l].reshape(KV_LORA, B_WIDTH).T.astype(BF16),
            cq_t, sq_t, ck_t, sk_t, S)
        qkv = ((qa, ka, va), (qa4, ka4, va4), (qa16, ka16, va16))
        outs = [_dilated_pattern(*qkv[p], biases[p], B, S, A_PATTERNS[p][1]) for p in range(3)]
        ob = _mla_attention(qb.reshape(B, S, -1), kb.reshape(B, S, -1), vb, B, S)

        wr = jnp.pad(w_router[l], ((0, 0), (0, LOGIT_PAD - N_EXPERTS)))
        wrh = wr.astype(BF16)
        wrl = (wr - wrh.astype(F32)).astype(BF16)
        br = jnp.pad(b_router[l], (0, LOGIT_PAD - N_EXPERTS), constant_values=NEG)[None]
        x1, h2, route, count_rows = _outproj(
            outs[0][0], outs[1][0], outs[2][0], outs[0][1], outs[1][1], outs[2][1],
            ob.reshape(T, B_WIDTH), gates, xt,
            w_a_out[l].astype(BF16), w_b_out[l].astype(BF16), w_o[l].astype(BF16),
            norm2_g[l][None], wrh, wrl, br)

        counts = count_rows[0, :N_EXPERTS].astype(jnp.int32)
        padded = (counts + bm - 1) // bm * bm
        pend = jnp.cumsum(padded)
        pstart = pend - padded
        NP = T * TOP_K + N_EXPERTS * bm
        nblk = NP // bm
        blk_start = jnp.arange(nblk, dtype=jnp.int32) * bm
        blk_e = jnp.minimum(jnp.sum(pend[None, :] <= blk_start[:, None], axis=1), N_EXPERTS - 1).astype(jnp.int32)
        n_used = (pend[-1] // bm).astype(jnp.int32)[None]
        n_valid = jnp.clip(pstart[blk_e] + counts[blk_e] - blk_start, 0, bm).astype(jnp.int32)

        pstart_row = jnp.pad(pstart.astype(F32), (0, LOGIT_PAD - N_EXPERTS))[None]
        dest_kt = _dest_rows(route, pstart_row).reshape(-1)
        xb = _scatter_rows(h2, dest_kt, NP)
        yb = _moe_experts(xb, blk_e, n_used, n_valid, w_gate_up[l], b_gate_up[l][:, None, :],
                          w_down[l], b_down[l][:, None, :], bm)
        yk = _gather_rows(yb, dest_kt).reshape(TOP_K, T, HALF_D)
        if l + 1 < depth:
            raise NotImplementedError("depth > 1")
        out = _final(x1, yk, route, final_g[None])
    return out.reshape(B, S, D)
```

```python
import functools

import jax
import jax.numpy as jnp
import numpy as np
from jax import lax
from jax.experimental import pallas as pl
from jax.experimental.pallas import tpu as pltpu
from jax.experimental.pallas import tpu_sc as plsc

F32 = jnp.float32
BF16 = jnp.bfloat16

D_MODEL = 1024
EPS = 1e-5
NEG = -1e30

A_HEADS = 8
A_HEAD_DIM = 64
A_WIDTH = 512
A_PATTERNS = ((128, 1), (512, 4), (2048, 16))
A_HALF = 64
A_QB = 128
A_WIN = A_QB + 2 * A_HALF
assert all(w // (2 * d) == A_HALF for w, d in A_PATTERNS) and [d for _, d in A_PATTERNS] == [1, 4, 16]

B_HEADS = 8
B_NOPE = 64
B_ROPE = 32
B_V = 64
B_WIDTH = 512
Q_LORA = 256
KV_LORA = 128
ROPE_THETA = 10000.0
HEAD_PAD = 128
ONES_ROWS = 16
MLA_SCORE_BUFS = 4
MLA_HEADS_PER_STEP = 4

N_EXPERTS = 32
TOP_K = 4
SWIGLU_LIMIT = 7.0
SWIGLU_ALPHA = 1.702
LOGIT_PAD = 128

LOG2E = 1.4426950408889634

VMEM_LIMIT = 56 * 1024 * 1024


def _cparams(*sem):
    return pltpu.CompilerParams(dimension_semantics=sem, vmem_limit_bytes=VMEM_LIMIT)


def _inproj_kernel(x_ref, g1_ref, wp_ref, gq_ref, gkv_ref, wq1_ref, wq2_ref, wuk_ref, wuv_ref,
                   cq_ref, sq_ref, ck_ref, sk_ref,
                   qa_ref, ka_ref, va_ref, qa4_ref, ka4_ref, va4_ref, qa16_ref, ka16_ref, va16_ref,
                   qb_ref, kb_ref, vb_ref, gate_ref, perm_ref, perm4_ref):
    tm = x_ref.shape[0]
    x = x_ref[...]
    h = x * lax.rsqrt(jnp.mean(x * x, axis=-1, keepdims=True) + EPS) * g1_ref[...]
    h = h.astype(BF16)

    def proj(c0, c1):
        return jnp.dot(h, wp_ref[:, c0:c1], preferred_element_type=F32)

    def emit(which, z, nat_ref, v4_ref, v16_ref):
        nat_ref[...] = z.astype(BF16)
        nc = A_WIDTH // 128
        for c in range(nc):
            perm_ref[which, c] = z[:, c * 128:(c + 1) * 128]
        for r1 in range(4):
            for c in range(nc):
                quarter = perm_ref[which, c, pl.ds(r1, tm // 4, stride=4), :]
                v4_ref[:, r1 * A_WIDTH + c * 128:r1 * A_WIDTH + (c + 1) * 128] = quarter.astype(BF16)
                perm4_ref[which, r1, c] = quarter
        for r1 in range(4):
            for r2 in range(4):
                r = 4 * r2 + r1
                for c in range(nc):
                    v16_ref[:, r * A_WIDTH + c * 128:r * A_WIDTH + (c + 1) * 128] = (
                        perm4_ref[which, r1, c, pl.ds(r2, tm // 16, stride=4), :].astype(BF16))

    emit(0, proj(0, 512) * (A_HEAD_DIM ** -0.5), qa_ref, qa4_ref, qa16_ref)
    emit(1, proj(512, 1024), ka_ref, ka4_ref, ka16_ref)
    emit(2, proj(1024, 1536), va_ref, va4_ref, va16_ref)

    cq = proj(1536, 1792)
    cq = cq * lax.rsqrt(jnp.mean(cq * cq, axis=-1, keepdims=True) + EPS) * gq_ref[...]
    cq = cq.astype(BF16)
    q_main = jnp.dot(cq, wq1_ref[...], preferred_element_type=F32)
    q_rot = jnp.dot(cq, wq2_ref[...], preferred_element_type=F32)
    cq_t = cq_ref[...]
    sq_t = sq_ref[...]
    for hd in range(B_HEADS):
        sl = slice(hd * HEAD_PAD, (hd + 1) * HEAD_PAD)
        qb_ref[:, sl] = (q_main[:, sl] * cq_t + q_rot[:, sl] * sq_t).astype(BF16)

    ckv = proj(1792, 1920)
    ckv = ckv * lax.rsqrt(jnp.mean(ckv * ckv, axis=-1, keepdims=True) + EPS) * gkv_ref[...]
    ckv = ckv.astype(BF16)
    k_nope = jnp.dot(ckv, wuk_ref[...], preferred_element_type=F32)
    vb_ref[...] = lax.dot_general(wuv_ref[...], ckv, (((1,), (1,)), ((), ())),
                                  preferred_element_type=F32).astype(BF16)
    k_rope = proj(3968, 4096) * ck_ref[...] + proj(4096, 4224) * sk_ref[...]
    for hd in range(B_HEADS):
        sl = slice(hd * HEAD_PAD, (hd + 1) * HEAD_PAD)
        kb_ref[:, sl] = (k_nope[:, sl] + k_rope).astype(BF16)

    gate_ref[...] = jax.nn.sigmoid(proj(1920, 3968)).astype(BF16)


def _inproj(x2, g1, wp, gq, gkv, wq1, wq2, wuk, wuv, cq_t, sq_t, ck_t, sk_t, seq, tm=512):
    T = x2.shape[0]
    nseq = seq // tm
    row = lambda i: (i, 0)
    fix = lambda i: (0, 0)
    tab = lambda i: (i % nseq, 0)
    full = lambda a: pl.BlockSpec(a.shape, fix)
    sds = lambda r, c: jax.ShapeDtypeStruct((r, c), BF16)
    out_shape = (
        sds(T, A_WIDTH), sds(T, A_WIDTH), sds(T, A_WIDTH),
        sds(T // 4, 4 * A_WIDTH), sds(T // 4, 4 * A_WIDTH), sds(T // 4, 4 * A_WIDTH),
        sds(T // 16, 16 * A_WIDTH), sds(T // 16, 16 * A_WIDTH), sds(T // 16, 16 * A_WIDTH),
        sds(T, B_HEADS * HEAD_PAD), sds(T, B_HEADS * HEAD_PAD),
        sds(B_WIDTH, T),
        sds(T, 2 * D_MODEL),
    )
    out_specs = [pl.BlockSpec((tm * s.shape[0] // T, s.shape[1]), row) for s in out_shape]
    out_specs[11] = pl.BlockSpec((B_WIDTH, tm), lambda i: (0, i))
    return pl.pallas_call(
        _inproj_kernel,
        grid=(T // tm,),
        in_specs=[pl.BlockSpec((tm, D_MODEL), row), full(g1), full(wp), full(gq), full(gkv),
                  full(wq1), full(wq2), full(wuk), full(wuv),
                  pl.BlockSpec((tm, HEAD_PAD), tab), pl.BlockSpec((tm, HEAD_PAD), tab),
                  pl.BlockSpec((tm, HEAD_PAD), tab), pl.BlockSpec((tm, HEAD_PAD), tab)],
        out_specs=out_specs,
        out_shape=out_shape,
        scratch_shapes=[pltpu.VMEM((3, A_WIDTH // 128, tm, 128), F32),
                        pltpu.VMEM((3, 4, A_WIDTH // 128, tm // 4, 128), F32)],
        compiler_params=_cparams("parallel"),
        name="inproj",
    )(x2, g1, wp, gq, gkv, wq1, wq2, wuk, wuv, cq_t, sq_t, ck_t, sk_t)


def _dilated_kernel(q_ref, kc_ref, kp_ref, kn_ref, vc_ref, vp_ref, vn_ref, bias_ref,
                    o_ref, lse_ref, kbuf, vbuf, *, rows):
    i = pl.program_id(2)
    last = pl.num_programs(2) - 1
    kbuf[0:A_HALF, :] = kp_ref[0]
    kbuf[A_HALF:A_HALF + rows, :] = kc_ref[0]
    kbuf[A_HALF + rows:, :] = kn_ref[0]
    vbuf[0:A_HALF, :] = vp_ref[0]
    vbuf[A_HALF:A_HALF + rows, :] = vc_ref[0]
    vbuf[A_HALF + rows:, :] = vn_ref[0]

    nj = rows // A_QB
    npair = A_HEADS // 2
    lane = lax.broadcasted_iota(jnp.int32, (A_QB, 128), 1)
    lo_half = lane < A_HEAD_DIM

    def body(j, carry):
        r0 = pl.multiple_of(j * A_QB, A_QB)
        variant = (jnp.logical_and(i == 0, j == 0).astype(jnp.int32)
                   + 2 * jnp.logical_and(i == last, j == nj - 1).astype(jnp.int32))
        scores = []
        for pr in range(npair):
            ls = slice(pr * 128, (pr + 1) * 128)
            qp = q_ref[0, pl.ds(r0, A_QB), ls]
            zero = jnp.zeros_like(qp)
            q2 = jnp.concatenate([jnp.where(lo_half, qp, zero), jnp.where(lo_half, zero, qp)], axis=0)
            s = lax.dot_general(q2, kbuf[pl.ds(r0, A_WIN), ls], (((1,), (1,)), ((), ())),
                                preferred_element_type=F32)
            scores.append(s + bias_ref[variant * npair + pr])
        probs = []
        for s in scores:
            m = jnp.max(s, axis=-1, keepdims=True)
            p = jnp.exp(s - m)
            l = jnp.sum(p, axis=-1, keepdims=True)
            probs.append((p.astype(BF16), l, m + jnp.log(l)))
        for pr, (p, l, lse) in enumerate(probs):
            ls = slice(pr * 128, (pr + 1) * 128)
            o = jnp.dot(p, vbuf[pl.ds(r0, A_WIN), ls], preferred_element_type=F32) / l
            lse = jnp.broadcast_to(lse, (2 * A_QB, 128))
            o_ref[0, pl.ds(r0, A_QB), ls] = jnp.where(lo_half, o[:A_QB], o[A_QB:]).astype(o_ref.dtype)
            lse_ref[0, pl.ds(r0, A_QB), ls] = jnp.where(lo_half, lse[:A_QB], lse[A_QB:])
        return carry

    lax.fori_loop(0, nj, body, 0)


def _dilated_pattern(qa, ka, va, bias, batch, seq, dil, rows=1024):
    L = seq // dil
    rows = min(rows, L)
    nblk = L // rows
    hb = rows // A_HALF
    nh = L // A_HALF
    view = lambda a: a.reshape(batch, L, dil * A_WIDTH)
    cur = lambda b, r, i: (b, i, r)
    prev = lambda b, r, i: (b, jnp.maximum(i * hb - 1, 0), r)
    nxt = lambda b, r, i: (b, jnp.minimum((i + 1) * hb, nh - 1), r)
    cur_spec = pl.BlockSpec((1, rows, A_WIDTH), cur)
    out_sds = lambda dt: jax.ShapeDtypeStruct((batch, L, dil * A_WIDTH), dt)
    o, lse = pl.pallas_call(
        functools.partial(_dilated_kernel, rows=rows),
        grid=(batch, dil, nblk),
        in_specs=[cur_spec, cur_spec,
                  pl.BlockSpec((1, A_HALF, A_WIDTH), prev), pl.BlockSpec((1, A_HALF, A_WIDTH), nxt),
                  cur_spec,
                  pl.BlockSpec((1, A_HALF, A_WIDTH), prev), pl.BlockSpec((1, A_HALF, A_WIDTH), nxt),
                  pl.BlockSpec(bias.shape, lambda b, r, i: (0, 0, 0))],
        out_specs=[cur_spec, cur_spec],
        out_shape=(out_sds(BF16), out_sds(F32)),
        scratch_shapes=[pltpu.VMEM((rows + 2 * A_HALF, A_WIDTH), BF16),
                        pltpu.VMEM((rows + 2 * A_HALF, A_WIDTH), BF16)],
        compiler_params=_cparams("parallel", "parallel", "parallel"),
        name=f"dilated_d{dil}",
    )(view(qa), view(ka), view(ka), view(ka), view(va), view(va), view(va), bias)
    return o.reshape(batch * L, dil * A_WIDTH), lse.reshape(batch * L, dil * A_WIDTH)


def _dilated_bias(dil):
    slopes = 2.0 ** (-8.0 * (np.arange(A_HEADS, dtype=np.float64) + 1.0) / A_HEADS)
    col = np.arange(A_WIN)[None, :]
    rel = (col - A_HALF) - np.arange(A_QB)[:, None]
    bias = -slopes[:, None, None] * (np.abs(rel) * dil).astype(np.float64)[None]
    bias = np.where((np.abs(rel) <= A_HALF)[None], bias, NEG)
    variants = []
    for v in range(4):
        ok = np.ones((1, A_WIN), bool)
        if v & 1:
            ok = np.logical_and(ok, col >= A_HALF)
        if v & 2:
            ok = np.logical_and(ok, col < A_HALF + A_QB)
        variants.append(np.where(ok[None], bias, NEG).reshape(A_HEADS // 2, 2 * A_QB, A_WIN))
    return jnp.asarray(np.concatenate(variants, axis=0), F32)


def _mla_kernel(q_ref, k_ref, vt_ref, o_ref, *s_bufs, tk):
    tq = q_ref.shape[1]
    nk = k_ref.shape[1] // tk
    nh = q_ref.shape[2] // HEAD_PAD
    qs = [q_ref[0, :, hh * HEAD_PAD:(hh + 1) * HEAD_PAD] for hh in range(nh)]

    def scores(j, s_ref, hh):
        off = pl.multiple_of(j * tk, tk)
        s_ref[hh] = lax.dot_general(k_ref[0, pl.ds(off, tk), hh * HEAD_PAD:(hh + 1) * HEAD_PAD], qs[hh],
                                    (((1,), (1,)), ((), ())), preferred_element_type=F32)

    ones = jnp.ones((ONES_ROWS, tk), BF16)

    def probabilities(s_ref, hh, m):
        s = s_ref[hh]
        m_new = jnp.maximum(m, jnp.max(s, axis=0, keepdims=True))
        return m_new, jnp.exp2(m - m_new), jnp.exp2(s - m_new).astype(BF16)

    def values(j, hh, alpha, p, acc):
        off = pl.multiple_of(j * tk, tk)
        vt = jnp.concatenate([vt_ref[hh * B_V:(hh + 1) * B_V, pl.ds(off, tk)], ones], axis=0)
        return alpha * acc + jnp.dot(vt, p, preferred_element_type=F32)

    def stage(j, s_ref, j_next, s_next, stats):
        new = []
        for hh in range(nh):
            scores(j_next, s_next, hh)
            m_new, alpha, p = probabilities(s_ref, hh, stats[hh][0])
            new.append((m_new, values(j, hh, alpha, p, stats[hh][1])))
        return tuple(new)

    nbuf = len(s_bufs)

    def body(jj, stats):
        j = nbuf * jj
        for u in range(nbuf):
            stats = stage(j + u, s_bufs[u], jnp.minimum(j + u + 1, nk - 1), s_bufs[(u + 1) % nbuf], stats)
        return stats

    init = (jnp.full((1, tq), NEG, F32), jnp.zeros((B_V + ONES_ROWS, tq), F32))
    for hh in range(nh):
        scores(0, s_bufs[0], hh)
    stats = lax.fori_loop(0, nk // nbuf, body, (init,) * nh)
    o_t = jnp.concatenate([a[:B_V] / a[B_V:B_V + 1] for _, a in stats], axis=0)
    o_ref[0] = o_t.T.astype(BF16)


def _mla_attention(qb, kb, vbt, batch, seq, tq=256, tk=512, nh=MLA_HEADS_PER_STEP):
    once = dict(pipeline_mode=pl.Buffered(1))
    return pl.pallas_call(
        functools.partial(_mla_kernel, tk=tk),
        grid=(batch, B_HEADS // nh, seq // tq),
        in_specs=[pl.BlockSpec((1, tq, nh * HEAD_PAD), lambda b, p, i: (b, i, p)),
                  pl.BlockSpec((1, seq, nh * HEAD_PAD), lambda b, p, i: (b, 0, p), **once),
                  pl.BlockSpec((nh * B_V, seq), lambda b, p, i: (p, b), **once)],
        out_specs=pl.BlockSpec((1, tq, nh * B_V), lambda b, p, i: (b, i, p)),
        out_shape=jax.ShapeDtypeStruct((batch, seq, B_WIDTH), BF16),
        scratch_shapes=[pltpu.VMEM((nh, tk, tq), F32) for _ in range(MLA_SCORE_BUFS)],
        compiler_params=_cparams("parallel", "parallel", "parallel"),
        name="mla_attention",
    )(qb, kb, vbt)


def _outproj_kernel(o1_ref, o2_ref, o3_ref, l1_ref, l2_ref, l3_ref, ob_ref, gate_ref, x_ref,
                    wa_ref, wb_ref, wo_ref, g2_ref, wrh_ref, wrl_ref, br_ref,
                    tri_ref, x1_ref, h2_ref, route_ref, count_ref, perm_ref, perm4_ref, carry_ref):
    tm = x_ref.shape[0]

    @pl.when(pl.program_id(0) == 0)
    def _():
        carry_ref[...] = jnp.zeros_like(carry_ref)

    nc = A_WIDTH // 128

    def lanes(r, c):
        return slice(r * A_WIDTH + c * 128, r * A_WIDTH + (c + 1) * 128)

    def token_order(which, ref, d):
        if d == 4:
            for r in range(4):
                for c in range(nc):
                    perm_ref[which, c, pl.ds(r, tm // 4, stride=4), :] = ref[:, lanes(r, c)].astype(F32)
        else:
            for r1 in range(4):
                for r2 in range(4):
                    for c in range(nc):
                        perm4_ref[which - 2, r1, c, pl.ds(r2, tm // 16, stride=4), :] = (
                            ref[:, lanes(4 * r2 + r1, c)].astype(F32))
            for r1 in range(4):
                for c in range(nc):
                    perm_ref[which, c, pl.ds(r1, tm // 4, stride=4), :] = perm4_ref[which - 2, r1, c]
        return jnp.concatenate([perm_ref[which, c] for c in range(nc)], axis=1)

    l1 = l1_ref[...]
    l2, o2 = token_order(0, l2_ref, 4), token_order(1, o2_ref, 4)
    l3, o3 = token_order(2, l3_ref, 16), token_order(3, o3_ref, 16)
    mx = jnp.maximum(jnp.maximum(l1, l2), l3)
    e1, e2, e3 = jnp.exp(l1 - mx), jnp.exp(l2 - mx), jnp.exp(l3 - mx)
    oa = (e1 * o1_ref[...] + e2 * o2 + e3 * o3) / (e1 + e2 + e3)
    ya = jnp.dot(oa.astype(BF16), wa_ref[...], preferred_element_type=F32)
    yb = jnp.dot(ob_ref[...], wb_ref[...], preferred_element_type=F32)
    merged = gate_ref[:, :D_MODEL].astype(F32) * ya + gate_ref[:, D_MODEL:].astype(F32) * yb
    x1 = x_ref[...] + jnp.dot(merged.astype(BF16), wo_ref[...], preferred_element_type=F32)
    x1_ref[...] = x1
    h2 = x1 * lax.rsqrt(jnp.mean(x1 * x1, axis=-1, keepdims=True) + EPS) * g2_ref[...]
    hi = h2.astype(BF16)
    h2_ref[...] = _pack_rows(h2)
    lo = (h2 - hi.astype(F32)).astype(BF16)
    logits = jnp.dot(hi, wrh_ref[...], preferred_element_type=F32)
    logits = logits + jnp.dot(lo, wrh_ref[...], preferred_element_type=F32)
    logits = logits + jnp.dot(hi, wrl_ref[...], preferred_element_type=F32)
    logits = logits + br_ref[...]

    lane = lax.broadcasted_iota(jnp.int32, (tm, LOGIT_PAD), 1)
    work = logits
    sel = jnp.zeros((tm, LOGIT_PAD), F32)
    idx_cols, val_cols = [], []
    for _ in range(TOP_K):
        mx = jnp.max(work, axis=-1, keepdims=True)
        idx = jnp.min(jnp.where(work == mx, lane, LOGIT_PAD), axis=-1, keepdims=True)
        hit = lane == idx
        sel = sel + hit.astype(F32)
        work = jnp.where(hit, 2.0 * NEG, work)
        idx_cols.append(idx)
        val_cols.append(mx)
    exps = [jnp.exp(v - val_cols[0]) for v in val_cols]
    denom = exps[0] + exps[1] + exps[2] + exps[3]
    gate_cols = [e / denom for e in exps]

    before = jnp.dot(tri_ref[...], sel.astype(BF16), preferred_element_type=F32) + carry_ref[0:1, :]
    rank_cols = [jnp.sum(jnp.where(lane == idx, before, 0.0), axis=-1, keepdims=True) for idx in idx_cols]
    carry_ref[...] = carry_ref[...] + jnp.sum(sel, axis=0, keepdims=True)
    count_ref[...] = carry_ref[...]

    cols = [c.astype(F32) for c in idx_cols] + rank_cols + gate_cols
    route = jnp.zeros((tm, LOGIT_PAD), F32)
    for c, col in enumerate(cols):
        route = jnp.where(lane == c, col, route)
    route_ref[...] = route


def _outproj(o1, o2, o3, l1, l2, l3, ob, gates, x2, wa, wb, wo, g2, wrh, wrl, br, tm=512):
    T = x2.shape[0]
    row = lambda i: (i, 0)
    fix = lambda i: (0, 0)
    full = lambda a: pl.BlockSpec(a.shape, fix)
    rows = lambda a: pl.BlockSpec((tm * a.shape[0] // T, a.shape[1]), row)
    tri = (jnp.arange(tm)[:, None] > jnp.arange(tm)[None, :]).astype(BF16)
    out_shape = (jax.ShapeDtypeStruct((T, D_MODEL), F32),
                 jax.ShapeDtypeStruct((T, HALF_D), jnp.int32),
                 jax.ShapeDtypeStruct((T, LOGIT_PAD), F32),
                 jax.ShapeDtypeStruct((8, LOGIT_PAD), F32))
    return pl.pallas_call(
        _outproj_kernel,
        grid=(T // tm,),
        in_specs=[rows(o1), rows(o2), rows(o3), rows(l1), rows(l2), rows(l3), rows(ob), rows(gates),
                  rows(x2), full(wa), full(wb), full(wo), full(g2), full(wrh), full(wrl), full(br), full(tri)],
        out_specs=[pl.BlockSpec((tm, D_MODEL), row), pl.BlockSpec((tm, HALF_D), row),
                   pl.BlockSpec((tm, LOGIT_PAD), row), pl.BlockSpec((8, LOGIT_PAD), fix)],
        out_shape=out_shape,
        scratch_shapes=[pltpu.VMEM((4, A_WIDTH // 128, tm, 128), F32),
                        pltpu.VMEM((2, 4, A_WIDTH // 128, tm // 4, 128), F32),
                        pltpu.VMEM((8, LOGIT_PAD), F32)],
        compiler_params=_cparams("arbitrary"),
        name="outproj",
    )(o1, o2, o3, l1, l2, l3, ob, gates, x2, wa, wb, wo, g2, wrh, wrl, br, tri)


HALF_D = D_MODEL // 2


def _pack_rows(x):
    lo = pltpu.bitcast(x[:, :HALF_D].astype(BF16).astype(F32), jnp.uint32)
    hi = pltpu.bitcast(x[:, HALF_D:].astype(BF16).astype(F32), jnp.uint32)
    word = jnp.bitwise_or(lax.shift_right_logical(lo, jnp.uint32(16)), jnp.bitwise_and(hi, jnp.uint32(0xFFFF0000)))
    return pltpu.bitcast(word, jnp.int32)


def _unpack_rows(w):
    u = pltpu.bitcast(w, jnp.uint32)
    lo = pltpu.bitcast(lax.shift_left(u, jnp.uint32(16)), F32)
    hi = pltpu.bitcast(jnp.bitwise_and(u, jnp.uint32(0xFFFF0000)), F32)
    return jnp.concatenate([lo, hi], axis=1)


SC_WINDOW = 128


def _gather_rows(table, idx):
    M = idx.shape[0]
    width = table.shape[1]
    mesh = plsc.VectorSubcoreMesh(core_axis_name="core", subcore_axis_name="subcore")

    @functools.partial(pl.kernel, out_type=jax.ShapeDtypeStruct((M, width), table.dtype), mesh=mesh,
                       name="gather_rows")
    def gather(table_hbm, idx_hbm, out_hbm):
        def body(idx_vmem, out_vmem):
            pltpu.sync_copy(table_hbm.at[idx_vmem.at[0]], out_vmem)

        pltpu.emit_pipeline(
            body,
            grid=(M // SC_WINDOW,),
            in_specs=[pl.BlockSpec((1, SC_WINDOW), index_map=lambda i: (0, i))],
            out_specs=[pl.BlockSpec((SC_WINDOW, width), index_map=lambda i: (i, 0),
                                    pipeline_mode=pl.Buffered(1))],
            core_axis_name=("core", "subcore"),
            dimension_semantics=(pltpu.PARALLEL,),
        )(idx_hbm, out_hbm)

    return gather(table, idx.reshape(1, M))


def _scatter_rows(rows, idx, n_out):
    M = idx.shape[0]
    R, width = rows.shape
    mesh = plsc.VectorSubcoreMesh(core_axis_name="core", subcore_axis_name="subcore")

    @functools.partial(pl.kernel, out_type=jax.ShapeDtypeStruct((n_out, width), rows.dtype), mesh=mesh,
                       name="scatter_rows")
    def scatter(rows_hbm, idx_hbm, out_hbm):
        def body(rows_vmem, idx_vmem):
            pltpu.sync_copy(rows_vmem, out_hbm.at[idx_vmem.at[0]])

        pltpu.emit_pipeline(
            body,
            grid=(M // SC_WINDOW,),
            in_specs=[pl.BlockSpec((SC_WINDOW, width), index_map=lambda i: (i % (R // SC_WINDOW), 0),
                                   pipeline_mode=pl.Buffered(1)),
                      pl.BlockSpec((1, SC_WINDOW), index_map=lambda i: (0, i))],
            out_specs=[],
            core_axis_name=("core", "subcore"),
            dimension_semantics=(pltpu.PARALLEL,),
        )(rows_hbm, idx_hbm)

    return scatter(rows, idx.reshape(1, M))


def _moe_kernel(be_ref, nu_ref, nv_ref, x_ref, wgu_ref, bgu_ref, wd_ref, bd_ref, y_ref, wgu_bf, wd_bf):
    i = pl.program_id(0)

    @pl.when(i >= nu_ref[0])
    def _():
        y_ref[...] = jnp.zeros_like(y_ref)

    @pl.when(jnp.logical_or(i == 0, be_ref[i] != be_ref[jnp.maximum(i - 1, 0)]))
    def _():
        rows = 128

        def cast(c, carry):
            r0 = pl.multiple_of(c * rows, rows)
            wgu_bf[pl.ds(r0, rows), :] = wgu_ref[0, pl.ds(r0, rows), :].astype(BF16)
            wd_bf[pl.ds(r0, rows), :] = wd_ref[0, pl.ds(r0, rows), :].astype(BF16)
            return carry

        lax.fori_loop(0, D_MODEL // rows, cast, 0)

    @pl.when(i < nu_ref[0])
    def _():
        live = lax.broadcasted_iota(jnp.int32, (x_ref.shape[0], 1), 0) < nv_ref[i]
        x = jnp.where(live, _unpack_rows(x_ref[...]), 0.0).astype(BF16)
        gu = jnp.dot(x, wgu_bf[...], preferred_element_type=F32) + bgu_ref[0]
        gate = jnp.minimum(gu[:, :D_MODEL], SWIGLU_LIMIT)
        up = jnp.clip(gu[:, D_MODEL:], -SWIGLU_LIMIT, SWIGLU_LIMIT)
        act = (up + 1.0) * (gate * jax.nn.sigmoid(SWIGLU_ALPHA * gate))
        y = jnp.dot(act.astype(BF16), wd_bf[...], preferred_element_type=F32) + bd_ref[0]
        y_ref[...] = _pack_rows(y)


def _moe_experts(xb, blk_e, n_used, n_valid, wgu, bgu, wd, bd, bm):
    NP = xb.shape[0]
    nblk = NP // bm
    grid_spec = pltpu.PrefetchScalarGridSpec(
        num_scalar_prefetch=3,
        grid=(nblk,),
        in_specs=[pl.BlockSpec((bm, HALF_D), lambda i, be, nu, nv: (jnp.minimum(i, nu[0] - 1), 0)),
                  pl.BlockSpec((1, D_MODEL, 2 * D_MODEL), lambda i, be, nu, nv: (be[i], 0, 0)),
                  pl.BlockSpec((1, 1, 2 * D_MODEL), lambda i, be, nu, nv: (be[i], 0, 0)),
                  pl.BlockSpec((1, D_MODEL, D_MODEL), lambda i, be, nu, nv: (be[i], 0, 0)),
                  pl.BlockSpec((1, 1, D_MODEL), lambda i, be, nu, nv: (be[i], 0, 0))],
        out_specs=pl.BlockSpec((bm, HALF_D), lambda i, be, nu, nv: (i, 0)),
        scratch_shapes=[pltpu.VMEM((D_MODEL, 2 * D_MODEL), BF16), pltpu.VMEM((D_MODEL, D_MODEL), BF16)],
    )
    return pl.pallas_call(
        _moe_kernel,
        grid_spec=grid_spec,
        out_shape=jax.ShapeDtypeStruct((NP, HALF_D), jnp.int32),
        compiler_params=_cparams("arbitrary"),
        name="moe_experts",
    )(blk_e, n_used, n_valid, xb, wgu, bgu, wd, bd)


def _final_kernel(x1_ref, y_ref, route_ref, g_ref, o_ref):
    w = route_ref[...]
    acc = x1_ref[...]
    for k in range(TOP_K):
        acc = acc + w[:, 2 * TOP_K + k:2 * TOP_K + k + 1] * _unpack_rows(y_ref[k])
    o_ref[...] = acc * lax.rsqrt(jnp.mean(acc * acc, axis=-1, keepdims=True) + EPS) * g_ref[...]


def _final(x1, yk, route, g, tm=512):
    T = x1.shape[0]
    return pl.pallas_call(
        _final_kernel,
        grid=(T // tm,),
        in_specs=[pl.BlockSpec((tm, D_MODEL), lambda i: (i, 0)),
                  pl.BlockSpec((TOP_K, tm, HALF_D), lambda i: (0, i, 0)),
                  pl.BlockSpec((tm, LOGIT_PAD), lambda i: (i, 0)),
                  pl.BlockSpec((1, D_MODEL), lambda i: (0, 0))],
        out_specs=pl.BlockSpec((tm, D_MODEL), lambda i: (i, 0)),
        out_shape=jax.ShapeDtypeStruct((T, D_MODEL), F32),
        compiler_params=_cparams("parallel"),
        name="final_norm",
    )(x1, yk, route, g)


def _dest_kernel(route_ref, pstart_ref, dest_ref):
    tm = route_ref.shape[0]
    route = route_ref[...]
    lane = lax.broadcasted_iota(jnp.int32, (tm, LOGIT_PAD), 1)
    expert = jnp.where(lane < TOP_K, route, 0.0).astype(jnp.int32)
    start = jnp.take_along_axis(jnp.broadcast_to(pstart_ref[...], (tm, LOGIT_PAD)), expert, axis=1)
    position = pltpu.roll(route, LOGIT_PAD - TOP_K, axis=1)
    rows = start + position
    dest_ref[...] = rows.T[:TOP_K].astype(jnp.int32)


def _dest_rows(route, pstart, tm=2048):
    T = route.shape[0]
    return pl.pallas_call(
        _dest_kernel,
        grid=(T // tm,),
        in_specs=[pl.BlockSpec((tm, LOGIT_PAD), lambda i: (i, 0)),
                  pl.BlockSpec((1, LOGIT_PAD), lambda i: (0, 0))],
        out_specs=pl.BlockSpec((TOP_K, tm), lambda i: (0, i)),
        out_shape=jax.ShapeDtypeStruct((TOP_K, T), jnp.int32),
        compiler_params=_cparams("parallel"),
        name="dest_rows",
    )(route, pstart)


def _pack_w_in(w):
    kr = w[:, 1920:1952]
    half = B_ROPE // 2
    z = lambda n: jnp.zeros((w.shape[0], n), w.dtype)
    kr_placed = jnp.concatenate([z(B_NOPE), kr, z(HEAD_PAD - B_NOPE - B_ROPE)], axis=1)
    kr_rot = jnp.concatenate([z(B_NOPE), -kr[:, half:], kr[:, :half], z(HEAD_PAD - B_NOPE - B_ROPE)], axis=1)
    return jnp.concatenate([w[:, :1920], w[:, 1952:4000], kr_placed, kr_rot], axis=1).astype(BF16)


def _pack_w_uq(w):
    half = B_ROPE // 2
    r = w.shape[0]
    nope, x1, x2 = w[..., :B_NOPE], w[..., B_NOPE:B_NOPE + half], w[..., B_NOPE + half:]
    zpad = jnp.zeros((r, B_HEADS, HEAD_PAD - B_NOPE - B_ROPE), w.dtype)
    main = jnp.concatenate([nope, x1, x2, zpad], axis=-1)
    rot = jnp.concatenate([jnp.zeros_like(nope), -x2, x1, zpad], axis=-1)
    return (main.reshape(r, B_HEADS * HEAD_PAD).astype(BF16), rot.reshape(r, B_HEADS * HEAD_PAD).astype(BF16))


def _pack_w_uk(w):
    r = w.shape[0]
    zpad = jnp.zeros((r, B_HEADS, HEAD_PAD - B_NOPE), w.dtype)
    return jnp.concatenate([w, zpad], axis=-1).reshape(r, B_HEADS * HEAD_PAD).astype(BF16)


def _rope_tables(seq):
    inv_freq = ROPE_THETA ** (-np.arange(0, B_ROPE, 2, dtype=np.float64) / B_ROPE)
    ang = np.arange(seq, dtype=np.float64)[:, None] * inv_freq[None, :]
    cos, sin = jnp.asarray(np.cos(ang), F32), jnp.asarray(np.sin(ang), F32)
    one = jnp.ones((seq, B_NOPE), F32)
    zero = jnp.zeros((seq, B_NOPE), F32)
    zpad = jnp.zeros((seq, HEAD_PAD - B_NOPE - B_ROPE), F32)
    qs = ((B_NOPE + B_ROPE) ** -0.5) * LOG2E
    cq = jnp.concatenate([one, cos, cos, zpad], axis=1) * qs
    sq = jnp.concatenate([zero, sin, sin, zpad], axis=1) * qs
    ck = jnp.concatenate([zero, cos, cos, zpad], axis=1)
    sk = jnp.concatenate([zero, sin, sin, zpad], axis=1)
    return cq, sq, ck, sk


def kernel(x, norm1_g, w_in, q_a_norm_g, kv_a_norm_g, w_uq, w_uk, w_uv, w_a_out, w_b_out, w_o,
           norm2_g, w_router, b_router, w_gate_up, b_gate_up, w_down, b_down, final_g):
    B, S, D = x.shape
    T = B * S
    depth = norm1_g.shape[0]
    xt = x.reshape(T, D)
    cq_t, sq_t, ck_t, sk_t = _rope_tables(S)
    biases = [_dilated_bias(d) for _, d in A_PATTERNS]
    bm = 512
    for l in range(depth):
        wq1, wq2 = _pack_w_uq(w_uq[l])
        qa, ka, va, qa4, ka4, va4, qa16, ka16, va16, qb, kb, vb, gates = _inproj(
            xt, norm1_g[l][None], _pack_w_in(w_in[l]), q_a_norm_g[l][None], kv_a_norm_g[l][None],
            wq1, wq2, _pack_w_uk(w_uk[l]), w_uv[l].reshape(KV_LORA, B_WIDTH).T.astype(BF16),
            cq_t, sq_t, ck_t, sk_t, S)
        qkv = ((qa, ka, va), (qa4, ka4, va4), (qa16, ka16, va16))
        outs = [_dilated_pattern(*qkv[p], biases[p], B, S, A_PATTERNS[p][1]) for p in range(3)]
        ob = _mla_attention(qb.reshape(B, S, -1), kb.reshape(B, S, -1), vb, B, S)

        wr = jnp.pad(w_router[l], ((0, 0), (0, LOGIT_PAD - N_EXPERTS)))
        wrh = wr.astype(BF16)
        wrl = (wr - wrh.astype(F32)).astype(BF16)
        br = jnp.pad(b_router[l], (0, LOGIT_PAD - N_EXPERTS), constant_values=NEG)[None]
        x1, h2, route, count_rows = _outproj(
            outs[0][0], outs[1][0], outs[2][0], outs[0][1], outs[1][1], outs[2][1],
            ob.reshape(T, B_WIDTH), gates, xt,
            w_a_out[l].astype(BF16), w_b_out[l].astype(BF16), w_o[l].astype(BF16),
            norm2_g[l][None], wrh, wrl, br)

        counts = count_rows[0, :N_EXPERTS].astype(jnp.int32)
        padded = (counts + bm - 1) // bm * bm
        pend = jnp.cumsum(padded)
        pstart = pend - padded
        NP = T * TOP_K + N_EXPERTS * bm
        nblk = NP // bm
        blk_start = jnp.arange(nblk, dtype=jnp.int32) * bm
        blk_e = jnp.minimum(jnp.sum(pend[None, :] <= blk_start[:, None], axis=1), N_EXPERTS - 1).astype(jnp.int32)
        n_used = (pend[-1] // bm).astype(jnp.int32)[None]
        n_valid = jnp.clip(pstart[blk_e] + counts[blk_e] - blk_start, 0, bm).astype(jnp.int32)

        pstart_row = jnp.pad(pstart.astype(F32), (0, LOGIT_PAD - N_EXPERTS))[None]
        dest_kt = _dest_rows(route, pstart_row).reshape(-1)
        xb = _scatter_rows(h2, dest_kt, NP)
        yb = _moe_experts(xb, blk_e, n_used, n_valid, w_gate_up[l], b_gate_up[l][:, None, :],
                          w_down[l], b_down[l][:, None, :], bm)
        yk = _gather_rows(yb, dest_kt).reshape(TOP_K, T, HALF_D)
        if l + 1 < depth:
            raise NotImplementedError("depth > 1")
        out = _final(x1, yk, route, final_g[None])
    return out.reshape(B, S, D)
```

```python
import functools

import jax
import jax.numpy as jnp
import numpy as np
from jax import lax
from jax.experimental import pallas as pl
from jax.experimental.pallas import tpu as pltpu
from jax.experimental.pallas import tpu_sc as plsc

F32 = jnp.float32
BF16 = jnp.bfloat16

D_MODEL = 1024
EPS = 1e-5
NEG = -1e30

A_HEADS = 8
A_HEAD_DIM = 64
A_WIDTH = 512
A_PATTERNS = ((128, 1), (512, 4), (2048, 16))
A_HALF = 64
A_QB = 128
A_WIN = A_QB + 2 * A_HALF
assert all(w // (2 * d) == A_HALF for w, d in A_PATTERNS) and [d for _, d in A_PATTERNS] == [1, 4, 16]

B_HEADS = 8
B_NOPE = 64
B_ROPE = 32
B_V = 64
B_WIDTH = 512
Q_LORA = 256
KV_LORA = 128
ROPE_THETA = 10000.0
HEAD_PAD = 128
ONES_ROWS = 16
MLA_SCORE_BUFS = 4
MLA_HEADS_PER_STEP = 4

N_EXPERTS = 32
TOP_K = 4
SWIGLU_LIMIT = 7.0
SWIGLU_ALPHA = 1.702
LOGIT_PAD = 128

LOG2E = 1.4426950408889634

VMEM_LIMIT = 56 * 1024 * 1024


def _cparams(*sem):
    return pltpu.CompilerParams(dimension_semantics=sem, vmem_limit_bytes=VMEM_LIMIT)


def _inproj_kernel(x_ref, g1_ref, wp_ref, gq_ref, gkv_ref, wq1_ref, wq2_ref, wuk_ref, wuv_ref,
                   cq_ref, sq_ref, ck_ref, sk_ref,
                   qa_ref, ka_ref, va_ref, qa4_ref, ka4_ref, va4_ref, qa16_ref, ka16_ref, va16_ref,
                   qb_ref, kb_ref, vb_ref, gate_ref, perm_ref, perm4_ref):
    tm = x_ref.shape[0]
    x = x_ref[...]
    h = x * lax.rsqrt(jnp.mean(x * x, axis=-1, keepdims=True) + EPS) * g1_ref[...]
    h = h.astype(BF16)

    def proj(c0, c1):
        return jnp.dot(h, wp_ref[:, c0:c1], preferred_element_type=F32)

    def emit(which, z, nat_ref, v4_ref, v16_ref):
        nat_ref[...] = z.astype(BF16)
        nc = A_WIDTH // 128
        for c in range(nc):
            perm_ref[which, c] = z[:, c * 128:(c + 1) * 128]
        for r1 in range(4):
            for c in range(nc):
                quarter = perm_ref[which, c, pl.ds(r1, tm // 4, stride=4), :]
                v4_ref[:, r1 * A_WIDTH + c * 128:r1 * A_WIDTH + (c + 1) * 128] = quarter.astype(BF16)
                perm4_ref[which, r1, c] = quarter
        for r1 in range(4):
            for r2 in range(4):
                r = 4 * r2 + r1
                for c in range(nc):
                    v16_ref[:, r * A_WIDTH + c * 128:r * A_WIDTH + (c + 1) * 128] = (
                        perm4_ref[which, r1, c, pl.ds(r2, tm // 16, stride=4), :].astype(BF16))

    emit(0, proj(0, 512) * (A_HEAD_DIM ** -0.5), qa_ref, qa4_ref, qa16_ref)
    emit(1, proj(512, 1024), ka_ref, ka4_ref, ka16_ref)
    emit(2, proj(1024, 1536), va_ref, va4_ref, va16_ref)

    cq = proj(1536, 1792)
    cq = cq * lax.rsqrt(jnp.mean(cq * cq, axis=-1, keepdims=True) + EPS) * gq_ref[...]
    cq = cq.astype(BF16)
    q_main = jnp.dot(cq, wq1_ref[...], preferred_element_type=F32)
    q_rot = jnp.dot(cq, wq2_ref[...], preferred_element_type=F32)
    cq_t = cq_ref[...]
    sq_t = sq_ref[...]
    for hd in range(B_HEADS):
        sl = slice(hd * HEAD_PAD, (hd + 1) * HEAD_PAD)
        qb_ref[:, sl] = (q_main[:, sl] * cq_t + q_rot[:, sl] * sq_t).astype(BF16)

    ckv = proj(1792, 1920)
    ckv = ckv * lax.rsqrt(jnp.mean(ckv * ckv, axis=-1, keepdims=True) + EPS) * gkv_ref[...]
    ckv = ckv.astype(BF16)
    k_nope = jnp.dot(ckv, wuk_ref[...], preferred_element_type=F32)
    vb_ref[...] = lax.dot_general(wuv_ref[...], ckv, (((1,), (1,)), ((), ())),
                                  preferred_element_type=F32).astype(BF16)
    k_rope = proj(3968, 4096) * ck_ref[...] + proj(4096, 4224) * sk_ref[...]
    for hd in range(B_HEADS):
        sl = slice(hd * HEAD_PAD, (hd + 1) * HEAD_PAD)
        kb_ref[:, sl] = (k_nope[:, sl] + k_rope).astype(BF16)

    gate_ref[...] = jax.nn.sigmoid(proj(1920, 3968)).astype(BF16)


def _inproj(x2, g1, wp, gq, gkv, wq1, wq2, wuk, wuv, cq_t, sq_t, ck_t, sk_t, seq, tm=512):
    T = x2.shape[0]
    nseq = seq // tm
    row = lambda i: (i, 0)
    fix = lambda i: (0, 0)
    tab = lambda i: (i % nseq, 0)
    full = lambda a: pl.BlockSpec(a.shape, fix)
    sds = lambda r, c: jax.ShapeDtypeStruct((r, c), BF16)
    out_shape = (
        sds(T, A_WIDTH), sds(T, A_WIDTH), sds(T, A_WIDTH),
        sds(T // 4, 4 * A_WIDTH), sds(T // 4, 4 * A_WIDTH), sds(T // 4, 4 * A_WIDTH),
        sds(T // 16, 16 * A_WIDTH), sds(T // 16, 16 * A_WIDTH), sds(T // 16, 16 * A_WIDTH),
        sds(T, B_HEADS * HEAD_PAD), sds(T, B_HEADS * HEAD_PAD),
        sds(B_WIDTH, T),
        sds(T, 2 * D_MODEL),
    )
    out_specs = [pl.BlockSpec((tm * s.shape[0] // T, s.shape[1]), row) for s in out_shape]
    out_specs[11] = pl.BlockSpec((B_WIDTH, tm), lambda i: (0, i))
    return pl.pallas_call(
        _inproj_kernel,
        grid=(T // tm,),
        in_specs=[pl.BlockSpec((tm, D_MODEL), row), full(g1), full(wp), full(gq), full(gkv),
                  full(wq1), full(wq2), full(wuk), full(wuv),
                  pl.BlockSpec((tm, HEAD_PAD), tab), pl.BlockSpec((tm, HEAD_PAD), tab),
                  pl.BlockSpec((tm, HEAD_PAD), tab), pl.BlockSpec((tm, HEAD_PAD), tab)],
        out_specs=out_specs,
        out_shape=out_shape,
        scratch_shapes=[pltpu.VMEM((3, A_WIDTH // 128, tm, 128), F32),
                        pltpu.VMEM((3, 4, A_WIDTH // 128, tm // 4, 128), F32)],
        compiler_params=_cparams("parallel"),
        name="inproj",
    )(x2, g1, wp, gq, gkv, wq1, wq2, wuk, wuv, cq_t, sq_t, ck_t, sk_t)


def _dilated_kernel(q_ref, kc_ref, kp_ref, kn_ref, vc_ref, vp_ref, vn_ref, bias_ref,
                    o_ref, lse_ref, kbuf, vbuf, *, rows):
    i = pl.program_id(2)
    last = pl.num_programs(2) - 1
    kbuf[0:A_HALF, :] = kp_ref[0]
    kbuf[A_HALF:A_HALF + rows, :] = kc_ref[0]
    kbuf[A_HALF + rows:, :] = kn_ref[0]
    vbuf[0:A_HALF, :] = vp_ref[0]
    vbuf[A_HALF:A_HALF + rows, :] = vc_ref[0]
    vbuf[A_HALF + rows:, :] = vn_ref[0]

    nj = rows // A_QB
    npair = A_HEADS // 2
    lane = lax.broadcasted_iota(jnp.int32, (A_QB, 128), 1)
    lo_half = lane < A_HEAD_DIM

    def body(j, carry):
        r0 = pl.multiple_of(j * A_QB, A_QB)
        variant = (jnp.logical_and(i == 0, j == 0).astype(jnp.int32)
                   + 2 * jnp.logical_and(i == last, j == nj - 1).astype(jnp.int32))
        scores = []
        for pr in range(npair):
            ls = slice(pr * 128, (pr + 1) * 128)
            qp = q_ref[0, pl.ds(r0, A_QB), ls]
            zero = jnp.zeros_like(qp)
            q2 = jnp.concatenate([jnp.where(lo_half, qp, zero), jnp.where(lo_half, zero, qp)], axis=0)
            s = lax.dot_general(q2, kbuf[pl.ds(r0, A_WIN), ls], (((1,), (1,)), ((), ())),
                                preferred_element_type=F32)
            scores.append(s + bias_ref[variant * npair + pr])
        probs = []
        for s in scores:
            m = jnp.max(s, axis=-1, keepdims=True)
            p = jnp.exp(s - m)
            l = jnp.sum(p, axis=-1, keepdims=True)
            probs.append((p.astype(BF16), l, m + jnp.log(l)))
        for pr, (p, l, lse) in enumerate(probs):
            ls = slice(pr * 128, (pr + 1) * 128)
            o = jnp.dot(p, vbuf[pl.ds(r0, A_WIN), ls], preferred_element_type=F32) / l
            lse = jnp.broadcast_to(lse, (2 * A_QB, 128))
            o_ref[0, pl.ds(r0, A_QB), ls] = jnp.where(lo_half, o[:A_QB], o[A_QB:]).astype(o_ref.dtype)
            lse_ref[0, pl.ds(r0, A_QB), ls] = jnp.where(lo_half, lse[:A_QB], lse[A_QB:])
        return carry

    lax.fori_loop(0, nj, body, 0)


def _dilated_pattern(qa, ka, va, bias, batch, seq, dil, rows=1024):
    L = seq // dil
    rows = min(rows, L)
    nblk = L // rows
    hb = rows // A_HALF
    nh = L // A_HALF
    view = lambda a: a.reshape(batch, L, dil * A_WIDTH)
    cur = lambda b, r, i: (b, i, r)
    prev = lambda b, r, i: (b, jnp.maximum(i * hb - 1, 0), r)
    nxt = lambda b, r, i: (b, jnp.minimum((i + 1) * hb, nh - 1), r)
    cur_spec = pl.BlockSpec((1, rows, A_WIDTH), cur)
    out_sds = lambda dt: jax.ShapeDtypeStruct((batch, L, dil * A_WIDTH), dt)
    o, lse = pl.pallas_call(
        functools.partial(_dilated_kernel, rows=rows),
        grid=(batch, dil, nblk),
        in_specs=[cur_spec, cur_spec,
                  pl.BlockSpec((1, A_HALF, A_WIDTH), prev), pl.BlockSpec((1, A_HALF, A_WIDTH), nxt),
                  cur_spec,
                  pl.BlockSpec((1, A_HALF, A_WIDTH), prev), pl.BlockSpec((1, A_HALF, A_WIDTH), nxt),
                  pl.BlockSpec(bias.shape, lambda b, r, i: (0, 0, 0))],
        out_specs=[cur_spec, cur_spec],
        out_shape=(out_sds(BF16), out_sds(F32)),
        scratch_shapes=[pltpu.VMEM((rows + 2 * A_HALF, A_WIDTH), BF16),
                        pltpu.VMEM((rows + 2 * A_HALF, A_WIDTH), BF16)],
        compiler_params=_cparams("parallel", "parallel", "parallel"),
        name=f"dilated_d{dil}",
    )(view(qa), view(ka), view(ka), view(ka), view(va), view(va), view(va), bias)
    return o.reshape(batch * L, dil * A_WIDTH), lse.reshape(batch * L, dil * A_WIDTH)


def _dilated_bias(dil):
    slopes = 2.0 ** (-8.0 * (np.arange(A_HEADS, dtype=np.float64) + 1.0) / A_HEADS)
    col = np.arange(A_WIN)[None, :]
    rel = (col - A_HALF) - np.arange(A_QB)[:, None]
    bias = -slopes[:, None, None] * (np.abs(rel) * dil).astype(np.float64)[None]
    bias = np.where((np.abs(rel) <= A_HALF)[None], bias, NEG)
    variants = []
    for v in range(4):
        ok = np.ones((1, A_WIN), bool)
        if v & 1:
            ok = np.logical_and(ok, col >= A_HALF)
        if v & 2:
            ok = np.logical_and(ok, col < A_HALF + A_QB)
        variants.append(np.where(ok[None], bias, NEG).reshape(A_HEADS // 2, 2 * A_QB, A_WIN))
    return jnp.asarray(np.concatenate(variants, axis=0), F32)


def _mla_kernel(q_ref, k_ref, vt_ref, o_ref, *s_bufs, tk):
    tq = q_ref.shape[1]
    nk = k_ref.shape[1] // tk
    nh = q_ref.shape[2] // HEAD_PAD
    qs = [q_ref[0, :, hh * HEAD_PAD:(hh + 1) * HEAD_PAD] for hh in range(nh)]

    def scores(j, s_ref, hh, part=None):
        rows = tk if part is None else tk // 2
        r0 = 0 if part is None else part * rows
        off = pl.multiple_of(j * tk, tk) + r0
        s_ref[hh, r0:r0 + rows, :] = lax.dot_general(
            k_ref[0, pl.ds(off, rows), hh * HEAD_PAD:(hh + 1) * HEAD_PAD], qs[hh],
            (((1,), (1,)), ((), ())), preferred_element_type=F32)

    ones = jnp.ones((ONES_ROWS, tk), BF16)

    def values(j, hh, alpha, p, acc):
        off = pl.multiple_of(j * tk, tk)
        vt = jnp.concatenate([vt_ref[hh * B_V:(hh + 1) * B_V, pl.ds(off, tk)], ones], axis=0)
        return alpha * acc + jnp.dot(vt, p, preferred_element_type=F32)

    def stage(j, s_ref, j_next, s_next, stats):
        new = []
        for hh in range(nh):
            m, acc = stats[hh]
            scores(j_next, s_next, hh, 0)
            s = s_ref[hh]
            m_new = jnp.maximum(m, jnp.max(s, axis=0, keepdims=True))
            scores(j_next, s_next, hh, 1)
            p = jnp.exp2(s - m_new).astype(BF16)
            new.append((m_new, values(j, hh, jnp.exp2(m - m_new), p, acc)))
        return tuple(new)

    nbuf = len(s_bufs)

    def body(jj, stats):
        j = nbuf * jj
        for u in range(nbuf):
            stats = stage(j + u, s_bufs[u], jnp.minimum(j + u + 1, nk - 1), s_bufs[(u + 1) % nbuf], stats)
        return stats

    init = (jnp.full((1, tq), NEG, F32), jnp.zeros((B_V + ONES_ROWS, tq), F32))
    for hh in range(nh):
        scores(0, s_bufs[0], hh)
    stats = lax.fori_loop(0, nk // nbuf, body, (init,) * nh)
    o_t = jnp.concatenate([a[:B_V] / a[B_V:B_V + 1] for _, a in stats], axis=0)
    o_ref[0] = o_t.T.astype(BF16)


def _mla_attention(qb, kb, vbt, batch, seq, tq=256, tk=512, nh=MLA_HEADS_PER_STEP):
    once = dict(pipeline_mode=pl.Buffered(1))
    return pl.pallas_call(
        functools.partial(_mla_kernel, tk=tk),
        grid=(batch, B_HEADS // nh, seq // tq),
        in_specs=[pl.BlockSpec((1, tq, nh * HEAD_PAD), lambda b, p, i: (b, i, p)),
                  pl.BlockSpec((1, seq, nh * HEAD_PAD), lambda b, p, i: (b, 0, p), **once),
                  pl.BlockSpec((nh * B_V, seq), lambda b, p, i: (p, b), **once)],
        out_specs=pl.BlockSpec((1, tq, nh * B_V), lambda b, p, i: (b, i, p)),
        out_shape=jax.ShapeDtypeStruct((batch, seq, B_WIDTH), BF16),
        scratch_shapes=[pltpu.VMEM((nh, tk, tq), F32) for _ in range(MLA_SCORE_BUFS)],
        compiler_params=_cparams("parallel", "parallel", "parallel"),
        name="mla_attention",
    )(qb, kb, vbt)


def _outproj_kernel(o1_ref, o2_ref, o3_ref, l1_ref, l2_ref, l3_ref, ob_ref, gate_ref, x_ref,
                    wa_ref, wb_ref, wo_ref, g2_ref, wrh_ref, wrl_ref, br_ref,
                    tri_ref, x1_ref, h2_ref, route_ref, count_ref, perm_ref, perm4_ref, carry_ref):
    tm = x_ref.shape[0]

    @pl.when(pl.program_id(0) == 0)
    def _():
        carry_ref[...] = jnp.zeros_like(carry_ref)

    nc = A_WIDTH // 128

    def lanes(r, c):
        return slice(r * A_WIDTH + c * 128, r * A_WIDTH + (c + 1) * 128)

    def token_order(which, ref, d):
        if d == 4:
            for r in range(4):
                for c in range(nc):
                    perm_ref[which, c, pl.ds(r, tm // 4, stride=4), :] = ref[:, lanes(r, c)].astype(F32)
        else:
            for r1 in range(4):
                for r2 in range(4):
                    for c in range(nc):
                        perm4_ref[which - 2, r1, c, pl.ds(r2, tm // 16, stride=4), :] = (
                            ref[:, lanes(4 * r2 + r1, c)].astype(F32))
            for r1 in range(4):
                for c in range(nc):
                    perm_ref[which, c, pl.ds(r1, tm // 4, stride=4), :] = perm4_ref[which - 2, r1, c]
        return jnp.concatenate([perm_ref[which, c] for c in range(nc)], axis=1)

    l1 = l1_ref[...]
    l2, o2 = token_order(0, l2_ref, 4), token_order(1, o2_ref, 4)
    l3, o3 = token_order(2, l3_ref, 16), token_order(3, o3_ref, 16)
    mx = jnp.maximum(jnp.maximum(l1, l2), l3)
    e1, e2, e3 = jnp.exp(l1 - mx), jnp.exp(l2 - mx), jnp.exp(l3 - mx)
    oa = (e1 * o1_ref[...] + e2 * o2 + e3 * o3) / (e1 + e2 + e3)
    ya = jnp.dot(oa.astype(BF16), wa_ref[...], preferred_element_type=F32)
    yb = jnp.dot(ob_ref[...], wb_ref[...], preferred_element_type=F32)
    merged = gate_ref[:, :D_MODEL].astype(F32) * ya + gate_ref[:, D_MODEL:].astype(F32) * yb
    x1 = x_ref[...] + jnp.dot(merged.astype(BF16), wo_ref[...], preferred_element_type=F32)
    x1_ref[...] = x1
    h2 = x1 * lax.rsqrt(jnp.mean(x1 * x1, axis=-1, keepdims=True) + EPS) * g2_ref[...]
    hi = h2.astype(BF16)
    h2_ref[...] = _pack_rows(h2)
    lo = (h2 - hi.astype(F32)).astype(BF16)
    logits = jnp.dot(hi, wrh_ref[...], preferred_element_type=F32)
    logits = logits + jnp.dot(lo, wrh_ref[...], preferred_element_type=F32)
    logits = logits + jnp.dot(hi, wrl_ref[...], preferred_element_type=F32)
    logits = logits + br_ref[...]

    lane = lax.broadcasted_iota(jnp.int32, (tm, LOGIT_PAD), 1)
    work = logits
    sel = jnp.zeros((tm, LOGIT_PAD), F32)
    idx_cols, val_cols = [], []
    for _ in range(TOP_K):
        mx = jnp.max(work, axis=-1, keepdims=True)
        idx = jnp.min(jnp.where(work == mx, lane, LOGIT_PAD), axis=-1, keepdims=True)
        hit = lane == idx
        sel = sel + hit.astype(F32)
        work = jnp.where(hit, 2.0 * NEG, work)
        idx_cols.append(idx)
        val_cols.append(mx)
    exps = [jnp.exp(v - val_cols[0]) for v in val_cols]
    denom = exps[0] + exps[1] + exps[2] + exps[3]
    gate_cols = [e / denom for e in exps]

    before = jnp.dot(tri_ref[...], sel.astype(BF16), preferred_element_type=F32) + carry_ref[0:1, :]
    rank_cols = [jnp.sum(jnp.where(lane == idx, before, 0.0), axis=-1, keepdims=True) for idx in idx_cols]
    carry_ref[...] = carry_ref[...] + jnp.sum(sel, axis=0, keepdims=True)
    count_ref[...] = carry_ref[...]

    cols = [c.astype(F32) for c in idx_cols] + rank_cols + gate_cols
    route = jnp.zeros((tm, LOGIT_PAD), F32)
    for c, col in enumerate(cols):
        route = jnp.where(lane == c, col, route)
    route_ref[...] = route


def _outproj(o1, o2, o3, l1, l2, l3, ob, gates, x2, wa, wb, wo, g2, wrh, wrl, br, tm=512):
    T = x2.shape[0]
    row = lambda i: (i, 0)
    fix = lambda i: (0, 0)
    full = lambda a: pl.BlockSpec(a.shape, fix)
    rows = lambda a: pl.BlockSpec((tm * a.shape[0] // T, a.shape[1]), row)
    tri = (jnp.arange(tm)[:, None] > jnp.arange(tm)[None, :]).astype(BF16)
    out_shape = (jax.ShapeDtypeStruct((T, D_MODEL), F32),
                 jax.ShapeDtypeStruct((T, HALF_D), jnp.int32),
                 jax.ShapeDtypeStruct((T, LOGIT_PAD), F32),
                 jax.ShapeDtypeStruct((8, LOGIT_PAD), F32))
    return pl.pallas_call(
        _outproj_kernel,
        grid=(T // tm,),
        in_specs=[rows(o1), rows(o2), rows(o3), rows(l1), rows(l2), rows(l3), rows(ob), rows(gates),
                  rows(x2), full(wa), full(wb), full(wo), full(g2), full(wrh), full(wrl), full(br), full(tri)],
        out_specs=[pl.BlockSpec((tm, D_MODEL), row), pl.BlockSpec((tm, HALF_D), row),
                   pl.BlockSpec((tm, LOGIT_PAD), row), pl.BlockSpec((8, LOGIT_PAD), fix)],
        out_shape=out_shape,
        scratch_shapes=[pltpu.VMEM((4, A_WIDTH // 128, tm, 128), F32),
                        pltpu.VMEM((2, 4, A_WIDTH // 128, tm // 4, 128), F32),
                        pltpu.VMEM((8, LOGIT_PAD), F32)],
        compiler_params=_cparams("arbitrary"),
        name="outproj",
    )(o1, o2, o3, l1, l2, l3, ob, gates, x2, wa, wb, wo, g2, wrh, wrl, br, tri)


HALF_D = D_MODEL // 2


def _pack_rows(x):
    lo = pltpu.bitcast(x[:, :HALF_D].astype(BF16).astype(F32), jnp.uint32)
    hi = pltpu.bitcast(x[:, HALF_D:].astype(BF16).astype(F32), jnp.uint32)
    word = jnp.bitwise_or(lax.shift_right_logical(lo, jnp.uint32(16)), jnp.bitwise_and(hi, jnp.uint32(0xFFFF0000)))
    return pltpu.bitcast(word, jnp.int32)


def _unpack_rows(w):
    u = pltpu.bitcast(w, jnp.uint32)
    lo = pltpu.bitcast(lax.shift_left(u, jnp.uint32(16)), F32)
    hi = pltpu.bitcast(jnp.bitwise_and(u, jnp.uint32(0xFFFF0000)), F32)
    return jnp.concatenate([lo, hi], axis=1)


SC_WINDOW = 128


def _gather_rows(table, idx):
    M = idx.shape[0]
    width = table.shape[1]
    mesh = plsc.VectorSubcoreMesh(core_axis_name="core", subcore_axis_name="subcore")

    @functools.partial(pl.kernel, out_type=jax.ShapeDtypeStruct((M, width), table.dtype), mesh=mesh,
                       name="gather_rows")
    def gather(table_hbm, idx_hbm, out_hbm):
        def body(idx_vmem, out_vmem):
            pltpu.sync_copy(table_hbm.at[idx_vmem.at[0]], out_vmem)

        pltpu.emit_pipeline(
            body,
            grid=(M // SC_WINDOW,),
            in_specs=[pl.BlockSpec((1, SC_WINDOW), index_map=lambda i: (0, i))],
            out_specs=[pl.BlockSpec((SC_WINDOW, width), index_map=lambda i: (i, 0),
                                    pipeline_mode=pl.Buffered(1))],
            core_axis_name=("core", "subcore"),
            dimension_semantics=(pltpu.PARALLEL,),
        )(idx_hbm, out_hbm)

    return gather(table, idx.reshape(1, M))


def _scatter_rows(rows, idx, n_out):
    M = idx.shape[0]
    R, width = rows.shape
    mesh = plsc.VectorSubcoreMesh(core_axis_name="core", subcore_axis_name="subcore")

    @functools.partial(pl.kernel, out_type=jax.ShapeDtypeStruct((n_out, width), rows.dtype), mesh=mesh,
                       name="scatter_rows")
    def scatter(rows_hbm, idx_hbm, out_hbm):
        def body(rows_vmem, idx_vmem):
            pltpu.sync_copy(rows_vmem, out_hbm.at[idx_vmem.at[0]])

        pltpu.emit_pipeline(
            body,
            grid=(M // SC_WINDOW,),
            in_specs=[pl.BlockSpec((SC_WINDOW, width), index_map=lambda i: (i % (R // SC_WINDOW), 0),
                                   pipeline_mode=pl.Buffered(1)),
                      pl.BlockSpec((1, SC_WINDOW), index_map=lambda i: (0, i))],
            out_specs=[],
            core_axis_name=("core", "subcore"),
            dimension_semantics=(pltpu.PARALLEL,),
        )(rows_hbm, idx_hbm)

    return scatter(rows, idx.reshape(1, M))


def _moe_kernel(be_ref, nu_ref, nv_ref, x_ref, wgu_ref, bgu_ref, wd_ref, bd_ref, y_ref, wgu_bf, wd_bf):
    i = pl.program_id(0)

    @pl.when(i >= nu_ref[0])
    def _():
        y_ref[...] = jnp.zeros_like(y_ref)

    @pl.when(jnp.logical_or(i == 0, be_ref[i] != be_ref[jnp.maximum(i - 1, 0)]))
    def _():
        rows = 128

        def cast(c, carry):
            r0 = pl.multiple_of(c * rows, rows)
            wgu_bf[pl.ds(r0, rows), :] = wgu_ref[0, pl.ds(r0, rows), :].astype(BF16)
            wd_bf[pl.ds(r0, rows), :] = wd_ref[0, pl.ds(r0, rows), :].astype(BF16)
            return carry

        lax.fori_loop(0, D_MODEL // rows, cast, 0)

    @pl.when(i < nu_ref[0])
    def _():
        live = lax.broadcasted_iota(jnp.int32, (x_ref.shape[0], 1), 0) < nv_ref[i]
        x = jnp.where(live, _unpack_rows(x_ref[...]), 0.0).astype(BF16)
        gu = jnp.dot(x, wgu_bf[...], preferred_element_type=F32) + bgu_ref[0]
        gate = jnp.minimum(gu[:, :D_MODEL], SWIGLU_LIMIT)
        up = jnp.clip(gu[:, D_MODEL:], -SWIGLU_LIMIT, SWIGLU_LIMIT)
        act = (up + 1.0) * (gate * jax.nn.sigmoid(SWIGLU_ALPHA * gate))
        y = jnp.dot(act.astype(BF16), wd_bf[...], preferred_element_type=F32) + bd_ref[0]
        y_ref[...] = _pack_rows(y)


def _moe_experts(xb, blk_e, n_used, n_valid, wgu, bgu, wd, bd, bm):
    NP = xb.shape[0]
    nblk = NP // bm
    grid_spec = pltpu.PrefetchScalarGridSpec(
        num_scalar_prefetch=3,
        grid=(nblk,),
        in_specs=[pl.BlockSpec((bm, HALF_D), lambda i, be, nu, nv: (jnp.minimum(i, nu[0] - 1), 0)),
                  pl.BlockSpec((1, D_MODEL, 2 * D_MODEL), lambda i, be, nu, nv: (be[i], 0, 0)),
                  pl.BlockSpec((1, 1, 2 * D_MODEL), lambda i, be, nu, nv: (be[i], 0, 0)),
                  pl.BlockSpec((1, D_MODEL, D_MODEL), lambda i, be, nu, nv: (be[i], 0, 0)),
                  pl.BlockSpec((1, 1, D_MODEL), lambda i, be, nu, nv: (be[i], 0, 0))],
        out_specs=pl.BlockSpec((bm, HALF_D), lambda i, be, nu, nv: (i, 0)),
        scratch_shapes=[pltpu.VMEM((D_MODEL, 2 * D_MODEL), BF16), pltpu.VMEM((D_MODEL, D_MODEL), BF16)],
    )
    return pl.pallas_call(
        _moe_kernel,
        grid_spec=grid_spec,
        out_shape=jax.ShapeDtypeStruct((NP, HALF_D), jnp.int32),
        compiler_params=_cparams("arbitrary"),
        name="moe_experts",
    )(blk_e, n_used, n_valid, xb, wgu, bgu, wd, bd)


def _final_kernel(x1_ref, y_ref, route_ref, g_ref, o_ref):
    w = route_ref[...]
    acc = x1_ref[...]
    for k in range(TOP_K):
        acc = acc + w[:, 2 * TOP_K + k:2 * TOP_K + k + 1] * _unpack_rows(y_ref[k])
    o_ref[...] = acc * lax.rsqrt(jnp.mean(acc * acc, axis=-1, keepdims=True) + EPS) * g_ref[...]


def _final(x1, yk, route, g, tm=512):
    T = x1.shape[0]
    return pl.pallas_call(
        _final_kernel,
        grid=(T // tm,),
        in_specs=[pl.BlockSpec((tm, D_MODEL), lambda i: (i, 0)),
                  pl.BlockSpec((TOP_K, tm, HALF_D), lambda i: (0, i, 0)),
                  pl.BlockSpec((tm, LOGIT_PAD), lambda i: (i, 0)),
                  pl.BlockSpec((1, D_MODEL), lambda i: (0, 0))],
        out_specs=pl.BlockSpec((tm, D_MODEL), lambda i: (i, 0)),
        out_shape=jax.ShapeDtypeStruct((T, D_MODEL), F32),
        compiler_params=_cparams("parallel"),
        name="final_norm",
    )(x1, yk, route, g)


def _dest_kernel(route_ref, pstart_ref, dest_ref):
    tm = route_ref.shape[0]
    route = route_ref[...]
    lane = lax.broadcasted_iota(jnp.int32, (tm, LOGIT_PAD), 1)
    expert = jnp.where(lane < TOP_K, route, 0.0).astype(jnp.int32)
    start = jnp.take_along_axis(jnp.broadcast_to(pstart_ref[...], (tm, LOGIT_PAD)), expert, axis=1)
    position = pltpu.roll(route, LOGIT_PAD - TOP_K, axis=1)
    rows = start + position
    dest_ref[...] = rows.T[:TOP_K].astype(jnp.int32)


def _dest_rows(route, pstart, tm=2048):
    T = route.shape[0]
    return pl.pallas_call(
        _dest_kernel,
        grid=(T // tm,),
        in_specs=[pl.BlockSpec((tm, LOGIT_PAD), lambda i: (i, 0)),
                  pl.BlockSpec((1, LOGIT_PAD), lambda i: (0, 0))],
        out_specs=pl.BlockSpec((TOP_K, tm), lambda i: (0, i)),
        out_shape=jax.ShapeDtypeStruct((TOP_K, T), jnp.int32),
        compiler_params=_cparams("parallel"),
        name="dest_rows",
    )(route, pstart)


def _pack_w_in(w):
    kr = w[:, 1920:1952]
    half = B_ROPE // 2
    z = lambda n: jnp.zeros((w.shape[0], n), w.dtype)
    kr_placed = jnp.concatenate([z(B_NOPE), kr, z(HEAD_PAD - B_NOPE - B_ROPE)], axis=1)
    kr_rot = jnp.concatenate([z(B_NOPE), -kr[:, half:], kr[:, :half], z(HEAD_PAD - B_NOPE - B_ROPE)], axis=1)
    return jnp.concatenate([w[:, :1920], w[:, 1952:4000], kr_placed, kr_rot], axis=1).astype(BF16)


def _pack_w_uq(w):
    half = B_ROPE // 2
    r = w.shape[0]
    nope, x1, x2 = w[..., :B_NOPE], w[..., B_NOPE:B_NOPE + half], w[..., B_NOPE + half:]
    zpad = jnp.zeros((r, B_HEADS, HEAD_PAD - B_NOPE - B_ROPE), w.dtype)
    main = jnp.concatenate([nope, x1, x2, zpad], axis=-1)
    rot = jnp.concatenate([jnp.zeros_like(nope), -x2, x1, zpad], axis=-1)
    return (main.reshape(r, B_HEADS * HEAD_PAD).astype(BF16), rot.reshape(r, B_HEADS * HEAD_PAD).astype(BF16))


def _pack_w_uk(w):
    r = w.shape[0]
    zpad = jnp.zeros((r, B_HEADS, HEAD_PAD - B_NOPE), w.dtype)
    return jnp.concatenate([w, zpad], axis=-1).reshape(r, B_HEADS * HEAD_PAD).astype(BF16)


def _rope_tables(seq):
    inv_freq = ROPE_THETA ** (-np.arange(0, B_ROPE, 2, dtype=np.float64) / B_ROPE)
    ang = np.arange(seq, dtype=np.float64)[:, None] * inv_freq[None, :]
    cos, sin = jnp.asarray(np.cos(ang), F32), jnp.asarray(np.sin(ang), F32)
    one = jnp.ones((seq, B_NOPE), F32)
    zero = jnp.zeros((seq, B_NOPE), F32)
    zpad = jnp.zeros((seq, HEAD_PAD - B_NOPE - B_ROPE), F32)
    qs = ((B_NOPE + B_ROPE) ** -0.5) * LOG2E
    cq = jnp.concatenate([one, cos, cos, zpad], axis=1) * qs
    sq = jnp.concatenate([zero, sin, sin, zpad], axis=1) * qs
    ck = jnp.concatenate([zero, cos, cos, zpad], axis=1)
    sk = jnp.concatenate([zero, sin, sin, zpad], axis=1)
    return cq, sq, ck, sk


def kernel(x, norm1_g, w_in, q_a_norm_g, kv_a_norm_g, w_uq, w_uk, w_uv, w_a_out, w_b_out, w_o,
           norm2_g, w_router, b_router, w_gate_up, b_gate_up, w_down, b_down, final_g):
    B, S, D = x.shape
    T = B * S
    depth = norm1_g.shape[0]
    xt = x.reshape(T, D)
    cq_t, sq_t, ck_t, sk_t = _rope_tables(S)
    biases = [_dilated_bias(d) for _, d in A_PATTERNS]
    bm = 512
    for l in range(depth):
        wq1, wq2 = _pack_w_uq(w_uq[l])
        qa, ka, va, qa4, ka4, va4, qa16, ka16, va16, qb, kb, vb, gates = _inproj(
            xt, norm1_g[l][None], _pack_w_in(w_in[l]), q_a_norm_g[l][None], kv_a_norm_g[l][None],
            wq1, wq2, _pack_w_uk(w_uk[l]), w_uv[l].reshape(KV_LORA, B_WIDTH).T.astype(BF16),
            cq_t, sq_t, ck_t, sk_t, S)
        qkv = ((qa, ka, va), (qa4, ka4, va4), (qa16, ka16, va16))
        outs = [_dilated_pattern(*qkv[p], biases[p], B, S, A_PATTERNS[p][1]) for p in range(3)]
        ob = _mla_attention(qb.reshape(B, S, -1), kb.reshape(B, S, -1), vb, B, S)

        wr = jnp.pad(w_router[l], ((0, 0), (0, LOGIT_PAD - N_EXPERTS)))
        wrh = wr.astype(BF16)
        wrl = (wr - wrh.astype(F32)).astype(BF16)
        br = jnp.pad(b_router[l], (0, LOGIT_PAD - N_EXPERTS), constant_values=NEG)[None]
        x1, h2, route, count_rows = _outproj(
            outs[0][0], outs[1][0], outs[2][0], outs[0][1], outs[1][1], outs[2][1],
            ob.reshape(T, B_WIDTH), gates, xt,
            w_a_out[l].astype(BF16), w_b_out[l].astype(BF16), w_o[l].astype(BF16),
            norm2_g[l][None], wrh, wrl, br)

        counts = count_rows[0, :N_EXPERTS].astype(jnp.int32)
        padded = (counts + bm - 1) // bm * bm
        pend = jnp.cumsum(padded)
        pstart = pend - padded
        NP = T * TOP_K + N_EXPERTS * bm
        nblk = NP // bm
        blk_start = jnp.arange(nblk, dtype=jnp.int32) * bm
        blk_e = jnp.minimum(jnp.sum(pend[None, :] <= blk_start[:, None], axis=1), N_EXPERTS - 1).astype(jnp.int32)
        n_used = (pend[-1] // bm).astype(jnp.int32)[None]
        n_valid = jnp.clip(pstart[blk_e] + counts[blk_e] - blk_start, 0, bm).astype(jnp.int32)

        pstart_row = jnp.pad(pstart.astype(F32), (0, LOGIT_PAD - N_EXPERTS))[None]
        dest_kt = _dest_rows(route, pstart_row).reshape(-1)
        xb = _scatter_rows(h2, dest_kt, NP)
        yb = _moe_experts(xb, blk_e, n_used, n_valid, w_gate_up[l], b_gate_up[l][:, None, :],
                          w_down[l], b_down[l][:, None, :], bm)
        yk = _gather_rows(yb, dest_kt).reshape(TOP_K, T, HALF_D)
        if l + 1 < depth:
            raise NotImplementedError("depth > 1")
        out = _final(x1, yk, route, final_g[None])
    return out.reshape(B, S, D)
```

```python
import functools

import jax
import jax.numpy as jnp
import numpy as np
from jax import lax
from jax.experimental import pallas as pl
from jax.experimental.pallas import tpu as pltpu
from jax.experimental.pallas import tpu_sc as plsc

F32 = jnp.float32
BF16 = jnp.bfloat16

D_MODEL = 1024
EPS = 1e-5
NEG = -1e30

A_HEADS = 8
A_HEAD_DIM = 64
A_WIDTH = 512
A_PATTERNS = ((128, 1), (512, 4), (2048, 16))
A_HALF = 64
A_QB = 128
A_WIN = A_QB + 2 * A_HALF
assert all(w // (2 * d) == A_HALF for w, d in A_PATTERNS) and [d for _, d in A_PATTERNS] == [1, 4, 16]

B_HEADS = 8
B_NOPE = 64
B_ROPE = 32
B_V = 64
B_WIDTH = 512
Q_LORA = 256
KV_LORA = 128
ROPE_THETA = 10000.0
HEAD_PAD = 128
ONES_ROWS = 16
MLA_SCORE_BUFS = 8
MLA_HEADS_PER_STEP = 4

N_EXPERTS = 32
TOP_K = 4
SWIGLU_LIMIT = 7.0
SWIGLU_ALPHA = 1.702
LOGIT_PAD = 128

LOG2E = 1.4426950408889634

VMEM_LIMIT = 56 * 1024 * 1024


def _cparams(*sem):
    return pltpu.CompilerParams(dimension_semantics=sem, vmem_limit_bytes=VMEM_LIMIT)


def _inproj_kernel(x_ref, g1_ref, wp_ref, gq_ref, gkv_ref, wq1_ref, wq2_ref, wuk_ref, wuv_ref,
                   cq_ref, sq_ref, ck_ref, sk_ref,
                   qa_ref, ka_ref, va_ref, qa4_ref, ka4_ref, va4_ref, qa16_ref, ka16_ref, va16_ref,
                   qb_ref, kb_ref, vb_ref, gate_ref, perm_ref, perm4_ref):
    tm = x_ref.shape[0]
    x = x_ref[...]
    h = x * lax.rsqrt(jnp.mean(x * x, axis=-1, keepdims=True) + EPS) * g1_ref[...]
    h = h.astype(BF16)

    def proj(c0, c1):
        return jnp.dot(h, wp_ref[:, c0:c1], preferred_element_type=F32)

    def emit(which, z, nat_ref, v4_ref, v16_ref):
        nat_ref[...] = z.astype(BF16)
        nc = A_WIDTH // 128
        for c in range(nc):
            perm_ref[which, c] = z[:, c * 128:(c + 1) * 128]
        for r1 in range(4):
            for c in range(nc):
                quarter = perm_ref[which, c, pl.ds(r1, tm // 4, stride=4), :]
                v4_ref[:, r1 * A_WIDTH + c * 128:r1 * A_WIDTH + (c + 1) * 128] = quarter.astype(BF16)
                perm4_ref[which, r1, c] = quarter
        for r1 in range(4):
            for r2 in range(4):
                r = 4 * r2 + r1
                for c in range(nc):
                    v16_ref[:, r * A_WIDTH + c * 128:r * A_WIDTH + (c + 1) * 128] = (
                        perm4_ref[which, r1, c, pl.ds(r2, tm // 16, stride=4), :].astype(BF16))

    emit(0, proj(0, 512) * (A_HEAD_DIM ** -0.5), qa_ref, qa4_ref, qa16_ref)
    emit(1, proj(512, 1024), ka_ref, ka4_ref, ka16_ref)
    emit(2, proj(1024, 1536), va_ref, va4_ref, va16_ref)

    cq = proj(1536, 1792)
    cq = cq * lax.rsqrt(jnp.mean(cq * cq, axis=-1, keepdims=True) + EPS) * gq_ref[...]
    cq = cq.astype(BF16)
    q_main = jnp.dot(cq, wq1_ref[...], preferred_element_type=F32)
    q_rot = jnp.dot(cq, wq2_ref[...], preferred_element_type=F32)
    cq_t = cq_ref[...]
    sq_t = sq_ref[...]
    for hd in range(B_HEADS):
        sl = slice(hd * HEAD_PAD, (hd + 1) * HEAD_PAD)
        qb_ref[:, sl] = (q_main[:, sl] * cq_t + q_rot[:, sl] * sq_t).astype(BF16)

    ckv = proj(1792, 1920)
    ckv = ckv * lax.rsqrt(jnp.mean(ckv * ckv, axis=-1, keepdims=True) + EPS) * gkv_ref[...]
    ckv = ckv.astype(BF16)
    k_nope = jnp.dot(ckv, wuk_ref[...], preferred_element_type=F32)
    vb_ref[...] = lax.dot_general(wuv_ref[...], ckv, (((1,), (1,)), ((), ())),
                                  preferred_element_type=F32).astype(BF16)
    k_rope = proj(3968, 4096) * ck_ref[...] + proj(4096, 4224) * sk_ref[...]
    for hd in range(B_HEADS):
        sl = slice(hd * HEAD_PAD, (hd + 1) * HEAD_PAD)
        kb_ref[:, sl] = (k_nope[:, sl] + k_rope).astype(BF16)

    gate_ref[...] = jax.nn.sigmoid(proj(1920, 3968)).astype(BF16)


def _inproj(x2, g1, wp, gq, gkv, wq1, wq2, wuk, wuv, cq_t, sq_t, ck_t, sk_t, seq, tm=512):
    T = x2.shape[0]
    nseq = seq // tm
    row = lambda i: (i, 0)
    fix = lambda i: (0, 0)
    tab = lambda i: (i % nseq, 0)
    full = lambda a: pl.BlockSpec(a.shape, fix)
    sds = lambda r, c: jax.ShapeDtypeStruct((r, c), BF16)
    out_shape = (
        sds(T, A_WIDTH), sds(T, A_WIDTH), sds(T, A_WIDTH),
        sds(T // 4, 4 * A_WIDTH), sds(T // 4, 4 * A_WIDTH), sds(T // 4, 4 * A_WIDTH),
        sds(T // 16, 16 * A_WIDTH), sds(T // 16, 16 * A_WIDTH), sds(T // 16, 16 * A_WIDTH),
        sds(T, B_HEADS * HEAD_PAD), sds(T, B_HEADS * HEAD_PAD),
        sds(B_WIDTH, T),
        sds(T, 2 * D_MODEL),
    )
    out_specs = [pl.BlockSpec((tm * s.shape[0] // T, s.shape[1]), row) for s in out_shape]
    out_specs[11] = pl.BlockSpec((B_WIDTH, tm), lambda i: (0, i))
    return pl.pallas_call(
        _inproj_kernel,
        grid=(T // tm,),
        in_specs=[pl.BlockSpec((tm, D_MODEL), row), full(g1), full(wp), full(gq), full(gkv),
                  full(wq1), full(wq2), full(wuk), full(wuv),
                  pl.BlockSpec((tm, HEAD_PAD), tab), pl.BlockSpec((tm, HEAD_PAD), tab),
                  pl.BlockSpec((tm, HEAD_PAD), tab), pl.BlockSpec((tm, HEAD_PAD), tab)],
        out_specs=out_specs,
        out_shape=out_shape,
        scratch_shapes=[pltpu.VMEM((3, A_WIDTH // 128, tm, 128), F32),
                        pltpu.VMEM((3, 4, A_WIDTH // 128, tm // 4, 128), F32)],
        compiler_params=_cparams("parallel"),
        name="inproj",
    )(x2, g1, wp, gq, gkv, wq1, wq2, wuk, wuv, cq_t, sq_t, ck_t, sk_t)


def _dilated_kernel(q_ref, kc_ref, kp_ref, kn_ref, vc_ref, vp_ref, vn_ref, bias_ref,
                    o_ref, lse_ref, kbuf, vbuf, *, rows):
    i = pl.program_id(2)
    last = pl.num_programs(2) - 1
    kbuf[0:A_HALF, :] = kp_ref[0]
    kbuf[A_HALF:A_HALF + rows, :] = kc_ref[0]
    kbuf[A_HALF + rows:, :] = kn_ref[0]
    vbuf[0:A_HALF, :] = vp_ref[0]
    vbuf[A_HALF:A_HALF + rows, :] = vc_ref[0]
    vbuf[A_HALF + rows:, :] = vn_ref[0]

    nj = rows // A_QB
    npair = A_HEADS // 2
    lane = lax.broadcasted_iota(jnp.int32, (A_QB, 128), 1)
    lo_half = lane < A_HEAD_DIM

    def body(j, carry):
        r0 = pl.multiple_of(j * A_QB, A_QB)
        variant = (jnp.logical_and(i == 0, j == 0).astype(jnp.int32)
                   + 2 * jnp.logical_and(i == last, j == nj - 1).astype(jnp.int32))
        scores = []
        for pr in range(npair):
            ls = slice(pr * 128, (pr + 1) * 128)
            qp = q_ref[0, pl.ds(r0, A_QB), ls]
            zero = jnp.zeros_like(qp)
            q2 = jnp.concatenate([jnp.where(lo_half, qp, zero), jnp.where(lo_half, zero, qp)], axis=0)
            s = lax.dot_general(q2, kbuf[pl.ds(r0, A_WIN), ls], (((1,), (1,)), ((), ())),
                                preferred_element_type=F32)
            scores.append(s + bias_ref[variant * npair + pr])
        probs = []
        for s in scores:
            m = jnp.max(s, axis=-1, keepdims=True)
            p = jnp.exp(s - m)
            l = jnp.sum(p, axis=-1, keepdims=True)
            probs.append((p.astype(BF16), l, m + jnp.log(l)))
        for pr, (p, l, lse) in enumerate(probs):
            ls = slice(pr * 128, (pr + 1) * 128)
            o = jnp.dot(p, vbuf[pl.ds(r0, A_WIN), ls], preferred_element_type=F32) / l
            lse = jnp.broadcast_to(lse, (2 * A_QB, 128))
            o_ref[0, pl.ds(r0, A_QB), ls] = jnp.where(lo_half, o[:A_QB], o[A_QB:]).astype(o_ref.dtype)
            lse_ref[0, pl.ds(r0, A_QB), ls] = jnp.where(lo_half, lse[:A_QB], lse[A_QB:])
        return carry

    lax.fori_loop(0, nj, body, 0)


def _dilated_pattern(qa, ka, va, bias, batch, seq, dil, rows=1024):
    L = seq // dil
    rows = min(rows, L)
    nblk = L // rows
    hb = rows // A_HALF
    nh = L // A_HALF
    view = lambda a: a.reshape(batch, L, dil * A_WIDTH)
    cur = lambda b, r, i: (b, i, r)
    prev = lambda b, r, i: (b, jnp.maximum(i * hb - 1, 0), r)
    nxt = lambda b, r, i: (b, jnp.minimum((i + 1) * hb, nh - 1), r)
    cur_spec = pl.BlockSpec((1, rows, A_WIDTH), cur)
    out_sds = lambda dt: jax.ShapeDtypeStruct((batch, L, dil * A_WIDTH), dt)
    o, lse = pl.pallas_call(
        functools.partial(_dilated_kernel, rows=rows),
        grid=(batch, dil, nblk),
        in_specs=[cur_spec, cur_spec,
                  pl.BlockSpec((1, A_HALF, A_WIDTH), prev), pl.BlockSpec((1, A_HALF, A_WIDTH), nxt),
                  cur_spec,
                  pl.BlockSpec((1, A_HALF, A_WIDTH), prev), pl.BlockSpec((1, A_HALF, A_WIDTH), nxt),
                  pl.BlockSpec(bias.shape, lambda b, r, i: (0, 0, 0))],
        out_specs=[cur_spec, cur_spec],
        out_shape=(out_sds(BF16), out_sds(F32)),
        scratch_shapes=[pltpu.VMEM((rows + 2 * A_HALF, A_WIDTH), BF16),
                        pltpu.VMEM((rows + 2 * A_HALF, A_WIDTH), BF16)],
        compiler_params=_cparams("parallel", "parallel", "parallel"),
        name=f"dilated_d{dil}",
    )(view(qa), view(ka), view(ka), view(ka), view(va), view(va), view(va), bias)
    return o.reshape(batch * L, dil * A_WIDTH), lse.reshape(batch * L, dil * A_WIDTH)


def _dilated_bias(dil):
    slopes = 2.0 ** (-8.0 * (np.arange(A_HEADS, dtype=np.float64) + 1.0) / A_HEADS)
    col = np.arange(A_WIN)[None, :]
    rel = (col - A_HALF) - np.arange(A_QB)[:, None]
    bias = -slopes[:, None, None] * (np.abs(rel) * dil).astype(np.float64)[None]
    bias = np.where((np.abs(rel) <= A_HALF)[None], bias, NEG)
    variants = []
    for v in range(4):
        ok = np.ones((1, A_WIN), bool)
        if v & 1:
            ok = np.logical_and(ok, col >= A_HALF)
        if v & 2:
            ok = np.logical_and(ok, col < A_HALF + A_QB)
        variants.append(np.where(ok[None], bias, NEG).reshape(A_HEADS // 2, 2 * A_QB, A_WIN))
    return jnp.asarray(np.concatenate(variants, axis=0), F32)


def _mla_kernel(q_ref, k_ref, vt_ref, o_ref, *s_bufs, tk):
    tq = q_ref.shape[1]
    nk = k_ref.shape[1] // tk
    nh = q_ref.shape[2] // HEAD_PAD
    qs = [q_ref[0, :, hh * HEAD_PAD:(hh + 1) * HEAD_PAD] for hh in range(nh)]

    def scores(j, s_ref, hh, part=None):
        rows = tk if part is None else tk // 2
        r0 = 0 if part is None else part * rows
        off = pl.multiple_of(j * tk, tk) + r0
        s_ref[hh, r0:r0 + rows, :] = lax.dot_general(
            k_ref[0, pl.ds(off, rows), hh * HEAD_PAD:(hh + 1) * HEAD_PAD], qs[hh],
            (((1,), (1,)), ((), ())), preferred_element_type=F32)

    ones = jnp.ones((ONES_ROWS, tk), BF16)

    def values(j, hh, alpha, p, acc):
        off = pl.multiple_of(j * tk, tk)
        vt = jnp.concatenate([vt_ref[hh * B_V:(hh + 1) * B_V, pl.ds(off, tk)], ones], axis=0)
        return alpha * acc + jnp.dot(vt, p, preferred_element_type=F32)

    def stage(j, s_ref, j_next, s_next, stats):
        new = []
        for hh in range(nh):
            m, acc = stats[hh]
            scores(j_next, s_next, hh, 0)
            s = s_ref[hh]
            m_new = jnp.maximum(m, jnp.max(s, axis=0, keepdims=True))
            scores(j_next, s_next, hh, 1)
            p = jnp.exp2(s - m_new).astype(BF16)
            new.append((m_new, values(j, hh, jnp.exp2(m - m_new), p, acc)))
        return tuple(new)

    nbuf = len(s_bufs)

    def body(jj, stats):
        j = nbuf * jj
        for u in range(nbuf):
            stats = stage(j + u, s_bufs[u], jnp.minimum(j + u + 1, nk - 1), s_bufs[(u + 1) % nbuf], stats)
        return stats

    init = (jnp.full((1, tq), NEG, F32), jnp.zeros((B_V + ONES_ROWS, tq), F32))
    for hh in range(nh):
        scores(0, s_bufs[0], hh)
    stats = lax.fori_loop(0, nk // nbuf, body, (init,) * nh)
    o_t = jnp.concatenate([a[:B_V] / a[B_V:B_V + 1] for _, a in stats], axis=0)
    o_ref[0] = o_t.T.astype(BF16)


def _mla_attention(qb, kb, vbt, batch, seq, tq=256, tk=512, nh=MLA_HEADS_PER_STEP):
    once = dict(pipeline_mode=pl.Buffered(1))
    return pl.pallas_call(
        functools.partial(_mla_kernel, tk=tk),
        grid=(batch, B_HEADS // nh, seq // tq),
        in_specs=[pl.BlockSpec((1, tq, nh * HEAD_PAD), lambda b, p, i: (b, i, p)),
                  pl.BlockSpec((1, seq, nh * HEAD_PAD), lambda b, p, i: (b, 0, p), **once),
                  pl.BlockSpec((nh * B_V, seq), lambda b, p, i: (p, b), **once)],
        out_specs=pl.BlockSpec((1, tq, nh * B_V), lambda b, p, i: (b, i, p)),
        out_shape=jax.ShapeDtypeStruct((batch, seq, B_WIDTH), BF16),
        scratch_shapes=[pltpu.VMEM((nh, tk, tq), F32) for _ in range(MLA_SCORE_BUFS)],
        compiler_params=_cparams("parallel", "parallel", "parallel"),
        name="mla_attention",
    )(qb, kb, vbt)


def _outproj_kernel(o1_ref, o2_ref, o3_ref, l1_ref, l2_ref, l3_ref, ob_ref, gate_ref, x_ref,
                    wa_ref, wb_ref, wo_ref, g2_ref, wrh_ref, wrl_ref, br_ref,
                    tri_ref, x1_ref, h2_ref, route_ref, count_ref, perm_ref, perm4_ref, carry_ref):
    tm = x_ref.shape[0]

    @pl.when(pl.program_id(0) == 0)
    def _():
        carry_ref[...] = jnp.zeros_like(carry_ref)

    nc = A_WIDTH // 128

    def lanes(r, c):
        return slice(r * A_WIDTH + c * 128, r * A_WIDTH + (c + 1) * 128)

    def token_order(which, ref, d):
        if d == 4:
            for r in range(4):
                for c in range(nc):
                    perm_ref[which, c, pl.ds(r, tm // 4, stride=4), :] = ref[:, lanes(r, c)].astype(F32)
        else:
            for r1 in range(4):
                for r2 in range(4):
                    for c in range(nc):
                        perm4_ref[which - 2, r1, c, pl.ds(r2, tm // 16, stride=4), :] = (
                            ref[:, lanes(4 * r2 + r1, c)].astype(F32))
            for r1 in range(4):
                for c in range(nc):
                    perm_ref[which, c, pl.ds(r1, tm // 4, stride=4), :] = perm4_ref[which - 2, r1, c]
        return jnp.concatenate([perm_ref[which, c] for c in range(nc)], axis=1)

    l1 = l1_ref[...]
    l2, o2 = token_order(0, l2_ref, 4), token_order(1, o2_ref, 4)
    l3, o3 = token_order(2, l3_ref, 16), token_order(3, o3_ref, 16)
    mx = jnp.maximum(jnp.maximum(l1, l2), l3)
    e1, e2, e3 = jnp.exp(l1 - mx), jnp.exp(l2 - mx), jnp.exp(l3 - mx)
    oa = (e1 * o1_ref[...] + e2 * o2 + e3 * o3) / (e1 + e2 + e3)
    ya = jnp.dot(oa.astype(BF16), wa_ref[...], preferred_element_type=F32)
    yb = jnp.dot(ob_ref[...], wb_ref[...], preferred_element_type=F32)
    merged = gate_ref[:, :D_MODEL].astype(F32) * ya + gate_ref[:, D_MODEL:].astype(F32) * yb
    x1 = x_ref[...] + jnp.dot(merged.astype(BF16), wo_ref[...], preferred_element_type=F32)
    x1_ref[...] = x1
    h2 = x1 * lax.rsqrt(jnp.mean(x1 * x1, axis=-1, keepdims=True) + EPS) * g2_ref[...]
    hi = h2.astype(BF16)
    h2_ref[...] = _pack_rows(h2)
    lo = (h2 - hi.astype(F32)).astype(BF16)
    logits = jnp.dot(hi, wrh_ref[...], preferred_element_type=F32)
    logits = logits + jnp.dot(lo, wrh_ref[...], preferred_element_type=F32)
    logits = logits + jnp.dot(hi, wrl_ref[...], preferred_element_type=F32)
    logits = logits + br_ref[...]

    lane = lax.broadcasted_iota(jnp.int32, (tm, LOGIT_PAD), 1)
    work = logits
    sel = jnp.zeros((tm, LOGIT_PAD), F32)
    idx_cols, val_cols = [], []
    for _ in range(TOP_K):
        mx = jnp.max(work, axis=-1, keepdims=True)
        idx = jnp.min(jnp.where(work == mx, lane, LOGIT_PAD), axis=-1, keepdims=True)
        hit = lane == idx
        sel = sel + hit.astype(F32)
        work = jnp.where(hit, 2.0 * NEG, work)
        idx_cols.append(idx)
        val_cols.append(mx)
    exps = [jnp.exp(v - val_cols[0]) for v in val_cols]
    denom = exps[0] + exps[1] + exps[2] + exps[3]
    gate_cols = [e / denom for e in exps]

    before = jnp.dot(tri_ref[...], sel.astype(BF16), preferred_element_type=F32) + carry_ref[0:1, :]
    rank_cols = [jnp.sum(jnp.where(lane == idx, before, 0.0), axis=-1, keepdims=True) for idx in idx_cols]
    carry_ref[...] = carry_ref[...] + jnp.sum(sel, axis=0, keepdims=True)
    count_ref[...] = carry_ref[...]

    cols = [c.astype(F32) for c in idx_cols] + rank_cols + gate_cols
    route = jnp.zeros((tm, LOGIT_PAD), F32)
    for c, col in enumerate(cols):
        route = jnp.where(lane == c, col, route)
    route_ref[...] = route


def _outproj(o1, o2, o3, l1, l2, l3, ob, gates, x2, wa, wb, wo, g2, wrh, wrl, br, tm=512):
    T = x2.shape[0]
    row = lambda i: (i, 0)
    fix = lambda i: (0, 0)
    full = lambda a: pl.BlockSpec(a.shape, fix)
    rows = lambda a: pl.BlockSpec((tm * a.shape[0] // T, a.shape[1]), row)
    tri = (jnp.arange(tm)[:, None] > jnp.arange(tm)[None, :]).astype(BF16)
    out_shape = (jax.ShapeDtypeStruct((T, D_MODEL), F32),
                 jax.ShapeDtypeStruct((T, HALF_D), jnp.int32),
                 jax.ShapeDtypeStruct((T, LOGIT_PAD), F32),
                 jax.ShapeDtypeStruct((8, LOGIT_PAD), F32))
    return pl.pallas_call(
        _outproj_kernel,
        grid=(T // tm,),
        in_specs=[rows(o1), rows(o2), rows(o3), rows(l1), rows(l2), rows(l3), rows(ob), rows(gates),
                  rows(x2), full(wa), full(wb), full(wo), full(g2), full(wrh), full(wrl), full(br), full(tri)],
        out_specs=[pl.BlockSpec((tm, D_MODEL), row), pl.BlockSpec((tm, HALF_D), row),
                   pl.BlockSpec((tm, LOGIT_PAD), row), pl.BlockSpec((8, LOGIT_PAD), fix)],
        out_shape=out_shape,
        scratch_shapes=[pltpu.VMEM((4, A_WIDTH // 128, tm, 128), F32),
                        pltpu.VMEM((2, 4, A_WIDTH // 128, tm // 4, 128), F32),
                        pltpu.VMEM((8, LOGIT_PAD), F32)],
        compiler_params=_cparams("arbitrary"),
        name="outproj",
    )(o1, o2, o3, l1, l2, l3, ob, gates, x2, wa, wb, wo, g2, wrh, wrl, br, tri)


HALF_D = D_MODEL // 2


def _pack_rows(x):
    lo = pltpu.bitcast(x[:, :HALF_D].astype(BF16).astype(F32), jnp.uint32)
    hi = pltpu.bitcast(x[:, HALF_D:].astype(BF16).astype(F32), jnp.uint32)
    word = jnp.bitwise_or(lax.shift_right_logical(lo, jnp.uint32(16)), jnp.bitwise_and(hi, jnp.uint32(0xFFFF0000)))
    return pltpu.bitcast(word, jnp.int32)


def _unpack_rows(w):
    u = pltpu.bitcast(w, jnp.uint32)
    lo = pltpu.bitcast(lax.shift_left(u, jnp.uint32(16)), F32)
    hi = pltpu.bitcast(jnp.bitwise_and(u, jnp.uint32(0xFFFF0000)), F32)
    return jnp.concatenate([lo, hi], axis=1)


SC_WINDOW = 128


def _gather_rows(table, idx):
    M = idx.shape[0]
    width = table.shape[1]
    mesh = plsc.VectorSubcoreMesh(core_axis_name="core", subcore_axis_name="subcore")

    @functools.partial(pl.kernel, out_type=jax.ShapeDtypeStruct((M, width), table.dtype), mesh=mesh,
                       name="gather_rows")
    def gather(table_hbm, idx_hbm, out_hbm):
        def body(idx_vmem, out_vmem):
            pltpu.sync_copy(table_hbm.at[idx_vmem.at[0]], out_vmem)

        pltpu.emit_pipeline(
            body,
            grid=(M // SC_WINDOW,),
            in_specs=[pl.BlockSpec((1, SC_WINDOW), index_map=lambda i: (0, i))],
            out_specs=[pl.BlockSpec((SC_WINDOW, width), index_map=lambda i: (i, 0),
                                    pipeline_mode=pl.Buffered(1))],
            core_axis_name=("core", "subcore"),
            dimension_semantics=(pltpu.PARALLEL,),
        )(idx_hbm, out_hbm)

    return gather(table, idx.reshape(1, M))


def _scatter_rows(rows, idx, n_out):
    M = idx.shape[0]
    R, width = rows.shape
    mesh = plsc.VectorSubcoreMesh(core_axis_name="core", subcore_axis_name="subcore")

    @functools.partial(pl.kernel, out_type=jax.ShapeDtypeStruct((n_out, width), rows.dtype), mesh=mesh,
                       name="scatter_rows")
    def scatter(rows_hbm, idx_hbm, out_hbm):
        def body(rows_vmem, idx_vmem):
            pltpu.sync_copy(rows_vmem, out_hbm.at[idx_vmem.at[0]])

        pltpu.emit_pipeline(
            body,
            grid=(M // SC_WINDOW,),
            in_specs=[pl.BlockSpec((SC_WINDOW, width), index_map=lambda i: (i % (R // SC_WINDOW), 0),
                                   pipeline_mode=pl.Buffered(1)),
                      pl.BlockSpec((1, SC_WINDOW), index_map=lambda i: (0, i))],
            out_specs=[],
            core_axis_name=("core", "subcore"),
            dimension_semantics=(pltpu.PARALLEL,),
        )(rows_hbm, idx_hbm)

    return scatter(rows, idx.reshape(1, M))


def _moe_kernel(be_ref, nu_ref, nv_ref, x_ref, wgu_ref, bgu_ref, wd_ref, bd_ref, y_ref, wgu_bf, wd_bf):
    i = pl.program_id(0)

    @pl.when(i >= nu_ref[0])
    def _():
        y_ref[...] = jnp.zeros_like(y_ref)

    @pl.when(jnp.logical_or(i == 0, be_ref[i] != be_ref[jnp.maximum(i - 1, 0)]))
    def _():
        rows = 128

        def cast(c, carry):
            r0 = pl.multiple_of(c * rows, rows)
            wgu_bf[pl.ds(r0, rows), :] = wgu_ref[0, pl.ds(r0, rows), :].astype(BF16)
            wd_bf[pl.ds(r0, rows), :] = wd_ref[0, pl.ds(r0, rows), :].astype(BF16)
            return carry

        lax.fori_loop(0, D_MODEL // rows, cast, 0)

    @pl.when(i < nu_ref[0])
    def _():
        live = lax.broadcasted_iota(jnp.int32, (x_ref.shape[0], 1), 0) < nv_ref[i]
        x = jnp.where(live, _unpack_rows(x_ref[...]), 0.0).astype(BF16)
        gu = jnp.dot(x, wgu_bf[...], preferred_element_type=F32) + bgu_ref[0]
        gate = jnp.minimum(gu[:, :D_MODEL], SWIGLU_LIMIT)
        up = jnp.clip(gu[:, D_MODEL:], -SWIGLU_LIMIT, SWIGLU_LIMIT)
        act = (up + 1.0) * (gate * jax.nn.sigmoid(SWIGLU_ALPHA * gate))
        y = jnp.dot(act.astype(BF16), wd_bf[...], preferred_element_type=F32) + bd_ref[0]
        y_ref[...] = _pack_rows(y)


def _moe_experts(xb, blk_e, n_used, n_valid, wgu, bgu, wd, bd, bm):
    NP = xb.shape[0]
    nblk = NP // bm
    grid_spec = pltpu.PrefetchScalarGridSpec(
        num_scalar_prefetch=3,
        grid=(nblk,),
        in_specs=[pl.BlockSpec((bm, HALF_D), lambda i, be, nu, nv: (jnp.minimum(i, nu[0] - 1), 0)),
                  pl.BlockSpec((1, D_MODEL, 2 * D_MODEL), lambda i, be, nu, nv: (be[i], 0, 0)),
                  pl.BlockSpec((1, 1, 2 * D_MODEL), lambda i, be, nu, nv: (be[i], 0, 0)),
                  pl.BlockSpec((1, D_MODEL, D_MODEL), lambda i, be, nu, nv: (be[i], 0, 0)),
                  pl.BlockSpec((1, 1, D_MODEL), lambda i, be, nu, nv: (be[i], 0, 0))],
        out_specs=pl.BlockSpec((bm, HALF_D), lambda i, be, nu, nv: (i, 0)),
        scratch_shapes=[pltpu.VMEM((D_MODEL, 2 * D_MODEL), BF16), pltpu.VMEM((D_MODEL, D_MODEL), BF16)],
    )
    return pl.pallas_call(
        _moe_kernel,
        grid_spec=grid_spec,
        out_shape=jax.ShapeDtypeStruct((NP, HALF_D), jnp.int32),
        compiler_params=_cparams("arbitrary"),
        name="moe_experts",
    )(blk_e, n_used, n_valid, xb, wgu, bgu, wd, bd)


def _final_kernel(x1_ref, y_ref, route_ref, g_ref, o_ref):
    w = route_ref[...]
    acc = x1_ref[...]
    for k in range(TOP_K):
        acc = acc + w[:, 2 * TOP_K + k:2 * TOP_K + k + 1] * _unpack_rows(y_ref[k])
    o_ref[...] = acc * lax.rsqrt(jnp.mean(acc * acc, axis=-1, keepdims=True) + EPS) * g_ref[...]


def _final(x1, yk, route, g, tm=512):
    T = x1.shape[0]
    return pl.pallas_call(
        _final_kernel,
        grid=(T // tm,),
        in_specs=[pl.BlockSpec((tm, D_MODEL), lambda i: (i, 0)),
                  pl.BlockSpec((TOP_K, tm, HALF_D), lambda i: (0, i, 0)),
                  pl.BlockSpec((tm, LOGIT_PAD), lambda i: (i, 0)),
                  pl.BlockSpec((1, D_MODEL), lambda i: (0, 0))],
        out_specs=pl.BlockSpec((tm, D_MODEL), lambda i: (i, 0)),
        out_shape=jax.ShapeDtypeStruct((T, D_MODEL), F32),
        compiler_params=_cparams("parallel"),
        name="final_norm",
    )(x1, yk, route, g)


def _dest_kernel(route_ref, pstart_ref, dest_ref):
    tm = route_ref.shape[0]
    route = route_ref[...]
    lane = lax.broadcasted_iota(jnp.int32, (tm, LOGIT_PAD), 1)
    expert = jnp.where(lane < TOP_K, route, 0.0).astype(jnp.int32)
    start = jnp.take_along_axis(jnp.broadcast_to(pstart_ref[...], (tm, LOGIT_PAD)), expert, axis=1)
    position = pltpu.roll(route, LOGIT_PAD - TOP_K, axis=1)
    rows = start + position
    dest_ref[...] = rows.T[:TOP_K].astype(jnp.int32)


def _dest_rows(route, pstart, tm=2048):
    T = route.shape[0]
    return pl.pallas_call(
        _dest_kernel,
        grid=(T // tm,),
        in_specs=[pl.BlockSpec((tm, LOGIT_PAD), lambda i: (i, 0)),
                  pl.BlockSpec((1, LOGIT_PAD), lambda i: (0, 0))],
        out_specs=pl.BlockSpec((TOP_K, tm), lambda i: (0, i)),
        out_shape=jax.ShapeDtypeStruct((TOP_K, T), jnp.int32),
        compiler_params=_cparams("parallel"),
        name="dest_rows",
    )(route, pstart)


def _pack_w_in(w):
    kr = w[:, 1920:1952]
    half = B_ROPE // 2
    z = lambda n: jnp.zeros((w.shape[0], n), w.dtype)
    kr_placed = jnp.concatenate([z(B_NOPE), kr, z(HEAD_PAD - B_NOPE - B_ROPE)], axis=1)
    kr_rot = jnp.concatenate([z(B_NOPE), -kr[:, half:], kr[:, :half], z(HEAD_PAD - B_NOPE - B_ROPE)], axis=1)
    return jnp.concatenate([w[:, :1920], w[:, 1952:4000], kr_placed, kr_rot], axis=1).astype(BF16)


def _pack_w_uq(w):
    half = B_ROPE // 2
    r = w.shape[0]
    nope, x1, x2 = w[..., :B_NOPE], w[..., B_NOPE:B_NOPE + half], w[..., B_NOPE + half:]
    zpad = jnp.zeros((r, B_HEADS, HEAD_PAD - B_NOPE - B_ROPE), w.dtype)
    main = jnp.concatenate([nope, x1, x2, zpad], axis=-1)
    rot = jnp.concatenate([jnp.zeros_like(nope), -x2, x1, zpad], axis=-1)
    return (main.reshape(r, B_HEADS * HEAD_PAD).astype(BF16), rot.reshape(r, B_HEADS * HEAD_PAD).astype(BF16))


def _pack_w_uk(w):
    r = w.shape[0]
    zpad = jnp.zeros((r, B_HEADS, HEAD_PAD - B_NOPE), w.dtype)
    return jnp.concatenate([w, zpad], axis=-1).reshape(r, B_HEADS * HEAD_PAD).astype(BF16)


def _rope_tables(seq):
    inv_freq = ROPE_THETA ** (-np.arange(0, B_ROPE, 2, dtype=np.float64) / B_ROPE)
    ang = np.arange(seq, dtype=np.float64)[:, None] * inv_freq[None, :]
    cos, sin = jnp.asarray(np.cos(ang), F32), jnp.asarray(np.sin(ang), F32)
    one = jnp.ones((seq, B_NOPE), F32)
    zero = jnp.zeros((seq, B_NOPE), F32)
    zpad = jnp.zeros((seq, HEAD_PAD - B_NOPE - B_ROPE), F32)
    qs = ((B_NOPE + B_ROPE) ** -0.5) * LOG2E
    cq = jnp.concatenate([one, cos, cos, zpad], axis=1) * qs
    sq = jnp.concatenate([zero, sin, sin, zpad], axis=1) * qs
    ck = jnp.concatenate([zero, cos, cos, zpad], axis=1)
    sk = jnp.concatenate([zero, sin, sin, zpad], axis=1)
    return cq, sq, ck, sk


def kernel(x, norm1_g, w_in, q_a_norm_g, kv_a_norm_g, w_uq, w_uk, w_uv, w_a_out, w_b_out, w_o,
           norm2_g, w_router, b_router, w_gate_up, b_gate_up, w_down, b_down, final_g):
    B, S, D = x.shape
    T = B * S
    depth = norm1_g.shape[0]
    xt = x.reshape(T, D)
    cq_t, sq_t, ck_t, sk_t = _rope_tables(S)
    biases = [_dilated_bias(d) for _, d in A_PATTERNS]
    bm = 512
    for l in range(depth):
        wq1, wq2 = _pack_w_uq(w_uq[l])
        qa, ka, va, qa4, ka4, va4, qa16, ka16, va16, qb, kb, vb, gates = _inproj(
            xt, norm1_g[l][None], _pack_w_in(w_in[l]), q_a_norm_g[l][None], kv_a_norm_g[l][None],
            wq1, wq2, _pack_w_uk(w_uk[l]), w_uv[l].reshape(KV_LORA, B_WIDTH).T.astype(BF16),
            cq_t, sq_t, ck_t, sk_t, S)
        qkv = ((qa, ka, va), (qa4, ka4, va4), (qa16, ka16, va16))
        outs = [_dilated_pattern(*qkv[p], biases[p], B, S, A_PATTERNS[p][1]) for p in range(3)]
        ob = _mla_attention(qb.reshape(B, S, -1), kb.reshape(B, S, -1), vb, B, S)

        wr = jnp.pad(w_router[l], ((0, 0), (0, LOGIT_PAD - N_EXPERTS)))
        wrh = wr.astype(BF16)
        wrl = (wr - wrh.astype(F32)).astype(BF16)
        br = jnp.pad(b_router[l], (0, LOGIT_PAD - N_EXPERTS), constant_values=NEG)[None]
        x1, h2, route, count_rows = _outproj(
            outs[0][0], outs[1][0], outs[2][0], outs[0][1], outs[1][1], outs[2][1],
            ob.reshape(T, B_WIDTH), gates, xt,
            w_a_out[l].astype(BF16), w_b_out[l].astype(BF16), w_o[l].astype(BF16),
            norm2_g[l][None], wrh, wrl, br)

        counts = count_rows[0, :N_EXPERTS].astype(jnp.int32)
        padded = (counts + bm - 1) // bm * bm
        pend = jnp.cumsum(padded)
        pstart = pend - padded
        NP = T * TOP_K + N_EXPERTS * bm
        nblk = NP // bm
        blk_start = jnp.arange(nblk, dtype=jnp.int32) * bm
        blk_e = jnp.minimum(jnp.sum(pend[None, :] <= blk_start[:, None], axis=1), N_EXPERTS - 1).astype(jnp.int32)
        n_used = (pend[-1] // bm).astype(jnp.int32)[None]
        n_valid = jnp.clip(pstart[blk_e] + counts[blk_e] - blk_start, 0, bm).astype(jnp.int32)

        pstart_row = jnp.pad(pstart.astype(F32), (0, LOGIT_PAD - N_EXPERTS))[None]
        dest_kt = _dest_rows(route, pstart_row).reshape(-1)
        xb = _scatter_rows(h2, dest_kt, NP)
        yb = _moe_experts(xb, blk_e, n_used, n_valid, w_gate_up[l], b_gate_up[l][:, None, :],
                          w_down[l], b_down[l][:, None, :], bm)
        yk = _gather_rows(yb, dest_kt).reshape(TOP_K, T, HALF_D)
        if l + 1 < depth:
            raise NotImplementedError("depth > 1")
        out = _final(x1, yk, route, final_g[None])
    return out.reshape(B, S, D)
```

```python
import functools

import jax
import jax.numpy as jnp
import numpy as np
from jax import lax
from jax.experimental import pallas as pl
from jax.experimental.pallas import tpu as pltpu
from jax.experimental.pallas import tpu_sc as plsc

F32 = jnp.float32
BF16 = jnp.bfloat16

D_MODEL = 1024
EPS = 1e-5
NEG = -1e30

A_HEADS = 8
A_HEAD_DIM = 64
A_WIDTH = 512
A_PATTERNS = ((128, 1), (512, 4), (2048, 16))
A_HALF = 64
A_QB = 128
A_WIN = A_QB + 2 * A_HALF
assert all(w // (2 * d) == A_HALF for w, d in A_PATTERNS) and [d for _, d in A_PATTERNS] == [1, 4, 16]

B_HEADS = 8
B_NOPE = 64
B_ROPE = 32
B_V = 64
B_WIDTH = 512
Q_LORA = 256
KV_LORA = 128
ROPE_THETA = 10000.0
HEAD_PAD = 128
ONES_ROWS = 16
MLA_SCORE_BUFS = 4
MLA_TRIP_CHUNKS = 16
MLA_HEADS_PER_STEP = 4

N_EXPERTS = 32
TOP_K = 4
SWIGLU_LIMIT = 7.0
SWIGLU_ALPHA = 1.702
LOGIT_PAD = 128

LOG2E = 1.4426950408889634

VMEM_LIMIT = 56 * 1024 * 1024


def _cparams(*sem):
    return pltpu.CompilerParams(dimension_semantics=sem, vmem_limit_bytes=VMEM_LIMIT)


def _inproj_kernel(x_ref, g1_ref, wp_ref, gq_ref, gkv_ref, wq1_ref, wq2_ref, wuk_ref, wuv_ref,
                   cq_ref, sq_ref, ck_ref, sk_ref,
                   qa_ref, ka_ref, va_ref, qa4_ref, ka4_ref, va4_ref, qa16_ref, ka16_ref, va16_ref,
                   qb_ref, kb_ref, vb_ref, gate_ref, perm_ref, perm4_ref):
    tm = x_ref.shape[0]
    x = x_ref[...]
    h = x * lax.rsqrt(jnp.mean(x * x, axis=-1, keepdims=True) + EPS) * g1_ref[...]
    h = h.astype(BF16)

    def proj(c0, c1):
        return jnp.dot(h, wp_ref[:, c0:c1], preferred_element_type=F32)

    def emit(which, z, nat_ref, v4_ref, v16_ref):
        nat_ref[...] = z.astype(BF16)
        nc = A_WIDTH // 128
        for c in range(nc):
            perm_ref[which, c] = z[:, c * 128:(c + 1) * 128]
        for r1 in range(4):
            for c in range(nc):
                quarter = perm_ref[which, c, pl.ds(r1, tm // 4, stride=4), :]
                v4_ref[:, r1 * A_WIDTH + c * 128:r1 * A_WIDTH + (c + 1) * 128] = quarter.astype(BF16)
                perm4_ref[which, r1, c] = quarter
        for r1 in range(4):
            for r2 in range(4):
                r = 4 * r2 + r1
                for c in range(nc):
                    v16_ref[:, r * A_WIDTH + c * 128:r * A_WIDTH + (c + 1) * 128] = (
                        perm4_ref[which, r1, c, pl.ds(r2, tm // 16, stride=4), :].astype(BF16))

    emit(0, proj(0, 512) * (A_HEAD_DIM ** -0.5), qa_ref, qa4_ref, qa16_ref)
    emit(1, proj(512, 1024), ka_ref, ka4_ref, ka16_ref)
    emit(2, proj(1024, 1536), va_ref, va4_ref, va16_ref)

    cq = proj(1536, 1792)
    cq = cq * lax.rsqrt(jnp.mean(cq * cq, axis=-1, keepdims=True) + EPS) * gq_ref[...]
    cq = cq.astype(BF16)
    q_main = jnp.dot(cq, wq1_ref[...], preferred_element_type=F32)
    q_rot = jnp.dot(cq, wq2_ref[...], preferred_element_type=F32)
    cq_t = cq_ref[...]
    sq_t = sq_ref[...]
    for hd in range(B_HEADS):
        sl = slice(hd * HEAD_PAD, (hd + 1) * HEAD_PAD)
        qb_ref[:, sl] = (q_main[:, sl] * cq_t + q_rot[:, sl] * sq_t).astype(BF16)

    ckv = proj(1792, 1920)
    ckv = ckv * lax.rsqrt(jnp.mean(ckv * ckv, axis=-1, keepdims=True) + EPS) * gkv_ref[...]
    ckv = ckv.astype(BF16)
    k_nope = jnp.dot(ckv, wuk_ref[...], preferred_element_type=F32)
    vb_ref[...] = lax.dot_general(wuv_ref[...], ckv, (((1,), (1,)), ((), ())),
                                  preferred_element_type=F32).astype(BF16)
    k_rope = proj(3968, 4096) * ck_ref[...] + proj(4096, 4224) * sk_ref[...]
    for hd in range(B_HEADS):
        sl = slice(hd * HEAD_PAD, (hd + 1) * HEAD_PAD)
        kb_ref[:, sl] = (k_nope[:, sl] + k_rope).astype(BF16)

    gate_ref[...] = jax.nn.sigmoid(proj(1920, 3968)).astype(BF16)


def _inproj(x2, g1, wp, gq, gkv, wq1, wq2, wuk, wuv, cq_t, sq_t, ck_t, sk_t, seq, tm=512):
    T = x2.shape[0]
    nseq = seq // tm
    row = lambda i: (i, 0)
    fix = lambda i: (0, 0)
    tab = lambda i: (i % nseq, 0)
    full = lambda a: pl.BlockSpec(a.shape, fix)
    sds = lambda r, c: jax.ShapeDtypeStruct((r, c), BF16)
    out_shape = (
        sds(T, A_WIDTH), sds(T, A_WIDTH), sds(T, A_WIDTH),
        sds(T // 4, 4 * A_WIDTH), sds(T // 4, 4 * A_WIDTH), sds(T // 4, 4 * A_WIDTH),
        sds(T // 16, 16 * A_WIDTH), sds(T // 16, 16 * A_WIDTH), sds(T // 16, 16 * A_WIDTH),
        sds(T, B_HEADS * HEAD_PAD), sds(T, B_HEADS * HEAD_PAD),
        sds(B_WIDTH, T),
        sds(T, 2 * D_MODEL),
    )
    out_specs = [pl.BlockSpec((tm * s.shape[0] // T, s.shape[1]), row) for s in out_shape]
    out_specs[11] = pl.BlockSpec((B_WIDTH, tm), lambda i: (0, i))
    return pl.pallas_call(
        _inproj_kernel,
        grid=(T // tm,),
        in_specs=[pl.BlockSpec((tm, D_MODEL), row), full(g1), full(wp), full(gq), full(gkv),
                  full(wq1), full(wq2), full(wuk), full(wuv),
                  pl.BlockSpec((tm, HEAD_PAD), tab), pl.BlockSpec((tm, HEAD_PAD), tab),
                  pl.BlockSpec((tm, HEAD_PAD), tab), pl.BlockSpec((tm, HEAD_PAD), tab)],
        out_specs=out_specs,
        out_shape=out_shape,
        scratch_shapes=[pltpu.VMEM((3, A_WIDTH // 128, tm, 128), F32),
                        pltpu.VMEM((3, 4, A_WIDTH // 128, tm // 4, 128), F32)],
        compiler_params=_cparams("parallel"),
        name="inproj",
    )(x2, g1, wp, gq, gkv, wq1, wq2, wuk, wuv, cq_t, sq_t, ck_t, sk_t)


def _dilated_kernel(q_ref, kc_ref, kp_ref, kn_ref, vc_ref, vp_ref, vn_ref, bias_ref,
                    o_ref, lse_ref, kbuf, vbuf, *, rows):
    i = pl.program_id(2)
    last = pl.num_programs(2) - 1
    kbuf[0:A_HALF, :] = kp_ref[0]
    kbuf[A_HALF:A_HALF + rows, :] = kc_ref[0]
    kbuf[A_HALF + rows:, :] = kn_ref[0]
    vbuf[0:A_HALF, :] = vp_ref[0]
    vbuf[A_HALF:A_HALF + rows, :] = vc_ref[0]
    vbuf[A_HALF + rows:, :] = vn_ref[0]

    nj = rows // A_QB
    npair = A_HEADS // 2
    lane = lax.broadcasted_iota(jnp.int32, (A_QB, 128), 1)
    lo_half = lane < A_HEAD_DIM

    def body(j, carry):
        r0 = pl.multiple_of(j * A_QB, A_QB)
        variant = (jnp.logical_and(i == 0, j == 0).astype(jnp.int32)
                   + 2 * jnp.logical_and(i == last, j == nj - 1).astype(jnp.int32))
        scores = []
        for pr in range(npair):
            ls = slice(pr * 128, (pr + 1) * 128)
            qp = q_ref[0, pl.ds(r0, A_QB), ls]
            zero = jnp.zeros_like(qp)
            q2 = jnp.concatenate([jnp.where(lo_half, qp, zero), jnp.where(lo_half, zero, qp)], axis=0)
            s = lax.dot_general(q2, kbuf[pl.ds(r0, A_WIN), ls], (((1,), (1,)), ((), ())),
                                preferred_element_type=F32)
            scores.append(s + bias_ref[variant * npair + pr])
        probs = []
        for s in scores:
            m = jnp.max(s, axis=-1, keepdims=True)
            p = jnp.exp(s - m)
            l = jnp.sum(p, axis=-1, keepdims=True)
            probs.append((p.astype(BF16), l, m + jnp.log(l)))
        for pr, (p, l, lse) in enumerate(probs):
            ls = slice(pr * 128, (pr + 1) * 128)
            o = jnp.dot(p, vbuf[pl.ds(r0, A_WIN), ls], preferred_element_type=F32) / l
            lse = jnp.broadcast_to(lse, (2 * A_QB, 128))
            o_ref[0, pl.ds(r0, A_QB), ls] = jnp.where(lo_half, o[:A_QB], o[A_QB:]).astype(o_ref.dtype)
            lse_ref[0, pl.ds(r0, A_QB), ls] = jnp.where(lo_half, lse[:A_QB], lse[A_QB:])
        return carry

    lax.fori_loop(0, nj, body, 0)


def _dilated_pattern(qa, ka, va, bias, batch, seq, dil, rows=1024):
    L = seq // dil
    rows = min(rows, L)
    nblk = L // rows
    hb = rows // A_HALF
    nh = L // A_HALF
    view = lambda a: a.reshape(batch, L, dil * A_WIDTH)
    cur = lambda b, r, i: (b, i, r)
    prev = lambda b, r, i: (b, jnp.maximum(i * hb - 1, 0), r)
    nxt = lambda b, r, i: (b, jnp.minimum((i + 1) * hb, nh - 1), r)
    cur_spec = pl.BlockSpec((1, rows, A_WIDTH), cur)
    out_sds = lambda dt: jax.ShapeDtypeStruct((batch, L, dil * A_WIDTH), dt)
    o, lse = pl.pallas_call(
        functools.partial(_dilated_kernel, rows=rows),
        grid=(batch, dil, nblk),
        in_specs=[cur_spec, cur_spec,
                  pl.BlockSpec((1, A_HALF, A_WIDTH), prev), pl.BlockSpec((1, A_HALF, A_WIDTH), nxt),
                  cur_spec,
                  pl.BlockSpec((1, A_HALF, A_WIDTH), prev), pl.BlockSpec((1, A_HALF, A_WIDTH), nxt),
                  pl.BlockSpec(bias.shape, lambda b, r, i: (0, 0, 0))],
        out_specs=[cur_spec, cur_spec],
        out_shape=(out_sds(BF16), out_sds(F32)),
        scratch_shapes=[pltpu.VMEM((rows + 2 * A_HALF, A_WIDTH), BF16),
                        pltpu.VMEM((rows + 2 * A_HALF, A_WIDTH), BF16)],
        compiler_params=_cparams("parallel", "parallel", "parallel"),
        name=f"dilated_d{dil}",
    )(view(qa), view(ka), view(ka), view(ka), view(va), view(va), view(va), bias)
    return o.reshape(batch * L, dil * A_WIDTH), lse.reshape(batch * L, dil * A_WIDTH)


def _dilated_bias(dil):
    slopes = 2.0 ** (-8.0 * (np.arange(A_HEADS, dtype=np.float64) + 1.0) / A_HEADS)
    col = np.arange(A_WIN)[None, :]
    rel = (col - A_HALF) - np.arange(A_QB)[:, None]
    bias = -slopes[:, None, None] * (np.abs(rel) * dil).astype(np.float64)[None]
    bias = np.where((np.abs(rel) <= A_HALF)[None], bias, NEG)
    variants = []
    for v in range(4):
        ok = np.ones((1, A_WIN), bool)
        if v & 1:
            ok = np.logical_and(ok, col >= A_HALF)
        if v & 2:
            ok = np.logical_and(ok, col < A_HALF + A_QB)
        variants.append(np.where(ok[None], bias, NEG).reshape(A_HEADS // 2, 2 * A_QB, A_WIN))
    return jnp.asarray(np.concatenate(variants, axis=0), F32)


def _mla_kernel(q_ref, k_ref, vt_ref, o_ref, *s_bufs, tk):
    tq = q_ref.shape[1]
    nk = k_ref.shape[1] // tk
    nh = q_ref.shape[2] // HEAD_PAD
    qs = [q_ref[0, :, hh * HEAD_PAD:(hh + 1) * HEAD_PAD] for hh in range(nh)]

    def scores(j, s_ref, hh, part=None):
        rows = tk if part is None else tk // 2
        r0 = 0 if part is None else part * rows
        off = pl.multiple_of(j * tk, tk) + r0
        s_ref[hh, r0:r0 + rows, :] = lax.dot_general(
            k_ref[0, pl.ds(off, rows), hh * HEAD_PAD:(hh + 1) * HEAD_PAD], qs[hh],
            (((1,), (1,)), ((), ())), preferred_element_type=F32)

    ones = jnp.ones((ONES_ROWS, tk), BF16)

    def values(j, hh, alpha, p, acc):
        off = pl.multiple_of(j * tk, tk)
        vt = jnp.concatenate([vt_ref[hh * B_V:(hh + 1) * B_V, pl.ds(off, tk)], ones], axis=0)
        return alpha * acc + jnp.dot(vt, p, preferred_element_type=F32)

    def stage(j, s_ref, j_next, s_next, stats):
        new = []
        for hh in range(nh):
            m, acc = stats[hh]
            scores(j_next, s_next, hh, 0)
            s = s_ref[hh]
            m_new = jnp.maximum(m, jnp.max(s, axis=0, keepdims=True))
            scores(j_next, s_next, hh, 1)
            p = jnp.exp2(s - m_new).astype(BF16)
            new.append((m_new, values(j, hh, jnp.exp2(m - m_new), p, acc)))
        return tuple(new)

    nbuf = len(s_bufs)
    trip = min(MLA_TRIP_CHUNKS, nk)

    def body(jj, stats):
        j = trip * jj
        for u in range(trip):
            stats = stage(j + u, s_bufs[u % nbuf], jnp.minimum(j + u + 1, nk - 1), s_bufs[(u + 1) % nbuf], stats)
        return stats

    init = (jnp.full((1, tq), NEG, F32), jnp.zeros((B_V + ONES_ROWS, tq), F32))
    for hh in range(nh):
        scores(0, s_bufs[0], hh)
    stats = lax.fori_loop(0, nk // trip, body, (init,) * nh)
    o_t = jnp.concatenate([a[:B_V] / a[B_V:B_V + 1] for _, a in stats], axis=0)
    o_ref[0] = o_t.T.astype(BF16)


def _mla_attention(qb, kb, vbt, batch, seq, tq=256, tk=512, nh=MLA_HEADS_PER_STEP):
    once = dict(pipeline_mode=pl.Buffered(1))
    return pl.pallas_call(
        functools.partial(_mla_kernel, tk=tk),
        grid=(batch, B_HEADS // nh, seq // tq),
        in_specs=[pl.BlockSpec((1, tq, nh * HEAD_PAD), lambda b, p, i: (b, i, p)),
                  pl.BlockSpec((1, seq, nh * HEAD_PAD), lambda b, p, i: (b, 0, p), **once),
                  pl.BlockSpec((nh * B_V, seq), lambda b, p, i: (p, b), **once)],
        out_specs=pl.BlockSpec((1, tq, nh * B_V), lambda b, p, i: (b, i, p)),
        out_shape=jax.ShapeDtypeStruct((batch, seq, B_WIDTH), BF16),
        scratch_shapes=[pltpu.VMEM((nh, tk, tq), F32) for _ in range(MLA_SCORE_BUFS)],
        compiler_params=_cparams("parallel", "parallel", "parallel"),
        name="mla_attention",
    )(qb, kb, vbt)


def _outproj_kernel(o1_ref, o2_ref, o3_ref, l1_ref, l2_ref, l3_ref, ob_ref, gate_ref, x_ref,
                    wa_ref, wb_ref, wo_ref, g2_ref, wrh_ref, wrl_ref, br_ref,
                    tri_ref, x1_ref, h2_ref, route_ref, count_ref, perm_ref, perm4_ref, carry_ref):
    tm = x_ref.shape[0]

    @pl.when(pl.program_id(0) == 0)
    def _():
        carry_ref[...] = jnp.zeros_like(carry_ref)

    nc = A_WIDTH // 128

    def lanes(r, c):
        return slice(r * A_WIDTH + c * 128, r * A_WIDTH + (c + 1) * 128)

    def token_order(which, ref, d):
        if d == 4:
            for r in range(4):
                for c in range(nc):
                    perm_ref[which, c, pl.ds(r, tm // 4, stride=4), :] = ref[:, lanes(r, c)].astype(F32)
        else:
            for r1 in range(4):
                for r2 in range(4):
                    for c in range(nc):
                        perm4_ref[which - 2, r1, c, pl.ds(r2, tm // 16, stride=4), :] = (
                            ref[:, lanes(4 * r2 + r1, c)].astype(F32))
            for r1 in range(4):
                for c in range(nc):
                    perm_ref[which, c, pl.ds(r1, tm // 4, stride=4), :] = perm4_ref[which - 2, r1, c]
        return jnp.concatenate([perm_ref[which, c] for c in range(nc)], axis=1)

    l1 = l1_ref[...]
    l2, o2 = token_order(0, l2_ref, 4), token_order(1, o2_ref, 4)
    l3, o3 = token_order(2, l3_ref, 16), token_order(3, o3_ref, 16)
    mx = jnp.maximum(jnp.maximum(l1, l2), l3)
    e1, e2, e3 = jnp.exp(l1 - mx), jnp.exp(l2 - mx), jnp.exp(l3 - mx)
    oa = (e1 * o1_ref[...] + e2 * o2 + e3 * o3) / (e1 + e2 + e3)
    ya = jnp.dot(oa.astype(BF16), wa_ref[...], preferred_element_type=F32)
    yb = jnp.dot(ob_ref[...], wb_ref[...], preferred_element_type=F32)
    merged = gate_ref[:, :D_MODEL].astype(F32) * ya + gate_ref[:, D_MODEL:].astype(F32) * yb
    x1 = x_ref[...] + jnp.dot(merged.astype(BF16), wo_ref[...], preferred_element_type=F32)
    x1_ref[...] = x1
    h2 = x1 * lax.rsqrt(jnp.mean(x1 * x1, axis=-1, keepdims=True) + EPS) * g2_ref[...]
    hi = h2.astype(BF16)
    h2_ref[...] = _pack_rows(h2)
    lo = (h2 - hi.astype(F32)).astype(BF16)
    logits = jnp.dot(hi, wrh_ref[...], preferred_element_type=F32)
    logits = logits + jnp.dot(lo, wrh_ref[...], preferred_element_type=F32)
    logits = logits + jnp.dot(hi, wrl_ref[...], preferred_element_type=F32)
    logits = logits + br_ref[...]

    lane = lax.broadcasted_iota(jnp.int32, (tm, LOGIT_PAD), 1)
    work = logits
    sel = jnp.zeros((tm, LOGIT_PAD), F32)
    idx_cols, val_cols = [], []
    for _ in range(TOP_K):
        mx = jnp.max(work, axis=-1, keepdims=True)
        idx = jnp.min(jnp.where(work == mx, lane, LOGIT_PAD), axis=-1, keepdims=True)
        hit = lane == idx
        sel = sel + hit.astype(F32)
        work = jnp.where(hit, 2.0 * NEG, work)
        idx_cols.append(idx)
        val_cols.append(mx)
    exps = [jnp.exp(v - val_cols[0]) for v in val_cols]
    denom = exps[0] + exps[1] + exps[2] + exps[3]
    gate_cols = [e / denom for e in exps]

    before = jnp.dot(tri_ref[...], sel.astype(BF16), preferred_element_type=F32) + carry_ref[0:1, :]
    rank_cols = [jnp.sum(jnp.where(lane == idx, before, 0.0), axis=-1, keepdims=True) for idx in idx_cols]
    carry_ref[...] = carry_ref[...] + jnp.sum(sel, axis=0, keepdims=True)
    count_ref[...] = carry_ref[...]

    cols = [c.astype(F32) for c in idx_cols] + rank_cols + gate_cols
    route = jnp.zeros((tm, LOGIT_PAD), F32)
    for c, col in enumerate(cols):
        route = jnp.where(lane == c, col, route)
    route_ref[...] = route


def _outproj(o1, o2, o3, l1, l2, l3, ob, gates, x2, wa, wb, wo, g2, wrh, wrl, br, tm=512):
    T = x2.shape[0]
    row = lambda i: (i, 0)
    fix = lambda i: (0, 0)
    full = lambda a: pl.BlockSpec(a.shape, fix)
    rows = lambda a: pl.BlockSpec((tm * a.shape[0] // T, a.shape[1]), row)
    tri = (jnp.arange(tm)[:, None] > jnp.arange(tm)[None, :]).astype(BF16)
    out_shape = (jax.ShapeDtypeStruct((T, D_MODEL), F32),
                 jax.ShapeDtypeStruct((T, HALF_D), jnp.int32),
                 jax.ShapeDtypeStruct((T, LOGIT_PAD), F32),
                 jax.ShapeDtypeStruct((8, LOGIT_PAD), F32))
    return pl.pallas_call(
        _outproj_kernel,
        grid=(T // tm,),
        in_specs=[rows(o1), rows(o2), rows(o3), rows(l1), rows(l2), rows(l3), rows(ob), rows(gates),
                  rows(x2), full(wa), full(wb), full(wo), full(g2), full(wrh), full(wrl), full(br), full(tri)],
        out_specs=[pl.BlockSpec((tm, D_MODEL), row), pl.BlockSpec((tm, HALF_D), row),
                   pl.BlockSpec((tm, LOGIT_PAD), row), pl.BlockSpec((8, LOGIT_PAD), fix)],
        out_shape=out_shape,
        scratch_shapes=[pltpu.VMEM((4, A_WIDTH // 128, tm, 128), F32),
                        pltpu.VMEM((2, 4, A_WIDTH // 128, tm // 4, 128), F32),
                        pltpu.VMEM((8, LOGIT_PAD), F32)],
        compiler_params=_cparams("arbitrary"),
        name="outproj",
    )(o1, o2, o3, l1, l2, l3, ob, gates, x2, wa, wb, wo, g2, wrh, wrl, br, tri)


HALF_D = D_MODEL // 2


def _pack_rows(x):
    lo = pltpu.bitcast(x[:, :HALF_D].astype(BF16).astype(F32), jnp.uint32)
    hi = pltpu.bitcast(x[:, HALF_D:].astype(BF16).astype(F32), jnp.uint32)
    word = jnp.bitwise_or(lax.shift_right_logical(lo, jnp.uint32(16)), jnp.bitwise_and(hi, jnp.uint32(0xFFFF0000)))
    return pltpu.bitcast(word, jnp.int32)


def _unpack_rows(w):
    u = pltpu.bitcast(w, jnp.uint32)
    lo = pltpu.bitcast(lax.shift_left(u, jnp.uint32(16)), F32)
    hi = pltpu.bitcast(jnp.bitwise_and(u, jnp.uint32(0xFFFF0000)), F32)
    return jnp.concatenate([lo, hi], axis=1)


SC_WINDOW = 128


def _gather_rows(table, idx):
    M = idx.shape[0]
    width = table.shape[1]
    mesh = plsc.VectorSubcoreMesh(core_axis_name="core", subcore_axis_name="subcore")

    @functools.partial(pl.kernel, out_type=jax.ShapeDtypeStruct((M, width), table.dtype), mesh=mesh,
                       name="gather_rows")
    def gather(table_hbm, idx_hbm, out_hbm):
        def body(idx_vmem, out_vmem):
            pltpu.sync_copy(table_hbm.at[idx_vmem.at[0]], out_vmem)

        pltpu.emit_pipeline(
            body,
            grid=(M // SC_WINDOW,),
            in_specs=[pl.BlockSpec((1, SC_WINDOW), index_map=lambda i: (0, i))],
            out_specs=[pl.BlockSpec((SC_WINDOW, width), index_map=lambda i: (i, 0),
                                    pipeline_mode=pl.Buffered(1))],
            core_axis_name=("core", "subcore"),
            dimension_semantics=(pltpu.PARALLEL,),
        )(idx_hbm, out_hbm)

    return gather(table, idx.reshape(1, M))


def _scatter_rows(rows, idx, n_out):
    M = idx.shape[0]
    R, width = rows.shape
    mesh = plsc.VectorSubcoreMesh(core_axis_name="core", subcore_axis_name="subcore")

    @functools.partial(pl.kernel, out_type=jax.ShapeDtypeStruct((n_out, width), rows.dtype), mesh=mesh,
                       name="scatter_rows")
    def scatter(rows_hbm, idx_hbm, out_hbm):
        def body(rows_vmem, idx_vmem):
            pltpu.sync_copy(rows_vmem, out_hbm.at[idx_vmem.at[0]])

        pltpu.emit_pipeline(
            body,
            grid=(M // SC_WINDOW,),
            in_specs=[pl.BlockSpec((SC_WINDOW, width), index_map=lambda i: (i % (R // SC_WINDOW), 0),
                                   pipeline_mode=pl.Buffered(1)),
                      pl.BlockSpec((1, SC_WINDOW), index_map=lambda i: (0, i))],
            out_specs=[],
            core_axis_name=("core", "subcore"),
            dimension_semantics=(pltpu.PARALLEL,),
        )(rows_hbm, idx_hbm)

    return scatter(rows, idx.reshape(1, M))


def _moe_kernel(be_ref, nu_ref, nv_ref, x_ref, wgu_ref, bgu_ref, wd_ref, bd_ref, y_ref, wgu_bf, wd_bf):
    i = pl.program_id(0)

    @pl.when(i >= nu_ref[0])
    def _():
        y_ref[...] = jnp.zeros_like(y_ref)

    @pl.when(jnp.logical_or(i == 0, be_ref[i] != be_ref[jnp.maximum(i - 1, 0)]))
    def _():
        rows = 128

        def cast(c, carry):
            r0 = pl.multiple_of(c * rows, rows)
            wgu_bf[pl.ds(r0, rows), :] = wgu_ref[0, pl.ds(r0, rows), :].astype(BF16)
            wd_bf[pl.ds(r0, rows), :] = wd_ref[0, pl.ds(r0, rows), :].astype(BF16)
            return carry

        lax.fori_loop(0, D_MODEL // rows, cast, 0)

    @pl.when(i < nu_ref[0])
    def _():
        live = lax.broadcasted_iota(jnp.int32, (x_ref.shape[0], 1), 0) < nv_ref[i]
        x = jnp.where(live, _unpack_rows(x_ref[...]), 0.0).astype(BF16)
        gu = jnp.dot(x, wgu_bf[...], preferred_element_type=F32) + bgu_ref[0]
        gate = jnp.minimum(gu[:, :D_MODEL], SWIGLU_LIMIT)
        up = jnp.clip(gu[:, D_MODEL:], -SWIGLU_LIMIT, SWIGLU_LIMIT)
        act = (up + 1.0) * (gate * jax.nn.sigmoid(SWIGLU_ALPHA * gate))
        y = jnp.dot(act.astype(BF16), wd_bf[...], preferred_element_type=F32) + bd_ref[0]
        y_ref[...] = _pack_rows(y)


def _moe_experts(xb, blk_e, n_used, n_valid, wgu, bgu, wd, bd, bm):
    NP = xb.shape[0]
    nblk = NP // bm
    grid_spec = pltpu.PrefetchScalarGridSpec(
        num_scalar_prefetch=3,
        grid=(nblk,),
        in_specs=[pl.BlockSpec((bm, HALF_D), lambda i, be, nu, nv: (jnp.minimum(i, nu[0] - 1), 0)),
                  pl.BlockSpec((1, D_MODEL, 2 * D_MODEL), lambda i, be, nu, nv: (be[i], 0, 0)),
                  pl.BlockSpec((1, 1, 2 * D_MODEL), lambda i, be, nu, nv: (be[i], 0, 0)),
                  pl.BlockSpec((1, D_MODEL, D_MODEL), lambda i, be, nu, nv: (be[i], 0, 0)),
                  pl.BlockSpec((1, 1, D_MODEL), lambda i, be, nu, nv: (be[i], 0, 0))],
        out_specs=pl.BlockSpec((bm, HALF_D), lambda i, be, nu, nv: (i, 0)),
        scratch_shapes=[pltpu.VMEM((D_MODEL, 2 * D_MODEL), BF16), pltpu.VMEM((D_MODEL, D_MODEL), BF16)],
    )
    return pl.pallas_call(
        _moe_kernel,
        grid_spec=grid_spec,
        out_shape=jax.ShapeDtypeStruct((NP, HALF_D), jnp.int32),
        compiler_params=_cparams("arbitrary"),
        name="moe_experts",
    )(blk_e, n_used, n_valid, xb, wgu, bgu, wd, bd)


def _final_kernel(x1_ref, y_ref, route_ref, g_ref, o_ref):
    w = route_ref[...]
    acc = x1_ref[...]
    for k in range(TOP_K):
        acc = acc + w[:, 2 * TOP_K + k:2 * TOP_K + k + 1] * _unpack_rows(y_ref[k])
    o_ref[...] = acc * lax.rsqrt(jnp.mean(acc * acc, axis=-1, keepdims=True) + EPS) * g_ref[...]


def _final(x1, yk, route, g, tm=512):
    T = x1.shape[0]
    return pl.pallas_call(
        _final_kernel,
        grid=(T // tm,),
        in_specs=[pl.BlockSpec((tm, D_MODEL), lambda i: (i, 0)),
                  pl.BlockSpec((TOP_K, tm, HALF_D), lambda i: (0, i, 0)),
                  pl.BlockSpec((tm, LOGIT_PAD), lambda i: (i, 0)),
                  pl.BlockSpec((1, D_MODEL), lambda i: (0, 0))],
        out_specs=pl.BlockSpec((tm, D_MODEL), lambda i: (i, 0)),
        out_shape=jax.ShapeDtypeStruct((T, D_MODEL), F32),
        compiler_params=_cparams("parallel"),
        name="final_norm",
    )(x1, yk, route, g)


def _dest_kernel(route_ref, pstart_ref, dest_ref):
    tm = route_ref.shape[0]
    route = route_ref[...]
    lane = lax.broadcasted_iota(jnp.int32, (tm, LOGIT_PAD), 1)
    expert = jnp.where(lane < TOP_K, route, 0.0).astype(jnp.int32)
    start = jnp.take_along_axis(jnp.broadcast_to(pstart_ref[...], (tm, LOGIT_PAD)), expert, axis=1)
    position = pltpu.roll(route, LOGIT_PAD - TOP_K, axis=1)
    rows = start + position
    dest_ref[...] = rows.T[:TOP_K].astype(jnp.int32)


def _dest_rows(route, pstart, tm=2048):
    T = route.shape[0]
    return pl.pallas_call(
        _dest_kernel,
        grid=(T // tm,),
        in_specs=[pl.BlockSpec((tm, LOGIT_PAD), lambda i: (i, 0)),
                  pl.BlockSpec((1, LOGIT_PAD), lambda i: (0, 0))],
        out_specs=pl.BlockSpec((TOP_K, tm), lambda i: (0, i)),
        out_shape=jax.ShapeDtypeStruct((TOP_K, T), jnp.int32),
        compiler_params=_cparams("parallel"),
        name="dest_rows",
    )(route, pstart)


def _pack_w_in(w):
    kr = w[:, 1920:1952]
    half = B_ROPE // 2
    z = lambda n: jnp.zeros((w.shape[0], n), w.dtype)
    kr_placed = jnp.concatenate([z(B_NOPE), kr, z(HEAD_PAD - B_NOPE - B_ROPE)], axis=1)
    kr_rot = jnp.concatenate([z(B_NOPE), -kr[:, half:], kr[:, :half], z(HEAD_PAD - B_NOPE - B_ROPE)], axis=1)
    return jnp.concatenate([w[:, :1920], w[:, 1952:4000], kr_placed, kr_rot], axis=1).astype(BF16)


def _pack_w_uq(w):
    half = B_ROPE // 2
    r = w.shape[0]
    nope, x1, x2 = w[..., :B_NOPE], w[..., B_NOPE:B_NOPE + half], w[..., B_NOPE + half:]
    zpad = jnp.zeros((r, B_HEADS, HEAD_PAD - B_NOPE - B_ROPE), w.dtype)
    main = jnp.concatenate([nope, x1, x2, zpad], axis=-1)
    rot = jnp.concatenate([jnp.zeros_like(nope), -x2, x1, zpad], axis=-1)
    return (main.reshape(r, B_HEADS * HEAD_PAD).astype(BF16), rot.reshape(r, B_HEADS * HEAD_PAD).astype(BF16))


def _pack_w_uk(w):
    r = w.shape[0]
    zpad = jnp.zeros((r, B_HEADS, HEAD_PAD - B_NOPE), w.dtype)
    return jnp.concatenate([w, zpad], axis=-1).reshape(r, B_HEADS * HEAD_PAD).astype(BF16)


def _rope_tables(seq):
    inv_freq = ROPE_THETA ** (-np.arange(0, B_ROPE, 2, dtype=np.float64) / B_ROPE)
    ang = np.arange(seq, dtype=np.float64)[:, None] * inv_freq[None, :]
    cos, sin = jnp.asarray(np.cos(ang), F32), jnp.asarray(np.sin(ang), F32)
    one = jnp.ones((seq, B_NOPE), F32)
    zero = jnp.zeros((seq, B_NOPE), F32)
    zpad = jnp.zeros((seq, HEAD_PAD - B_NOPE - B_ROPE), F32)
    qs = ((B_NOPE + B_ROPE) ** -0.5) * LOG2E
    cq = jnp.concatenate([one, cos, cos, zpad], axis=1) * qs
    sq = jnp.concatenate([zero, sin, sin, zpad], axis=1) * qs
    ck = jnp.concatenate([zero, cos, cos, zpad], axis=1)
    sk = jnp.concatenate([zero, sin, sin, zpad], axis=1)
    return cq, sq, ck, sk


def kernel(x, norm1_g, w_in, q_a_norm_g, kv_a_norm_g, w_uq, w_uk, w_uv, w_a_out, w_b_out, w_o,
           norm2_g, w_router, b_router, w_gate_up, b_gate_up, w_down, b_down, final_g):
    B, S, D = x.shape
    T = B * S
    depth = norm1_g.shape[0]
    xt = x.reshape(T, D)
    cq_t, sq_t, ck_t, sk_t = _rope_tables(S)
    biases = [_dilated_bias(d) for _, d in A_PATTERNS]
    bm = 512
    for l in range(depth):
        wq1, wq2 = _pack_w_uq(w_uq[l])
        qa, ka, va, qa4, ka4, va4, qa16, ka16, va16, qb, kb, vb, gates = _inproj(
            xt, norm1_g[l][None], _pack_w_in(w_in[l]), q_a_norm_g[l][None], kv_a_norm_g[l][None],
            wq1, wq2, _pack_w_uk(w_uk[l]), w_uv[l].reshape(KV_LORA, B_WIDTH).T.astype(BF16),
            cq_t, sq_t, ck_t, sk_t, S)
        qkv = ((qa, ka, va), (qa4, ka4, va4), (qa16, ka16, va16))
        outs = [_dilated_pattern(*qkv[p], biases[p], B, S, A_PATTERNS[p][1]) for p in range(3)]
        ob = _mla_attention(qb.reshape(B, S, -1), kb.reshape(B, S, -1), vb, B, S)

        wr = jnp.pad(w_router[l], ((0, 0), (0, LOGIT_PAD - N_EXPERTS)))
        wrh = wr.astype(BF16)
        wrl = (wr - wrh.astype(F32)).astype(BF16)
        br = jnp.pad(b_router[l], (0, LOGIT_PAD - N_EXPERTS), constant_values=NEG)[None]
        x1, h2, route, count_rows = _outproj(
            outs[0][0], outs[1][0], outs[2][0], outs[0][1], outs[1][1], outs[2][1],
            ob.reshape(T, B_WIDTH), gates, xt,
            w_a_out[l].astype(BF16), w_b_out[l].astype(BF16), w_o[l].astype(BF16),
            norm2_g[l][None], wrh, wrl, br)

        counts = count_rows[0, :N_EXPERTS].astype(jnp.int32)
        padded = (counts + bm - 1) // bm * bm
        pend = jnp.cumsum(padded)
        pstart = pend - padded
        NP = T * TOP_K + N_EXPERTS * bm
        nblk = NP // bm
        blk_start = jnp.arange(nblk, dtype=jnp.int32) * bm
        blk_e = jnp.minimum(jnp.sum(pend[None, :] <= blk_start[:, None], axis=1), N_EXPERTS - 1).astype(jnp.int32)
        n_used = (pend[-1] // bm).astype(jnp.int32)[None]
        n_valid = jnp.clip(pstart[blk_e] + counts[blk_e] - blk_start, 0, bm).astype(jnp.int32)

        pstart_row = jnp.pad(pstart.astype(F32), (0, LOGIT_PAD - N_EXPERTS))[None]
        dest_kt = _dest_rows(route, pstart_row).reshape(-1)
        xb = _scatter_rows(h2, dest_kt, NP)
        yb = _moe_experts(xb, blk_e, n_used, n_valid, w_gate_up[l], b_gate_up[l][:, None, :],
                          w_down[l], b_down[l][:, None, :], bm)
        yk = _gather_rows(yb, dest_kt).reshape(TOP_K, T, HALF_D)
        if l + 1 < depth:
            raise NotImplementedError("depth > 1")
        out = _final(x1, yk, route, final_g[None])
    return out.reshape(B, S, D)
```

```python
import functools

import jax
import jax.numpy as jnp
import numpy as np
from jax import lax
from jax.experimental import pallas as pl
from jax.experimental.pallas import tpu as pltpu
from jax.experimental.pallas import tpu_sc as plsc

F32 = jnp.float32
BF16 = jnp.bfloat16

D_MODEL = 1024
EPS = 1e-5
NEG = -1e30

A_HEADS = 8
A_HEAD_DIM = 64
A_WIDTH = 512
A_PATTERNS = ((128, 1), (512, 4), (2048, 16))
A_HALF = 64
A_QB = 128
A_WIN = A_QB + 2 * A_HALF
DILATED_BLOCKS_PER_TRIP = 4
assert all(w // (2 * d) == A_HALF for w, d in A_PATTERNS) and [d for _, d in A_PATTERNS] == [1, 4, 16]

B_HEADS = 8
B_NOPE = 64
B_ROPE = 32
B_V = 64
B_WIDTH = 512
Q_LORA = 256
KV_LORA = 128
ROPE_THETA = 10000.0
HEAD_PAD = 128
ONES_ROWS = 16
MLA_SCORE_BUFS = 4
MLA_TRIP_CHUNKS = 16
MLA_HEADS_PER_STEP = 4

N_EXPERTS = 32
TOP_K = 4
SWIGLU_LIMIT = 7.0
SWIGLU_ALPHA = 1.702
LOGIT_PAD = 128

LOG2E = 1.4426950408889634

VMEM_LIMIT = 56 * 1024 * 1024


def _cparams(*sem):
    return pltpu.CompilerParams(dimension_semantics=sem, vmem_limit_bytes=VMEM_LIMIT)


def _inproj_kernel(x_ref, g1_ref, wp_ref, gq_ref, gkv_ref, wq1_ref, wq2_ref, wuk_ref, wuv_ref,
                   cq_ref, sq_ref, ck_ref, sk_ref,
                   qa_ref, ka_ref, va_ref, qa4_ref, ka4_ref, va4_ref, qa16_ref, ka16_ref, va16_ref,
                   qb_ref, kb_ref, vb_ref, gate_ref, perm_ref, perm4_ref):
    tm = x_ref.shape[0]
    x = x_ref[...]
    h = x * lax.rsqrt(jnp.mean(x * x, axis=-1, keepdims=True) + EPS) * g1_ref[...]
    h = h.astype(BF16)

    def proj(c0, c1):
        return jnp.dot(h, wp_ref[:, c0:c1], preferred_element_type=F32)

    def emit(which, z, nat_ref, v4_ref, v16_ref):
        nat_ref[...] = z.astype(BF16)
        nc = A_WIDTH // 128
        for c in range(nc):
            perm_ref[which, c] = z[:, c * 128:(c + 1) * 128]
        for r1 in range(4):
            for c in range(nc):
                quarter = perm_ref[which, c, pl.ds(r1, tm // 4, stride=4), :]
                v4_ref[:, r1 * A_WIDTH + c * 128:r1 * A_WIDTH + (c + 1) * 128] = quarter.astype(BF16)
                perm4_ref[which, r1, c] = quarter
        for r1 in range(4):
            for r2 in range(4):
                r = 4 * r2 + r1
                for c in range(nc):
                    v16_ref[:, r * A_WIDTH + c * 128:r * A_WIDTH + (c + 1) * 128] = (
                        perm4_ref[which, r1, c, pl.ds(r2, tm // 16, stride=4), :].astype(BF16))

    emit(0, proj(0, 512) * (A_HEAD_DIM ** -0.5), qa_ref, qa4_ref, qa16_ref)
    emit(1, proj(512, 1024), ka_ref, ka4_ref, ka16_ref)
    emit(2, proj(1024, 1536), va_ref, va4_ref, va16_ref)

    cq = proj(1536, 1792)
    cq = cq * lax.rsqrt(jnp.mean(cq * cq, axis=-1, keepdims=True) + EPS) * gq_ref[...]
    cq = cq.astype(BF16)
    q_main = jnp.dot(cq, wq1_ref[...], preferred_element_type=F32)
    q_rot = jnp.dot(cq, wq2_ref[...], preferred_element_type=F32)
    cq_t = cq_ref[...]
    sq_t = sq_ref[...]
    for hd in range(B_HEADS):
        sl = slice(hd * HEAD_PAD, (hd + 1) * HEAD_PAD)
        qb_ref[:, sl] = (q_main[:, sl] * cq_t + q_rot[:, sl] * sq_t).astype(BF16)

    ckv = proj(1792, 1920)
    ckv = ckv * lax.rsqrt(jnp.mean(ckv * ckv, axis=-1, keepdims=True) + EPS) * gkv_ref[...]
    ckv = ckv.astype(BF16)
    k_nope = jnp.dot(ckv, wuk_ref[...], preferred_element_type=F32)
    vb_ref[...] = lax.dot_general(wuv_ref[...], ckv, (((1,), (1,)), ((), ())),
                                  preferred_element_type=F32).astype(BF16)
    k_rope = proj(3968, 4096) * ck_ref[...] + proj(4096, 4224) * sk_ref[...]
    for hd in range(B_HEADS):
        sl = slice(hd * HEAD_PAD, (hd + 1) * HEAD_PAD)
        kb_ref[:, sl] = (k_nope[:, sl] + k_rope).astype(BF16)

    gate_ref[...] = jax.nn.sigmoid(proj(1920, 3968)).astype(BF16)


def _inproj(x2, g1, wp, gq, gkv, wq1, wq2, wuk, wuv, cq_t, sq_t, ck_t, sk_t, seq, tm=512):
    T = x2.shape[0]
    nseq = seq // tm
    row = lambda i: (i, 0)
    fix = lambda i: (0, 0)
    tab = lambda i: (i % nseq, 0)
    full = lambda a: pl.BlockSpec(a.shape, fix)
    sds = lambda r, c: jax.ShapeDtypeStruct((r, c), BF16)
    out_shape = (
        sds(T, A_WIDTH), sds(T, A_WIDTH), sds(T, A_WIDTH),
        sds(T // 4, 4 * A_WIDTH), sds(T // 4, 4 * A_WIDTH), sds(T // 4, 4 * A_WIDTH),
        sds(T // 16, 16 * A_WIDTH), sds(T // 16, 16 * A_WIDTH), sds(T // 16, 16 * A_WIDTH),
        sds(T, B_HEADS * HEAD_PAD), sds(T, B_HEADS * HEAD_PAD),
        sds(B_WIDTH, T),
        sds(T, 2 * D_MODEL),
    )
    out_specs = [pl.BlockSpec((tm * s.shape[0] // T, s.shape[1]), row) for s in out_shape]
    out_specs[11] = pl.BlockSpec((B_WIDTH, tm), lambda i: (0, i))
    return pl.pallas_call(
        _inproj_kernel,
        grid=(T // tm,),
        in_specs=[pl.BlockSpec((tm, D_MODEL), row), full(g1), full(wp), full(gq), full(gkv),
                  full(wq1), full(wq2), full(wuk), full(wuv),
                  pl.BlockSpec((tm, HEAD_PAD), tab), pl.BlockSpec((tm, HEAD_PAD), tab),
                  pl.BlockSpec((tm, HEAD_PAD), tab), pl.BlockSpec((tm, HEAD_PAD), tab)],
        out_specs=out_specs,
        out_shape=out_shape,
        scratch_shapes=[pltpu.VMEM((3, A_WIDTH // 128, tm, 128), F32),
                        pltpu.VMEM((3, 4, A_WIDTH // 128, tm // 4, 128), F32)],
        compiler_params=_cparams("parallel"),
        name="inproj",
    )(x2, g1, wp, gq, gkv, wq1, wq2, wuk, wuv, cq_t, sq_t, ck_t, sk_t)


def _dilated_kernel(q_ref, kc_ref, kp_ref, kn_ref, vc_ref, vp_ref, vn_ref, bias_ref,
                    o_ref, lse_ref, kbuf, vbuf, *, rows):
    i = pl.program_id(2)
    last = pl.num_programs(2) - 1
    kbuf[0:A_HALF, :] = kp_ref[0]
    kbuf[A_HALF:A_HALF + rows, :] = kc_ref[0]
    kbuf[A_HALF + rows:, :] = kn_ref[0]
    vbuf[0:A_HALF, :] = vp_ref[0]
    vbuf[A_HALF:A_HALF + rows, :] = vc_ref[0]
    vbuf[A_HALF + rows:, :] = vn_ref[0]

    nj = rows // A_QB
    npair = A_HEADS // 2
    lane = lax.broadcasted_iota(jnp.int32, (A_QB, 128), 1)
    lo_half = lane < A_HEAD_DIM

    per_trip = DILATED_BLOCKS_PER_TRIP if nj % DILATED_BLOCKS_PER_TRIP == 0 else 1

    def body(jj, carry):
        work = []
        for b in range(per_trip):
            j = jj * per_trip + b
            r0 = pl.multiple_of(j * A_QB, A_QB)
            variant = (jnp.logical_and(i == 0, j == 0).astype(jnp.int32)
                       + 2 * jnp.logical_and(i == last, j == nj - 1).astype(jnp.int32))
            work.extend((r0, variant, pr) for pr in range(npair))
        scores = []
        for r0, variant, pr in work:
            ls = slice(pr * 128, (pr + 1) * 128)
            qp = q_ref[0, pl.ds(r0, A_QB), ls]
            zero = jnp.zeros_like(qp)
            q2 = jnp.concatenate([jnp.where(lo_half, qp, zero), jnp.where(lo_half, zero, qp)], axis=0)
            s = lax.dot_general(q2, kbuf[pl.ds(r0, A_WIN), ls], (((1,), (1,)), ((), ())),
                                preferred_element_type=F32)
            scores.append(s + bias_ref[variant * npair + pr])
        probs = []
        for s in scores:
            m = jnp.max(s, axis=-1, keepdims=True)
            p = jnp.exp(s - m)
            l = jnp.sum(p, axis=-1, keepdims=True)
            probs.append((p.astype(BF16), l, m + jnp.log(l)))
        for (r0, _, pr), (p, l, lse) in zip(work, probs):
            ls = slice(pr * 128, (pr + 1) * 128)
            o = jnp.dot(p, vbuf[pl.ds(r0, A_WIN), ls], preferred_element_type=F32) / l
            lse = jnp.broadcast_to(lse, (2 * A_QB, 128))
            o_ref[0, pl.ds(r0, A_QB), ls] = jnp.where(lo_half, o[:A_QB], o[A_QB:]).astype(o_ref.dtype)
            lse_ref[0, pl.ds(r0, A_QB), ls] = jnp.where(lo_half, lse[:A_QB], lse[A_QB:])
        return carry

    lax.fori_loop(0, nj // per_trip, body, 0)


def _dilated_pattern(qa, ka, va, bias, batch, seq, dil, rows=1024):
    L = seq // dil
    rows = min(rows, L)
    nblk = L // rows
    hb = rows // A_HALF
    nh = L // A_HALF
    view = lambda a: a.reshape(batch, L, dil * A_WIDTH)
    cur = lambda b, r, i: (b, i, r)
    prev = lambda b, r, i: (b, jnp.maximum(i * hb - 1, 0), r)
    nxt = lambda b, r, i: (b, jnp.minimum((i + 1) * hb, nh - 1), r)
    cur_spec = pl.BlockSpec((1, rows, A_WIDTH), cur)
    out_sds = lambda dt: jax.ShapeDtypeStruct((batch, L, dil * A_WIDTH), dt)
    o, lse = pl.pallas_call(
        functools.partial(_dilated_kernel, rows=rows),
        grid=(batch, dil, nblk),
        in_specs=[cur_spec, cur_spec,
                  pl.BlockSpec((1, A_HALF, A_WIDTH), prev), pl.BlockSpec((1, A_HALF, A_WIDTH), nxt),
                  cur_spec,
                  pl.BlockSpec((1, A_HALF, A_WIDTH), prev), pl.BlockSpec((1, A_HALF, A_WIDTH), nxt),
                  pl.BlockSpec(bias.shape, lambda b, r, i: (0, 0, 0))],
        out_specs=[cur_spec, cur_spec],
        out_shape=(out_sds(BF16), out_sds(F32)),
        scratch_shapes=[pltpu.VMEM((rows + 2 * A_HALF, A_WIDTH), BF16),
                        pltpu.VMEM((rows + 2 * A_HALF, A_WIDTH), BF16)],
        compiler_params=_cparams("parallel", "parallel", "parallel"),
        name=f"dilated_d{dil}",
    )(view(qa), view(ka), view(ka), view(ka), view(va), view(va), view(va), bias)
    return o.reshape(batch * L, dil * A_WIDTH), lse.reshape(batch * L, dil * A_WIDTH)


def _dilated_bias(dil):
    slopes = 2.0 ** (-8.0 * (np.arange(A_HEADS, dtype=np.float64) + 1.0) / A_HEADS)
    col = np.arange(A_WIN)[None, :]
    rel = (col - A_HALF) - np.arange(A_QB)[:, None]
    bias = -slopes[:, None, None] * (np.abs(rel) * dil).astype(np.float64)[None]
    bias = np.where((np.abs(rel) <= A_HALF)[None], bias, NEG)
    variants = []
    for v in range(4):
        ok = np.ones((1, A_WIN), bool)
        if v & 1:
            ok = np.logical_and(ok, col >= A_HALF)
        if v & 2:
            ok = np.logical_and(ok, col < A_HALF + A_QB)
        variants.append(np.where(ok[None], bias, NEG).reshape(A_HEADS // 2, 2 * A_QB, A_WIN))
    return jnp.asarray(np.concatenate(variants, axis=0), F32)


def _mla_kernel(q_ref, k_ref, vt_ref, o_ref, *s_bufs, tk):
    tq = q_ref.shape[1]
    nk = k_ref.shape[1] // tk
    nh = q_ref.shape[2] // HEAD_PAD
    qs = [q_ref[0, :, hh * HEAD_PAD:(hh + 1) * HEAD_PAD] for hh in range(nh)]

    def scores(j, s_ref, hh, part=None):
        rows = tk if part is None else tk // 2
        r0 = 0 if part is None else part * rows
        off = pl.multiple_of(j * tk, tk) + r0
        s_ref[hh, r0:r0 + rows, :] = lax.dot_general(
            k_ref[0, pl.ds(off, rows), hh * HEAD_PAD:(hh + 1) * HEAD_PAD], qs[hh],
            (((1,), (1,)), ((), ())), preferred_element_type=F32)

    ones = jnp.ones((ONES_ROWS, tk), BF16)

    def values(j, hh, alpha, p, acc):
        off = pl.multiple_of(j * tk, tk)
        vt = jnp.concatenate([vt_ref[hh * B_V:(hh + 1) * B_V, pl.ds(off, tk)], ones], axis=0)
        return alpha * acc + jnp.dot(vt, p, preferred_element_type=F32)

    def stage(j, s_ref, j_next, s_next, stats):
        new = []
        for hh in range(nh):
            m, acc = stats[hh]
            scores(j_next, s_next, hh, 0)
            s = s_ref[hh]
            m_new = jnp.maximum(m, jnp.max(s, axis=0, keepdims=True))
            scores(j_next, s_next, hh, 1)
            p = jnp.exp2(s - m_new).astype(BF16)
            new.append((m_new, values(j, hh, jnp.exp2(m - m_new), p, acc)))
        return tuple(new)

    nbuf = len(s_bufs)
    trip = min(MLA_TRIP_CHUNKS, nk)

    def body(jj, stats):
        j = trip * jj
        for u in range(trip):
            stats = stage(j + u, s_bufs[u % nbuf], jnp.minimum(j + u + 1, nk - 1), s_bufs[(u + 1) % nbuf], stats)
        return stats

    init = (jnp.full((1, tq), NEG, F32), jnp.zeros((B_V + ONES_ROWS, tq), F32))
    for hh in range(nh):
        scores(0, s_bufs[0], hh)
    stats = lax.fori_loop(0, nk // trip, body, (init,) * nh)
    o_t = jnp.concatenate([a[:B_V] / a[B_V:B_V + 1] for _, a in stats], axis=0)
    o_ref[0] = o_t.T.astype(BF16)


def _mla_attention(qb, kb, vbt, batch, seq, tq=256, tk=512, nh=MLA_HEADS_PER_STEP):
    once = dict(pipeline_mode=pl.Buffered(1))
    return pl.pallas_call(
        functools.partial(_mla_kernel, tk=tk),
        grid=(batch, B_HEADS // nh, seq // tq),
        in_specs=[pl.BlockSpec((1, tq, nh * HEAD_PAD), lambda b, p, i: (b, i, p)),
                  pl.BlockSpec((1, seq, nh * HEAD_PAD), lambda b, p, i: (b, 0, p), **once),
                  pl.BlockSpec((nh * B_V, seq), lambda b, p, i: (p, b), **once)],
        out_specs=pl.BlockSpec((1, tq, nh * B_V), lambda b, p, i: (b, i, p)),
        out_shape=jax.ShapeDtypeStruct((batch, seq, B_WIDTH), BF16),
        scratch_shapes=[pltpu.VMEM((nh, tk, tq), F32) for _ in range(MLA_SCORE_BUFS)],
        compiler_params=_cparams("parallel", "parallel", "parallel"),
        name="mla_attention",
    )(qb, kb, vbt)


def _outproj_kernel(o1_ref, o2_ref, o3_ref, l1_ref, l2_ref, l3_ref, ob_ref, gate_ref, x_ref,
                    wa_ref, wb_ref, wo_ref, g2_ref, wrh_ref, wrl_ref, br_ref,
                    tri_ref, x1_ref, h2_ref, route_ref, count_ref, perm_ref, perm4_ref, carry_ref):
    tm = x_ref.shape[0]

    @pl.when(pl.program_id(0) == 0)
    def _():
        carry_ref[...] = jnp.zeros_like(carry_ref)

    nc = A_WIDTH // 128

    def lanes(r, c):
        return slice(r * A_WIDTH + c * 128, r * A_WIDTH + (c + 1) * 128)

    def token_order(which, ref, d):
        if d == 4:
            for r in range(4):
                for c in range(nc):
                    perm_ref[which, c, pl.ds(r, tm // 4, stride=4), :] = ref[:, lanes(r, c)].astype(F32)
        else:
            for r1 in range(4):
                for r2 in range(4):
                    for c in range(nc):
                        perm4_ref[which - 2, r1, c, pl.ds(r2, tm // 16, stride=4), :] = (
                            ref[:, lanes(4 * r2 + r1, c)].astype(F32))
            for r1 in range(4):
                for c in range(nc):
                    perm_ref[which, c, pl.ds(r1, tm // 4, stride=4), :] = perm4_ref[which - 2, r1, c]
        return jnp.concatenate([perm_ref[which, c] for c in range(nc)], axis=1)

    l1 = l1_ref[...]
    l2, o2 = token_order(0, l2_ref, 4), token_order(1, o2_ref, 4)
    l3, o3 = token_order(2, l3_ref, 16), token_order(3, o3_ref, 16)
    mx = jnp.maximum(jnp.maximum(l1, l2), l3)
    e1, e2, e3 = jnp.exp(l1 - mx), jnp.exp(l2 - mx), jnp.exp(l3 - mx)
    oa = (e1 * o1_ref[...] + e2 * o2 + e3 * o3) / (e1 + e2 + e3)
    ya = jnp.dot(oa.astype(BF16), wa_ref[...], preferred_element_type=F32)
    yb = jnp.dot(ob_ref[...], wb_ref[...], preferred_element_type=F32)
    merged = gate_ref[:, :D_MODEL].astype(F32) * ya + gate_ref[:, D_MODEL:].astype(F32) * yb
    x1 = x_ref[...] + jnp.dot(merged.astype(BF16), wo_ref[...], preferred_element_type=F32)
    x1_ref[...] = x1
    h2 = x1 * lax.rsqrt(jnp.mean(x1 * x1, axis=-1, keepdims=True) + EPS) * g2_ref[...]
    hi = h2.astype(BF16)
    h2_ref[...] = _pack_rows(h2)
    lo = (h2 - hi.astype(F32)).astype(BF16)
    logits = jnp.dot(hi, wrh_ref[...], preferred_element_type=F32)
    logits = logits + jnp.dot(lo, wrh_ref[...], preferred_element_type=F32)
    logits = logits + jnp.dot(hi, wrl_ref[...], preferred_element_type=F32)
    logits = logits + br_ref[...]

    lane = lax.broadcasted_iota(jnp.int32, (tm, LOGIT_PAD), 1)
    work = logits
    sel = jnp.zeros((tm, LOGIT_PAD), F32)
    idx_cols, val_cols = [], []
    for _ in range(TOP_K):
        mx = jnp.max(work, axis=-1, keepdims=True)
        idx = jnp.min(jnp.where(work == mx, lane, LOGIT_PAD), axis=-1, keepdims=True)
        hit = lane == idx
        sel = sel + hit.astype(F32)
        work = jnp.where(hit, 2.0 * NEG, work)
        idx_cols.append(idx)
        val_cols.append(mx)
    exps = [jnp.exp(v - val_cols[0]) for v in val_cols]
    denom = exps[0] + exps[1] + exps[2] + exps[3]
    gate_cols = [e / denom for e in exps]

    before = jnp.dot(tri_ref[...], sel.astype(BF16), preferred_element_type=F32) + carry_ref[0:1, :]
    rank_cols = [jnp.sum(jnp.where(lane == idx, before, 0.0), axis=-1, keepdims=True) for idx in idx_cols]
    carry_ref[...] = carry_ref[...] + jnp.sum(sel, axis=0, keepdims=True)
    count_ref[...] = carry_ref[...]

    cols = [c.astype(F32) for c in idx_cols] + rank_cols + gate_cols
    route = jnp.zeros((tm, LOGIT_PAD), F32)
    for c, col in enumerate(cols):
        route = jnp.where(lane == c, col, route)
    route_ref[...] = route


def _outproj(o1, o2, o3, l1, l2, l3, ob, gates, x2, wa, wb, wo, g2, wrh, wrl, br, tm=512):
    T = x2.shape[0]
    row = lambda i: (i, 0)
    fix = lambda i: (0, 0)
    full = lambda a: pl.BlockSpec(a.shape, fix)
    rows = lambda a: pl.BlockSpec((tm * a.shape[0] // T, a.shape[1]), row)
    tri = (jnp.arange(tm)[:, None] > jnp.arange(tm)[None, :]).astype(BF16)
    out_shape = (jax.ShapeDtypeStruct((T, D_MODEL), F32),
                 jax.ShapeDtypeStruct((T, HALF_D), jnp.int32),
                 jax.ShapeDtypeStruct((T, LOGIT_PAD), F32),
                 jax.ShapeDtypeStruct((8, LOGIT_PAD), F32))
    return pl.pallas_call(
        _outproj_kernel,
        grid=(T // tm,),
        in_specs=[rows(o1), rows(o2), rows(o3), rows(l1), rows(l2), rows(l3), rows(ob), rows(gates),
                  rows(x2), full(wa), full(wb), full(wo), full(g2), full(wrh), full(wrl), full(br), full(tri)],
        out_specs=[pl.BlockSpec((tm, D_MODEL), row), pl.BlockSpec((tm, HALF_D), row),
                   pl.BlockSpec((tm, LOGIT_PAD), row), pl.BlockSpec((8, LOGIT_PAD), fix)],
        out_shape=out_shape,
        scratch_shapes=[pltpu.VMEM((4, A_WIDTH // 128, tm, 128), F32),
                        pltpu.VMEM((2, 4, A_WIDTH // 128, tm // 4, 128), F32),
                        pltpu.VMEM((8, LOGIT_PAD), F32)],
        compiler_params=_cparams("arbitrary"),
        name="outproj",
    )(o1, o2, o3, l1, l2, l3, ob, gates, x2, wa, wb, wo, g2, wrh, wrl, br, tri)


HALF_D = D_MODEL // 2


def _pack_rows(x):
    lo = pltpu.bitcast(x[:, :HALF_D].astype(BF16).astype(F32), jnp.uint32)
    hi = pltpu.bitcast(x[:, HALF_D:].astype(BF16).astype(F32), jnp.uint32)
    word = jnp.bitwise_or(lax.shift_right_logical(lo, jnp.uint32(16)), jnp.bitwise_and(hi, jnp.uint32(0xFFFF0000)))
    return pltpu.bitcast(word, jnp.int32)


def _unpack_rows(w):
    u = pltpu.bitcast(w, jnp.uint32)
    lo = pltpu.bitcast(lax.shift_left(u, jnp.uint32(16)), F32)
    hi = pltpu.bitcast(jnp.bitwise_and(u, jnp.uint32(0xFFFF0000)), F32)
    return jnp.concatenate([lo, hi], axis=1)


SC_WINDOW = 128


def _gather_rows(table, idx):
    M = idx.shape[0]
    width = table.shape[1]
    mesh = plsc.VectorSubcoreMesh(core_axis_name="core", subcore_axis_name="subcore")

    @functools.partial(pl.kernel, out_type=jax.ShapeDtypeStruct((M, width), table.dtype), mesh=mesh,
                       name="gather_rows")
    def gather(table_hbm, idx_hbm, out_hbm):
        def body(idx_vmem, out_vmem):
            pltpu.sync_copy(table_hbm.at[idx_vmem.at[0]], out_vmem)

        pltpu.emit_pipeline(
            body,
            grid=(M // SC_WINDOW,),
            in_specs=[pl.BlockSpec((1, SC_WINDOW), index_map=lambda i: (0, i))],
            out_specs=[pl.BlockSpec((SC_WINDOW, width), index_map=lambda i: (i, 0),
                                    pipeline_mode=pl.Buffered(1))],
            core_axis_name=("core", "subcore"),
            dimension_semantics=(pltpu.PARALLEL,),
        )(idx_hbm, out_hbm)

    return gather(table, idx.reshape(1, M))


def _scatter_rows(rows, idx, n_out):
    M = idx.shape[0]
    R, width = rows.shape
    mesh = plsc.VectorSubcoreMesh(core_axis_name="core", subcore_axis_name="subcore")

    @functools.partial(pl.kernel, out_type=jax.ShapeDtypeStruct((n_out, width), rows.dtype), mesh=mesh,
                       name="scatter_rows")
    def scatter(rows_hbm, idx_hbm, out_hbm):
        def body(rows_vmem, idx_vmem):
            pltpu.sync_copy(rows_vmem, out_hbm.at[idx_vmem.at[0]])

        pltpu.emit_pipeline(
            body,
            grid=(M // SC_WINDOW,),
            in_specs=[pl.BlockSpec((SC_WINDOW, width), index_map=lambda i: (i % (R // SC_WINDOW), 0),
                                   pipeline_mode=pl.Buffered(1)),
                      pl.BlockSpec((1, SC_WINDOW), index_map=lambda i: (0, i))],
            out_specs=[],
            core_axis_name=("core", "subcore"),
            dimension_semantics=(pltpu.PARALLEL,),
        )(rows_hbm, idx_hbm)

    return scatter(rows, idx.reshape(1, M))


def _moe_kernel(be_ref, nu_ref, nv_ref, x_ref, wgu_ref, bgu_ref, wd_ref, bd_ref, y_ref, wgu_bf, wd_bf):
    i = pl.program_id(0)

    @pl.when(i >= nu_ref[0])
    def _():
        y_ref[...] = jnp.zeros_like(y_ref)

    @pl.when(jnp.logical_or(i == 0, be_ref[i] != be_ref[jnp.maximum(i - 1, 0)]))
    def _():
        rows = 128

        def cast(c, carry):
            r0 = pl.multiple_of(c * rows, rows)
            wgu_bf[pl.ds(r0, rows), :] = wgu_ref[0, pl.ds(r0, rows), :].astype(BF16)
            wd_bf[pl.ds(r0, rows), :] = wd_ref[0, pl.ds(r0, rows), :].astype(BF16)
            return carry

        lax.fori_loop(0, D_MODEL // rows, cast, 0)

    @pl.when(i < nu_ref[0])
    def _():
        live = lax.broadcasted_iota(jnp.int32, (x_ref.shape[0], 1), 0) < nv_ref[i]
        x = jnp.where(live, _unpack_rows(x_ref[...]), 0.0).astype(BF16)
        gu = jnp.dot(x, wgu_bf[...], preferred_element_type=F32) + bgu_ref[0]
        gate = jnp.minimum(gu[:, :D_MODEL], SWIGLU_LIMIT)
        up = jnp.clip(gu[:, D_MODEL:], -SWIGLU_LIMIT, SWIGLU_LIMIT)
        act = (up + 1.0) * (gate * jax.nn.sigmoid(SWIGLU_ALPHA * gate))
        y = jnp.dot(act.astype(BF16), wd_bf[...], preferred_element_type=F32) + bd_ref[0]
        y_ref[...] = _pack_rows(y)


def _moe_experts(xb, blk_e, n_used, n_valid, wgu, bgu, wd, bd, bm):
    NP = xb.shape[0]
    nblk = NP // bm
    grid_spec = pltpu.PrefetchScalarGridSpec(
        num_scalar_prefetch=3,
        grid=(nblk,),
        in_specs=[pl.BlockSpec((bm, HALF_D), lambda i, be, nu, nv: (jnp.minimum(i, nu[0] - 1), 0)),
                  pl.BlockSpec((1, D_MODEL, 2 * D_MODEL), lambda i, be, nu, nv: (be[i], 0, 0)),
                  pl.BlockSpec((1, 1, 2 * D_MODEL), lambda i, be, nu, nv: (be[i], 0, 0)),
                  pl.BlockSpec((1, D_MODEL, D_MODEL), lambda i, be, nu, nv: (be[i], 0, 0)),
                  pl.BlockSpec((1, 1, D_MODEL), lambda i, be, nu, nv: (be[i], 0, 0))],
        out_specs=pl.BlockSpec((bm, HALF_D), lambda i, be, nu, nv: (i, 0)),
        scratch_shapes=[pltpu.VMEM((D_MODEL, 2 * D_MODEL), BF16), pltpu.VMEM((D_MODEL, D_MODEL), BF16)],
    )
    return pl.pallas_call(
        _moe_kernel,
        grid_spec=grid_spec,
        out_shape=jax.ShapeDtypeStruct((NP, HALF_D), jnp.int32),
        compiler_params=_cparams("arbitrary"),
        name="moe_experts",
    )(blk_e, n_used, n_valid, xb, wgu, bgu, wd, bd)


def _final_kernel(x1_ref, y_ref, route_ref, g_ref, o_ref):
    w = route_ref[...]
    acc = x1_ref[...]
    for k in range(TOP_K):
        acc = acc + w[:, 2 * TOP_K + k:2 * TOP_K + k + 1] * _unpack_rows(y_ref[k])
    o_ref[...] = acc * lax.rsqrt(jnp.mean(acc * acc, axis=-1, keepdims=True) + EPS) * g_ref[...]


def _final(x1, yk, route, g, tm=512):
    T = x1.shape[0]
    return pl.pallas_call(
        _final_kernel,
        grid=(T // tm,),
        in_specs=[pl.BlockSpec((tm, D_MODEL), lambda i: (i, 0)),
                  pl.BlockSpec((TOP_K, tm, HALF_D), lambda i: (0, i, 0)),
                  pl.BlockSpec((tm, LOGIT_PAD), lambda i: (i, 0)),
                  pl.BlockSpec((1, D_MODEL), lambda i: (0, 0))],
        out_specs=pl.BlockSpec((tm, D_MODEL), lambda i: (i, 0)),
        out_shape=jax.ShapeDtypeStruct((T, D_MODEL), F32),
        compiler_params=_cparams("parallel"),
        name="final_norm",
    )(x1, yk, route, g)


def _dest_kernel(route_ref, pstart_ref, dest_ref):
    tm = route_ref.shape[0]
    route = route_ref[...]
    lane = lax.broadcasted_iota(jnp.int32, (tm, LOGIT_PAD), 1)
    expert = jnp.where(lane < TOP_K, route, 0.0).astype(jnp.int32)
    start = jnp.take_along_axis(jnp.broadcast_to(pstart_ref[...], (tm, LOGIT_PAD)), expert, axis=1)
    position = pltpu.roll(route, LOGIT_PAD - TOP_K, axis=1)
    rows = start + position
    dest_ref[...] = rows.T[:TOP_K].astype(jnp.int32)


def _dest_rows(route, pstart, tm=2048):
    T = route.shape[0]
    return pl.pallas_call(
        _dest_kernel,
        grid=(T // tm,),
        in_specs=[pl.BlockSpec((tm, LOGIT_PAD), lambda i: (i, 0)),
                  pl.BlockSpec((1, LOGIT_PAD), lambda i: (0, 0))],
        out_specs=pl.BlockSpec((TOP_K, tm), lambda i: (0, i)),
        out_shape=jax.ShapeDtypeStruct((TOP_K, T), jnp.int32),
        compiler_params=_cparams("parallel"),
        name="dest_rows",
    )(route, pstart)


def _pack_w_in(w):
    kr = w[:, 1920:1952]
    half = B_ROPE // 2
    z = lambda n: jnp.zeros((w.shape[0], n), w.dtype)
    kr_placed = jnp.concatenate([z(B_NOPE), kr, z(HEAD_PAD - B_NOPE - B_ROPE)], axis=1)
    kr_rot = jnp.concatenate([z(B_NOPE), -kr[:, half:], kr[:, :half], z(HEAD_PAD - B_NOPE - B_ROPE)], axis=1)
    return jnp.concatenate([w[:, :1920], w[:, 1952:4000], kr_placed, kr_rot], axis=1).astype(BF16)


def _pack_w_uq(w):
    half = B_ROPE // 2
    r = w.shape[0]
    nope, x1, x2 = w[..., :B_NOPE], w[..., B_NOPE:B_NOPE + half], w[..., B_NOPE + half:]
    zpad = jnp.zeros((r, B_HEADS, HEAD_PAD - B_NOPE - B_ROPE), w.dtype)
    main = jnp.concatenate([nope, x1, x2, zpad], axis=-1)
    rot = jnp.concatenate([jnp.zeros_like(nope), -x2, x1, zpad], axis=-1)
    return (main.reshape(r, B_HEADS * HEAD_PAD).astype(BF16), rot.reshape(r, B_HEADS * HEAD_PAD).astype(BF16))


def _pack_w_uk(w):
    r = w.shape[0]
    zpad = jnp.zeros((r, B_HEADS, HEAD_PAD - B_NOPE), w.dtype)
    return jnp.concatenate([w, zpad], axis=-1).reshape(r, B_HEADS * HEAD_PAD).astype(BF16)


def _rope_tables(seq):
    inv_freq = ROPE_THETA ** (-np.arange(0, B_ROPE, 2, dtype=np.float64) / B_ROPE)
    ang = np.arange(seq, dtype=np.float64)[:, None] * inv_freq[None, :]
    cos, sin = jnp.asarray(np.cos(ang), F32), jnp.asarray(np.sin(ang), F32)
    one = jnp.ones((seq, B_NOPE), F32)
    zero = jnp.zeros((seq, B_NOPE), F32)
    zpad = jnp.zeros((seq, HEAD_PAD - B_NOPE - B_ROPE), F32)
    qs = ((B_NOPE + B_ROPE) ** -0.5) * LOG2E
    cq = jnp.concatenate([one, cos, cos, zpad], axis=1) * qs
    sq = jnp.concatenate([zero, sin, sin, zpad], axis=1) * qs
    ck = jnp.concatenate([zero, cos, cos, zpad], axis=1)
    sk = jnp.concatenate([zero, sin, sin, zpad], axis=1)
    return cq, sq, ck, sk


def kernel(x, norm1_g, w_in, q_a_norm_g, kv_a_norm_g, w_uq, w_uk, w_uv, w_a_out, w_b_out, w_o,
           norm2_g, w_router, b_router, w_gate_up, b_gate_up, w_down, b_down, final_g):
    B, S, D = x.shape
    T = B * S
    depth = norm1_g.shape[0]
    xt = x.reshape(T, D)
    cq_t, sq_t, ck_t, sk_t = _rope_tables(S)
    biases = [_dilated_bias(d) for _, d in A_PATTERNS]
    bm = 512
    for l in range(depth):
        wq1, wq2 = _pack_w_uq(w_uq[l])
        qa, ka, va, qa4, ka4, va4, qa16, ka16, va16, qb, kb, vb, gates = _inproj(
            xt, norm1_g[l][None], _pack_w_in(w_in[l]), q_a_norm_g[l][None], kv_a_norm_g[l][None],
            wq1, wq2, _pack_w_uk(w_uk[l]), w_uv[l].reshape(KV_LORA, B_WIDTH).T.astype(BF16),
            cq_t, sq_t, ck_t, sk_t, S)
        qkv = ((qa, ka, va), (qa4, ka4, va4), (qa16, ka16, va16))
        outs = [_dilated_pattern(*qkv[p], biases[p], B, S, A_PATTERNS[p][1]) for p in range(3)]
        ob = _mla_attention(qb.reshape(B, S, -1), kb.reshape(B, S, -1), vb, B, S)

        wr = jnp.pad(w_router[l], ((0, 0), (0, LOGIT_PAD - N_EXPERTS)))
        wrh = wr.astype(BF16)
        wrl = (wr - wrh.astype(F32)).astype(BF16)
        br = jnp.pad(b_router[l], (0, LOGIT_PAD - N_EXPERTS), constant_values=NEG)[None]
        x1, h2, route, count_rows = _outproj(
            outs[0][0], outs[1][0], outs[2][0], outs[0][1], outs[1][1], outs[2][1],
            ob.reshape(T, B_WIDTH), gates, xt,
            w_a_out[l].astype(BF16), w_b_out[l].astype(BF16), w_o[l].astype(BF16),
            norm2_g[l][None], wrh, wrl, br)

        counts = count_rows[0, :N_EXPERTS].astype(jnp.int32)
        padded = (counts + bm - 1) // bm * bm
        pend = jnp.cumsum(padded)
        pstart = pend - padded
        NP = T * TOP_K + N_EXPERTS * bm
        nblk = NP // bm
        blk_start = jnp.arange(nblk, dtype=jnp.int32) * bm
        blk_e = jnp.minimum(jnp.sum(pend[None, :] <= blk_start[:, None], axis=1), N_EXPERTS - 1).astype(jnp.int32)
        n_used = (pend[-1] // bm).astype(jnp.int32)[None]
        n_valid = jnp.clip(pstart[blk_e] + counts[blk_e] - blk_start, 0, bm).astype(jnp.int32)

        pstart_row = jnp.pad(pstart.astype(F32), (0, LOGIT_PAD - N_EXPERTS))[None]
        dest_kt = _dest_rows(route, pstart_row).reshape(-1)
        xb = _scatter_rows(h2, dest_kt, NP)
        yb = _moe_experts(xb, blk_e, n_used, n_valid, w_gate_up[l], b_gate_up[l][:, None, :],
                          w_down[l], b_down[l][:, None, :], bm)
        yk = _gather_rows(yb, dest_kt).reshape(TOP_K, T, HALF_D)
        if l + 1 < depth:
            raise NotImplementedError("depth > 1")
        out = _final(x1, yk, route, final_g[None])
    return out.reshape(B, S, D)
```

```python
import functools

import jax
import jax.numpy as jnp
import numpy as np
from jax import lax
from jax.experimental import pallas as pl
from jax.experimental.pallas import tpu as pltpu
from jax.experimental.pallas import tpu_sc as plsc

F32 = jnp.float32
BF16 = jnp.bfloat16

D_MODEL = 1024
EPS = 1e-5
NEG = -1e30

A_HEADS = 8
A_HEAD_DIM = 64
A_WIDTH = 512
A_PATTERNS = ((128, 1), (512, 4), (2048, 16))
A_HALF = 64
A_QB = 128
A_WIN = A_QB + 2 * A_HALF
DILATED_BLOCKS_PER_TRIP = 4
assert all(w // (2 * d) == A_HALF for w, d in A_PATTERNS) and [d for _, d in A_PATTERNS] == [1, 4, 16]

B_HEADS = 8
B_NOPE = 64
B_ROPE = 32
B_V = 64
B_WIDTH = 512
Q_LORA = 256
KV_LORA = 128
ROPE_THETA = 10000.0
HEAD_PAD = 128
ONES_ROWS = 16
MLA_SCORE_BUFS = 4
MLA_TRIP_CHUNKS = 16
MLA_HEADS_PER_STEP = 4

N_EXPERTS = 32
TOP_K = 4
SWIGLU_LIMIT = 7.0
SWIGLU_ALPHA = 1.702
LOGIT_PAD = 128

LOG2E = 1.4426950408889634

VMEM_LIMIT = 56 * 1024 * 1024


def _cparams(*sem):
    return pltpu.CompilerParams(dimension_semantics=sem, vmem_limit_bytes=VMEM_LIMIT)


def _inproj_kernel(x_ref, g1_ref, wp_ref, gq_ref, gkv_ref, wq1_ref, wq2_ref, wuk_ref, wuv_ref,
                   cq_ref, sq_ref, ck_ref, sk_ref,
                   qa_ref, ka_ref, va_ref, qa4_ref, ka4_ref, va4_ref, qa16_ref, ka16_ref, va16_ref,
                   qb_ref, kb_ref, vb_ref, gate_ref, perm_ref, perm4_ref):
    tm = x_ref.shape[0]
    x = x_ref[...]
    h = x * lax.rsqrt(jnp.mean(x * x, axis=-1, keepdims=True) + EPS) * g1_ref[...]
    h = h.astype(BF16)

    def proj(c0, c1):
        return jnp.dot(h, wp_ref[:, c0:c1], preferred_element_type=F32)

    def emit(which, z, nat_ref, v4_ref, v16_ref):
        nat_ref[...] = z.astype(BF16)
        nc = A_WIDTH // 128
        for c in range(nc):
            perm_ref[which, c] = z[:, c * 128:(c + 1) * 128]
        for r1 in range(4):
            for c in range(nc):
                quarter = perm_ref[which, c, pl.ds(r1, tm // 4, stride=4), :]
                v4_ref[:, r1 * A_WIDTH + c * 128:r1 * A_WIDTH + (c + 1) * 128] = quarter.astype(BF16)
                perm4_ref[which, r1, c] = quarter
        for r1 in range(4):
            for r2 in range(4):
                r = 4 * r2 + r1
                for c in range(nc):
                    v16_ref[:, r * A_WIDTH + c * 128:r * A_WIDTH + (c + 1) * 128] = (
                        perm4_ref[which, r1, c, pl.ds(r2, tm // 16, stride=4), :].astype(BF16))

    emit(0, proj(0, 512) * (A_HEAD_DIM ** -0.5), qa_ref, qa4_ref, qa16_ref)
    emit(1, proj(512, 1024), ka_ref, ka4_ref, ka16_ref)
    emit(2, proj(1024, 1536), va_ref, va4_ref, va16_ref)

    cq = proj(1536, 1792)
    cq = cq * lax.rsqrt(jnp.mean(cq * cq, axis=-1, keepdims=True) + EPS) * gq_ref[...]
    cq = cq.astype(BF16)
    q_main = jnp.dot(cq, wq1_ref[...], preferred_element_type=F32)
    q_rot = jnp.dot(cq, wq2_ref[...], preferred_element_type=F32)
    cq_t = cq_ref[...]
    sq_t = sq_ref[...]
    for hd in range(B_HEADS):
        sl = slice(hd * HEAD_PAD, (hd + 1) * HEAD_PAD)
        qb_ref[:, sl] = (q_main[:, sl] * cq_t + q_rot[:, sl] * sq_t).astype(BF16)

    ckv = proj(1792, 1920)
    ckv = ckv * lax.rsqrt(jnp.mean(ckv * ckv, axis=-1, keepdims=True) + EPS) * gkv_ref[...]
    ckv = ckv.astype(BF16)
    k_nope = jnp.dot(ckv, wuk_ref[...], preferred_element_type=F32)
    vb_ref[...] = lax.dot_general(wuv_ref[...], ckv, (((1,), (1,)), ((), ())),
                                  preferred_element_type=F32).astype(BF16)
    k_rope = proj(3968, 4096) * ck_ref[...] + proj(4096, 4224) * sk_ref[...]
    for hd in range(B_HEADS):
        sl = slice(hd * HEAD_PAD, (hd + 1) * HEAD_PAD)
        kb_ref[:, sl] = (k_nope[:, sl] + k_rope).astype(BF16)

    gate_ref[...] = jax.nn.sigmoid(proj(1920, 3968)).astype(BF16)


def _inproj(x2, g1, wp, gq, gkv, wq1, wq2, wuk, wuv, cq_t, sq_t, ck_t, sk_t, seq, tm=512):
    T = x2.shape[0]
    nseq = seq // tm
    row = lambda i: (i, 0)
    fix = lambda i: (0, 0)
    tab = lambda i: (i % nseq, 0)
    full = lambda a: pl.BlockSpec(a.shape, fix)
    sds = lambda r, c: jax.ShapeDtypeStruct((r, c), BF16)
    out_shape = (
        sds(T, A_WIDTH), sds(T, A_WIDTH), sds(T, A_WIDTH),
        sds(T // 4, 4 * A_WIDTH), sds(T // 4, 4 * A_WIDTH), sds(T // 4, 4 * A_WIDTH),
        sds(T // 16, 16 * A_WIDTH), sds(T // 16, 16 * A_WIDTH), sds(T // 16, 16 * A_WIDTH),
        sds(T, B_HEADS * HEAD_PAD), sds(T, B_HEADS * HEAD_PAD),
        sds(B_WIDTH, T),
        sds(T, 2 * D_MODEL),
    )
    out_specs = [pl.BlockSpec((tm * s.shape[0] // T, s.shape[1]), row) for s in out_shape]
    out_specs[11] = pl.BlockSpec((B_WIDTH, tm), lambda i: (0, i))
    return pl.pallas_call(
        _inproj_kernel,
        grid=(T // tm,),
        in_specs=[pl.BlockSpec((tm, D_MODEL), row), full(g1), full(wp), full(gq), full(gkv),
                  full(wq1), full(wq2), full(wuk), full(wuv),
                  pl.BlockSpec((tm, HEAD_PAD), tab), pl.BlockSpec((tm, HEAD_PAD), tab),
                  pl.BlockSpec((tm, HEAD_PAD), tab), pl.BlockSpec((tm, HEAD_PAD), tab)],
        out_specs=out_specs,
        out_shape=out_shape,
        scratch_shapes=[pltpu.VMEM((3, A_WIDTH // 128, tm, 128), F32),
                        pltpu.VMEM((3, 4, A_WIDTH // 128, tm // 4, 128), F32)],
        compiler_params=_cparams("parallel"),
        name="inproj",
    )(x2, g1, wp, gq, gkv, wq1, wq2, wuk, wuv, cq_t, sq_t, ck_t, sk_t)


def _dilated_kernel(q_ref, kc_ref, kp_ref, kn_ref, vc_ref, vp_ref, vn_ref, bias_ref,
                    o_ref, lse_ref, kbuf, vbuf, *, rows):
    i = pl.program_id(2)
    last = pl.num_programs(2) - 1
    kbuf[0:A_HALF, :] = kp_ref[0]
    kbuf[A_HALF:A_HALF + rows, :] = kc_ref[0]
    kbuf[A_HALF + rows:, :] = kn_ref[0]
    vbuf[0:A_HALF, :] = vp_ref[0]
    vbuf[A_HALF:A_HALF + rows, :] = vc_ref[0]
    vbuf[A_HALF + rows:, :] = vn_ref[0]

    nj = rows // A_QB
    npair = A_HEADS // 2
    lane = lax.broadcasted_iota(jnp.int32, (A_QB, 128), 1)
    lo_half = lane < A_HEAD_DIM

    per_trip = DILATED_BLOCKS_PER_TRIP if nj % DILATED_BLOCKS_PER_TRIP == 0 else 1

    def body(jj, carry):
        work = []
        for b in range(per_trip):
            j = jj * per_trip + b
            r0 = pl.multiple_of(j * A_QB, A_QB)
            variant = (jnp.logical_and(i == 0, j == 0).astype(jnp.int32)
                       + 2 * jnp.logical_and(i == last, j == nj - 1).astype(jnp.int32))
            work.extend((r0, variant, pr) for pr in range(npair))
        scores = []
        for r0, variant, pr in work:
            ls = slice(pr * 128, (pr + 1) * 128)
            qp = q_ref[0, pl.ds(r0, A_QB), ls]
            zero = jnp.zeros_like(qp)
            q2 = jnp.concatenate([jnp.where(lo_half, qp, zero), jnp.where(lo_half, zero, qp)], axis=0)
            s = lax.dot_general(q2, kbuf[pl.ds(r0, A_WIN), ls], (((1,), (1,)), ((), ())),
                                preferred_element_type=F32)
            scores.append(s + bias_ref[variant * npair + pr])
        probs = []
        for s in scores:
            m = jnp.max(s, axis=-1, keepdims=True)
            p = jnp.exp(s - m)
            l = jnp.sum(p, axis=-1, keepdims=True)
            probs.append((p.astype(BF16), l, m + jnp.log(l)))
        for (r0, _, pr), (p, l, lse) in zip(work, probs):
            ls = slice(pr * 128, (pr + 1) * 128)
            o = jnp.dot(p, vbuf[pl.ds(r0, A_WIN), ls], preferred_element_type=F32) / l
            lse = jnp.broadcast_to(lse, (2 * A_QB, 128))
            o_ref[0, pl.ds(r0, A_QB), ls] = jnp.where(lo_half, o[:A_QB], o[A_QB:]).astype(o_ref.dtype)
            lse_ref[0, pl.ds(r0, A_QB), ls] = jnp.where(lo_half, lse[:A_QB], lse[A_QB:])
        return carry

    lax.fori_loop(0, nj // per_trip, body, 0)


def _dilated_pattern(qa, ka, va, bias, batch, seq, dil, rows=1024):
    L = seq // dil
    rows = min(rows, L)
    nblk = L // rows
    hb = rows // A_HALF
    nh = L // A_HALF
    view = lambda a: a.reshape(batch, L, dil * A_WIDTH)
    cur = lambda b, r, i: (b, i, r)
    prev = lambda b, r, i: (b, jnp.maximum(i * hb - 1, 0), r)
    nxt = lambda b, r, i: (b, jnp.minimum((i + 1) * hb, nh - 1), r)
    cur_spec = pl.BlockSpec((1, rows, A_WIDTH), cur)
    out_sds = lambda dt: jax.ShapeDtypeStruct((batch, L, dil * A_WIDTH), dt)
    o, lse = pl.pallas_call(
        functools.partial(_dilated_kernel, rows=rows),
        grid=(batch, dil, nblk),
        in_specs=[cur_spec, cur_spec,
                  pl.BlockSpec((1, A_HALF, A_WIDTH), prev), pl.BlockSpec((1, A_HALF, A_WIDTH), nxt),
                  cur_spec,
                  pl.BlockSpec((1, A_HALF, A_WIDTH), prev), pl.BlockSpec((1, A_HALF, A_WIDTH), nxt),
                  pl.BlockSpec(bias.shape, lambda b, r, i: (0, 0, 0))],
        out_specs=[cur_spec, cur_spec],
        out_shape=(out_sds(BF16), out_sds(F32)),
        scratch_shapes=[pltpu.VMEM((rows + 2 * A_HALF, A_WIDTH), BF16),
                        pltpu.VMEM((rows + 2 * A_HALF, A_WIDTH), BF16)],
        compiler_params=_cparams("parallel", "parallel", "parallel"),
        name=f"dilated_d{dil}",
    )(view(qa), view(ka), view(ka), view(ka), view(va), view(va), view(va), bias)
    return o.reshape(batch * L, dil * A_WIDTH), lse.reshape(batch * L, dil * A_WIDTH)


def _dilated_bias(dil):
    slopes = 2.0 ** (-8.0 * (np.arange(A_HEADS, dtype=np.float64) + 1.0) / A_HEADS)
    col = np.arange(A_WIN)[None, :]
    rel = (col - A_HALF) - np.arange(A_QB)[:, None]
    bias = -slopes[:, None, None] * (np.abs(rel) * dil).astype(np.float64)[None]
    bias = np.where((np.abs(rel) <= A_HALF)[None], bias, NEG)
    variants = []
    for v in range(4):
        ok = np.ones((1, A_WIN), bool)
        if v & 1:
            ok = np.logical_and(ok, col >= A_HALF)
        if v & 2:
            ok = np.logical_and(ok, col < A_HALF + A_QB)
        variants.append(np.where(ok[None], bias, NEG).reshape(A_HEADS // 2, 2 * A_QB, A_WIN))
    return jnp.asarray(np.concatenate(variants, axis=0), F32)


def _mla_kernel(q_ref, k_ref, vt_ref, o_ref, *s_bufs, tk):
    tq = q_ref.shape[1]
    nk = k_ref.shape[1] // tk
    nh = q_ref.shape[2] // HEAD_PAD
    qs = [q_ref[0, :, hh * HEAD_PAD:(hh + 1) * HEAD_PAD] for hh in range(nh)]

    def scores(j, s_ref, hh):
        off = pl.multiple_of(j * tk, tk)
        s_ref[hh] = lax.dot_general(k_ref[0, pl.ds(off, tk), hh * HEAD_PAD:(hh + 1) * HEAD_PAD], qs[hh],
                                    (((1,), (1,)), ((), ())), preferred_element_type=F32)

    ones = jnp.ones((ONES_ROWS, tk), BF16)

    def values(j, hh, alpha, p, acc):
        off = pl.multiple_of(j * tk, tk)
        vt = jnp.concatenate([vt_ref[hh * B_V:(hh + 1) * B_V, pl.ds(off, tk)], ones], axis=0)
        return alpha * acc + jnp.dot(vt, p, preferred_element_type=F32)

    def stage(j, s_ref, j_next, s_next, stats):
        new = []
        for hh in range(nh):
            m, acc = stats[hh]
            scores(j_next, s_next, hh)
            s = s_ref[hh]
            m_new = jnp.maximum(m, jnp.max(s, axis=0, keepdims=True))
            p = jnp.exp2(s - m_new).astype(BF16)
            new.append((m_new, values(j, hh, jnp.exp2(m - m_new), p, acc)))
        return tuple(new)

    nbuf = len(s_bufs)
    trip = min(MLA_TRIP_CHUNKS, nk)

    def body(jj, stats):
        j = trip * jj
        for u in range(trip):
            stats = stage(j + u, s_bufs[u % nbuf], jnp.minimum(j + u + 1, nk - 1), s_bufs[(u + 1) % nbuf], stats)
        return stats

    init = (jnp.full((1, tq), NEG, F32), jnp.zeros((B_V + ONES_ROWS, tq), F32))
    for hh in range(nh):
        scores(0, s_bufs[0], hh)
    stats = lax.fori_loop(0, nk // trip, body, (init,) * nh)
    o_t = jnp.concatenate([a[:B_V] / a[B_V:B_V + 1] for _, a in stats], axis=0)
    o_ref[0] = o_t.T.astype(BF16)


def _mla_attention(qb, kb, vbt, batch, seq, tq=256, tk=512, nh=MLA_HEADS_PER_STEP):
    once = dict(pipeline_mode=pl.Buffered(1))
    return pl.pallas_call(
        functools.partial(_mla_kernel, tk=tk),
        grid=(batch, B_HEADS // nh, seq // tq),
        in_specs=[pl.BlockSpec((1, tq, nh * HEAD_PAD), lambda b, p, i: (b, i, p)),
                  pl.BlockSpec((1, seq, nh * HEAD_PAD), lambda b, p, i: (b, 0, p), **once),
                  pl.BlockSpec((nh * B_V, seq), lambda b, p, i: (p, b), **once)],
        out_specs=pl.BlockSpec((1, tq, nh * B_V), lambda b, p, i: (b, i, p)),
        out_shape=jax.ShapeDtypeStruct((batch, seq, B_WIDTH), BF16),
        scratch_shapes=[pltpu.VMEM((nh, tk, tq), F32) for _ in range(MLA_SCORE_BUFS)],
        compiler_params=_cparams("parallel", "parallel", "parallel"),
        name="mla_attention",
    )(qb, kb, vbt)


def _outproj_kernel(o1_ref, o2_ref, o3_ref, l1_ref, l2_ref, l3_ref, ob_ref, gate_ref, x_ref,
                    wa_ref, wb_ref, wo_ref, g2_ref, wrh_ref, wrl_ref, br_ref,
                    tri_ref, x1_ref, h2_ref, route_ref, count_ref, perm_ref, perm4_ref, carry_ref):
    tm = x_ref.shape[0]

    @pl.when(pl.program_id(0) == 0)
    def _():
        carry_ref[...] = jnp.zeros_like(carry_ref)

    nc = A_WIDTH // 128

    def lanes(r, c):
        return slice(r * A_WIDTH + c * 128, r * A_WIDTH + (c + 1) * 128)

    def token_order(which, ref, d):
        if d == 4:
            for r in range(4):
                for c in range(nc):
                    perm_ref[which, c, pl.ds(r, tm // 4, stride=4), :] = ref[:, lanes(r, c)].astype(F32)
        else:
            for r1 in range(4):
                for r2 in range(4):
                    for c in range(nc):
                        perm4_ref[which - 2, r1, c, pl.ds(r2, tm // 16, stride=4), :] = (
                            ref[:, lanes(4 * r2 + r1, c)].astype(F32))
            for r1 in range(4):
                for c in range(nc):
                    perm_ref[which, c, pl.ds(r1, tm // 4, stride=4), :] = perm4_ref[which - 2, r1, c]
        return jnp.concatenate([perm_ref[which, c] for c in range(nc)], axis=1)

    l1 = l1_ref[...]
    l2, o2 = token_order(0, l2_ref, 4), token_order(1, o2_ref, 4)
    l3, o3 = token_order(2, l3_ref, 16), token_order(3, o3_ref, 16)
    mx = jnp.maximum(jnp.maximum(l1, l2), l3)
    e1, e2, e3 = jnp.exp(l1 - mx), jnp.exp(l2 - mx), jnp.exp(l3 - mx)
    oa = (e1 * o1_ref[...] + e2 * o2 + e3 * o3) / (e1 + e2 + e3)
    ya = jnp.dot(oa.astype(BF16), wa_ref[...], preferred_element_type=F32)
    yb = jnp.dot(ob_ref[...], wb_ref[...], preferred_element_type=F32)
    merged = gate_ref[:, :D_MODEL].astype(F32) * ya + gate_ref[:, D_MODEL:].astype(F32) * yb
    x1 = x_ref[...] + jnp.dot(merged.astype(BF16), wo_ref[...], preferred_element_type=F32)
    x1_ref[...] = x1
    h2 = x1 * lax.rsqrt(jnp.mean(x1 * x1, axis=-1, keepdims=True) + EPS) * g2_ref[...]
    hi = h2.astype(BF16)
    h2_ref[...] = _pack_rows(h2)
    lo = (h2 - hi.astype(F32)).astype(BF16)
    logits = jnp.dot(hi, wrh_ref[...], preferred_element_type=F32)
    logits = logits + jnp.dot(lo, wrh_ref[...], preferred_element_type=F32)
    logits = logits + jnp.dot(hi, wrl_ref[...], preferred_element_type=F32)
    logits = logits + br_ref[...]

    lane = lax.broadcasted_iota(jnp.int32, (tm, LOGIT_PAD), 1)
    work = logits
    sel = jnp.zeros((tm, LOGIT_PAD), F32)
    idx_cols, val_cols = [], []
    for _ in range(TOP_K):
        mx = jnp.max(work, axis=-1, keepdims=True)
        idx = jnp.min(jnp.where(work == mx, lane, LOGIT_PAD), axis=-1, keepdims=True)
        hit = lane == idx
        sel = sel + hit.astype(F32)
        work = jnp.where(hit, 2.0 * NEG, work)
        idx_cols.append(idx)
        val_cols.append(mx)
    exps = [jnp.exp(v - val_cols[0]) for v in val_cols]
    denom = exps[0] + exps[1] + exps[2] + exps[3]
    gate_cols = [e / denom for e in exps]

    before = jnp.dot(tri_ref[...], sel.astype(BF16), preferred_element_type=F32) + carry_ref[0:1, :]
    rank_cols = [jnp.sum(jnp.where(lane == idx, before, 0.0), axis=-1, keepdims=True) for idx in idx_cols]
    carry_ref[...] = carry_ref[...] + jnp.sum(sel, axis=0, keepdims=True)
    count_ref[...] = carry_ref[...]

    cols = [c.astype(F32) for c in idx_cols] + rank_cols + gate_cols
    route = jnp.zeros((tm, LOGIT_PAD), F32)
    for c, col in enumerate(cols):
        route = jnp.where(lane == c, col, route)
    route_ref[...] = route


def _outproj(o1, o2, o3, l1, l2, l3, ob, gates, x2, wa, wb, wo, g2, wrh, wrl, br, tm=512):
    T = x2.shape[0]
    row = lambda i: (i, 0)
    fix = lambda i: (0, 0)
    full = lambda a: pl.BlockSpec(a.shape, fix)
    rows = lambda a: pl.BlockSpec((tm * a.shape[0] // T, a.shape[1]), row)
    tri = (jnp.arange(tm)[:, None] > jnp.arange(tm)[None, :]).astype(BF16)
    out_shape = (jax.ShapeDtypeStruct((T, D_MODEL), F32),
                 jax.ShapeDtypeStruct((T, HALF_D), jnp.int32),
                 jax.ShapeDtypeStruct((T, LOGIT_PAD), F32),
                 jax.ShapeDtypeStruct((8, LOGIT_PAD), F32))
    return pl.pallas_call(
        _outproj_kernel,
        grid=(T // tm,),
        in_specs=[rows(o1), rows(o2), rows(o3), rows(l1), rows(l2), rows(l3), rows(ob), rows(gates),
                  rows(x2), full(wa), full(wb), full(wo), full(g2), full(wrh), full(wrl), full(br), full(tri)],
        out_specs=[pl.BlockSpec((tm, D_MODEL), row), pl.BlockSpec((tm, HALF_D), row),
                   pl.BlockSpec((tm, LOGIT_PAD), row), pl.BlockSpec((8, LOGIT_PAD), fix)],
        out_shape=out_shape,
        scratch_shapes=[pltpu.VMEM((4, A_WIDTH // 128, tm, 128), F32),
                        pltpu.VMEM((2, 4, A_WIDTH // 128, tm // 4, 128), F32),
                        pltpu.VMEM((8, LOGIT_PAD), F32)],
        compiler_params=_cparams("arbitrary"),
        name="outproj",
    )(o1, o2, o3, l1, l2, l3, ob, gates, x2, wa, wb, wo, g2, wrh, wrl, br, tri)


HALF_D = D_MODEL // 2


def _pack_rows(x):
    lo = pltpu.bitcast(x[:, :HALF_D].astype(BF16).astype(F32), jnp.uint32)
    hi = pltpu.bitcast(x[:, HALF_D:].astype(BF16).astype(F32), jnp.uint32)
    word = jnp.bitwise_or(lax.shift_right_logical(lo, jnp.uint32(16)), jnp.bitwise_and(hi, jnp.uint32(0xFFFF0000)))
    return pltpu.bitcast(word, jnp.int32)


def _unpack_rows(w):
    u = pltpu.bitcast(w, jnp.uint32)
    lo = pltpu.bitcast(lax.shift_left(u, jnp.uint32(16)), F32)
    hi = pltpu.bitcast(jnp.bitwise_and(u, jnp.uint32(0xFFFF0000)), F32)
    return jnp.concatenate([lo, hi], axis=1)


SC_WINDOW = 128


def _gather_rows(table, idx):
    M = idx.shape[0]
    width = table.shape[1]
    mesh = plsc.VectorSubcoreMesh(core_axis_name="core", subcore_axis_name="subcore")

    @functools.partial(pl.kernel, out_type=jax.ShapeDtypeStruct((M, width), table.dtype), mesh=mesh,
                       name="gather_rows")
    def gather(table_hbm, idx_hbm, out_hbm):
        def body(idx_vmem, out_vmem):
            pltpu.sync_copy(table_hbm.at[idx_vmem.at[0]], out_vmem)

        pltpu.emit_pipeline(
            body,
            grid=(M // SC_WINDOW,),
            in_specs=[pl.BlockSpec((1, SC_WINDOW), index_map=lambda i: (0, i))],
            out_specs=[pl.BlockSpec((SC_WINDOW, width), index_map=lambda i: (i, 0),
                                    pipeline_mode=pl.Buffered(1))],
            core_axis_name=("core", "subcore"),
            dimension_semantics=(pltpu.PARALLEL,),
        )(idx_hbm, out_hbm)

    return gather(table, idx.reshape(1, M))


def _scatter_rows(rows, idx, n_out):
    M = idx.shape[0]
    R, width = rows.shape
    mesh = plsc.VectorSubcoreMesh(core_axis_name="core", subcore_axis_name="subcore")

    @functools.partial(pl.kernel, out_type=jax.ShapeDtypeStruct((n_out, width), rows.dtype), mesh=mesh,
                       name="scatter_rows")
    def scatter(rows_hbm, idx_hbm, out_hbm):
        def body(rows_vmem, idx_vmem):
            pltpu.sync_copy(rows_vmem, out_hbm.at[idx_vmem.at[0]])

        pltpu.emit_pipeline(
            body,
            grid=(M // SC_WINDOW,),
            in_specs=[pl.BlockSpec((SC_WINDOW, width), index_map=lambda i: (i % (R // SC_WINDOW), 0),
                                   pipeline_mode=pl.Buffered(1)),
                      pl.BlockSpec((1, SC_WINDOW), index_map=lambda i: (0, i))],
            out_specs=[],
            core_axis_name=("core", "subcore"),
            dimension_semantics=(pltpu.PARALLEL,),
        )(rows_hbm, idx_hbm)

    return scatter(rows, idx.reshape(1, M))


def _moe_kernel(be_ref, nu_ref, nv_ref, x_ref, wgu_ref, bgu_ref, wd_ref, bd_ref, y_ref, wgu_bf, wd_bf):
    i = pl.program_id(0)

    @pl.when(i >= nu_ref[0])
    def _():
        y_ref[...] = jnp.zeros_like(y_ref)

    @pl.when(jnp.logical_or(i == 0, be_ref[i] != be_ref[jnp.maximum(i - 1, 0)]))
    def _():
        rows = 128

        def cast(c, carry):
            r0 = pl.multiple_of(c * rows, rows)
            wgu_bf[pl.ds(r0, rows), :] = wgu_ref[0, pl.ds(r0, rows), :].astype(BF16)
            wd_bf[pl.ds(r0, rows), :] = wd_ref[0, pl.ds(r0, rows), :].astype(BF16)
            return carry

        lax.fori_loop(0, D_MODEL // rows, cast, 0)

    @pl.when(i < nu_ref[0])
    def _():
        live = lax.broadcasted_iota(jnp.int32, (x_ref.shape[0], 1), 0) < nv_ref[i]
        x = jnp.where(live, _unpack_rows(x_ref[...]), 0.0).astype(BF16)
        gu = jnp.dot(x, wgu_bf[...], preferred_element_type=F32) + bgu_ref[0]
        gate = jnp.minimum(gu[:, :D_MODEL], SWIGLU_LIMIT)
        up = jnp.clip(gu[:, D_MODEL:], -SWIGLU_LIMIT, SWIGLU_LIMIT)
        act = (up + 1.0) * (gate * jax.nn.sigmoid(SWIGLU_ALPHA * gate))
        y = jnp.dot(act.astype(BF16), wd_bf[...], preferred_element_type=F32) + bd_ref[0]
        y_ref[...] = _pack_rows(y)


def _moe_experts(xb, blk_e, n_used, n_valid, wgu, bgu, wd, bd, bm):
    NP = xb.shape[0]
    nblk = NP // bm
    grid_spec = pltpu.PrefetchScalarGridSpec(
        num_scalar_prefetch=3,
        grid=(nblk,),
        in_specs=[pl.BlockSpec((bm, HALF_D), lambda i, be, nu, nv: (jnp.minimum(i, nu[0] - 1), 0)),
                  pl.BlockSpec((1, D_MODEL, 2 * D_MODEL), lambda i, be, nu, nv: (be[i], 0, 0)),
                  pl.BlockSpec((1, 1, 2 * D_MODEL), lambda i, be, nu, nv: (be[i], 0, 0)),
                  pl.BlockSpec((1, D_MODEL, D_MODEL), lambda i, be, nu, nv: (be[i], 0, 0)),
                  pl.BlockSpec((1, 1, D_MODEL), lambda i, be, nu, nv: (be[i], 0, 0))],
        out_specs=pl.BlockSpec((bm, HALF_D), lambda i, be, nu, nv: (i, 0)),
        scratch_shapes=[pltpu.VMEM((D_MODEL, 2 * D_MODEL), BF16), pltpu.VMEM((D_MODEL, D_MODEL), BF16)],
    )
    return pl.pallas_call(
        _moe_kernel,
        grid_spec=grid_spec,
        out_shape=jax.ShapeDtypeStruct((NP, HALF_D), jnp.int32),
        compiler_params=_cparams("arbitrary"),
        name="moe_experts",
    )(blk_e, n_used, n_valid, xb, wgu, bgu, wd, bd)


def _final_kernel(x1_ref, y_ref, route_ref, g_ref, o_ref):
    w = route_ref[...]
    acc = x1_ref[...]
    for k in range(TOP_K):
        acc = acc + w[:, 2 * TOP_K + k:2 * TOP_K + k + 1] * _unpack_rows(y_ref[k])
    o_ref[...] = acc * lax.rsqrt(jnp.mean(acc * acc, axis=-1, keepdims=True) + EPS) * g_ref[...]


def _final(x1, yk, route, g, tm=512):
    T = x1.shape[0]
    return pl.pallas_call(
        _final_kernel,
        grid=(T // tm,),
        in_specs=[pl.BlockSpec((tm, D_MODEL), lambda i: (i, 0)),
                  pl.BlockSpec((TOP_K, tm, HALF_D), lambda i: (0, i, 0)),
                  pl.BlockSpec((tm, LOGIT_PAD), lambda i: (i, 0)),
                  pl.BlockSpec((1, D_MODEL), lambda i: (0, 0))],
        out_specs=pl.BlockSpec((tm, D_MODEL), lambda i: (i, 0)),
        out_shape=jax.ShapeDtypeStruct((T, D_MODEL), F32),
        compiler_params=_cparams("parallel"),
        name="final_norm",
    )(x1, yk, route, g)


def _dest_kernel(route_ref, pstart_ref, dest_ref):
    tm = route_ref.shape[0]
    route = route_ref[...]
    lane = lax.broadcasted_iota(jnp.int32, (tm, LOGIT_PAD), 1)
    expert = jnp.where(lane < TOP_K, route, 0.0).astype(jnp.int32)
    start = jnp.take_along_axis(jnp.broadcast_to(pstart_ref[...], (tm, LOGIT_PAD)), expert, axis=1)
    position = pltpu.roll(route, LOGIT_PAD - TOP_K, axis=1)
    rows = start + position
    dest_ref[...] = rows.T[:TOP_K].astype(jnp.int32)


def _dest_rows(route, pstart, tm=2048):
    T = route.shape[0]
    return pl.pallas_call(
        _dest_kernel,
        grid=(T // tm,),
        in_specs=[pl.BlockSpec((tm, LOGIT_PAD), lambda i: (i, 0)),
                  pl.BlockSpec((1, LOGIT_PAD), lambda i: (0, 0))],
        out_specs=pl.BlockSpec((TOP_K, tm), lambda i: (0, i)),
        out_shape=jax.ShapeDtypeStruct((TOP_K, T), jnp.int32),
        compiler_params=_cparams("parallel"),
        name="dest_rows",
    )(route, pstart)


def _pack_w_in(w):
    kr = w[:, 1920:1952]
    half = B_ROPE // 2
    z = lambda n: jnp.zeros((w.shape[0], n), w.dtype)
    kr_placed = jnp.concatenate([z(B_NOPE), kr, z(HEAD_PAD - B_NOPE - B_ROPE)], axis=1)
    kr_rot = jnp.concatenate([z(B_NOPE), -kr[:, half:], kr[:, :half], z(HEAD_PAD - B_NOPE - B_ROPE)], axis=1)
    return jnp.concatenate([w[:, :1920], w[:, 1952:4000], kr_placed, kr_rot], axis=1).astype(BF16)


def _pack_w_uq(w):
    half = B_ROPE // 2
    r = w.shape[0]
    nope, x1, x2 = w[..., :B_NOPE], w[..., B_NOPE:B_NOPE + half], w[..., B_NOPE + half:]
    zpad = jnp.zeros((r, B_HEADS, HEAD_PAD - B_NOPE - B_ROPE), w.dtype)
    main = jnp.concatenate([nope, x1, x2, zpad], axis=-1)
    rot = jnp.concatenate([jnp.zeros_like(nope), -x2, x1, zpad], axis=-1)
    return (main.reshape(r, B_HEADS * HEAD_PAD).astype(BF16), rot.reshape(r, B_HEADS * HEAD_PAD).astype(BF16))


def _pack_w_uk(w):
    r = w.shape[0]
    zpad = jnp.zeros((r, B_HEADS, HEAD_PAD - B_NOPE), w.dtype)
    return jnp.concatenate([w, zpad], axis=-1).reshape(r, B_HEADS * HEAD_PAD).astype(BF16)


def _rope_tables(seq):
    inv_freq = ROPE_THETA ** (-np.arange(0, B_ROPE, 2, dtype=np.float64) / B_ROPE)
    ang = np.arange(seq, dtype=np.float64)[:, None] * inv_freq[None, :]
    cos, sin = jnp.asarray(np.cos(ang), F32), jnp.asarray(np.sin(ang), F32)
    one = jnp.ones((seq, B_NOPE), F32)
    zero = jnp.zeros((seq, B_NOPE), F32)
    zpad = jnp.zeros((seq, HEAD_PAD - B_NOPE - B_ROPE), F32)
    qs = ((B_NOPE + B_ROPE) ** -0.5) * LOG2E
    cq = jnp.concatenate([one, cos, cos, zpad], axis=1) * qs
    sq = jnp.concatenate([zero, sin, sin, zpad], axis=1) * qs
    ck = jnp.concatenate([zero, cos, cos, zpad], axis=1)
    sk = jnp.concatenate([zero, sin, sin, zpad], axis=1)
    return cq, sq, ck, sk


def kernel(x, norm1_g, w_in, q_a_norm_g, kv_a_norm_g, w_uq, w_uk, w_uv, w_a_out, w_b_out, w_o,
           norm2_g, w_router, b_router, w_gate_up, b_gate_up, w_down, b_down, final_g):
    B, S, D = x.shape
    T = B * S
    depth = norm1_g.shape[0]
    xt = x.reshape(T, D)
    cq_t, sq_t, ck_t, sk_t = _rope_tables(S)
    biases = [_dilated_bias(d) for _, d in A_PATTERNS]
    bm = 512
    for l in range(depth):
        wq1, wq2 = _pack_w_uq(w_uq[l])
        qa, ka, va, qa4, ka4, va4, qa16, ka16, va16, qb, kb, vb, gates = _inproj(
            xt, norm1_g[l][None], _pack_w_in(w_in[l]), q_a_norm_g[l][None], kv_a_norm_g[l][None],
            wq1, wq2, _pack_w_uk(w_uk[l]), w_uv[l].reshape(KV_LORA, B_WIDTH).T.astype(BF16),
            cq_t, sq_t, ck_t, sk_t, S)
        qkv = ((qa, ka, va), (qa4, ka4, va4), (qa16, ka16, va16))
        outs = [_dilated_pattern(*qkv[p], biases[p], B, S, A_PATTERNS[p][1]) for p in range(3)]
        ob = _mla_attention(qb.reshape(B, S, -1), kb.reshape(B, S, -1), vb, B, S)

        wr = jnp.pad(w_router[l], ((0, 0), (0, LOGIT_PAD - N_EXPERTS)))
        wrh = wr.astype(BF16)
        wrl = (wr - wrh.astype(F32)).astype(BF16)
        br = jnp.pad(b_router[l], (0, LOGIT_PAD - N_EXPERTS), constant_values=NEG)[None]
        x1, h2, route, count_rows = _outproj(
            outs[0][0], outs[1][0], outs[2][0], outs[0][1], outs[1][1], outs[2][1],
            ob.reshape(T, B_WIDTH), gates, xt,
            w_a_out[l].astype(BF16), w_b_out[l].astype(BF16), w_o[l].astype(BF16),
            norm2_g[l][None], wrh, wrl, br)

        counts = count_rows[0, :N_EXPERTS].astype(jnp.int32)
        padded = (counts + bm - 1) // bm * bm
        pend = jnp.cumsum(padded)
        pstart = pend - padded
        NP = T * TOP_K + N_EXPERTS * bm
        nblk = NP // bm
        blk_start = jnp.arange(nblk, dtype=jnp.int32) * bm
        blk_e = jnp.minimum(jnp.sum(pend[None, :] <= blk_start[:, None], axis=1), N_EXPERTS - 1).astype(jnp.int32)
        n_used = (pend[-1] // bm).astype(jnp.int32)[None]
        n_valid = jnp.clip(pstart[blk_e] + counts[blk_e] - blk_start, 0, bm).astype(jnp.int32)

        pstart_row = jnp.pad(pstart.astype(F32), (0, LOGIT_PAD - N_EXPERTS))[None]
        dest_kt = _dest_rows(route, pstart_row).reshape(-1)
        xb = _scatter_rows(h2, dest_kt, NP)
        yb = _moe_experts(xb, blk_e, n_used, n_valid, w_gate_up[l], b_gate_up[l][:, None, :],
                          w_down[l], b_down[l][:, None, :], bm)
        yk = _gather_rows(yb, dest_kt).reshape(TOP_K, T, HALF_D)
        if l + 1 < depth:
            raise NotImplementedError("depth > 1")
        out = _final(x1, yk, route, final_g[None])
    return out.reshape(B, S, D)
```

```python
import functools

import jax
import jax.numpy as jnp
import numpy as np
from jax import lax
from jax.experimental import pallas as pl
from jax.experimental.pallas import tpu as pltpu
from jax.experimental.pallas import tpu_sc as plsc

F32 = jnp.float32
BF16 = jnp.bfloat16

D_MODEL = 1024
EPS = 1e-5
NEG = -1e30

A_HEADS = 8
A_HEAD_DIM = 64
A_WIDTH = 512
A_PATTERNS = ((128, 1), (512, 4), (2048, 16))
A_HALF = 64
A_QB = 128
A_WIN = A_QB + 2 * A_HALF
DILATED_BLOCKS_PER_TRIP = 4
assert all(w // (2 * d) == A_HALF for w, d in A_PATTERNS) and [d for _, d in A_PATTERNS] == [1, 4, 16]

B_HEADS = 8
B_NOPE = 64
B_ROPE = 32
B_V = 64
B_WIDTH = 512
Q_LORA = 256
KV_LORA = 128
ROPE_THETA = 10000.0
HEAD_PAD = 128
ONES_ROWS = 16
MLA_SCORE_BUFS = 4
MLA_TRIP_CHUNKS = 16
MLA_HEADS_PER_STEP = 4

N_EXPERTS = 32
TOP_K = 4
SWIGLU_LIMIT = 7.0
SWIGLU_ALPHA = 1.702
LOGIT_PAD = 128

LOG2E = 1.4426950408889634

VMEM_LIMIT = 56 * 1024 * 1024


def _cparams(*sem):
    return pltpu.CompilerParams(dimension_semantics=sem, vmem_limit_bytes=VMEM_LIMIT)


def _inproj_kernel(x_ref, g1_ref, wp_ref, gq_ref, gkv_ref, wq1_ref, wq2_ref, wuk_ref, wuv_ref,
                   cq_ref, sq_ref, ck_ref, sk_ref,
                   qa_ref, ka_ref, va_ref, qa4_ref, ka4_ref, va4_ref, qa16_ref, ka16_ref, va16_ref,
                   qb_ref, kb_ref, vb_ref, gate_ref, perm_ref, perm4_ref):
    tm = x_ref.shape[0]
    x = x_ref[...]
    h = x * lax.rsqrt(jnp.mean(x * x, axis=-1, keepdims=True) + EPS) * g1_ref[...]
    h = h.astype(BF16)

    def proj(c0, c1):
        return jnp.dot(h, wp_ref[:, c0:c1], preferred_element_type=F32)

    def emit(which, z, nat_ref, v4_ref, v16_ref):
        nat_ref[...] = z.astype(BF16)
        nc = A_WIDTH // 128
        for c in range(nc):
            perm_ref[which, c] = z[:, c * 128:(c + 1) * 128]
        for r1 in range(4):
            for c in range(nc):
                quarter = perm_ref[which, c, pl.ds(r1, tm // 4, stride=4), :]
                v4_ref[:, r1 * A_WIDTH + c * 128:r1 * A_WIDTH + (c + 1) * 128] = quarter.astype(BF16)
                perm4_ref[which, r1, c] = quarter
        for r1 in range(4):
            for r2 in range(4):
                r = 4 * r2 + r1
                for c in range(nc):
                    v16_ref[:, r * A_WIDTH + c * 128:r * A_WIDTH + (c + 1) * 128] = (
                        perm4_ref[which, r1, c, pl.ds(r2, tm // 16, stride=4), :].astype(BF16))

    emit(0, proj(0, 512) * (A_HEAD_DIM ** -0.5), qa_ref, qa4_ref, qa16_ref)
    emit(1, proj(512, 1024), ka_ref, ka4_ref, ka16_ref)
    emit(2, proj(1024, 1536), va_ref, va4_ref, va16_ref)

    cq = proj(1536, 1792)
    cq = cq * lax.rsqrt(jnp.mean(cq * cq, axis=-1, keepdims=True) + EPS) * gq_ref[...]
    cq = cq.astype(BF16)
    q_main = jnp.dot(cq, wq1_ref[...], preferred_element_type=F32)
    q_rot = jnp.dot(cq, wq2_ref[...], preferred_element_type=F32)
    cq_t = cq_ref[...]
    sq_t = sq_ref[...]
    for hd in range(B_HEADS):
        sl = slice(hd * HEAD_PAD, (hd + 1) * HEAD_PAD)
        qb_ref[:, sl] = (q_main[:, sl] * cq_t + q_rot[:, sl] * sq_t).astype(BF16)

    ckv = proj(1792, 1920)
    ckv = ckv * lax.rsqrt(jnp.mean(ckv * ckv, axis=-1, keepdims=True) + EPS) * gkv_ref[...]
    ckv = ckv.astype(BF16)
    k_nope = jnp.dot(ckv, wuk_ref[...], preferred_element_type=F32)
    vb_ref[...] = lax.dot_general(wuv_ref[...], ckv, (((1,), (1,)), ((), ())),
                                  preferred_element_type=F32).astype(BF16)
    k_rope = proj(3968, 4096) * ck_ref[...] + proj(4096, 4224) * sk_ref[...]
    for hd in range(B_HEADS):
        sl = slice(hd * HEAD_PAD, (hd + 1) * HEAD_PAD)
        kb_ref[:, sl] = (k_nope[:, sl] + k_rope).astype(BF16)

    gate_ref[...] = jax.nn.sigmoid(proj(1920, 3968)).astype(BF16)


def _inproj(x2, g1, wp, gq, gkv, wq1, wq2, wuk, wuv, cq_t, sq_t, ck_t, sk_t, seq, tm=512):
    T = x2.shape[0]
    nseq = seq // tm
    row = lambda i: (i, 0)
    fix = lambda i: (0, 0)
    tab = lambda i: (i % nseq, 0)
    full = lambda a: pl.BlockSpec(a.shape, fix)
    sds = lambda r, c: jax.ShapeDtypeStruct((r, c), BF16)
    out_shape = (
        sds(T, A_WIDTH), sds(T, A_WIDTH), sds(T, A_WIDTH),
        sds(T // 4, 4 * A_WIDTH), sds(T // 4, 4 * A_WIDTH), sds(T // 4, 4 * A_WIDTH),
        sds(T // 16, 16 * A_WIDTH), sds(T // 16, 16 * A_WIDTH), sds(T // 16, 16 * A_WIDTH),
        sds(T, B_HEADS * HEAD_PAD), sds(T, B_HEADS * HEAD_PAD),
        sds(B_WIDTH, T),
        sds(T, 2 * D_MODEL),
    )
    out_specs = [pl.BlockSpec((tm * s.shape[0] // T, s.shape[1]), row) for s in out_shape]
    out_specs[11] = pl.BlockSpec((B_WIDTH, tm), lambda i: (0, i))
    return pl.pallas_call(
        _inproj_kernel,
        grid=(T // tm,),
        in_specs=[pl.BlockSpec((tm, D_MODEL), row), full(g1), full(wp), full(gq), full(gkv),
                  full(wq1), full(wq2), full(wuk), full(wuv),
                  pl.BlockSpec((tm, HEAD_PAD), tab), pl.BlockSpec((tm, HEAD_PAD), tab),
                  pl.BlockSpec((tm, HEAD_PAD), tab), pl.BlockSpec((tm, HEAD_PAD), tab)],
        out_specs=out_specs,
        out_shape=out_shape,
        scratch_shapes=[pltpu.VMEM((3, A_WIDTH // 128, tm, 128), F32),
                        pltpu.VMEM((3, 4, A_WIDTH // 128, tm // 4, 128), F32)],
        compiler_params=_cparams("parallel"),
        name="inproj",
    )(x2, g1, wp, gq, gkv, wq1, wq2, wuk, wuv, cq_t, sq_t, ck_t, sk_t)


def _dilated_kernel(q_ref, kc_ref, kp_ref, kn_ref, vc_ref, vp_ref, vn_ref, bias_ref,
                    o_ref, lse_ref, kbuf, vbuf, *, rows):
    i = pl.program_id(2)
    last = pl.num_programs(2) - 1
    kbuf[0:A_HALF, :] = kp_ref[0]
    kbuf[A_HALF:A_HALF + rows, :] = kc_ref[0]
    kbuf[A_HALF + rows:, :] = kn_ref[0]
    vbuf[0:A_HALF, :] = vp_ref[0]
    vbuf[A_HALF:A_HALF + rows, :] = vc_ref[0]
    vbuf[A_HALF + rows:, :] = vn_ref[0]

    nj = rows // A_QB
    npair = A_HEADS // 2
    lane = lax.broadcasted_iota(jnp.int32, (A_QB, 128), 1)
    lo_half = lane < A_HEAD_DIM

    per_trip = DILATED_BLOCKS_PER_TRIP if nj % DILATED_BLOCKS_PER_TRIP == 0 else 1

    def body(jj, carry):
        work = []
        for b in range(per_trip):
            j = jj * per_trip + b
            r0 = pl.multiple_of(j * A_QB, A_QB)
            variant = (jnp.logical_and(i == 0, j == 0).astype(jnp.int32)
                       + 2 * jnp.logical_and(i == last, j == nj - 1).astype(jnp.int32))
            work.extend((r0, variant, pr) for pr in range(npair))
        scores = []
        for r0, variant, pr in work:
            ls = slice(pr * 128, (pr + 1) * 128)
            qp = q_ref[0, pl.ds(r0, A_QB), ls]
            zero = jnp.zeros_like(qp)
            q2 = jnp.concatenate([jnp.where(lo_half, qp, zero), jnp.where(lo_half, zero, qp)], axis=0)
            s = lax.dot_general(q2, kbuf[pl.ds(r0, A_WIN), ls], (((1,), (1,)), ((), ())),
                                preferred_element_type=F32)
            scores.append(s + bias_ref[variant * npair + pr])
        probs = []
        for s in scores:
            m = jnp.max(s, axis=-1, keepdims=True)
            p = jnp.exp(s - m)
            l = jnp.sum(p, axis=-1, keepdims=True)
            probs.append((p.astype(BF16), l, m + jnp.log(l)))
        for (r0, _, pr), (p, l, lse) in zip(work, probs):
            ls = slice(pr * 128, (pr + 1) * 128)
            o = jnp.dot(p, vbuf[pl.ds(r0, A_WIN), ls], preferred_element_type=F32) / l
            lse = jnp.broadcast_to(lse, (2 * A_QB, 128))
            o_ref[0, pl.ds(r0, A_QB), ls] = jnp.where(lo_half, o[:A_QB], o[A_QB:]).astype(o_ref.dtype)
            lse_ref[0, pl.ds(r0, A_QB), ls] = jnp.where(lo_half, lse[:A_QB], lse[A_QB:])
        return carry

    lax.fori_loop(0, nj // per_trip, body, 0)


def _dilated_pattern(qa, ka, va, bias, batch, seq, dil, rows=1024):
    L = seq // dil
    rows = min(rows, L)
    nblk = L // rows
    hb = rows // A_HALF
    nh = L // A_HALF
    view = lambda a: a.reshape(batch, L, dil * A_WIDTH)
    cur = lambda b, r, i: (b, i, r)
    prev = lambda b, r, i: (b, jnp.maximum(i * hb - 1, 0), r)
    nxt = lambda b, r, i: (b, jnp.minimum((i + 1) * hb, nh - 1), r)
    cur_spec = pl.BlockSpec((1, rows, A_WIDTH), cur)
    out_sds = lambda dt: jax.ShapeDtypeStruct((batch, L, dil * A_WIDTH), dt)
    o, lse = pl.pallas_call(
        functools.partial(_dilated_kernel, rows=rows),
        grid=(batch, dil, nblk),
        in_specs=[cur_spec, cur_spec,
                  pl.BlockSpec((1, A_HALF, A_WIDTH), prev), pl.BlockSpec((1, A_HALF, A_WIDTH), nxt),
                  cur_spec,
                  pl.BlockSpec((1, A_HALF, A_WIDTH), prev), pl.BlockSpec((1, A_HALF, A_WIDTH), nxt),
                  pl.BlockSpec(bias.shape, lambda b, r, i: (0, 0, 0))],
        out_specs=[cur_spec, cur_spec],
        out_shape=(out_sds(BF16), out_sds(F32)),
        scratch_shapes=[pltpu.VMEM((rows + 2 * A_HALF, A_WIDTH), BF16),
                        pltpu.VMEM((rows + 2 * A_HALF, A_WIDTH), BF16)],
        compiler_params=_cparams("parallel", "parallel", "parallel"),
        name=f"dilated_d{dil}",
    )(view(qa), view(ka), view(ka), view(ka), view(va), view(va), view(va), bias)
    return o.reshape(batch * L, dil * A_WIDTH), lse.reshape(batch * L, dil * A_WIDTH)


def _dilated_bias(dil):
    slopes = 2.0 ** (-8.0 * (np.arange(A_HEADS, dtype=np.float64) + 1.0) / A_HEADS)
    col = np.arange(A_WIN)[None, :]
    rel = (col - A_HALF) - np.arange(A_QB)[:, None]
    bias = -slopes[:, None, None] * (np.abs(rel) * dil).astype(np.float64)[None]
    bias = np.where((np.abs(rel) <= A_HALF)[None], bias, NEG)
    variants = []
    for v in range(4):
        ok = np.ones((1, A_WIN), bool)
        if v & 1:
            ok = np.logical_and(ok, col >= A_HALF)
        if v & 2:
            ok = np.logical_and(ok, col < A_HALF + A_QB)
        variants.append(np.where(ok[None], bias, NEG).reshape(A_HEADS // 2, 2 * A_QB, A_WIN))
    return jnp.asarray(np.concatenate(variants, axis=0), F32)


def _mla_kernel(q_ref, k_ref, vt_ref, o_ref, *s_bufs, tk):
    tq = s_bufs[0].shape[2]
    nk = k_ref.shape[1] // tk
    nh = q_ref.shape[2] // HEAD_PAD
    qs = [None] * nh

    def scores(j, s_ref, hh):
        off = pl.multiple_of(j * tk, tk)
        s_ref[hh] = lax.dot_general(k_ref[0, pl.ds(off, tk), hh * HEAD_PAD:(hh + 1) * HEAD_PAD], qs[hh],
                                    (((1,), (1,)), ((), ())), preferred_element_type=F32)

    ones = jnp.ones((ONES_ROWS, tk), BF16)

    def values(j, hh, alpha, p, acc):
        off = pl.multiple_of(j * tk, tk)
        vt = jnp.concatenate([vt_ref[hh * B_V:(hh + 1) * B_V, pl.ds(off, tk)], ones], axis=0)
        return alpha * acc + jnp.dot(vt, p, preferred_element_type=F32)

    def stage(j, s_ref, j_next, s_next, stats):
        new = []
        for hh in range(nh):
            m, acc = stats[hh]
            scores(j_next, s_next, hh)
            s = s_ref[hh]
            m_new = jnp.maximum(m, jnp.max(s, axis=0, keepdims=True))
            p = jnp.exp2(s - m_new).astype(BF16)
            new.append((m_new, values(j, hh, jnp.exp2(m - m_new), p, acc)))
        return tuple(new)

    nbuf = len(s_bufs)
    trip = min(MLA_TRIP_CHUNKS, nk)

    def body(jj, stats):
        j = trip * jj
        for u in range(trip):
            stats = stage(j + u, s_bufs[u % nbuf], jnp.minimum(j + u + 1, nk - 1), s_bufs[(u + 1) % nbuf], stats)
        return stats

    init = (jnp.full((1, tq), NEG, F32), jnp.zeros((B_V + ONES_ROWS, tq), F32))
    for t in range(q_ref.shape[1] // tq):
        rows = slice(t * tq, (t + 1) * tq)
        for hh in range(nh):
            qs[hh] = q_ref[0, rows, hh * HEAD_PAD:(hh + 1) * HEAD_PAD]
            scores(0, s_bufs[0], hh)
        stats = lax.fori_loop(0, nk // trip, functools.partial(body), (init,) * nh)
        o_t = jnp.concatenate([a[:B_V] / a[B_V:B_V + 1] for _, a in stats], axis=0)
        o_ref[0, rows, :] = o_t.T.astype(BF16)


def _mla_attention(qb, kb, vbt, batch, seq, tq=256, tk=512, nh=MLA_HEADS_PER_STEP, tiles=2):
    once = dict(pipeline_mode=pl.Buffered(1))
    return pl.pallas_call(
        functools.partial(_mla_kernel, tk=tk),
        grid=(batch, B_HEADS // nh, seq // (tiles * tq)),
        in_specs=[pl.BlockSpec((1, tiles * tq, nh * HEAD_PAD), lambda b, p, i: (b, i, p)),
                  pl.BlockSpec((1, seq, nh * HEAD_PAD), lambda b, p, i: (b, 0, p), **once),
                  pl.BlockSpec((nh * B_V, seq), lambda b, p, i: (p, b), **once)],
        out_specs=pl.BlockSpec((1, tiles * tq, nh * B_V), lambda b, p, i: (b, i, p)),
        out_shape=jax.ShapeDtypeStruct((batch, seq, B_WIDTH), BF16),
        scratch_shapes=[pltpu.VMEM((nh, tk, tq), F32) for _ in range(MLA_SCORE_BUFS)],
        compiler_params=_cparams("parallel", "parallel", "parallel"),
        name="mla_attention",
    )(qb, kb, vbt)


def _outproj_kernel(o1_ref, o2_ref, o3_ref, l1_ref, l2_ref, l3_ref, ob_ref, gate_ref, x_ref,
                    wa_ref, wb_ref, wo_ref, g2_ref, wrh_ref, wrl_ref, br_ref,
                    tri_ref, x1_ref, h2_ref, route_ref, count_ref, perm_ref, perm4_ref, carry_ref):
    tm = x_ref.shape[0]

    @pl.when(pl.program_id(0) == 0)
    def _():
        carry_ref[...] = jnp.zeros_like(carry_ref)

    nc = A_WIDTH // 128

    def lanes(r, c):
        return slice(r * A_WIDTH + c * 128, r * A_WIDTH + (c + 1) * 128)

    def token_order(which, ref, d):
        if d == 4:
            for r in range(4):
                for c in range(nc):
                    perm_ref[which, c, pl.ds(r, tm // 4, stride=4), :] = ref[:, lanes(r, c)].astype(F32)
        else:
            for r1 in range(4):
                for r2 in range(4):
                    for c in range(nc):
                        perm4_ref[which - 2, r1, c, pl.ds(r2, tm // 16, stride=4), :] = (
                            ref[:, lanes(4 * r2 + r1, c)].astype(F32))
            for r1 in range(4):
                for c in range(nc):
                    perm_ref[which, c, pl.ds(r1, tm // 4, stride=4), :] = perm4_ref[which - 2, r1, c]
        return jnp.concatenate([perm_ref[which, c] for c in range(nc)], axis=1)

    l1 = l1_ref[...]
    l2, o2 = token_order(0, l2_ref, 4), token_order(1, o2_ref, 4)
    l3, o3 = token_order(2, l3_ref, 16), token_order(3, o3_ref, 16)
    mx = jnp.maximum(jnp.maximum(l1, l2), l3)
    e1, e2, e3 = jnp.exp(l1 - mx), jnp.exp(l2 - mx), jnp.exp(l3 - mx)
    oa = (e1 * o1_ref[...] + e2 * o2 + e3 * o3) / (e1 + e2 + e3)
    ya = jnp.dot(oa.astype(BF16), wa_ref[...], preferred_element_type=F32)
    yb = jnp.dot(ob_ref[...], wb_ref[...], preferred_element_type=F32)
    merged = gate_ref[:, :D_MODEL].astype(F32) * ya + gate_ref[:, D_MODEL:].astype(F32) * yb
    x1 = x_ref[...] + jnp.dot(merged.astype(BF16), wo_ref[...], preferred_element_type=F32)
    x1_ref[...] = x1
    h2 = x1 * lax.rsqrt(jnp.mean(x1 * x1, axis=-1, keepdims=True) + EPS) * g2_ref[...]
    hi = h2.astype(BF16)
    h2_ref[...] = _pack_rows(h2)
    lo = (h2 - hi.astype(F32)).astype(BF16)
    logits = jnp.dot(hi, wrh_ref[...], preferred_element_type=F32)
    logits = logits + jnp.dot(lo, wrh_ref[...], preferred_element_type=F32)
    logits = logits + jnp.dot(hi, wrl_ref[...], preferred_element_type=F32)
    logits = logits + br_ref[...]

    lane = lax.broadcasted_iota(jnp.int32, (tm, LOGIT_PAD), 1)
    work = logits
    sel = jnp.zeros((tm, LOGIT_PAD), F32)
    idx_cols, val_cols = [], []
    for _ in range(TOP_K):
        mx = jnp.max(work, axis=-1, keepdims=True)
        idx = jnp.min(jnp.where(work == mx, lane, LOGIT_PAD), axis=-1, keepdims=True)
        hit = lane == idx
        sel = sel + hit.astype(F32)
        work = jnp.where(hit, 2.0 * NEG, work)
        idx_cols.append(idx)
        val_cols.append(mx)
    exps = [jnp.exp(v - val_cols[0]) for v in val_cols]
    denom = exps[0] + exps[1] + exps[2] + exps[3]
    gate_cols = [e / denom for e in exps]

    before = jnp.dot(tri_ref[...], sel.astype(BF16), preferred_element_type=F32) + carry_ref[0:1, :]
    rank_cols = [jnp.sum(jnp.where(lane == idx, before, 0.0), axis=-1, keepdims=True) for idx in idx_cols]
    carry_ref[...] = carry_ref[...] + jnp.sum(sel, axis=0, keepdims=True)
    count_ref[...] = carry_ref[...]

    cols = [c.astype(F32) for c in idx_cols] + rank_cols + gate_cols
    route = jnp.zeros((tm, LOGIT_PAD), F32)
    for c, col in enumerate(cols):
        route = jnp.where(lane == c, col, route)
    route_ref[...] = route


def _outproj(o1, o2, o3, l1, l2, l3, ob, gates, x2, wa, wb, wo, g2, wrh, wrl, br, tm=512):
    T = x2.shape[0]
    row = lambda i: (i, 0)
    fix = lambda i: (0, 0)
    full = lambda a: pl.BlockSpec(a.shape, fix)
    rows = lambda a: pl.BlockSpec((tm * a.shape[0] // T, a.shape[1]), row)
    tri = (jnp.arange(tm)[:, None] > jnp.arange(tm)[None, :]).astype(BF16)
    out_shape = (jax.ShapeDtypeStruct((T, D_MODEL), F32),
                 jax.ShapeDtypeStruct((T, HALF_D), jnp.int32),
                 jax.ShapeDtypeStruct((T, LOGIT_PAD), F32),
                 jax.ShapeDtypeStruct((8, LOGIT_PAD), F32))
    return pl.pallas_call(
        _outproj_kernel,
        grid=(T // tm,),
        in_specs=[rows(o1), rows(o2), rows(o3), rows(l1), rows(l2), rows(l3), rows(ob), rows(gates),
                  rows(x2), full(wa), full(wb), full(wo), full(g2), full(wrh), full(wrl), full(br), full(tri)],
        out_specs=[pl.BlockSpec((tm, D_MODEL), row), pl.BlockSpec((tm, HALF_D), row),
                   pl.BlockSpec((tm, LOGIT_PAD), row), pl.BlockSpec((8, LOGIT_PAD), fix)],
        out_shape=out_shape,
        scratch_shapes=[pltpu.VMEM((4, A_WIDTH // 128, tm, 128), F32),
                        pltpu.VMEM((2, 4, A_WIDTH // 128, tm // 4, 128), F32),
                        pltpu.VMEM((8, LOGIT_PAD), F32)],
        compiler_params=_cparams("arbitrary"),
        name="outproj",
    )(o1, o2, o3, l1, l2, l3, ob, gates, x2, wa, wb, wo, g2, wrh, wrl, br, tri)


HALF_D = D_MODEL // 2


def _pack_rows(x):
    lo = pltpu.bitcast(x[:, :HALF_D].astype(BF16).astype(F32), jnp.uint32)
    hi = pltpu.bitcast(x[:, HALF_D:].astype(BF16).astype(F32), jnp.uint32)
    word = jnp.bitwise_or(lax.shift_right_logical(lo, jnp.uint32(16)), jnp.bitwise_and(hi, jnp.uint32(0xFFFF0000)))
    return pltpu.bitcast(word, jnp.int32)


def _unpack_rows(w):
    u = pltpu.bitcast(w, jnp.uint32)
    lo = pltpu.bitcast(lax.shift_left(u, jnp.uint32(16)), F32)
    hi = pltpu.bitcast(jnp.bitwise_and(u, jnp.uint32(0xFFFF0000)), F32)
    return jnp.concatenate([lo, hi], axis=1)


SC_WINDOW = 128


def _gather_rows(table, idx):
    M = idx.shape[0]
    width = table.shape[1]
    mesh = plsc.VectorSubcoreMesh(core_axis_name="core", subcore_axis_name="subcore")

    @functools.partial(pl.kernel, out_type=jax.ShapeDtypeStruct((M, width), table.dtype), mesh=mesh,
                       name="gather_rows")
    def gather(table_hbm, idx_hbm, out_hbm):
        def body(idx_vmem, out_vmem):
            pltpu.sync_copy(table_hbm.at[idx_vmem.at[0]], out_vmem)

        pltpu.emit_pipeline(
            body,
            grid=(M // SC_WINDOW,),
            in_specs=[pl.BlockSpec((1, SC_WINDOW), index_map=lambda i: (0, i))],
            out_specs=[pl.BlockSpec((SC_WINDOW, width), index_map=lambda i: (i, 0),
                                    pipeline_mode=pl.Buffered(1))],
            core_axis_name=("core", "subcore"),
            dimension_semantics=(pltpu.PARALLEL,),
        )(idx_hbm, out_hbm)

    return gather(table, idx.reshape(1, M))


def _scatter_rows(rows, idx, n_out):
    M = idx.shape[0]
    R, width = rows.shape
    mesh = plsc.VectorSubcoreMesh(core_axis_name="core", subcore_axis_name="subcore")

    @functools.partial(pl.kernel, out_type=jax.ShapeDtypeStruct((n_out, width), rows.dtype), mesh=mesh,
                       name="scatter_rows")
    def scatter(rows_hbm, idx_hbm, out_hbm):
        def body(rows_vmem, idx_vmem):
            pltpu.sync_copy(rows_vmem, out_hbm.at[idx_vmem.at[0]])

        pltpu.emit_pipeline(
            body,
            grid=(M // SC_WINDOW,),
            in_specs=[pl.BlockSpec((SC_WINDOW, width), index_map=lambda i: (i % (R // SC_WINDOW), 0),
                                   pipeline_mode=pl.Buffered(1)),
                      pl.BlockSpec((1, SC_WINDOW), index_map=lambda i: (0, i))],
            out_specs=[],
            core_axis_name=("core", "subcore"),
            dimension_semantics=(pltpu.PARALLEL,),
        )(rows_hbm, idx_hbm)

    return scatter(rows, idx.reshape(1, M))


def _moe_kernel(be_ref, nu_ref, nv_ref, x_ref, wgu_ref, bgu_ref, wd_ref, bd_ref, y_ref, wgu_bf, wd_bf):
    i = pl.program_id(0)

    @pl.when(i >= nu_ref[0])
    def _():
        y_ref[...] = jnp.zeros_like(y_ref)

    @pl.when(jnp.logical_or(i == 0, be_ref[i] != be_ref[jnp.maximum(i - 1, 0)]))
    def _():
        rows = 128

        def cast(c, carry):
            r0 = pl.multiple_of(c * rows, rows)
            wgu_bf[pl.ds(r0, rows), :] = wgu_ref[0, pl.ds(r0, rows), :].astype(BF16)
            wd_bf[pl.ds(r0, rows), :] = wd_ref[0, pl.ds(r0, rows), :].astype(BF16)
            return carry

        lax.fori_loop(0, D_MODEL // rows, cast, 0)

    @pl.when(i < nu_ref[0])
    def _():
        live = lax.broadcasted_iota(jnp.int32, (x_ref.shape[0], 1), 0) < nv_ref[i]
        x = jnp.where(live, _unpack_rows(x_ref[...]), 0.0).astype(BF16)
        gu = jnp.dot(x, wgu_bf[...], preferred_element_type=F32) + bgu_ref[0]
        gate = jnp.minimum(gu[:, :D_MODEL], SWIGLU_LIMIT)
        up = jnp.clip(gu[:, D_MODEL:], -SWIGLU_LIMIT, SWIGLU_LIMIT)
        act = (up + 1.0) * (gate * jax.nn.sigmoid(SWIGLU_ALPHA * gate))
        y = jnp.dot(act.astype(BF16), wd_bf[...], preferred_element_type=F32) + bd_ref[0]
        y_ref[...] = _pack_rows(y)


def _moe_experts(xb, blk_e, n_used, n_valid, wgu, bgu, wd, bd, bm):
    NP = xb.shape[0]
    nblk = NP // bm
    grid_spec = pltpu.PrefetchScalarGridSpec(
        num_scalar_prefetch=3,
        grid=(nblk,),
        in_specs=[pl.BlockSpec((bm, HALF_D), lambda i, be, nu, nv: (jnp.minimum(i, nu[0] - 1), 0)),
                  pl.BlockSpec((1, D_MODEL, 2 * D_MODEL), lambda i, be, nu, nv: (be[i], 0, 0)),
                  pl.BlockSpec((1, 1, 2 * D_MODEL), lambda i, be, nu, nv: (be[i], 0, 0)),
                  pl.BlockSpec((1, D_MODEL, D_MODEL), lambda i, be, nu, nv: (be[i], 0, 0)),
                  pl.BlockSpec((1, 1, D_MODEL), lambda i, be, nu, nv: (be[i], 0, 0))],
        out_specs=pl.BlockSpec((bm, HALF_D), lambda i, be, nu, nv: (i, 0)),
        scratch_shapes=[pltpu.VMEM((D_MODEL, 2 * D_MODEL), BF16), pltpu.VMEM((D_MODEL, D_MODEL), BF16)],
    )
    return pl.pallas_call(
        _moe_kernel,
        grid_spec=grid_spec,
        out_shape=jax.ShapeDtypeStruct((NP, HALF_D), jnp.int32),
        compiler_params=_cparams("arbitrary"),
        name="moe_experts",
    )(blk_e, n_used, n_valid, xb, wgu, bgu, wd, bd)


def _final_kernel(x1_ref, y_ref, route_ref, g_ref, o_ref):
    w = route_ref[...]
    acc = x1_ref[...]
    for k in range(TOP_K):
        acc = acc + w[:, 2 * TOP_K + k:2 * TOP_K + k + 1] * _unpack_rows(y_ref[k])
    o_ref[...] = acc * lax.rsqrt(jnp.mean(acc * acc, axis=-1, keepdims=True) + EPS) * g_ref[...]


def _final(x1, yk, route, g, tm=512):
    T = x1.shape[0]
    return pl.pallas_call(
        _final_kernel,
        grid=(T // tm,),
        in_specs=[pl.BlockSpec((tm, D_MODEL), lambda i: (i, 0)),
                  pl.BlockSpec((TOP_K, tm, HALF_D), lambda i: (0, i, 0)),
                  pl.BlockSpec((tm, LOGIT_PAD), lambda i: (i, 0)),
                  pl.BlockSpec((1, D_MODEL), lambda i: (0, 0))],
        out_specs=pl.BlockSpec((tm, D_MODEL), lambda i: (i, 0)),
        out_shape=jax.ShapeDtypeStruct((T, D_MODEL), F32),
        compiler_params=_cparams("parallel"),
        name="final_norm",
    )(x1, yk, route, g)


def _dest_kernel(route_ref, pstart_ref, dest_ref):
    tm = route_ref.shape[0]
    route = route_ref[...]
    lane = lax.broadcasted_iota(jnp.int32, (tm, LOGIT_PAD), 1)
    expert = jnp.where(lane < TOP_K, route, 0.0).astype(jnp.int32)
    start = jnp.take_along_axis(jnp.broadcast_to(pstart_ref[...], (tm, LOGIT_PAD)), expert, axis=1)
    position = pltpu.roll(route, LOGIT_PAD - TOP_K, axis=1)
    rows = start + position
    dest_ref[...] = rows.T[:TOP_K].astype(jnp.int32)


def _dest_rows(route, pstart, tm=2048):
    T = route.shape[0]
    return pl.pallas_call(
        _dest_kernel,
        grid=(T // tm,),
        in_specs=[pl.BlockSpec((tm, LOGIT_PAD), lambda i: (i, 0)),
                  pl.BlockSpec((1, LOGIT_PAD), lambda i: (0, 0))],
        out_specs=pl.BlockSpec((TOP_K, tm), lambda i: (0, i)),
        out_shape=jax.ShapeDtypeStruct((TOP_K, T), jnp.int32),
        compiler_params=_cparams("parallel"),
        name="dest_rows",
    )(route, pstart)


def _pack_w_in(w):
    kr = w[:, 1920:1952]
    half = B_ROPE // 2
    z = lambda n: jnp.zeros((w.shape[0], n), w.dtype)
    kr_placed = jnp.concatenate([z(B_NOPE), kr, z(HEAD_PAD - B_NOPE - B_ROPE)], axis=1)
    kr_rot = jnp.concatenate([z(B_NOPE), -kr[:, half:], kr[:, :half], z(HEAD_PAD - B_NOPE - B_ROPE)], axis=1)
    return jnp.concatenate([w[:, :1920], w[:, 1952:4000], kr_placed, kr_rot], axis=1).astype(BF16)


def _pack_w_uq(w):
    half = B_ROPE // 2
    r = w.shape[0]
    nope, x1, x2 = w[..., :B_NOPE], w[..., B_NOPE:B_NOPE + half], w[..., B_NOPE + half:]
    zpad = jnp.zeros((r, B_HEADS, HEAD_PAD - B_NOPE - B_ROPE), w.dtype)
    main = jnp.concatenate([nope, x1, x2, zpad], axis=-1)
    rot = jnp.concatenate([jnp.zeros_like(nope), -x2, x1, zpad], axis=-1)
    return (main.reshape(r, B_HEADS * HEAD_PAD).astype(BF16), rot.reshape(r, B_HEADS * HEAD_PAD).astype(BF16))


def _pack_w_uk(w):
    r = w.shape[0]
    zpad = jnp.zeros((r, B_HEADS, HEAD_PAD - B_NOPE), w.dtype)
    return jnp.concatenate([w, zpad], axis=-1).reshape(r, B_HEADS * HEAD_PAD).astype(BF16)


def _rope_tables(seq):
    inv_freq = ROPE_THETA ** (-np.arange(0, B_ROPE, 2, dtype=np.float64) / B_ROPE)
    ang = np.arange(seq, dtype=np.float64)[:, None] * inv_freq[None, :]
    cos, sin = jnp.asarray(np.cos(ang), F32), jnp.asarray(np.sin(ang), F32)
    one = jnp.ones((seq, B_NOPE), F32)
    zero = jnp.zeros((seq, B_NOPE), F32)
    zpad = jnp.zeros((seq, HEAD_PAD - B_NOPE - B_ROPE), F32)
    qs = ((B_NOPE + B_ROPE) ** -0.5) * LOG2E
    cq = jnp.concatenate([one, cos, cos, zpad], axis=1) * qs
    sq = jnp.concatenate([zero, sin, sin, zpad], axis=1) * qs
    ck = jnp.concatenate([zero, cos, cos, zpad], axis=1)
    sk = jnp.concatenate([zero, sin, sin, zpad], axis=1)
    return cq, sq, ck, sk


def kernel(x, norm1_g, w_in, q_a_norm_g, kv_a_norm_g, w_uq, w_uk, w_uv, w_a_out, w_b_out, w_o,
           norm2_g, w_router, b_router, w_gate_up, b_gate_up, w_down, b_down, final_g):
    B, S, D = x.shape
    T = B * S
    depth = norm1_g.shape[0]
    xt = x.reshape(T, D)
    cq_t, sq_t, ck_t, sk_t = _rope_tables(S)
    biases = [_dilated_bias(d) for _, d in A_PATTERNS]
    bm = 512
    for l in range(depth):
        wq1, wq2 = _pack_w_uq(w_uq[l])
        qa, ka, va, qa4, ka4, va4, qa16, ka16, va16, qb, kb, vb, gates = _inproj(
            xt, norm1_g[l][None], _pack_w_in(w_in[l]), q_a_norm_g[l][None], kv_a_norm_g[l][None],
            wq1, wq2, _pack_w_uk(w_uk[l]), w_uv[l].reshape(KV_LORA, B_WIDTH).T.astype(BF16),
            cq_t, sq_t, ck_t, sk_t, S)
        qkv = ((qa, ka, va), (qa4, ka4, va4), (qa16, ka16, va16))
        outs = [_dilated_pattern(*qkv[p], biases[p], B, S, A_PATTERNS[p][1]) for p in range(3)]
        ob = _mla_attention(qb.reshape(B, S, -1), kb.reshape(B, S, -1), vb, B, S)

        wr = jnp.pad(w_router[l], ((0, 0), (0, LOGIT_PAD - N_EXPERTS)))
        wrh = wr.astype(BF16)
        wrl = (wr - wrh.astype(F32)).astype(BF16)
        br = jnp.pad(b_router[l], (0, LOGIT_PAD - N_EXPERTS), constant_values=NEG)[None]
        x1, h2, route, count_rows = _outproj(
            outs[0][0], outs[1][0], outs[2][0], outs[0][1], outs[1][1], outs[2][1],
            ob.reshape(T, B_WIDTH), gates, xt,
            w_a_out[l].astype(BF16), w_b_out[l].astype(BF16), w_o[l].astype(BF16),
            norm2_g[l][None], wrh, wrl, br)

        counts = count_rows[0, :N_EXPERTS].astype(jnp.int32)
        padded = (counts + bm - 1) // bm * bm
        pend = jnp.cumsum(padded)
        pstart = pend - padded
        NP = T * TOP_K + N_EXPERTS * bm
        nblk = NP // bm
        blk_start = jnp.arange(nblk, dtype=jnp.int32) * bm
        blk_e = jnp.minimum(jnp.sum(pend[None, :] <= blk_start[:, None], axis=1), N_EXPERTS - 1).astype(jnp.int32)
        n_used = (pend[-1] // bm).astype(jnp.int32)[None]
        n_valid = jnp.clip(pstart[blk_e] + counts[blk_e] - blk_start, 0, bm).astype(jnp.int32)

        pstart_row = jnp.pad(pstart.astype(F32), (0, LOGIT_PAD - N_EXPERTS))[None]
        dest_kt = _dest_rows(route, pstart_row).reshape(-1)
        xb = _scatter_rows(h2, dest_kt, NP)
        yb = _moe_experts(xb, blk_e, n_used, n_valid, w_gate_up[l], b_gate_up[l][:, None, :],
                          w_down[l], b_down[l][:, None, :], bm)
        yk = _gather_rows(yb, dest_kt).reshape(TOP_K, T, HALF_D)
        if l + 1 < depth:
            raise NotImplementedError("depth > 1")
        out = _final(x1, yk, route, final_g[None])
    return out.reshape(B, S, D)
```
